```python
import math
import jax, jax.numpy as jnp
from jax import lax
import numpy as np

D_MODEL = 1024
BATCH = 8
SEQ = 4096
DEPTH = 2

N_A_LAYERS = DEPTH // 2
N_B_LAYERS = DEPTH - N_A_LAYERS
D_SSM = D_MODEL
SSM_GROUP = 16
N_GROUPS = D_SSM // SSM_GROUP
STATE = 64
N_HEADS = 16
HEAD_DIM = 64
D_ATT = N_HEADS * HEAD_DIM
D_FF = 2816
CONV_W = 3
Q_BLOCK = 128
EPS = 1e-6
DT_MIN = 1e-3
DT_MAX = 1e-1

kernel_name = "yoco_s5_stickbreaking_convffn"


def rms_norm(x, g):
    xf = x.astype(jnp.float32)
    y = xf * lax.rsqrt(jnp.mean(xf * xf, axis=-1, keepdims=True) + EPS)
    return (y * g.astype(jnp.float32)).astype(x.dtype)


def s5_mixer(h, w_in, a_re, a_im, log_dt, b_re, b_im, c_re, c_im, d_skip, w_glu):
    bsz, seq, _ = h.shape
    f32 = jnp.float32
    u = (h @ w_in).astype(f32).reshape(bsz, seq, N_GROUPS, SSM_GROUP)
    a_re = a_re.astype(f32)
    a_im = a_im.astype(f32)
    dt = jnp.exp(log_dt.astype(f32))[:, None]
    mag = jnp.exp(a_re * dt)
    ab_re = mag * jnp.cos(a_im * dt)
    ab_im = mag * jnp.sin(a_im * dt)
    den = a_re * a_re + a_im * a_im
    f_re = ((ab_re - 1.0) * a_re + ab_im * a_im) / den
    f_im = (ab_im * a_re - (ab_re - 1.0) * a_im) / den
    b_re = b_re.astype(f32)
    b_im = b_im.astype(f32)
    bb_re = f_re[..., None] * b_re - f_im[..., None] * b_im
    bb_im = f_re[..., None] * b_im + f_im[..., None] * b_re
    bu_re = jnp.einsum('blgh,gph->blgp', u, bb_re)
    bu_im = jnp.einsum('blgh,gph->blgp', u, bb_im)
    shape_a = (1, seq, N_GROUPS, STATE)
    a_seq_re = jnp.broadcast_to(ab_re[None, None], shape_a)
    a_seq_im = jnp.broadcast_to(ab_im[None, None], shape_a)

    def combine(e1, e2):
        a1r, a1i, b1r, b1i = e1
        a2r, a2i, b2r, b2i = e2
        return (a2r * a1r - a2i * a1i,
                a2r * a1i + a2i * a1r,
                a2r * b1r - a2i * b1i + b2r,
                a2r * b1i + a2i * b1r + b2i)

    _, _, s_re, s_im = lax.associative_scan(
        combine, (a_seq_re, a_seq_im, bu_re, bu_im), axis=1)
    y = (jnp.einsum('blgp,ghp->blgh', s_re, c_re.astype(f32))
         - jnp.einsum('blgp,ghp->blgh', s_im, c_im.astype(f32))
         + d_skip.astype(f32) * u)
    y = jax.nn.gelu(y.reshape(bsz, seq, D_SSM)).astype(h.dtype)
    z = y @ w_glu
    return z[..., :D_MODEL] * jax.nn.sigmoid(z[..., D_MODEL:])


def stick_breaking_attention(h, w_q, k, v, w_o):
    bsz, seq, _ = h.shape
    scale = HEAD_DIM ** -0.5
    q = (h @ w_q).reshape(bsz, seq, N_HEADS, HEAD_DIM).transpose(0, 2, 1, 3) * scale
    outs = []
    for blk in range(seq // Q_BLOCK):
        t0 = blk * Q_BLOCK
        nk = t0 + Q_BLOCK
        qb = q[:, :, t0:nk]
        kb = k[:, :, :nk]
        vb = v[:, :, :nk]
        z = jnp.einsum('bhqd,bhkd->bhqk', qb, kb).astype(jnp.float32)
        t_idx = t0 + jnp.arange(Q_BLOCK)[:, None]
        s_idx = jnp.arange(nk)[None, :]
        causal = s_idx < t_idx
        log_beta = jax.nn.log_sigmoid(z)
        log_one_minus = jnp.where(causal, log_beta - z, 0.0)
        rem = lax.cumsum(log_one_minus, axis=3, reverse=True) - log_one_minus
        w = jnp.where(causal, jnp.exp(log_beta + rem), 0.0)
        outs.append(jnp.einsum('bhqk,bhkd->bhqd', w.astype(vb.dtype), vb))
    o = jnp.concatenate(outs, axis=2).transpose(0, 2, 1, 3).reshape(bsz, seq, D_ATT)
    return o @ w_o


def conv_ffn(h, w_up, conv_w, conv_b, w_down):
    gu = h @ w_up
    g = gu[..., :D_FF]
    u = gu[..., D_FF:]
    g = lax.conv_general_dilated(
        g, conv_w, window_strides=(1,), padding=[(CONV_W - 1, 0)],
        dimension_numbers=('NWC', 'WIO', 'NWC'), feature_group_count=D_FF) + conv_b
    return (jax.nn.silu(g) * u) @ w_down


def _fwd_setup_inputs(seed: int = 0) -> dict:
    key = jax.random.key(seed)
    ks = jax.random.split(key, 24)
    f32 = jnp.float32
    nrm = lambda k, shape, s: jax.random.normal(k, shape, f32) * s
    n_idx = jnp.arange(STATE, dtype=f32)
    return {
        "x": jax.random.normal(ks[0], (BATCH, SEQ, D_MODEL), f32),
        "norm_mix": 1.0 + nrm(ks[1], (DEPTH, D_MODEL), 0.02),
        "norm_ffn": 1.0 + nrm(ks[2], (DEPTH, D_MODEL), 0.02),
        "norm_kv": 1.0 + nrm(ks[3], (D_MODEL,), 0.02),
        "norm_final": 1.0 + nrm(ks[4], (D_MODEL,), 0.02),
        "ssm_w_in": nrm(ks[5], (N_A_LAYERS, D_MODEL, D_SSM), D_MODEL ** -0.5),
        "ssm_a_re": -0.5 * jnp.exp(nrm(ks[6], (N_A_LAYERS, N_GROUPS, STATE), 0.05)),
        "ssm_a_im": math.pi * n_idx + nrm(ks[7], (N_A_LAYERS, N_GROUPS, STATE), 0.01),
        "ssm_log_dt": jax.random.uniform(ks[8], (N_A_LAYERS, N_GROUPS), f32,
                                         minval=math.log(DT_MIN), maxval=math.log(DT_MAX)),
        "ssm_b_re": nrm(ks[9], (N_A_LAYERS, N_GROUPS, STATE, SSM_GROUP), (2 * SSM_GROUP) ** -0.5),
        "ssm_b_im": nrm(ks[10], (N_A_LAYERS, N_GROUPS, STATE, SSM_GROUP), (2 * SSM_GROUP) ** -0.5),
        "ssm_c_re": nrm(ks[11], (N_A_LAYERS, N_GROUPS, SSM_GROUP, STATE), (2 * STATE) ** -0.5),
        "ssm_c_im": nrm(ks[12], (N_A_LAYERS, N_GROUPS, SSM_GROUP, STATE), (2 * STATE) ** -0.5),
        "ssm_d": nrm(ks[13], (N_A_LAYERS, N_GROUPS, SSM_GROUP), 1.0),
        "ssm_w_glu": nrm(ks[14], (N_A_LAYERS, D_SSM, 2 * D_MODEL), D_SSM ** -0.5),
        "kv_w": nrm(ks[15], (D_MODEL, 2 * D_ATT), D_MODEL ** -0.5),
        "attn_w_q": nrm(ks[16], (N_B_LAYERS, D_MODEL, D_ATT), D_MODEL ** -0.5),
        "attn_w_o": nrm(ks[17], (N_B_LAYERS, D_ATT, D_MODEL), D_ATT ** -0.5),
        "ffn_w_up": nrm(ks[18], (DEPTH, D_MODEL, 2 * D_FF), D_MODEL ** -0.5),
        "ffn_conv_w": nrm(ks[19], (DEPTH, CONV_W, 1, D_FF), CONV_W ** -0.5),
        "ffn_conv_b": nrm(ks[20], (DEPTH, D_FF), 0.01),
        "ffn_w_down": nrm(ks[21], (DEPTH, D_FF, D_MODEL), D_FF ** -0.5),
    }


def _fwd_reference(x, norm_mix, norm_ffn, norm_kv, norm_final,
              ssm_w_in, ssm_a_re, ssm_a_im, ssm_log_dt, ssm_b_re, ssm_b_im,
              ssm_c_re, ssm_c_im, ssm_d, ssm_w_glu,
              kv_w, attn_w_q, attn_w_o,
              ffn_w_up, ffn_conv_w, ffn_conv_b, ffn_w_down):
    bsz, seq, _ = x.shape
    k_shared = None
    v_shared = None
    for layer in range(DEPTH):
        h = rms_norm(x, norm_mix[layer])
        if layer < N_A_LAYERS:
            x = x + s5_mixer(h, ssm_w_in[layer], ssm_a_re[layer], ssm_a_im[layer],
                             ssm_log_dt[layer], ssm_b_re[layer], ssm_b_im[layer],
                             ssm_c_re[layer], ssm_c_im[layer], ssm_d[layer],
                             ssm_w_glu[layer])
        else:
            j = layer - N_A_LAYERS
            x = x + stick_breaking_attention(h, attn_w_q[j], k_shared, v_shared, attn_w_o[j])
        x = x + conv_ffn(rms_norm(x, norm_ffn[layer]), ffn_w_up[layer],
                         ffn_conv_w[layer], ffn_conv_b[layer], ffn_w_down[layer])
        if layer == N_A_LAYERS - 1:
            kv = rms_norm(x, norm_kv) @ kv_w
            k_shared = kv[..., :D_ATT].reshape(bsz, seq, N_HEADS, HEAD_DIM).transpose(0, 2, 1, 3)
            v_shared = kv[..., D_ATT:].reshape(bsz, seq, N_HEADS, HEAD_DIM).transpose(0, 2, 1, 3)
    return rms_norm(x, norm_final)


import jax as _jax
import jax.numpy as _jnp

TWIN_FORMAT = 'train_step'
FWD_PARAMS = ['x', 'norm_mix', 'norm_ffn', 'norm_kv', 'norm_final', 'ssm_w_in', 'ssm_a_re', 'ssm_a_im', 'ssm_log_dt', 'ssm_b_re', 'ssm_b_im', 'ssm_c_re', 'ssm_c_im', 'ssm_d', 'ssm_w_glu', 'kv_w', 'attn_w_q', 'attn_w_o', 'ffn_w_up', 'ffn_conv_w', 'ffn_conv_b', 'ffn_w_down']
TWIN_WEIGHTS = ['norm_mix', 'norm_ffn', 'norm_kv', 'norm_final', 'ssm_w_in', 'ssm_a_re', 'ssm_a_im', 'ssm_log_dt', 'ssm_b_re', 'ssm_b_im', 'ssm_c_re', 'ssm_c_im', 'ssm_d', 'ssm_w_glu', 'kv_w', 'attn_w_q', 'attn_w_o', 'ffn_w_up', 'ffn_conv_w', 'ffn_conv_b', 'ffn_w_down']
TWIN_DIFF_INPUT = 'x'
TWIN_INPUTS = ['x', 'norm_mix', 'norm_ffn', 'norm_kv', 'norm_final', 'ssm_w_in', 'ssm_a_re', 'ssm_a_im', 'ssm_log_dt', 'ssm_b_re', 'ssm_b_im', 'ssm_c_re', 'ssm_c_im', 'ssm_d', 'ssm_w_glu', 'kv_w', 'attn_w_q', 'attn_w_o', 'ffn_w_up', 'ffn_conv_w', 'ffn_conv_b', 'ffn_w_down', 'loss_target', 'm_norm_mix', 'm_norm_ffn', 'm_norm_kv', 'm_norm_final', 'm_ssm_w_in', 'm_ssm_a_re', 'm_ssm_a_im', 'm_ssm_log_dt', 'm_ssm_b_re', 'm_ssm_b_im', 'm_ssm_c_re', 'm_ssm_c_im', 'm_ssm_d', 'm_ssm_w_glu', 'm_kv_w', 'm_attn_w_q', 'm_attn_w_o', 'm_ffn_w_up', 'm_ffn_conv_w', 'm_ffn_conv_b', 'm_ffn_w_down', 'v_norm_mix', 'v_norm_ffn', 'v_norm_kv', 'v_norm_final', 'v_ssm_w_in', 'v_ssm_a_re', 'v_ssm_a_im', 'v_ssm_log_dt', 'v_ssm_b_re', 'v_ssm_b_im', 'v_ssm_c_re', 'v_ssm_c_im', 'v_ssm_d', 'v_ssm_w_glu', 'v_kv_w', 'v_attn_w_q', 'v_attn_w_o', 'v_ffn_w_up', 'v_ffn_conv_w', 'v_ffn_conv_b', 'v_ffn_w_down']
TWIN_OUTPUTS = ['loss', 'grad_x', 'grad_norm_mix', 'grad_norm_ffn', 'grad_norm_kv', 'grad_norm_final', 'grad_ssm_w_in', 'grad_ssm_a_re', 'grad_ssm_a_im', 'grad_ssm_log_dt', 'grad_ssm_b_re', 'grad_ssm_b_im', 'grad_ssm_c_re', 'grad_ssm_c_im', 'grad_ssm_d', 'grad_ssm_w_glu', 'grad_kv_w', 'grad_attn_w_q', 'grad_attn_w_o', 'grad_ffn_w_up', 'grad_ffn_conv_w', 'grad_ffn_conv_b', 'grad_ffn_w_down', 'delta_norm_mix', 'delta_norm_ffn', 'delta_norm_kv', 'delta_norm_final', 'delta_ssm_w_in', 'delta_ssm_a_re', 'delta_ssm_a_im', 'delta_ssm_log_dt', 'delta_ssm_b_re', 'delta_ssm_b_im', 'delta_ssm_c_re', 'delta_ssm_c_im', 'delta_ssm_d', 'delta_ssm_w_glu', 'delta_kv_w', 'delta_attn_w_q', 'delta_attn_w_o', 'delta_ffn_w_up', 'delta_ffn_conv_w', 'delta_ffn_conv_b', 'delta_ffn_w_down', 'new_m_norm_mix', 'new_m_norm_ffn', 'new_m_norm_kv', 'new_m_norm_final', 'new_m_ssm_w_in', 'new_m_ssm_a_re', 'new_m_ssm_a_im', 'new_m_ssm_log_dt', 'new_m_ssm_b_re', 'new_m_ssm_b_im', 'new_m_ssm_c_re', 'new_m_ssm_c_im', 'new_m_ssm_d', 'new_m_ssm_w_glu', 'new_m_kv_w', 'new_m_attn_w_q', 'new_m_attn_w_o', 'new_m_ffn_w_up', 'new_m_ffn_conv_w', 'new_m_ffn_conv_b', 'new_m_ffn_w_down', 'new_v_norm_mix', 'new_v_norm_ffn', 'new_v_norm_kv', 'new_v_norm_final', 'new_v_ssm_w_in', 'new_v_ssm_a_re', 'new_v_ssm_a_im', 'new_v_ssm_log_dt', 'new_v_ssm_b_re', 'new_v_ssm_b_im', 'new_v_ssm_c_re', 'new_v_ssm_c_im', 'new_v_ssm_d', 'new_v_ssm_w_glu', 'new_v_kv_w', 'new_v_attn_w_q', 'new_v_attn_w_o', 'new_v_ffn_w_up', 'new_v_ffn_conv_w', 'new_v_ffn_conv_b', 'new_v_ffn_w_down']
TWIN_LEAF_KINDS = {'loss': 'loss', 'grad_x': 'grad_x', 'grad_norm_mix': 'grad_w', 'grad_norm_ffn': 'grad_w', 'grad_norm_kv': 'grad_w', 'grad_norm_final': 'grad_w', 'grad_ssm_w_in': 'grad_w', 'grad_ssm_a_re': 'grad_w', 'grad_ssm_a_im': 'grad_w', 'grad_ssm_log_dt': 'grad_w', 'grad_ssm_b_re': 'grad_w', 'grad_ssm_b_im': 'grad_w', 'grad_ssm_c_re': 'grad_w', 'grad_ssm_c_im': 'grad_w', 'grad_ssm_d': 'grad_w', 'grad_ssm_w_glu': 'grad_w', 'grad_kv_w': 'grad_w', 'grad_attn_w_q': 'grad_w', 'grad_attn_w_o': 'grad_w', 'grad_ffn_w_up': 'grad_w', 'grad_ffn_conv_w': 'grad_w', 'grad_ffn_conv_b': 'grad_w', 'grad_ffn_w_down': 'grad_w', 'delta_norm_mix': 'delta_w', 'delta_norm_ffn': 'delta_w', 'delta_norm_kv': 'delta_w', 'delta_norm_final': 'delta_w', 'delta_ssm_w_in': 'delta_w', 'delta_ssm_a_re': 'delta_w', 'delta_ssm_a_im': 'delta_w', 'delta_ssm_log_dt': 'delta_w', 'delta_ssm_b_re': 'delta_w', 'delta_ssm_b_im': 'delta_w', 'delta_ssm_c_re': 'delta_w', 'delta_ssm_c_im': 'delta_w', 'delta_ssm_d': 'delta_w', 'delta_ssm_w_glu': 'delta_w', 'delta_kv_w': 'delta_w', 'delta_attn_w_q': 'delta_w', 'delta_attn_w_o': 'delta_w', 'delta_ffn_w_up': 'delta_w', 'delta_ffn_conv_w': 'delta_w', 'delta_ffn_conv_b': 'delta_w', 'delta_ffn_w_down': 'delta_w', 'new_m_norm_mix': 'new_m', 'new_m_norm_ffn': 'new_m', 'new_m_norm_kv': 'new_m', 'new_m_norm_final': 'new_m', 'new_m_ssm_w_in': 'new_m', 'new_m_ssm_a_re': 'new_m', 'new_m_ssm_a_im': 'new_m', 'new_m_ssm_log_dt': 'new_m', 'new_m_ssm_b_re': 'new_m', 'new_m_ssm_b_im': 'new_m', 'new_m_ssm_c_re': 'new_m', 'new_m_ssm_c_im': 'new_m', 'new_m_ssm_d': 'new_m', 'new_m_ssm_w_glu': 'new_m', 'new_m_kv_w': 'new_m', 'new_m_attn_w_q': 'new_m', 'new_m_attn_w_o': 'new_m', 'new_m_ffn_w_up': 'new_m', 'new_m_ffn_conv_w': 'new_m', 'new_m_ffn_conv_b': 'new_m', 'new_m_ffn_w_down': 'new_m', 'new_v_norm_mix': 'new_v', 'new_v_norm_ffn': 'new_v', 'new_v_norm_kv': 'new_v', 'new_v_norm_final': 'new_v', 'new_v_ssm_w_in': 'new_v', 'new_v_ssm_a_re': 'new_v', 'new_v_ssm_a_im': 'new_v', 'new_v_ssm_log_dt': 'new_v', 'new_v_ssm_b_re': 'new_v', 'new_v_ssm_b_im': 'new_v', 'new_v_ssm_c_re': 'new_v', 'new_v_ssm_c_im': 'new_v', 'new_v_ssm_d': 'new_v', 'new_v_ssm_w_glu': 'new_v', 'new_v_kv_w': 'new_v', 'new_v_attn_w_q': 'new_v', 'new_v_attn_w_o': 'new_v', 'new_v_ffn_w_up': 'new_v', 'new_v_ffn_conv_w': 'new_v', 'new_v_ffn_conv_b': 'new_v', 'new_v_ffn_w_down': 'new_v'}


def _forward(args):
    return _fwd_reference(*[args[k] for k in FWD_PARAMS])


def _output_shape():
    out = _jax.eval_shape(lambda: _forward(_fwd_setup_inputs(0)))
    return out.shape, out.dtype

N_MICROBATCH = 1
ADAM_LR = 0.001
ADAM_B1 = 0.9
ADAM_B2 = 0.999
ADAM_EPS = 1e-08
ADAM_WD = 0.01
ADAM_STEP = 10
PER_EXAMPLE_BATCH_AXIS = {'x': 0, 'loss_target': 0}
SHARED_INPUTS = []
_WEIGHT_DTYPES = {'norm_mix': _jnp.float32, 'norm_ffn': _jnp.float32, 'norm_kv': _jnp.float32, 'norm_final': _jnp.float32, 'ssm_w_in': _jnp.float32, 'ssm_a_re': _jnp.float32, 'ssm_a_im': _jnp.float32, 'ssm_log_dt': _jnp.float32, 'ssm_b_re': _jnp.float32, 'ssm_b_im': _jnp.float32, 'ssm_c_re': _jnp.float32, 'ssm_c_im': _jnp.float32, 'ssm_d': _jnp.float32, 'ssm_w_glu': _jnp.float32, 'kv_w': _jnp.float32, 'attn_w_q': _jnp.float32, 'attn_w_o': _jnp.float32, 'ffn_w_up': _jnp.float32, 'ffn_conv_w': _jnp.float32, 'ffn_conv_b': _jnp.float32, 'ffn_w_down': _jnp.float32}
MOMENT_SCALE = {'norm_mix': 6.503266e-02, 'norm_ffn': 1.241472e-01, 'norm_kv': 1.074441e-01, 'norm_final': 3.200332e+01, 'ssm_w_in': 8.108883e-02, 'ssm_a_re': 4.437427e-03, 'ssm_a_im': 3.601856e-03, 'ssm_log_dt': 3.933090e+00, 'ssm_b_re': 2.652648e-03, 'ssm_b_im': 2.593324e-03, 'ssm_c_re': 5.250564e-03, 'ssm_c_im': 5.263738e-03, 'ssm_d': 9.128512e-02, 'ssm_w_glu': 6.035885e-02, 'kv_w': 7.076048e-02, 'attn_w_q': 4.078800e-02, 'attn_w_o': 9.124501e-02, 'ffn_w_up': 5.283599e-02, 'ffn_conv_w': 5.453409e-02, 'ffn_conv_b': 5.190816e-02, 'ffn_w_down': 8.653008e-02}


def _to_microbatches(a, axis):
    t = _jnp.moveaxis(a, axis, 0)
    t = t.reshape((N_MICROBATCH, t.shape[0] // N_MICROBATCH) + t.shape[1:])
    return _jnp.moveaxis(t, 1, axis + 1)


def setup_inputs(seed: int = 0) -> dict:
    inp = _fwd_setup_inputs(seed)
    key = _jax.random.fold_in(_jax.random.key(seed), 7919)
    shape, _ = _output_shape()
    out = dict(inp)
    out["loss_target"] = _jax.random.normal(_jax.random.fold_in(key, 0), shape, _jnp.float32)
    for i, name in enumerate(TWIN_WEIGHTS):
        w = inp[name].astype(_jnp.float32)
        if MOMENT_SCALE is None:
            s = _jnp.sqrt(_jnp.mean(_jnp.square(w)) + 1e-30)
        else:
            s = MOMENT_SCALE[name]
        km, kv = _jax.random.split(_jax.random.fold_in(key, i + 1))
        out[name] = w
        out["m_" + name] = s * _jax.random.normal(km, w.shape, _jnp.float32)
        out["v_" + name] = (s * s) * _jax.random.uniform(kv, w.shape, _jnp.float32, 0.5, 1.5)
    if N_MICROBATCH > 1:
        for name, axis in PER_EXAMPLE_BATCH_AXIS.items():
            out[name] = _to_microbatches(out[name], axis)
    return {'x': out['x'], 'norm_mix': out['norm_mix'], 'norm_ffn': out['norm_ffn'], 'norm_kv': out['norm_kv'], 'norm_final': out['norm_final'], 'ssm_w_in': out['ssm_w_in'], 'ssm_a_re': out['ssm_a_re'], 'ssm_a_im': out['ssm_a_im'], 'ssm_log_dt': out['ssm_log_dt'], 'ssm_b_re': out['ssm_b_re'], 'ssm_b_im': out['ssm_b_im'], 'ssm_c_re': out['ssm_c_re'], 'ssm_c_im': out['ssm_c_im'], 'ssm_d': out['ssm_d'], 'ssm_w_glu': out['ssm_w_glu'], 'kv_w': out['kv_w'], 'attn_w_q': out['attn_w_q'], 'attn_w_o': out['attn_w_o'], 'ffn_w_up': out['ffn_w_up'], 'ffn_conv_w': out['ffn_conv_w'], 'ffn_conv_b': out['ffn_conv_b'], 'ffn_w_down': out['ffn_w_down'], 'loss_target': out['loss_target'], 'm_norm_mix': out['m_norm_mix'], 'm_norm_ffn': out['m_norm_ffn'], 'm_norm_kv': out['m_norm_kv'], 'm_norm_final': out['m_norm_final'], 'm_ssm_w_in': out['m_ssm_w_in'], 'm_ssm_a_re': out['m_ssm_a_re'], 'm_ssm_a_im': out['m_ssm_a_im'], 'm_ssm_log_dt': out['m_ssm_log_dt'], 'm_ssm_b_re': out['m_ssm_b_re'], 'm_ssm_b_im': out['m_ssm_b_im'], 'm_ssm_c_re': out['m_ssm_c_re'], 'm_ssm_c_im': out['m_ssm_c_im'], 'm_ssm_d': out['m_ssm_d'], 'm_ssm_w_glu': out['m_ssm_w_glu'], 'm_kv_w': out['m_kv_w'], 'm_attn_w_q': out['m_attn_w_q'], 'm_attn_w_o': out['m_attn_w_o'], 'm_ffn_w_up': out['m_ffn_w_up'], 'm_ffn_conv_w': out['m_ffn_conv_w'], 'm_ffn_conv_b': out['m_ffn_conv_b'], 'm_ffn_w_down': out['m_ffn_w_down'], 'v_norm_mix': out['v_norm_mix'], 'v_norm_ffn': out['v_norm_ffn'], 'v_norm_kv': out['v_norm_kv'], 'v_norm_final': out['v_norm_final'], 'v_ssm_w_in': out['v_ssm_w_in'], 'v_ssm_a_re': out['v_ssm_a_re'], 'v_ssm_a_im': out['v_ssm_a_im'], 'v_ssm_log_dt': out['v_ssm_log_dt'], 'v_ssm_b_re': out['v_ssm_b_re'], 'v_ssm_b_im': out['v_ssm_b_im'], 'v_ssm_c_re': out['v_ssm_c_re'], 'v_ssm_c_im': out['v_ssm_c_im'], 'v_ssm_d': out['v_ssm_d'], 'v_ssm_w_glu': out['v_ssm_w_glu'], 'v_kv_w': out['v_kv_w'], 'v_attn_w_q': out['v_attn_w_q'], 'v_attn_w_o': out['v_attn_w_o'], 'v_ffn_w_up': out['v_ffn_w_up'], 'v_ffn_conv_w': out['v_ffn_conv_w'], 'v_ffn_conv_b': out['v_ffn_conv_b'], 'v_ffn_w_down': out['v_ffn_w_down']}


def _loss(weights, diff, rest, loss_target):
    with _jax.named_scope("forward"):
        args = {**rest, TWIN_DIFF_INPUT: diff, **{k: w.astype(_WEIGHT_DTYPES[k]) for k, w in weights.items()}}
        y = _forward(args)
    with _jax.named_scope("loss_head"):
        err = _jnp.square(y.astype(_jnp.float32) - loss_target)
        return 0.5 * _jnp.sum(_jnp.mean(err, axis=-1)) if err.ndim else 0.5 * err


def _adamw(w, g, m, v):
    m = ADAM_B1 * m + (1.0 - ADAM_B1) * g
    v = ADAM_B2 * v + (1.0 - ADAM_B2) * _jnp.square(g)
    m_hat = m / (1.0 - ADAM_B1 ** ADAM_STEP)
    v_hat = v / (1.0 - ADAM_B2 ** ADAM_STEP)
    delta = -ADAM_LR * (m_hat / (_jnp.sqrt(v_hat) + ADAM_EPS) + ADAM_WD * w)
    return delta, m, v


def reference(x, norm_mix, norm_ffn, norm_kv, norm_final, ssm_w_in, ssm_a_re, ssm_a_im, ssm_log_dt, ssm_b_re, ssm_b_im, ssm_c_re, ssm_c_im, ssm_d, ssm_w_glu, kv_w, attn_w_q, attn_w_o, ffn_w_up, ffn_conv_w, ffn_conv_b, ffn_w_down, loss_target, m_norm_mix, m_norm_ffn, m_norm_kv, m_norm_final, m_ssm_w_in, m_ssm_a_re, m_ssm_a_im, m_ssm_log_dt, m_ssm_b_re, m_ssm_b_im, m_ssm_c_re, m_ssm_c_im, m_ssm_d, m_ssm_w_glu, m_kv_w, m_attn_w_q, m_attn_w_o, m_ffn_w_up, m_ffn_conv_w, m_ffn_conv_b, m_ffn_w_down, v_norm_mix, v_norm_ffn, v_norm_kv, v_norm_final, v_ssm_w_in, v_ssm_a_re, v_ssm_a_im, v_ssm_log_dt, v_ssm_b_re, v_ssm_b_im, v_ssm_c_re, v_ssm_c_im, v_ssm_d, v_ssm_w_glu, v_kv_w, v_attn_w_q, v_attn_w_o, v_ffn_w_up, v_ffn_conv_w, v_ffn_conv_b, v_ffn_w_down):
    given = dict(x=x, norm_mix=norm_mix, norm_ffn=norm_ffn, norm_kv=norm_kv, norm_final=norm_final, ssm_w_in=ssm_w_in, ssm_a_re=ssm_a_re, ssm_a_im=ssm_a_im, ssm_log_dt=ssm_log_dt, ssm_b_re=ssm_b_re, ssm_b_im=ssm_b_im, ssm_c_re=ssm_c_re, ssm_c_im=ssm_c_im, ssm_d=ssm_d, ssm_w_glu=ssm_w_glu, kv_w=kv_w, attn_w_q=attn_w_q, attn_w_o=attn_w_o, ffn_w_up=ffn_w_up, ffn_conv_w=ffn_conv_w, ffn_conv_b=ffn_conv_b, ffn_w_down=ffn_w_down, loss_target=loss_target, m_norm_mix=m_norm_mix, m_norm_ffn=m_norm_ffn, m_norm_kv=m_norm_kv, m_norm_final=m_norm_final, m_ssm_w_in=m_ssm_w_in, m_ssm_a_re=m_ssm_a_re, m_ssm_a_im=m_ssm_a_im, m_ssm_log_dt=m_ssm_log_dt, m_ssm_b_re=m_ssm_b_re, m_ssm_b_im=m_ssm_b_im, m_ssm_c_re=m_ssm_c_re, m_ssm_c_im=m_ssm_c_im, m_ssm_d=m_ssm_d, m_ssm_w_glu=m_ssm_w_glu, m_kv_w=m_kv_w, m_attn_w_q=m_attn_w_q, m_attn_w_o=m_attn_w_o, m_ffn_w_up=m_ffn_w_up, m_ffn_conv_w=m_ffn_conv_w, m_ffn_conv_b=m_ffn_conv_b, m_ffn_w_down=m_ffn_w_down, v_norm_mix=v_norm_mix, v_norm_ffn=v_norm_ffn, v_norm_kv=v_norm_kv, v_norm_final=v_norm_final, v_ssm_w_in=v_ssm_w_in, v_ssm_a_re=v_ssm_a_re, v_ssm_a_im=v_ssm_a_im, v_ssm_log_dt=v_ssm_log_dt, v_ssm_b_re=v_ssm_b_re, v_ssm_b_im=v_ssm_b_im, v_ssm_c_re=v_ssm_c_re, v_ssm_c_im=v_ssm_c_im, v_ssm_d=v_ssm_d, v_ssm_w_glu=v_ssm_w_glu, v_kv_w=v_kv_w, v_attn_w_q=v_attn_w_q, v_attn_w_o=v_attn_w_o, v_ffn_w_up=v_ffn_w_up, v_ffn_conv_w=v_ffn_conv_w, v_ffn_conv_b=v_ffn_conv_b, v_ffn_w_down=v_ffn_w_down)
    weights = {n: given[n] for n in TWIN_WEIGHTS}
    shared = {n: given[n] for n in SHARED_INPUTS}
    per_example = {n: given[n] for n in ['x']}
    grad_fn = _jax.value_and_grad(_loss, argnums=(0, 1))

    def one_microbatch(ex, loss_target):
        ex = dict(ex)
        diff = ex.pop(TWIN_DIFF_INPUT)
        return grad_fn(weights, diff, {**shared, **ex}, loss_target)

    if N_MICROBATCH == 1:
        loss, (grad_w, grad_x) = one_microbatch(per_example, given["loss_target"])
    else:
        def body(carry, xs):
            loss_sum, grad_sum = carry
            l_k, (gw_k, gx_k) = one_microbatch(xs[0], xs[1])
            with _jax.named_scope("update"):
                return (loss_sum + l_k, _jax.tree.map(_jnp.add, grad_sum, gw_k)), gx_k

        init = (_jnp.zeros((), _jnp.float32), _jax.tree.map(_jnp.zeros_like, weights))
        (loss, grad_w), grad_x = _jax.lax.scan(body, init, (per_example, given["loss_target"]))
    with _jax.named_scope("update"):
        delta_w, new_m, new_v = {}, {}, {}
        for n in TWIN_WEIGHTS:
            delta_w[n], new_m[n], new_v[n] = _adamw(weights[n], grad_w[n], given["m_" + n], given["v_" + n])
    return (loss, grad_x, *[grad_w[n] for n in TWIN_WEIGHTS], *[delta_w[n] for n in TWIN_WEIGHTS],
            *[new_m[n] for n in TWIN_WEIGHTS], *[new_v[n] for n in TWIN_WEIGHTS])
```

```python
import functools
import math

import jax
import jax.numpy as jnp
from jax import lax
from jax.experimental import pallas as pl
from jax.experimental.pallas import tpu as pltpu

F32 = jnp.float32
BF16 = jnp.bfloat16

EPS = 1e-6
HEAD_DIM = 64
CHUNK = 16
N_DEV = 8
ADAM_LR = 0.001
ADAM_B1 = 0.9
ADAM_B2 = 0.999
ADAM_EPS = 1e-08
ADAM_WD = 0.01
ADAM_STEP = 10
VMEM_LIMIT = 48 * 1024 * 1024
LANES = 128
ATT_TQ = 128
ATT_TK = 256

W_NAMES = ['norm_mix', 'norm_ffn', 'norm_kv', 'norm_final', 'ssm_w_in', 'ssm_a_re', 'ssm_a_im', 'ssm_log_dt',
           'ssm_b_re', 'ssm_b_im', 'ssm_c_re', 'ssm_c_im', 'ssm_d', 'ssm_w_glu', 'kv_w', 'attn_w_q', 'attn_w_o',
           'ffn_w_up', 'ffn_conv_w', 'ffn_conv_b', 'ffn_w_down']
BIG = ['ssm_w_in', 'ssm_w_glu', 'kv_w', 'attn_w_q', 'attn_w_o', 'ffn_w_up', 'ffn_w_down', 'ffn_conv_w']
SMALL = [n for n in W_NAMES if n not in BIG]
CONV_ROWS = 32


def _pcall(body, **kw):
    return pl.pallas_call(body, **kw)


def _params(sem=None):
    if sem is None:
        return pltpu.CompilerParams(vmem_limit_bytes=VMEM_LIMIT)
    return pltpu.CompilerParams(dimension_semantics=sem, vmem_limit_bytes=VMEM_LIMIT)


def _tile(n, pref, mult=8):
    best = None
    for t in range(mult, min(n, pref) + 1, mult):
        if n % t == 0:
            best = t
    return n if best is None else best


def _dot_tn(a, b):
    return lax.dot_general(a, b, (((0,), (0,)), ((), ())), preferred_element_type=F32)


def _mm(name, a, b, mode, out_dtype=F32, tm=512, tn=512, tk=512):
    if mode == 'nn':
        (M, K), (K2, N) = a.shape, b.shape
    elif mode == 'nt':
        (M, K), (N, K2) = a.shape, b.shape
    else:
        (K, M), (K2, N) = a.shape, b.shape
    assert K == K2, (name, a.shape, b.shape)
    tm, tn, tk = _tile(M, tm, LANES), _tile(N, tn, LANES), _tile(K, tk, LANES)
    nk = K // tk
    if mode == 'tn':
        a_spec = pl.BlockSpec((tk, tm), lambda i, j, k: (k, i))
    else:
        a_spec = pl.BlockSpec((tm, tk), lambda i, j, k: (i, k))
    if mode == 'nt':
        b_spec = pl.BlockSpec((tn, tk), lambda i, j, k: (j, k))
    else:
        b_spec = pl.BlockSpec((tk, tn), lambda i, j, k: (k, j))
    dn = {'nn': ((1,), (0,)), 'nt': ((1,), (1,)), 'tn': ((0,), (0,))}[mode]

    def body(a_ref, b_ref, o_ref, acc_ref):
        k = pl.program_id(2)
        part = lax.dot_general(a_ref[...].astype(BF16), b_ref[...].astype(BF16), (dn, ((), ())),
                               preferred_element_type=F32)

        @pl.when(k == 0)
        def _():
            acc_ref[...] = part

        @pl.when(k > 0)
        def _():
            acc_ref[...] += part

        @pl.when(k == nk - 1)
        def _():
            o_ref[...] = acc_ref[...].astype(o_ref.dtype)

    return _pcall(
        body, name=name, grid=(M // tm, N // tn, nk),
        in_specs=[a_spec, b_spec],
        out_specs=pl.BlockSpec((tm, tn), lambda i, j, k: (i, j)),
        out_shape=jax.ShapeDtypeStruct((M, N), out_dtype),
        scratch_shapes=[pltpu.VMEM((tm, tn), F32)],
        compiler_params=_params(("parallel", "parallel", "arbitrary")),
    )(a, b)


def _bmm(name, terms, out_dtype, gb=8, post=None, n_out=1):
    G = terms[0][0].shape[0]
    gb = _tile(G, gb, 1)
    dns = {'nn': ((1,), (0,)), 'nt': ((1,), (1,)), 'tn': ((0,), (0,))}

    def oshape(a, b, mode):
        m = a.shape[2] if mode == 'tn' else a.shape[1]
        n = b.shape[1] if mode == 'nt' else b.shape[2]
        return m, n

    m, n = oshape(*terms[0])
    out_dtypes = out_dtype if isinstance(out_dtype, (tuple, list)) else (out_dtype,)
    n_in = 2 * len(terms)

    def body(*refs):
        ins, outs = refs[:n_in], refs[n_in:]
        for gi in range(gb):
            acc = None
            for t, (_, _, mode) in enumerate(terms):
                part = lax.dot_general(ins[2 * t][gi].astype(BF16), ins[2 * t + 1][gi].astype(BF16),
                                       (dns[mode], ((), ())), preferred_element_type=F32)
                acc = part if acc is None else acc + part
            vals = (acc,) if post is None else post(acc)
            for o_ref, v in zip(outs, vals):
                o_ref[gi] = v.astype(o_ref.dtype)

    in_specs, args = [], []
    for a, b, _ in terms:
        in_specs += [pl.BlockSpec((gb,) + a.shape[1:], lambda g: (g, 0, 0)),
                     pl.BlockSpec((gb,) + b.shape[1:], lambda g: (g, 0, 0))]
        args += [a, b]
    res = _pcall(
        body, name=name, grid=(G // gb,), in_specs=in_specs,
        out_specs=[pl.BlockSpec((gb, m, n), lambda g: (g, 0, 0)) for _ in out_dtypes],
        out_shape=[jax.ShapeDtypeStruct((G, m, n), dt) for dt in out_dtypes],
        compiler_params=_params(("parallel",)),
    )(*args)
    return res[0] if len(out_dtypes) == 1 else res


def _rowwise(name, fn, n_steps, ins, outs, n_acc=0):
    n_in, n_out = len(ins), len(outs)

    def body(*refs):
        i = pl.program_id(0)
        vals = fn(i, *[r[...] for r in refs[:n_in]])
        o_refs = refs[n_in:]
        for j in range(n_out - n_acc):
            o_refs[j][...] = vals[j].astype(o_refs[j].dtype)
        if n_acc:
            @pl.when(i == 0)
            def _():
                for j in range(n_out - n_acc, n_out):
                    o_refs[j][...] = vals[j].astype(o_refs[j].dtype)

            @pl.when(i > 0)
            def _():
                for j in range(n_out - n_acc, n_out):
                    o_refs[j][...] += vals[j].astype(o_refs[j].dtype)

    res = _pcall(
        body, name=name, grid=(n_steps,),
        in_specs=[pl.BlockSpec(blk, im) for _, blk, im in ins],
        out_specs=[pl.BlockSpec(blk, im) for _, _, blk, im in outs],
        out_shape=[jax.ShapeDtypeStruct(s, d) for s, d, _, _ in outs],
        compiler_params=_params(("arbitrary",)),
    )(*[a for a, _, _ in ins])
    return res


def _rows(a, tm):
    return (a, (tm, a.shape[1]), lambda i: (i, 0))


def _whole(a):
    nd = a.ndim
    return (a, a.shape, lambda i: (0,) * nd)


def _orows(L, n, dtype, tm):
    return ((L, n), dtype, (tm, n), lambda i: (i, 0))


def _oacc(r, n):
    return ((r, n), F32, (r, n), lambda i: (0, 0))


def _rms_fwd(name, x, g, tm=512):
    L, D = x.shape
    tm = _tile(L, tm)

    def fn(i, xb, gb):
        r = lax.rsqrt(jnp.mean(xb * xb, axis=-1, keepdims=True) + EPS)
        return (xb * r * gb,)

    return _rowwise(name, fn, L // tm, [_rows(x, tm), _whole(g)], [_orows(L, D, BF16, tm)])[0]


def _rms_bwd(name, x, dres, branches, tm=256):
    L, D = x.shape
    tm = _tile(L, tm)
    nb = len(branches)

    def fn(i, xb, db, *rest):
        r = lax.rsqrt(jnp.mean(xb * xb, axis=-1, keepdims=True) + EPS)
        xh = xb * r
        dx = db
        dgs = []
        for b in range(nb):
            gb, dyb = rest[2 * b], rest[2 * b + 1].astype(F32)
            dxh = dyb * gb
            dx = dx + r * (dxh - xh * jnp.mean(dxh * xh, axis=-1, keepdims=True))
            dgs.append(jnp.sum(dyb * xh, axis=0, keepdims=True))
        return (dx, *dgs)

    ins = [_rows(x, tm), _rows(dres, tm)]
    for g, dy in branches:
        ins += [_whole(g), _rows(dy, tm)]
    outs = [_orows(L, D, F32, tm)] + [_oacc(1, D) for _ in range(nb)]
    return _rowwise(name, fn, L // tm, ins, outs, n_acc=nb)


def _final_loss(name, x, g, target, tm=256):
    L, D = x.shape
    tm = _tile(L, tm)

    def fn(i, xb, gb, tb):
        r = lax.rsqrt(jnp.mean(xb * xb, axis=-1, keepdims=True) + EPS)
        xh = xb * r
        err = xh * gb - tb
        dy = err * (1.0 / D)
        dxh = dy * gb
        dx = r * (dxh - xh * jnp.mean(dxh * xh, axis=-1, keepdims=True))
        dg = jnp.sum(dy * xh, axis=0, keepdims=True)
        per_row = jnp.mean(err * err, axis=-1, keepdims=True)
        loss = 0.5 * jnp.sum(per_row, axis=0, keepdims=True)
        return dx, dg, jnp.broadcast_to(loss, (1, LANES))

    return _rowwise(name, fn, L // tm, [_rows(x, tm), _whole(g), _rows(target, tm)],
                    [_orows(L, D, F32, tm), _oacc(1, D), _oacc(1, LANES)], n_acc=2)


def _glu_fwd(name, x, z, tm=256):
    L, D = x.shape
    tm = _tile(L, tm)

    def fn(i, xb, zb):
        return (xb + zb[:, :D] * jax.nn.sigmoid(zb[:, D:]),)

    return _rowwise(name, fn, L // tm, [_rows(x, tm), _rows(z, tm)], [_orows(L, D, F32, tm)])[0]


def _glu_bwd(name, dx, z, tm=256):
    L, D = dx.shape
    tm = _tile(L, tm)

    def fn(i, db, zb):
        z1, sg = zb[:, :D], jax.nn.sigmoid(zb[:, D:])
        return db * sg, db * z1 * sg * (1.0 - sg)

    d1, d2 = _rowwise(name, fn, L // tm, [_rows(dx, tm), _rows(z, tm)],
                      [_orows(L, D, BF16, tm), _orows(L, D, BF16, tm)])
    return d1, d2


def _add(name, a, b, tm=256):
    L, D = a.shape
    tm = _tile(L, tm)
    return _rowwise(name, lambda i, x, y: (x + y,), L // tm, [_rows(a, tm), _rows(b, tm)],
                    [_orows(L, D, F32, tm)])[0]


def _halo_prev(a, tm):
    return (a, (8, a.shape[1]), lambda i: (jnp.maximum(i * (tm // 8) - 1, 0), 0))


def _halo_next(a, tm):
    last = a.shape[0] // 8 - 1
    return (a, (8, a.shape[1]), lambda i: (jnp.minimum((i + 1) * (tm // 8), last), 0))


def _shift_down(cur, halo, k, first):
    tm = cur.shape[0]
    rolled = pltpu.roll(cur, k, 0)
    tail = pltpu.roll(halo, k, 0)
    tail = jnp.where(first, 0.0, tail)
    row = lax.broadcasted_iota(jnp.int32, (tm, 1), 0)
    head = jnp.concatenate([tail, jnp.zeros((tm - 8, cur.shape[1]), cur.dtype)], axis=0) if tm > 8 else tail
    return jnp.where(row < k, head, rolled)


def _shift_up(cur, halo, k, last):
    tm = cur.shape[0]
    rolled = pltpu.roll(cur, tm - k, 0)
    head = pltpu.roll(halo, 8 - k, 0)
    head = jnp.where(last, 0.0, head)
    row = lax.broadcasted_iota(jnp.int32, (tm, 1), 0)
    tail = jnp.concatenate([jnp.zeros((tm - 8, cur.shape[1]), cur.dtype), head], axis=0) if tm > 8 else head
    return jnp.where(row >= tm - k, tail, rolled)


def _conv_pre(gb, hb, cw, cb, first):
    g1 = _shift_down(gb, hb, 1, first)
    g2 = _shift_down(gb, hb, 2, first)
    return cw[0:1] * g2 + cw[1:2] * g1 + cw[2:3] * gb + cb, g1, g2


def _gate_fwd(name, gu, cw, cb, tm=256):
    L, F2 = gu.shape
    Fh = F2 // 2
    tm = _tile(L, tm)

    def fn(i, gub, halo, cwb, cbb):
        gc, _, _ = _conv_pre(gub[:, :Fh], halo[:, :Fh], cwb, cbb, i == 0)
        return (gc * jax.nn.sigmoid(gc) * gub[:, Fh:],)

    return _rowwise(name, fn, L // tm, [_rows(gu, tm), _halo_prev(gu, tm), _whole(cw), _whole(cb)],
                    [_orows(L, Fh, BF16, tm)])[0]


def _gate_bwd1(name, gu, da, cw, cb, tm=256):
    L, F2 = gu.shape
    Fh = F2 // 2
    tm = _tile(L, tm)

    def fn(i, gub, halo, dab, cwb, cbb):
        gb, ub = gub[:, :Fh], gub[:, Fh:]
        gc, g1, g2 = _conv_pre(gb, halo[:, :Fh], cwb, cbb, i == 0)
        sg = jax.nn.sigmoid(gc)
        dab = dab.astype(F32)
        du = dab * gc * sg
        dgc = dab * ub * sg * (1.0 + gc * (1.0 - sg))
        return (dgc, du, jnp.sum(dgc * g2, axis=0, keepdims=True), jnp.sum(dgc * g1, axis=0, keepdims=True),
                jnp.sum(dgc * gb, axis=0, keepdims=True), jnp.sum(dgc, axis=0, keepdims=True))

    return _rowwise(name, fn, L // tm,
                    [_rows(gu, tm), _halo_prev(gu, tm), _rows(da, tm), _whole(cw), _whole(cb)],
                    [_orows(L, Fh, F32, tm), _orows(L, Fh, BF16, tm)] + [_oacc(1, Fh) for _ in range(4)], n_acc=4)


def _gate_bwd2(name, dgc, du, cw, tm=256):
    L, Fh = dgc.shape
    tm = _tile(L, tm)
    n = L // tm

    def fn(i, db, halo, dub, cwb):
        d1 = _shift_up(db, halo, 1, i == n - 1)
        d2 = _shift_up(db, halo, 2, i == n - 1)
        dg = cwb[2:3] * db + cwb[1:2] * d1 + cwb[0:1] * d2
        return (jnp.concatenate([dg.astype(BF16), dub], axis=1),)

    return _rowwise(name, fn, n, [_rows(dgc, tm), _halo_next(dgc, tm), _rows(du, tm), _whole(cw)],
                    [_orows(L, 2 * Fh, BF16, tm)])[0]


def _s5_build(a_re, a_im, log_dt, b_re, b_im, c_re, c_im, d):
    T = CHUNK
    G, P = a_re.shape
    H = d.shape[1]
    hi = lax.Precision.HIGHEST
    dt = jnp.exp(log_dt)[:, None]
    mag = jnp.exp(a_re * dt)
    ab_re = mag * jnp.cos(a_im * dt)
    ab_im = mag * jnp.sin(a_im * dt)
    den = a_re * a_re + a_im * a_im
    f_re = ((ab_re - 1.0) * a_re + ab_im * a_im) / den
    f_im = (ab_im * a_re - (ab_re - 1.0) * a_im) / den
    bb_re = f_re[..., None] * b_re - f_im[..., None] * b_im
    bb_im = f_re[..., None] * b_im + f_im[..., None] * b_re
    tau = jnp.arange(T + 1, dtype=F32)[:, None, None]
    pmag = jnp.exp(tau * (a_re * dt)[None])
    pang = tau * (a_im * dt)[None]
    pw_re = pmag * jnp.cos(pang)
    pw_im = pmag * jnp.sin(pang)
    cp_re = c_re[None] * pw_re[:, :, None, :] - c_im[None] * pw_im[:, :, None, :]
    cp_im = c_re[None] * pw_im[:, :, None, :] + c_im[None] * pw_re[:, :, None, :]
    kt = (jnp.einsum('tghp,gpk->tghk', cp_re[:T], bb_re, precision=hi)
          - jnp.einsum('tghp,gpk->tghk', cp_im[:T], bb_im, precision=hi))
    kt = kt.at[0].add(d[:, :, None] * jnp.eye(H, dtype=F32)[None])
    diff = jnp.arange(T)[:, None] - jnp.arange(T)[None, :]
    kg = jnp.where((diff >= 0)[:, :, None, None, None], kt[jnp.clip(diff, 0)], 0.0)
    m_mat = kg.transpose(2, 0, 3, 1, 4).reshape(G, T * H, T * H)
    rev = T - 1 - jnp.arange(T)
    pr, pi = pw_re[rev], pw_im[rev]
    n_re = pr[..., None] * bb_re[None] - pi[..., None] * bb_im[None]
    n_im = pr[..., None] * bb_im[None] + pi[..., None] * bb_re[None]
    n_mat = jnp.concatenate([n_re, n_im], axis=2).transpose(1, 2, 0, 3).reshape(G, 2 * P, T * H)
    o_mat = jnp.concatenate([cp_re[1:], -cp_im[1:]], axis=-1).transpose(1, 0, 2, 3).reshape(G, T * H, 2 * P)
    lam1 = jnp.concatenate([pw_re[T], pw_re[T]], axis=-1)[:, None, :]
    lam2 = jnp.concatenate([-pw_im[T], pw_im[T]], axis=-1)[:, None, :]
    return m_mat, n_mat, o_mat, lam1, lam2


def _chunk_scan(name, v, lam1, lam2, reverse, s_fwd=None, gb=16):
    G, nc, W = v.shape
    gb = _tile(G, gb, 1)
    half = W // 2
    ntile = nc // 8

    def body(*refs):
        if reverse:
            v_ref, l1_ref, l2_ref, s_ref, o_ref, d1_ref, d2_ref = refs
        else:
            v_ref, l1_ref, l2_ref, o_ref = refs
        l1 = jnp.broadcast_to(l1_ref[...], (gb, 8, W))
        l2 = jnp.broadcast_to(l2_ref[...], (gb, 8, W))
        if reverse:
            l2 = -l2
        row = lax.broadcasted_iota(jnp.int32, (gb, 8, W), 1)

        def tile_step(n, carry):
            if reverse:
                st, a1, a2 = carry
                base = pl.multiple_of((ntile - 1 - n) * 8, 8)
            else:
                st = carry
                base = pl.multiple_of(n * 8, 8)
            vt = v_ref[:, pl.ds(base, 8), :]
            out = jnp.zeros((gb, 8, W), F32)
            order = range(7, -1, -1) if reverse else range(8)
            for r in order:
                out = jnp.where(row == r, st, out)
                vr = jnp.broadcast_to(vt[:, r:r + 1, :], (gb, 8, W))
                st = l1 * st + l2 * pltpu.roll(st, half, 2) + vr
            o_ref[:, pl.ds(base, 8), :] = out
            if reverse:
                sv = s_ref[:, pl.ds(base, 8), :]
                a1 = a1 + out * sv
                a2 = a2 + out * pltpu.roll(sv, half, 2)
                return st, a1, a2
            return st

        zero = jnp.zeros((gb, 8, W), F32)
        if reverse:
            _, a1, a2 = lax.fori_loop(0, ntile, tile_step, (zero, zero, zero))
            d1_ref[...] = jnp.sum(a1, axis=1, keepdims=True)
            d2_ref[...] = jnp.sum(a2, axis=1, keepdims=True)
        else:
            lax.fori_loop(0, ntile, tile_step, zero)

    big = pl.BlockSpec((gb, nc, W), lambda g: (g, 0, 0))
    vec = pl.BlockSpec((gb, 1, W), lambda g: (g, 0, 0))
    if reverse:
        return _pcall(body, name=name, grid=(G // gb,), in_specs=[big, vec, vec, big],
                      out_specs=[big, vec, vec],
                      out_shape=[jax.ShapeDtypeStruct((G, nc, W), F32), jax.ShapeDtypeStruct((G, 1, W), F32),
                                 jax.ShapeDtypeStruct((G, 1, W), F32)],
                      compiler_params=_params(("parallel",)))(v, lam1, lam2, s_fwd)
    return _pcall(body, name=name, grid=(G // gb,), in_specs=[big, vec, vec], out_specs=big,
                  out_shape=jax.ShapeDtypeStruct((G, nc, W), F32),
                  compiler_params=_params(("parallel",)))(v, lam1, lam2)


_GELU_C = math.sqrt(2.0 / math.pi)


def _gelu(y):
    return 0.5 * y * (1.0 + jnp.tanh(_GELU_C * (y + 0.044715 * y * y * y)))


def _gelu_grad(y):
    t = jnp.tanh(_GELU_C * (y + 0.044715 * y * y * y))
    return 0.5 * (1.0 + t) + 0.5 * y * (1.0 - t * t) * _GELU_C * (1.0 + 3.0 * 0.044715 * y * y)


def _to_groups(a, G, Hg):
    L = a.shape[0]
    return a.reshape(L // CHUNK, CHUNK, G, Hg).transpose(2, 0, 1, 3).reshape(G, L // CHUNK, CHUNK * Hg)


def _from_groups(a, Hg):
    G, nc, _ = a.shape
    return a.reshape(G, nc, CHUNK, Hg).transpose(1, 2, 0, 3).reshape(nc * CHUNK, G * Hg)


def _split_dot(x, u):
    hi = x.astype(BF16)
    lo = (x - hi.astype(F32)).astype(BF16)
    return (jnp.dot(hi, u, preferred_element_type=F32) + jnp.dot(lo, u, preferred_element_type=F32))


def _col_to_row(col, n):
    return jnp.broadcast_to(col, (n, LANES)).T[0:1, :]


def _row_to_col(row, n):
    return jnp.broadcast_to(row, (LANES, n)).T[:, 0:1]


def _attn_fwd(name, q, k, v, tq, tk):
    H, L, dh = q.shape
    nq, nkb = L // tq, L // tk

    def body(q_ref, k_ref, v_ref, o_ref, rc_ref):
        i = pl.program_id(1)
        qb = q_ref[...]
        t_idx = i * tq + lax.broadcasted_iota(jnp.int32, (tq, tk), 0)
        s_loc = lax.broadcasted_iota(jnp.int32, (tq, tk), 1)
        u_suf = (lax.broadcasted_iota(jnp.int32, (tk, tk), 0)
                 > lax.broadcasted_iota(jnp.int32, (tk, tk), 1)).astype(BF16)
        nblk = (i * tq + tq - 1) // tk + 1
        rc_ref[...] = jnp.zeros_like(rc_ref)

        def step(n, carry):
            run, acc = carry
            kb = nblk - 1 - n
            ks = pl.multiple_of(kb * tk, tk)
            kblk = k_ref[pl.ds(ks, tk), :]
            vblk = v_ref[pl.ds(ks, tk), :]
            z = lax.dot_general(qb, kblk, (((1,), (1,)), ((), ())), preferred_element_type=F32)
            causal = (ks + s_loc) < t_idx
            e = jnp.exp(-jnp.abs(z))
            sp = jnp.maximum(z, 0.0) + jnp.log(1.0 + e)
            lom = jnp.where(causal, -sp, 0.0)
            rem = run + _split_dot(lom, u_suf)
            w = jnp.where(causal, jnp.exp(z - sp + rem), 0.0)
            acc = acc + jnp.dot(w.astype(BF16), vblk, preferred_element_type=F32)
            rc_ref[kb] = _col_to_row(run, tq)
            return run + jnp.sum(lom, axis=1, keepdims=True), acc

        _, acc = lax.fori_loop(0, nblk, step, (jnp.zeros((tq, 1), F32), jnp.zeros((tq, dh), F32)))
        o_ref[...] = acc.astype(o_ref.dtype)

    qspec = pl.BlockSpec((None, tq, dh), lambda h, i: (h, i, 0))
    kspec = pl.BlockSpec((None, L, dh), lambda h, i: (h, 0, 0))
    return _pcall(
        body, name=name, grid=(H, nq), in_specs=[qspec, kspec, kspec],
        out_specs=[qspec, pl.BlockSpec((None, None, nkb, 1, tq), lambda h, i: (h, i, 0, 0, 0))],
        out_shape=[jax.ShapeDtypeStruct((H, L, dh), BF16), jax.ShapeDtypeStruct((H, nq, nkb, 1, tq), F32)],
        compiler_params=_params(("parallel", "arbitrary")),
    )(q, k, v)


def _attn_bwd(name, q, k, v, do, rc, tq, tk, scale):
    H, L, dh = q.shape
    nq, nkb = L // tq, L // tk

    def body(q_ref, k_ref, v_ref, do_ref, rc_ref, dq_ref, dk_ref, dv_ref):
        i = pl.program_id(1)

        @pl.when(i == 0)
        def _():
            dk_ref[...] = jnp.zeros_like(dk_ref)
            dv_ref[...] = jnp.zeros_like(dv_ref)

        qb = q_ref[...]
        dob = do_ref[...]
        t_idx = i * tq + lax.broadcasted_iota(jnp.int32, (tq, tk), 0)
        s_loc = lax.broadcasted_iota(jnp.int32, (tq, tk), 1)
        r_io = lax.broadcasted_iota(jnp.int32, (tk, tk), 0)
        c_io = lax.broadcasted_iota(jnp.int32, (tk, tk), 1)
        u_suf = (r_io > c_io).astype(BF16)
        u_pre = (r_io < c_io).astype(BF16)
        nblk = (i * tq + tq - 1) // tk + 1

        def step(kb, carry):
            pre, dq = carry
            ks = pl.multiple_of(kb * tk, tk)
            kblk = k_ref[pl.ds(ks, tk), :]
            vblk = v_ref[pl.ds(ks, tk), :]
            z = lax.dot_general(qb, kblk, (((1,), (1,)), ((), ())), preferred_element_type=F32)
            causal = (ks + s_loc) < t_idx
            e = jnp.exp(-jnp.abs(z))
            sp = jnp.maximum(z, 0.0) + jnp.log(1.0 + e)
            lom = jnp.where(causal, -sp, 0.0)
            rem = _row_to_col(rc_ref[kb], tq) + _split_dot(lom, u_suf)
            w = jnp.where(causal, jnp.exp(z - sp + rem), 0.0)
            dw = lax.dot_general(dob, vblk, (((1,), (1,)), ((), ())), preferred_element_type=F32)
            da = dw * w
            p = pre + _split_dot(da, u_pre)
            inv = 1.0 / (1.0 + e)
            beta = jnp.where(z >= 0.0, inv, e * inv)
            omb = jnp.where(z >= 0.0, e * inv, inv)
            dz = jnp.where(causal, da * omb - beta * p, 0.0)
            dq = dq + jnp.dot(dz.astype(BF16), kblk, preferred_element_type=F32)
            dk_ref[pl.ds(ks, tk), :] += jnp.dot(dz.T.astype(BF16), qb, preferred_element_type=F32)
            dv_ref[pl.ds(ks, tk), :] += jnp.dot(w.T.astype(BF16), dob, preferred_element_type=F32)
            return pre + jnp.sum(da, axis=1, keepdims=True), dq

        _, dq = lax.fori_loop(0, nblk, step, (jnp.zeros((tq, 1), F32), jnp.zeros((tq, dh), F32)))
        dq_ref[...] = (dq * scale).astype(dq_ref.dtype)

    qspec = pl.BlockSpec((None, tq, dh), lambda h, i: (h, i, 0))
    kspec = pl.BlockSpec((None, L, dh), lambda h, i: (h, 0, 0))
    return _pcall(
        body, name=name, grid=(H, nq),
        in_specs=[qspec, kspec, kspec, qspec, pl.BlockSpec((None, None, nkb, 1, tq), lambda h, i: (h, i, 0, 0, 0))],
        out_specs=[qspec, kspec, kspec],
        out_shape=[jax.ShapeDtypeStruct((H, L, dh), BF16), jax.ShapeDtypeStruct((H, L, dh), F32),
                   jax.ShapeDtypeStruct((H, L, dh), F32)],
        compiler_params=_params(("parallel", "arbitrary")),
    )(q, k, v, do, rc)


def _heads(a):
    L, D = a.shape
    return a.reshape(L, D // HEAD_DIM, HEAD_DIM).transpose(1, 0, 2)


def _unheads(a):
    H, L, dh = a.shape
    return a.transpose(1, 0, 2).reshape(L, H * dh)


_MESH = pl.DeviceIdType.MESH
_HBM = pl.BlockSpec(memory_space=pltpu.HBM)


def _all_gather(name, shard):
    R, C = shard.shape

    def body(x_ref, out_ref, send_sems, recv_sems, local_sem):
        x, y, c = lax.axis_index("x"), lax.axis_index("y"), lax.axis_index("c")
        me, sibling = (x, y, c), (x, y, 1 - c)
        chips = [(1 - x, y), (x, 1 - y), (1 - x, 1 - y)]

        def slot(px, py, pc):
            return out_ref.at[4 * px + 2 * py + pc]

        def copy(k, block, to, src=None):
            return pltpu.make_async_remote_copy(
                src_ref=slot(*block) if src is None else src, dst_ref=slot(*block),
                send_sem=send_sems.at[k], recv_sem=recv_sems.at[k], device_id=to, device_id_type=_MESH)

        mine = pltpu.make_async_copy(x_ref, slot(*me), local_sem)
        mine.start()
        first = [copy(0, me, sibling, src=x_ref)]
        first += [copy(1 + j, me, (*chip, c), src=x_ref) for j, chip in enumerate(chips)]
        for cp in first:
            cp.start()
        passed = [copy(4 + j, (*chip, c), sibling) for j, chip in enumerate(chips)]
        for j, chip in enumerate(chips):
            copy(1 + j, (*chip, c), me).wait_recv()
            passed[j].start()
        copy(0, sibling, me).wait_recv()
        for j, chip in enumerate(chips):
            copy(4 + j, (*chip, 1 - c), me).wait_recv()
        for cp in first + passed:
            cp.wait_send()
        mine.wait()

    return _pcall(
        body, name=name, out_shape=jax.ShapeDtypeStruct((N_DEV, R, C), shard.dtype),
        in_specs=[_HBM], out_specs=_HBM,
        scratch_shapes=[pltpu.SemaphoreType.DMA((7,)), pltpu.SemaphoreType.DMA((7,)), pltpu.SemaphoreType.DMA],
    )(shard)


def _sibling_exchange(name, x):
    def body(x_ref, out_ref, send_sem, recv_sem):
        sibling = (lax.axis_index("x"), lax.axis_index("y"), 1 - lax.axis_index("c"))
        cp = pltpu.make_async_remote_copy(src_ref=x_ref, dst_ref=out_ref, send_sem=send_sem, recv_sem=recv_sem,
                                          device_id=sibling, device_id_type=_MESH)
        cp.start()
        cp.wait()

    return _pcall(
        body, name=name, out_shape=jax.ShapeDtypeStruct(x.shape, x.dtype), in_specs=[_HBM], out_specs=_HBM,
        scratch_shapes=[pltpu.SemaphoreType.DMA, pltpu.SemaphoreType.DMA],
    )(x)


def _chip_exchange(name, x):
    _, R, C = x.shape

    def body(x_ref, out_ref, send_sems, recv_sems):
        mx, my, mc = lax.axis_index("x"), lax.axis_index("y"), lax.axis_index("c")
        chips = [(1 - mx, my), (mx, 1 - my), (1 - mx, 1 - my)]
        cps = [pltpu.make_async_remote_copy(src_ref=x_ref.at[2 * px + py], dst_ref=out_ref.at[j],
                                            send_sem=send_sems.at[j], recv_sem=recv_sems.at[j],
                                            device_id=(px, py, mc), device_id_type=_MESH)
               for j, (px, py) in enumerate(chips)]
        for cp in cps:
            cp.start()
        for cp in cps:
            cp.wait()

    return _pcall(
        body, name=name, out_shape=jax.ShapeDtypeStruct((3, R, C), x.dtype), in_specs=[_HBM], out_specs=_HBM,
        scratch_shapes=[pltpu.SemaphoreType.DMA((3,)), pltpu.SemaphoreType.DMA((3,))],
    )(x)


def _pair_sum(name, a, b):
    nb, R, C = a.shape
    tm = _tile(R, 512, 16)

    def fn(i, ab, bb):
        s = ab.astype(F32) + bb.astype(F32)
        return s, s

    spec = lambda arr: (arr, (1, tm, C), lambda i: (i // (R // tm), i % (R // tm), 0))
    out = lambda dt: ((nb, R, C), dt, (1, tm, C), lambda i: (i // (R // tm), i % (R // tm), 0))
    return _rowwise(name, fn, nb * (R // tm), [spec(a), spec(b)], [out(F32), out(BF16)])


def _sum8(name, g):
    n, R, C = g.shape
    tm = _tile(R, 256)

    def fn(i, gb):
        s = gb[0]
        for d in range(1, n):
            s = s + gb[d]
        return (s,)

    return _rowwise(name, fn, R // tm, [(g, (n, tm, C), lambda i: (0, i, 0))], [_orows(R, C, F32, tm)])[0]


def _adamw(name, w, m, v, grads, tm=512):
    R, C = w.shape
    tm = _tile(R, tm, 16)
    ng = len(grads)

    def fn(i, wb, mb, vb, *gs):
        g = gs[0].astype(F32)
        for t in gs[1:]:
            g = g + t.astype(F32)
        mn = ADAM_B1 * mb + (1.0 - ADAM_B1) * g
        vn = ADAM_B2 * vb + (1.0 - ADAM_B2) * (g * g)
        m_hat = mn / (1.0 - ADAM_B1 ** ADAM_STEP)
        v_hat = vn / (1.0 - ADAM_B2 ** ADAM_STEP)
        delta = -ADAM_LR * (m_hat / (jnp.sqrt(v_hat) + ADAM_EPS) + ADAM_WD * wb)
        return g, delta, mn, vn

    ins = [_rows(w, tm), _rows(m, tm), _rows(v, tm)] + [_rows(g, tm) for g in grads]
    return _rowwise(name, fn, R // tm, ins, [_orows(R, C, F32, tm) for _ in range(4)])


def _big_rows(D, Fh):
    return {'ssm_w_in': D // 8, 'ssm_w_glu': D // 4, 'kv_w': D // 4, 'attn_w_q': D // 8, 'attn_w_o': D // 8,
            'ffn_w_up': Fh // 2, 'ffn_w_down': Fh // 4, 'ffn_conv_w': CONV_ROWS}


def _pack_shards(d, D, wire=False):
    parts = []
    for n in BIG:
        a = d[n]
        if n == 'ffn_conv_w':
            flat = a.reshape(-1)
            if wire:
                flat = lax.bitcast_convert_type(flat, BF16).reshape(-1)
            a = jnp.pad(flat, (0, CONV_ROWS * D - flat.shape[0]))
        elif wire:
            a = a.astype(BF16)
        parts.append(a.reshape(-1, D))
    return jnp.concatenate(parts, axis=0)


def _unpack_shards(pack, shapes, D, Fh):
    rows = _big_rows(D, Fh)
    out, off = {}, 0
    for n in BIG:
        seg = pack[off:off + rows[n]]
        off += rows[n]
        size = math.prod(shapes[n])
        out[n] = seg.reshape(-1)[:size].reshape(shapes[n])
    return out


def _full_weights(gath, D, Fh):
    rows = _big_rows(D, Fh)
    seg, off = {}, 0
    for n in BIG:
        seg[n] = gath[:, off:off + rows[n]]
        off += rows[n]
    n8 = N_DEV
    w = {}
    w['ssm_w_in'] = seg['ssm_w_in'].reshape(D, D)
    w['attn_w_q'] = seg['attn_w_q'].reshape(D, D)
    w['attn_w_o'] = seg['attn_w_o'].reshape(D, D)
    w['ssm_w_glu'] = seg['ssm_w_glu'].reshape(n8, D, 2 * D // n8).transpose(1, 0, 2).reshape(D, 2 * D)
    w['kv_w'] = seg['kv_w'].reshape(n8, D, 2 * D // n8).transpose(1, 0, 2).reshape(D, 2 * D)
    w['ffn_w_up'] = seg['ffn_w_up'].reshape(n8, 2, D, 2 * Fh // n8).transpose(1, 2, 0, 3).reshape(2, D, 2 * Fh)
    w['ffn_w_down'] = seg['ffn_w_down'].reshape(n8, 2, Fh // n8, D).transpose(1, 0, 2, 3).reshape(2, Fh, D)
    cw = seg['ffn_conv_w'].reshape(n8, CONV_ROWS * D)[:, :12 * Fh // n8].reshape(n8, 6 * Fh // n8, 2)
    cw = lax.bitcast_convert_type(cw, F32)
    w['ffn_conv_w'] = cw.reshape(n8, 2, 3, Fh // n8).transpose(1, 2, 0, 3).reshape(2, 3, Fh)
    return w


def _grad_pack(g, D, Fh):
    n8 = N_DEV
    parts = {
        'ssm_w_in': g['ssm_w_in'].reshape(n8, D // n8, D),
        'attn_w_q': g['attn_w_q'].reshape(n8, D // n8, D),
        'attn_w_o': g['attn_w_o'].reshape(n8, D // n8, D),
        'ssm_w_glu': g['ssm_w_glu'].reshape(D, n8, 2 * D // n8).transpose(1, 0, 2).reshape(n8, D // 4, D),
        'kv_w': g['kv_w'].reshape(D, n8, 2 * D // n8).transpose(1, 0, 2).reshape(n8, D // 4, D),
        'ffn_w_up': g['ffn_w_up'].reshape(2, D, n8, 2 * Fh // n8).transpose(2, 0, 1, 3).reshape(n8, Fh // 2, D),
        'ffn_w_down': g['ffn_w_down'].reshape(2, n8, Fh // n8, D).transpose(1, 0, 2, 3).reshape(n8, Fh // 4, D),
    }
    cw = g['ffn_conv_w'].reshape(2, 3, n8, Fh // n8).transpose(2, 0, 1, 3).reshape(n8, 6 * Fh // n8)
    parts['ffn_conv_w'] = jnp.pad(cw, ((0, 0), (0, CONV_ROWS * D - cw.shape[1]))).reshape(n8, CONV_ROWS, D)
    return jnp.concatenate([parts[n] for n in BIG], axis=1)


def _small_layout(shapes, D):
    lay, off = {}, 0
    for n in SMALL:
        r = -(-math.prod(shapes[n]) // D)
        lay[n] = (off, r)
        off += r
    lay['loss'] = (off, 1)
    off += 1
    return lay, -(-off // 8) * 8


def _pack_small(d, lay, total, D):
    parts = []
    for n in SMALL + ['loss']:
        if n not in d:
            parts.append(jnp.zeros((lay[n][1], D), F32))
            continue
        flat = d[n].reshape(-1).astype(F32)
        parts.append(jnp.pad(flat, (0, lay[n][1] * D - flat.shape[0])).reshape(lay[n][1], D))
    used = sum(lay[n][1] for n in SMALL + ['loss'])
    if total > used:
        parts.append(jnp.zeros((total - used, D), F32))
    return jnp.concatenate(parts, axis=0)


def _unpack_small(pack, lay, shapes, D):
    out = {}
    for n in SMALL:
        off, r = lay[n]
        out[n] = pack[off:off + r].reshape(-1)[:math.prod(shapes[n])].reshape(shapes[n])
    return out


def _ffn_fwd(tag, x, g_norm, w_up, conv_w, conv_b, w_down):
    h = _rms_fwd(f"{tag}_norm", x, g_norm)
    gu = _mm(f"{tag}_up", h, w_up, 'nn', F32)
    a = _gate_fwd(f"{tag}_gate", gu, conv_w, conv_b)
    y = _mm(f"{tag}_down", a, w_down, 'nn', F32)
    return _add(f"{tag}_res", x, y), (h, gu, a)


def _ffn_bwd(tag, dres, x, saved, g_norm, w_up, conv_w, conv_b, w_down):
    h, gu, a = saved
    da = _mm(f"{tag}_dgate", dres, w_down, 'nt', F32)
    d_w_down = _mm(f"{tag}_dwdown", a, dres, 'tn', F32)
    dgc, du, dw0, dw1, dw2, dcb = _gate_bwd1(f"{tag}_gate_b1", gu, da, conv_w, conv_b)
    dgu = _gate_bwd2(f"{tag}_gate_b2", dgc, du, conv_w)
    d_w_up = _mm(f"{tag}_dwup", h, dgu, 'tn', F32)
    dh = _mm(f"{tag}_dh", dgu, w_up, 'nt', F32)
    dres, dg = _rms_bwd(f"{tag}_norm_b", x, dres, [(g_norm, dh)])
    return dres, dg, d_w_up, jnp.concatenate([dw0, dw1, dw2], axis=0), dcb[0], d_w_down


def kernel(x, norm_mix, norm_ffn, norm_kv, norm_final, ssm_w_in, ssm_a_re, ssm_a_im, ssm_log_dt, ssm_b_re, ssm_b_im, ssm_c_re, ssm_c_im, ssm_d, ssm_w_glu, kv_w, attn_w_q, attn_w_o, ffn_w_up, ffn_conv_w, ffn_conv_b, ffn_w_down, loss_target, m_norm_mix, m_norm_ffn, m_norm_kv, m_norm_final, m_ssm_w_in, m_ssm_a_re, m_ssm_a_im, m_ssm_log_dt, m_ssm_b_re, m_ssm_b_im, m_ssm_c_re, m_ssm_c_im, m_ssm_d, m_ssm_w_glu, m_kv_w, m_attn_w_q, m_attn_w_o, m_ffn_w_up, m_ffn_conv_w, m_ffn_conv_b, m_ffn_w_down, v_norm_mix, v_norm_ffn, v_norm_kv, v_norm_final, v_ssm_w_in, v_ssm_a_re, v_ssm_a_im, v_ssm_log_dt, v_ssm_b_re, v_ssm_b_im, v_ssm_c_re, v_ssm_c_im, v_ssm_d, v_ssm_w_glu, v_kv_w, v_attn_w_q, v_attn_w_o, v_ffn_w_up, v_ffn_conv_w, v_ffn_conv_b, v_ffn_w_down):
    wts = dict(zip(W_NAMES, (norm_mix, norm_ffn, norm_kv, norm_final, ssm_w_in, ssm_a_re, ssm_a_im, ssm_log_dt,
                             ssm_b_re, ssm_b_im, ssm_c_re, ssm_c_im, ssm_d, ssm_w_glu, kv_w, attn_w_q, attn_w_o,
                             ffn_w_up, ffn_conv_w, ffn_conv_b, ffn_w_down)))
    mom = dict(zip(W_NAMES, (m_norm_mix, m_norm_ffn, m_norm_kv, m_norm_final, m_ssm_w_in, m_ssm_a_re, m_ssm_a_im,
                             m_ssm_log_dt, m_ssm_b_re, m_ssm_b_im, m_ssm_c_re, m_ssm_c_im, m_ssm_d, m_ssm_w_glu,
                             m_kv_w, m_attn_w_q, m_attn_w_o, m_ffn_w_up, m_ffn_conv_w, m_ffn_conv_b, m_ffn_w_down)))
    vel = dict(zip(W_NAMES, (v_norm_mix, v_norm_ffn, v_norm_kv, v_norm_final, v_ssm_w_in, v_ssm_a_re, v_ssm_a_im,
                             v_ssm_log_dt, v_ssm_b_re, v_ssm_b_im, v_ssm_c_re, v_ssm_c_im, v_ssm_d, v_ssm_w_glu,
                             v_kv_w, v_attn_w_q, v_attn_w_o, v_ffn_w_up, v_ffn_conv_w, v_ffn_conv_b, v_ffn_w_down)))
    shapes = {n: wts[n].shape for n in W_NAMES}
    _, L, D = x.shape
    Fh = ffn_conv_b.shape[1]
    G, P = ssm_a_re.shape[1], ssm_a_re.shape[2]
    Hg = ssm_d.shape[2]
    x0 = x[0]
    target = loss_target[0]
    scale = HEAD_DIM ** -0.5

    gath = _all_gather("gather_weights", _pack_shards(wts, D, wire=True))
    w = _full_weights(gath, D, Fh)
    nm = [norm_mix[l:l + 1] for l in range(2)]
    nf = [norm_ffn[l:l + 1] for l in range(2)]
    nkv = norm_kv[None]
    nfin = norm_final[None]
    cb = [ffn_conv_b[l:l + 1] for l in range(2)]
    cwt = [w['ffn_conv_w'][l] for l in range(2)]

    s5p = (ssm_a_re[0], ssm_a_im[0], ssm_log_dt[0], ssm_b_re[0], ssm_b_im[0], ssm_c_re[0], ssm_c_im[0], ssm_d[0])
    (m_mat, n_mat, o_mat, lam1, lam2), s5_vjp = jax.vjp(_s5_build, *s5p)
    m_bf, n_bf, o_bf = m_mat.astype(BF16), n_mat.astype(BF16), o_mat.astype(BF16)

    h0 = _rms_fwd("l0_norm", x0, nm[0])
    u = _mm("l0_win", h0, w['ssm_w_in'], 'nn', BF16)
    ug = _to_groups(u, G, Hg)
    vloc = _bmm("s5_local_state", [(ug, n_bf, 'nt')], F32)
    st = _chunk_scan("s5_scan", vloc, lam1, lam2, reverse=False)
    yraw, yg_g = _bmm("s5_out", [(ug, m_bf, 'nt'), (st, o_bf, 'nt')], (F32, BF16),
                      post=lambda acc: (acc, _gelu(acc)))
    yg = _from_groups(yg_g, Hg)
    z = _mm("l0_wglu", yg, w['ssm_w_glu'], 'nn', F32)
    x1 = _glu_fwd("l0_glu", x0, z)
    x2, ffn0 = _ffn_fwd("f0", x1, nf[0], w['ffn_w_up'][0], cwt[0], cb[0], w['ffn_w_down'][0])

    hkv = _rms_fwd("kv_norm", x2, nkv)
    kvp = _mm("kv_proj", hkv, w['kv_w'], 'nn', BF16)
    h2 = _rms_fwd("l1_norm", x2, nm[1])
    qn = _rowwise("l1_qscale", lambda i, a: (a * scale,), L // _tile(L, 512),
                  [_rows(_mm("l1_wq", h2, w['attn_w_q'], 'nn', F32), _tile(L, 512))],
                  [_orows(L, D, BF16, _tile(L, 512))])[0]
    qh, kh, vh = _heads(qn), _heads(kvp[:, :D]), _heads(kvp[:, D:])
    oh, rc = _attn_fwd("attn_fwd", qh, kh, vh, ATT_TQ, ATT_TK)
    o = _unheads(oh)
    x3 = _add("l1_res", x2, _mm("l1_wo", o, w['attn_w_o'], 'nn', F32))
    x4, ffn1 = _ffn_fwd("f1", x3, nf[1], w['ffn_w_up'][1], cwt[1], cb[1], w['ffn_w_down'][1])

    dres, dg_final, loss_part = _final_loss("loss_head", x4, nfin, target)

    gw = {}
    dres, dg_nf1, gup1, gcw1, gcb1, gdn1 = _ffn_bwd("f1", dres, x3, ffn1, nf[1], w['ffn_w_up'][1], cwt[1], cb[1],
                                                    w['ffn_w_down'][1])
    do = _mm("l1_do", dres, w['attn_w_o'], 'nt', BF16)
    gw['attn_w_o'] = _mm("l1_dwo", o, dres, 'tn', F32)
    dqh, dkh, dvh = _attn_bwd("attn_bwd", qh, kh, vh, _heads(do), rc, ATT_TQ, ATT_TK, scale)
    dq = _unheads(dqh)
    dkv = jnp.concatenate([_unheads(dkh), _unheads(dvh)], axis=1).astype(BF16)
    gw['attn_w_q'] = _mm("l1_dwq", h2, dq, 'tn', F32)
    dh2 = _mm("l1_dh", dq, w['attn_w_q'], 'nt', F32)
    gw['kv_w'] = _mm("kv_dw", hkv, dkv, 'tn', F32)
    dhkv = _mm("kv_dh", dkv, w['kv_w'], 'nt', F32)
    dres, dg_nm1, dg_nkv = _rms_bwd("l1_norm_b", x2, dres, [(nm[1], dh2), (nkv, dhkv)])
    dres, dg_nf0, gup0, gcw0, gcb0, gdn0 = _ffn_bwd("f0", dres, x1, ffn0, nf[0], w['ffn_w_up'][0], cwt[0], cb[0],
                                                    w['ffn_w_down'][0])
    dz1, dz2 = _glu_bwd("l0_glu_b", dres, z)
    dz = jnp.concatenate([dz1, dz2], axis=1)
    gw['ssm_w_glu'] = _mm("l0_dwglu", yg, dz, 'tn', F32)
    dyg = _mm("l0_dyg", dz, w['ssm_w_glu'], 'nt', F32)
    dyg_g = _to_groups(dyg, G, Hg)
    gshape = yraw.shape
    dy = _rowwise("s5_gelu_b", lambda i, a, b: (a * _gelu_grad(b),), G,
                  [(dyg_g, (1,) + gshape[1:], lambda i: (i, 0, 0)), (yraw, (1,) + gshape[1:], lambda i: (i, 0, 0))],
                  [(gshape, BF16, (1,) + gshape[1:], lambda i: (i, 0, 0))])[0]
    ds = _bmm("s5_dstate", [(dy, o_bf, 'nn')], F32)
    dv_loc, dlam1, dlam2 = _chunk_scan("s5_scan_b", ds, lam1, lam2, reverse=True, s_fwd=st)
    du_g = _bmm("s5_du", [(dy, m_bf, 'nn'), (dv_loc, n_bf, 'nn')], BF16)
    d_m = _bmm("s5_dm", [(dy, ug, 'tn')], F32)
    d_o = _bmm("s5_do", [(dy, st, 'tn')], F32)
    d_n = _bmm("s5_dn", [(dv_loc, ug, 'tn')], F32)
    s5g = s5_vjp((d_m, d_n, d_o, dlam1, dlam2))
    du = _from_groups(du_g, Hg)
    gw['ssm_w_in'] = _mm("l0_dwin", h0, du, 'tn', F32)
    dh0 = _mm("l0_dh", du, w['ssm_w_in'], 'nt', F32)
    grad_x, dg_nm0 = _rms_bwd("l0_norm_b", x0, dres, [(nm[0], dh0)])

    gw['ffn_w_up'] = jnp.stack([gup0, gup1])
    gw['ffn_w_down'] = jnp.stack([gdn0, gdn1])
    gw['ffn_conv_w'] = jnp.stack([gcw0, gcw1])

    small_g = {
        'norm_mix': jnp.concatenate([dg_nm0, dg_nm1], axis=0), 'norm_ffn': jnp.concatenate([dg_nf0, dg_nf1], axis=0),
        'norm_kv': dg_nkv, 'norm_final': dg_final, 'ffn_conv_b': jnp.stack([gcb0, gcb1]),
        'ssm_a_re': s5g[0], 'ssm_a_im': s5g[1], 'ssm_log_dt': s5g[2], 'ssm_b_re': s5g[3], 'ssm_b_im': s5g[4],
        'ssm_c_re': s5g[5], 'ssm_c_im': s5g[6], 'ssm_d': s5g[7], 'loss': loss_part[0, 0:1],
    }
    lay, rs = _small_layout(shapes, D)
    small_sum = _sum8("small_sum", _all_gather("gather_small", _pack_small(small_g, lay, rs, D)))
    loss = small_sum[lay['loss'][0], 0]
    sg, sdelta, sm, sv = _adamw("adamw_small", _pack_small(wts, lay, rs, D), _pack_small(mom, lay, rs, D),
                                _pack_small(vel, lay, rs, D), [small_sum])
    small_out = [_unpack_small(t, lay, shapes, D) for t in (sg, sdelta, sm, sv)]

    gp = _grad_pack(gw, D, Fh).astype(BF16)
    rows = gp.shape[1]
    gp = gp.reshape(4, 2, rows, D)
    c = lax.axis_index("c")
    chip = 2 * lax.axis_index("x") + lax.axis_index("y")
    mine = lax.dynamic_index_in_dim(gp, c, axis=1, keepdims=False)
    theirs = lax.dynamic_index_in_dim(gp, 1 - c, axis=1, keepdims=False)
    got = _sibling_exchange("rs_sibling", theirs)
    chip_f32, chip_bf = _pair_sum("rs_pair_sum", mine, got)
    from_chips = _chip_exchange("rs_chips", chip_bf)
    own = lax.dynamic_index_in_dim(chip_f32, chip, axis=0, keepdims=False)
    bg, bdelta, bm, bv = _adamw("adamw_big", _pack_shards(wts, D), _pack_shards(mom, D), _pack_shards(vel, D),
                                [own, from_chips[0], from_chips[1], from_chips[2]])
    big_out = [_unpack_shards(t, shapes, D, Fh) for t in (bg, bdelta, bm, bv)]

    outs = [loss, grad_x[None]]
    for k in range(4):
        for n in W_NAMES:
            outs.append(big_out[k][n] if n in BIG else small_out[k][n])
    return tuple(outs)
```

```python
import functools
import math

import jax
import jax.numpy as jnp
from jax import lax
from jax.experimental import pallas as pl
from jax.experimental.pallas import tpu as pltpu

F32 = jnp.float32
BF16 = jnp.bfloat16

EPS = 1e-6
HEAD_DIM = 64
CHUNK = 16
N_DEV = 8
ADAM_LR = 0.001
ADAM_B1 = 0.9
ADAM_B2 = 0.999
ADAM_EPS = 1e-08
ADAM_WD = 0.01
ADAM_STEP = 10
VMEM_LIMIT = 48 * 1024 * 1024
LANES = 128
ATT_TQ = 128
ATT_TK = 256
ATT_HB_FWD = 4
ATT_HB_BWD = 2

W_NAMES = ['norm_mix', 'norm_ffn', 'norm_kv', 'norm_final', 'ssm_w_in', 'ssm_a_re', 'ssm_a_im', 'ssm_log_dt',
           'ssm_b_re', 'ssm_b_im', 'ssm_c_re', 'ssm_c_im', 'ssm_d', 'ssm_w_glu', 'kv_w', 'attn_w_q', 'attn_w_o',
           'ffn_w_up', 'ffn_conv_w', 'ffn_conv_b', 'ffn_w_down']
BIG = ['ssm_w_in', 'ssm_w_glu', 'kv_w', 'attn_w_q', 'attn_w_o', 'ffn_w_up', 'ffn_w_down', 'ffn_conv_w']
SMALL = [n for n in W_NAMES if n not in BIG]
CONV_ROWS = 32


def _pcall(body, **kw):
    return pl.pallas_call(body, **kw)


def _params(sem=None):
    if sem is None:
        return pltpu.CompilerParams(vmem_limit_bytes=VMEM_LIMIT)
    return pltpu.CompilerParams(dimension_semantics=sem, vmem_limit_bytes=VMEM_LIMIT)


def _tile(n, pref, mult=8):
    best = None
    for t in range(mult, min(n, pref) + 1, mult):
        if n % t == 0:
            best = t
    return n if best is None else best


def _dot_tn(a, b):
    return lax.dot_general(a, b, (((0,), (0,)), ((), ())), preferred_element_type=F32)


def _mm(name, a, b, mode, out_dtype=F32, tm=512, tn=512, scale=None, resid=None):
    if mode == 'nn':
        (M, K), (K2, N) = a.shape, b.shape
    elif mode == 'nt':
        (M, K), (N, K2) = a.shape, b.shape
    else:
        (K, M), (K2, N) = a.shape, b.shape
    assert K == K2, (name, a.shape, b.shape)
    tm, tn = _tile(M, tm, LANES), _tile(N, tn, LANES)
    if mode == 'tn':
        a_spec = pl.BlockSpec((K, tm), lambda i, j: (0, i))
    else:
        a_spec = pl.BlockSpec((tm, K), lambda i, j: (i, 0))
    if mode == 'nt':
        b_spec = pl.BlockSpec((tn, K), lambda i, j: (j, 0))
    else:
        b_spec = pl.BlockSpec((K, tn), lambda i, j: (0, j))
    o_spec = pl.BlockSpec((tm, tn), lambda i, j: (i, j))
    dn = {'nn': ((1,), (0,)), 'nt': ((1,), (1,)), 'tn': ((0,), (0,))}[mode]

    def body(*refs):
        a_ref, b_ref, o_ref = refs[0], refs[1], refs[-1]
        acc = lax.dot_general(a_ref[...].astype(BF16), b_ref[...].astype(BF16), (dn, ((), ())),
                              preferred_element_type=F32)
        if scale is not None:
            acc = acc * scale
        if resid is not None:
            acc = acc + refs[2][...]
        o_ref[...] = acc.astype(o_ref.dtype)

    return _pcall(
        body, name=name, grid=(M // tm, N // tn),
        in_specs=[a_spec, b_spec] + ([o_spec] if resid is not None else []),
        out_specs=o_spec,
        out_shape=jax.ShapeDtypeStruct((M, N), out_dtype),
        compiler_params=_params(("parallel", "parallel")),
    )(*([a, b] + ([resid] if resid is not None else [])))


def _bmm(name, terms, out_dtype, gb=8, post=None, n_out=1):
    G = terms[0][0].shape[0]
    gb = _tile(G, gb, 1)
    dns = {'nn': ((1,), (0,)), 'nt': ((1,), (1,)), 'tn': ((0,), (0,))}

    def oshape(a, b, mode):
        m = a.shape[2] if mode == 'tn' else a.shape[1]
        n = b.shape[1] if mode == 'nt' else b.shape[2]
        return m, n

    m, n = oshape(*terms[0])
    out_dtypes = out_dtype if isinstance(out_dtype, (tuple, list)) else (out_dtype,)
    n_in = 2 * len(terms)

    def body(*refs):
        ins, outs = refs[:n_in], refs[n_in:]
        for gi in range(gb):
            acc = None
            for t, (_, _, mode) in enumerate(terms):
                part = lax.dot_general(ins[2 * t][gi].astype(BF16), ins[2 * t + 1][gi].astype(BF16),
                                       (dns[mode], ((), ())), preferred_element_type=F32)
                acc = part if acc is None else acc + part
            vals = (acc,) if post is None else post(acc)
            for o_ref, v in zip(outs, vals):
                o_ref[gi] = v.astype(o_ref.dtype)

    in_specs, args = [], []
    for a, b, _ in terms:
        in_specs += [pl.BlockSpec((gb,) + a.shape[1:], lambda g: (g, 0, 0)),
                     pl.BlockSpec((gb,) + b.shape[1:], lambda g: (g, 0, 0))]
        args += [a, b]
    res = _pcall(
        body, name=name, grid=(G // gb,), in_specs=in_specs,
        out_specs=[pl.BlockSpec((gb, m, n), lambda g: (g, 0, 0)) for _ in out_dtypes],
        out_shape=[jax.ShapeDtypeStruct((G, m, n), dt) for dt in out_dtypes],
        compiler_params=_params(("parallel",)),
    )(*args)
    return res[0] if len(out_dtypes) == 1 else res


def _rowwise(name, fn, n_steps, ins, outs, n_acc=0):
    n_in, n_out = len(ins), len(outs)

    def body(*refs):
        i = pl.program_id(0)
        vals = fn(i, *[r[...] for r in refs[:n_in]])
        o_refs = refs[n_in:]
        for j in range(n_out - n_acc):
            o_refs[j][...] = vals[j].astype(o_refs[j].dtype)
        if n_acc:
            @pl.when(i == 0)
            def _():
                for j in range(n_out - n_acc, n_out):
                    o_refs[j][...] = vals[j].astype(o_refs[j].dtype)

            @pl.when(i > 0)
            def _():
                for j in range(n_out - n_acc, n_out):
                    o_refs[j][...] += vals[j].astype(o_refs[j].dtype)

    res = _pcall(
        body, name=name, grid=(n_steps,),
        in_specs=[pl.BlockSpec(blk, im) for _, blk, im in ins],
        out_specs=[pl.BlockSpec(blk, im) for _, _, blk, im in outs],
        out_shape=[jax.ShapeDtypeStruct(s, d) for s, d, _, _ in outs],
        compiler_params=_params(("arbitrary",)),
    )(*[a for a, _, _ in ins])
    return res


def _rows(a, tm):
    return (a, (tm, a.shape[1]), lambda i: (i, 0))


def _whole(a):
    nd = a.ndim
    return (a, a.shape, lambda i: (0,) * nd)


def _orows(L, n, dtype, tm):
    return ((L, n), dtype, (tm, n), lambda i: (i, 0))


def _oacc(r, n):
    return ((r, n), F32, (r, n), lambda i: (0, 0))


def _rms_fwd(name, x, g, tm=512):
    L, D = x.shape
    tm = _tile(L, tm)

    def fn(i, xb, gb):
        r = lax.rsqrt(jnp.mean(xb * xb, axis=-1, keepdims=True) + EPS)
        return (xb * r * gb,)

    return _rowwise(name, fn, L // tm, [_rows(x, tm), _whole(g)], [_orows(L, D, BF16, tm)])[0]


def _rms_bwd(name, x, dres, branches, tm=256):
    L, D = x.shape
    tm = _tile(L, tm)
    nb = len(branches)

    def fn(i, xb, db, *rest):
        r = lax.rsqrt(jnp.mean(xb * xb, axis=-1, keepdims=True) + EPS)
        xh = xb * r
        dx = db
        dgs = []
        for b in range(nb):
            gb, dyb = rest[2 * b], rest[2 * b + 1].astype(F32)
            dxh = dyb * gb
            dx = dx + r * (dxh - xh * jnp.mean(dxh * xh, axis=-1, keepdims=True))
            dgs.append(jnp.sum(dyb * xh, axis=0, keepdims=True))
        return (dx, dx, *dgs)

    ins = [_rows(x, tm), _rows(dres, tm)]
    for g, dy in branches:
        ins += [_whole(g), _rows(dy, tm)]
    outs = [_orows(L, D, F32, tm), _orows(L, D, BF16, tm)] + [_oacc(1, D) for _ in range(nb)]
    return _rowwise(name, fn, L // tm, ins, outs, n_acc=nb)


def _final_loss(name, x, g, target, tm=256):
    L, D = x.shape
    tm = _tile(L, tm)

    def fn(i, xb, gb, tb):
        r = lax.rsqrt(jnp.mean(xb * xb, axis=-1, keepdims=True) + EPS)
        xh = xb * r
        err = xh * gb - tb
        dy = err * (1.0 / D)
        dxh = dy * gb
        dx = r * (dxh - xh * jnp.mean(dxh * xh, axis=-1, keepdims=True))
        dg = jnp.sum(dy * xh, axis=0, keepdims=True)
        per_row = jnp.mean(err * err, axis=-1, keepdims=True)
        loss = 0.5 * jnp.sum(per_row, axis=0, keepdims=True)
        return dx, dx, dg, jnp.broadcast_to(loss, (1, LANES))

    return _rowwise(name, fn, L // tm, [_rows(x, tm), _whole(g), _rows(target, tm)],
                    [_orows(L, D, F32, tm), _orows(L, D, BF16, tm), _oacc(1, D), _oacc(1, LANES)], n_acc=2)


def _glu_fwd(name, x, z, tm=256):
    L, D = x.shape
    tm = _tile(L, tm)

    def fn(i, xb, zb):
        return (xb + zb[:, :D] * jax.nn.sigmoid(zb[:, D:]),)

    return _rowwise(name, fn, L // tm, [_rows(x, tm), _rows(z, tm)], [_orows(L, D, F32, tm)])[0]


def _glu_bwd(name, dx, z, tm=256):
    L, D = dx.shape
    tm = _tile(L, tm)

    def fn(i, db, zb):
        z1, sg = zb[:, :D], jax.nn.sigmoid(zb[:, D:])
        return db * sg, db * z1 * sg * (1.0 - sg)

    d1, d2 = _rowwise(name, fn, L // tm, [_rows(dx, tm), _rows(z, tm)],
                      [_orows(L, D, BF16, tm), _orows(L, D, BF16, tm)])
    return d1, d2


def _add(name, a, b, tm=256):
    L, D = a.shape
    tm = _tile(L, tm)
    return _rowwise(name, lambda i, x, y: (x + y,), L // tm, [_rows(a, tm), _rows(b, tm)],
                    [_orows(L, D, F32, tm)])[0]


def _halo_prev(a, tm):
    return (a, (8, a.shape[1]), lambda i: (jnp.maximum(i * (tm // 8) - 1, 0), 0))


def _halo_next(a, tm):
    last = a.shape[0] // 8 - 1
    return (a, (8, a.shape[1]), lambda i: (jnp.minimum((i + 1) * (tm // 8), last), 0))


def _shift_down(cur, halo, k, first):
    tm = cur.shape[0]
    rolled = pltpu.roll(cur, k, 0)
    tail = pltpu.roll(halo, k, 0)
    tail = jnp.where(first, 0.0, tail)
    row = lax.broadcasted_iota(jnp.int32, (tm, 1), 0)
    head = jnp.concatenate([tail, jnp.zeros((tm - 8, cur.shape[1]), cur.dtype)], axis=0) if tm > 8 else tail
    return jnp.where(row < k, head, rolled)


def _shift_up(cur, halo, k, last):
    tm = cur.shape[0]
    rolled = pltpu.roll(cur, tm - k, 0)
    head = pltpu.roll(halo, 8 - k, 0)
    head = jnp.where(last, 0.0, head)
    row = lax.broadcasted_iota(jnp.int32, (tm, 1), 0)
    tail = jnp.concatenate([jnp.zeros((tm - 8, cur.shape[1]), cur.dtype), head], axis=0) if tm > 8 else head
    return jnp.where(row >= tm - k, tail, rolled)


def _conv_pre(gb, hb, cw, cb, first):
    g1 = _shift_down(gb, hb, 1, first)
    g2 = _shift_down(gb, hb, 2, first)
    return cw[0:1] * g2 + cw[1:2] * g1 + cw[2:3] * gb + cb, g1, g2


def _gate_fwd(name, gu, cw, cb, tm=256):
    L, F2 = gu.shape
    Fh = F2 // 2
    tm = _tile(L, tm)

    def fn(i, gub, halo, cwb, cbb):
        gc, _, _ = _conv_pre(gub[:, :Fh], halo[:, :Fh], cwb, cbb, i == 0)
        return (gc * jax.nn.sigmoid(gc) * gub[:, Fh:],)

    return _rowwise(name, fn, L // tm, [_rows(gu, tm), _halo_prev(gu, tm), _whole(cw), _whole(cb)],
                    [_orows(L, Fh, BF16, tm)])[0]


def _gate_bwd1(name, gu, da, cw, cb, tm=256):
    L, F2 = gu.shape
    Fh = F2 // 2
    tm = _tile(L, tm)

    def fn(i, gub, halo, dab, cwb, cbb):
        gb, ub = gub[:, :Fh], gub[:, Fh:]
        gc, g1, g2 = _conv_pre(gb, halo[:, :Fh], cwb, cbb, i == 0)
        sg = jax.nn.sigmoid(gc)
        dab = dab.astype(F32)
        du = dab * gc * sg
        dgc = dab * ub * sg * (1.0 + gc * (1.0 - sg))
        return (dgc, du, jnp.sum(dgc * g2, axis=0, keepdims=True), jnp.sum(dgc * g1, axis=0, keepdims=True),
                jnp.sum(dgc * gb, axis=0, keepdims=True), jnp.sum(dgc, axis=0, keepdims=True))

    return _rowwise(name, fn, L // tm,
                    [_rows(gu, tm), _halo_prev(gu, tm), _rows(da, tm), _whole(cw), _whole(cb)],
                    [_orows(L, Fh, F32, tm), _orows(L, Fh, BF16, tm)] + [_oacc(1, Fh) for _ in range(4)], n_acc=4)


def _gate_bwd2(name, dgc, du, cw, tm=256):
    L, Fh = dgc.shape
    tm = _tile(L, tm)
    n = L // tm

    def fn(i, db, halo, dub, cwb):
        d1 = _shift_up(db, halo, 1, i == n - 1)
        d2 = _shift_up(db, halo, 2, i == n - 1)
        dg = cwb[2:3] * db + cwb[1:2] * d1 + cwb[0:1] * d2
        return (jnp.concatenate([dg.astype(BF16), dub], axis=1),)

    return _rowwise(name, fn, n, [_rows(dgc, tm), _halo_next(dgc, tm), _rows(du, tm), _whole(cw)],
                    [_orows(L, 2 * Fh, BF16, tm)])[0]


def _s5_build(a_re, a_im, log_dt, b_re, b_im, c_re, c_im, d):
    T = CHUNK
    G, P = a_re.shape
    H = d.shape[1]
    hi = lax.Precision.HIGHEST
    dt = jnp.exp(log_dt)[:, None]
    mag = jnp.exp(a_re * dt)
    ab_re = mag * jnp.cos(a_im * dt)
    ab_im = mag * jnp.sin(a_im * dt)
    den = a_re * a_re + a_im * a_im
    f_re = ((ab_re - 1.0) * a_re + ab_im * a_im) / den
    f_im = (ab_im * a_re - (ab_re - 1.0) * a_im) / den
    bb_re = f_re[..., None] * b_re - f_im[..., None] * b_im
    bb_im = f_re[..., None] * b_im + f_im[..., None] * b_re
    tau = jnp.arange(T + 1, dtype=F32)[:, None, None]
    pmag = jnp.exp(tau * (a_re * dt)[None])
    pang = tau * (a_im * dt)[None]
    pw_re = pmag * jnp.cos(pang)
    pw_im = pmag * jnp.sin(pang)
    cp_re = c_re[None] * pw_re[:, :, None, :] - c_im[None] * pw_im[:, :, None, :]
    cp_im = c_re[None] * pw_im[:, :, None, :] + c_im[None] * pw_re[:, :, None, :]
    kt = (jnp.einsum('tghp,gpk->tghk', cp_re[:T], bb_re, precision=hi)
          - jnp.einsum('tghp,gpk->tghk', cp_im[:T], bb_im, precision=hi))
    kt = kt.at[0].add(d[:, :, None] * jnp.eye(H, dtype=F32)[None])
    diff = jnp.arange(T)[:, None] - jnp.arange(T)[None, :]
    kg = jnp.where((diff >= 0)[:, :, None, None, None], kt[jnp.clip(diff, 0)], 0.0)
    m_mat = kg.transpose(2, 0, 3, 1, 4).reshape(G, T * H, T * H)
    rev = T - 1 - jnp.arange(T)
    pr, pi = pw_re[rev], pw_im[rev]
    n_re = pr[..., None] * bb_re[None] - pi[..., None] * bb_im[None]
    n_im = pr[..., None] * bb_im[None] + pi[..., None] * bb_re[None]
    n_mat = jnp.concatenate([n_re, n_im], axis=2).transpose(1, 2, 0, 3).reshape(G, 2 * P, T * H)
    o_mat = jnp.concatenate([cp_re[1:], -cp_im[1:]], axis=-1).transpose(1, 0, 2, 3).reshape(G, T * H, 2 * P)
    lam1 = jnp.concatenate([pw_re[T], pw_re[T]], axis=-1)[:, None, :]
    lam2 = jnp.concatenate([-pw_im[T], pw_im[T]], axis=-1)[:, None, :]
    return m_mat, n_mat, o_mat, lam1, lam2


def _chunk_scan(name, v, lam1, lam2, reverse, s_fwd=None, gb=16):
    G, nc, W = v.shape
    gb = _tile(G, gb, 1)
    half = W // 2
    ntile = nc // 8

    def body(*refs):
        if reverse:
            v_ref, l1_ref, l2_ref, s_ref, o_ref, d1_ref, d2_ref = refs
        else:
            v_ref, l1_ref, l2_ref, o_ref = refs
        l1 = jnp.broadcast_to(l1_ref[...], (gb, 8, W))
        l2 = jnp.broadcast_to(l2_ref[...], (gb, 8, W))
        if reverse:
            l2 = -l2
        row = lax.broadcasted_iota(jnp.int32, (gb, 8, W), 1)

        def tile_step(n, carry):
            if reverse:
                st, a1, a2 = carry
                base = pl.multiple_of((ntile - 1 - n) * 8, 8)
            else:
                st = carry
                base = pl.multiple_of(n * 8, 8)
            vt = v_ref[:, pl.ds(base, 8), :]
            out = jnp.zeros((gb, 8, W), F32)
            order = range(7, -1, -1) if reverse else range(8)
            for r in order:
                out = jnp.where(row == r, st, out)
                vr = jnp.broadcast_to(vt[:, r:r + 1, :], (gb, 8, W))
                st = l1 * st + l2 * pltpu.roll(st, half, 2) + vr
            o_ref[:, pl.ds(base, 8), :] = out
            if reverse:
                sv = s_ref[:, pl.ds(base, 8), :]
                a1 = a1 + out * sv
                a2 = a2 + out * pltpu.roll(sv, half, 2)
                return st, a1, a2
            return st

        zero = jnp.zeros((gb, 8, W), F32)
        if reverse:
            _, a1, a2 = lax.fori_loop(0, ntile, tile_step, (zero, zero, zero))
            d1_ref[...] = jnp.sum(a1, axis=1, keepdims=True)
            d2_ref[...] = jnp.sum(a2, axis=1, keepdims=True)
        else:
            lax.fori_loop(0, ntile, tile_step, zero)

    big = pl.BlockSpec((gb, nc, W), lambda g: (g, 0, 0))
    vec = pl.BlockSpec((gb, 1, W), lambda g: (g, 0, 0))
    if reverse:
        return _pcall(body, name=name, grid=(G // gb,), in_specs=[big, vec, vec, big],
                      out_specs=[big, vec, vec],
                      out_shape=[jax.ShapeDtypeStruct((G, nc, W), F32), jax.ShapeDtypeStruct((G, 1, W), F32),
                                 jax.ShapeDtypeStruct((G, 1, W), F32)],
                      compiler_params=_params(("parallel",)))(v, lam1, lam2, s_fwd)
    return _pcall(body, name=name, grid=(G // gb,), in_specs=[big, vec, vec], out_specs=big,
                  out_shape=jax.ShapeDtypeStruct((G, nc, W), F32),
                  compiler_params=_params(("parallel",)))(v, lam1, lam2)


_GELU_C = math.sqrt(2.0 / math.pi)


def _gelu(y):
    return 0.5 * y * (1.0 + jnp.tanh(_GELU_C * (y + 0.044715 * y * y * y)))


def _gelu_grad(y):
    t = jnp.tanh(_GELU_C * (y + 0.044715 * y * y * y))
    return 0.5 * (1.0 + t) + 0.5 * y * (1.0 - t * t) * _GELU_C * (1.0 + 3.0 * 0.044715 * y * y)


def _to_groups(a, G, Hg):
    L = a.shape[0]
    return a.reshape(L // CHUNK, CHUNK, G, Hg).transpose(2, 0, 1, 3).reshape(G, L // CHUNK, CHUNK * Hg)


def _from_groups(a, Hg):
    G, nc, _ = a.shape
    return a.reshape(G, nc, CHUNK, Hg).transpose(1, 2, 0, 3).reshape(nc * CHUNK, G * Hg)


def _split_dot(x, u):
    hi = x.astype(BF16)
    lo = (x - hi.astype(F32)).astype(BF16)
    return (jnp.dot(hi, u, preferred_element_type=F32) + jnp.dot(lo, u, preferred_element_type=F32))


def _col_to_row(col, n):
    return jnp.broadcast_to(col, (n, LANES)).T[0:1, :]


def _row_to_col(row, n):
    return jnp.broadcast_to(row, (LANES, n)).T[:, 0:1]


def _attn_fwd(name, q, k, v, tq, tk, hb):
    H, L, dh = q.shape
    nq, nkb = L // tq, L // tk
    hs = range(hb)

    def body(q_ref, k_ref, v_ref, o_ref, rc_ref):
        i = pl.program_id(1)
        diff = (lax.broadcasted_iota(jnp.int32, (tq, tk), 1) - lax.broadcasted_iota(jnp.int32, (tq, tk), 0))
        u_suf = (lax.broadcasted_iota(jnp.int32, (tk, tk), 0)
                 > lax.broadcasted_iota(jnp.int32, (tk, tk), 1)).astype(BF16)
        nblk = (i * tq + tq - 1) // tk + 1
        rc_ref[...] = jnp.zeros_like(rc_ref)

        def step(n, carry):
            runs, accs = carry
            kb = nblk - 1 - n
            ks = pl.multiple_of(kb * tk, tk)
            causal = diff < (i * tq - ks)
            kblk = [k_ref[h, pl.ds(ks, tk), :] for h in hs]
            vblk = [v_ref[h, pl.ds(ks, tk), :] for h in hs]
            z = [lax.dot_general(q_ref[h], kblk[h], (((1,), (1,)), ((), ())), preferred_element_type=F32)
                 for h in hs]
            e = [jnp.exp(-jnp.abs(z[h])) for h in hs]
            sp = [jnp.maximum(z[h], 0.0) + jnp.log(1.0 + e[h]) for h in hs]
            lom = [jnp.where(causal, -sp[h], 0.0) for h in hs]
            rem = [runs[h] + _split_dot(lom[h], u_suf) for h in hs]
            w = [jnp.where(causal, jnp.exp(z[h] - sp[h] + rem[h]), 0.0) for h in hs]
            accs = tuple(accs[h] + jnp.dot(w[h].astype(BF16), vblk[h], preferred_element_type=F32) for h in hs)
            for h in hs:
                rc_ref[h, kb] = _col_to_row(runs[h], tq)
            runs = tuple(runs[h] + jnp.sum(lom[h], axis=1, keepdims=True) for h in hs)
            return runs, accs

        init = (tuple(jnp.zeros((tq, 1), F32) for _ in hs), tuple(jnp.zeros((tq, dh), F32) for _ in hs))
        _, accs = lax.fori_loop(0, nblk, step, init)
        for h in hs:
            o_ref[h] = accs[h].astype(o_ref.dtype)

    qspec = pl.BlockSpec((hb, tq, dh), lambda h, i: (h, i, 0))
    kspec = pl.BlockSpec((hb, L, dh), lambda h, i: (h, 0, 0))
    return _pcall(
        body, name=name, grid=(H // hb, nq), in_specs=[qspec, kspec, kspec],
        out_specs=[qspec, pl.BlockSpec((hb, None, nkb, 1, tq), lambda h, i: (h, i, 0, 0, 0))],
        out_shape=[jax.ShapeDtypeStruct((H, L, dh), BF16), jax.ShapeDtypeStruct((H, nq, nkb, 1, tq), F32)],
        compiler_params=_params(("parallel", "arbitrary")),
    )(q, k, v)


def _attn_bwd(name, q, k, v, do, rc, tq, tk, scale, hb):
    H, L, dh = q.shape
    nq, nkb = L // tq, L // tk
    hs = range(hb)

    def body(q_ref, k_ref, v_ref, do_ref, rc_ref, dq_ref, dk_ref, dv_ref):
        i = pl.program_id(1)

        @pl.when(i == 0)
        def _():
            dk_ref[...] = jnp.zeros_like(dk_ref)
            dv_ref[...] = jnp.zeros_like(dv_ref)

        diff = (lax.broadcasted_iota(jnp.int32, (tq, tk), 1) - lax.broadcasted_iota(jnp.int32, (tq, tk), 0))
        r_io = lax.broadcasted_iota(jnp.int32, (tk, tk), 0)
        c_io = lax.broadcasted_iota(jnp.int32, (tk, tk), 1)
        u_suf = (r_io > c_io).astype(BF16)
        u_pre = (r_io < c_io).astype(BF16)
        nblk = (i * tq + tq - 1) // tk + 1

        def step(kb, carry):
            pres, dqs = carry
            ks = pl.multiple_of(kb * tk, tk)
            causal = diff < (i * tq - ks)
            kblk = [k_ref[h, pl.ds(ks, tk), :] for h in hs]
            vblk = [v_ref[h, pl.ds(ks, tk), :] for h in hs]
            qb = [q_ref[h] for h in hs]
            dob = [do_ref[h] for h in hs]
            z = [lax.dot_general(qb[h], kblk[h], (((1,), (1,)), ((), ())), preferred_element_type=F32) for h in hs]
            dw = [lax.dot_general(dob[h], vblk[h], (((1,), (1,)), ((), ())), preferred_element_type=F32)
                  for h in hs]
            e = [jnp.exp(-jnp.abs(z[h])) for h in hs]
            sp = [jnp.maximum(z[h], 0.0) + jnp.log(1.0 + e[h]) for h in hs]
            lom = [jnp.where(causal, -sp[h], 0.0) for h in hs]
            rem = [_row_to_col(rc_ref[h, kb], tq) + _split_dot(lom[h], u_suf) for h in hs]
            w = [jnp.where(causal, jnp.exp(z[h] - sp[h] + rem[h]), 0.0) for h in hs]
            da = [dw[h] * w[h] for h in hs]
            p = [pres[h] + _split_dot(da[h], u_pre) for h in hs]
            inv = [1.0 / (1.0 + e[h]) for h in hs]
            dz = []
            for h in hs:
                pos = z[h] >= 0.0
                beta = jnp.where(pos, inv[h], e[h] * inv[h])
                omb = jnp.where(pos, e[h] * inv[h], inv[h])
                dz.append(jnp.where(causal, da[h] * omb - beta * p[h], 0.0))
            dqs = tuple(dqs[h] + jnp.dot(dz[h].astype(BF16), kblk[h], preferred_element_type=F32) for h in hs)
            for h in hs:
                dk_ref[h, pl.ds(ks, tk), :] += jnp.dot(dz[h].T.astype(BF16), qb[h], preferred_element_type=F32)
                dv_ref[h, pl.ds(ks, tk), :] += jnp.dot(w[h].T.astype(BF16), dob[h], preferred_element_type=F32)
            pres = tuple(pres[h] + jnp.sum(da[h], axis=1, keepdims=True) for h in hs)
            return pres, dqs

        init = (tuple(jnp.zeros((tq, 1), F32) for _ in hs), tuple(jnp.zeros((tq, dh), F32) for _ in hs))
        _, dqs = lax.fori_loop(0, nblk, step, init)
        for h in hs:
            dq_ref[h] = (dqs[h] * scale).astype(dq_ref.dtype)

    qspec = pl.BlockSpec((hb, tq, dh), lambda h, i: (h, i, 0))
    kspec = pl.BlockSpec((hb, L, dh), lambda h, i: (h, 0, 0))
    return _pcall(
        body, name=name, grid=(H // hb, nq),
        in_specs=[qspec, kspec, kspec, qspec, pl.BlockSpec((hb, None, nkb, 1, tq), lambda h, i: (h, i, 0, 0, 0))],
        out_specs=[qspec, kspec, kspec],
        out_shape=[jax.ShapeDtypeStruct((H, L, dh), BF16), jax.ShapeDtypeStruct((H, L, dh), F32),
                   jax.ShapeDtypeStruct((H, L, dh), F32)],
        compiler_params=_params(("parallel", "arbitrary")),
    )(q, k, v, do, rc)


def _heads(a):
    L, D = a.shape
    return a.reshape(L, D // HEAD_DIM, HEAD_DIM).transpose(1, 0, 2)


def _unheads(a):
    H, L, dh = a.shape
    return a.transpose(1, 0, 2).reshape(L, H * dh)


_MESH = pl.DeviceIdType.MESH
_HBM = pl.BlockSpec(memory_space=pltpu.HBM)


def _all_gather(name, shard):
    R, C = shard.shape

    def body(x_ref, out_ref, send_sems, recv_sems, local_sem):
        x, y, c = lax.axis_index("x"), lax.axis_index("y"), lax.axis_index("c")
        me, sibling = (x, y, c), (x, y, 1 - c)
        chips = [(1 - x, y), (x, 1 - y), (1 - x, 1 - y)]

        def slot(px, py, pc):
            return out_ref.at[4 * px + 2 * py + pc]

        def copy(k, block, to, src=None):
            return pltpu.make_async_remote_copy(
                src_ref=slot(*block) if src is None else src, dst_ref=slot(*block),
                send_sem=send_sems.at[k], recv_sem=recv_sems.at[k], device_id=to, device_id_type=_MESH)

        mine = pltpu.make_async_copy(x_ref, slot(*me), local_sem)
        mine.start()
        first = [copy(0, me, sibling, src=x_ref)]
        first += [copy(1 + j, me, (*chip, c), src=x_ref) for j, chip in enumerate(chips)]
        for cp in first:
            cp.start()
        passed = [copy(4 + j, (*chip, c), sibling) for j, chip in enumerate(chips)]
        for j, chip in enumerate(chips):
            copy(1 + j, (*chip, c), me).wait_recv()
            passed[j].start()
        copy(0, sibling, me).wait_recv()
        for j, chip in enumerate(chips):
            copy(4 + j, (*chip, 1 - c), me).wait_recv()
        for cp in first + passed:
            cp.wait_send()
        mine.wait()

    return _pcall(
        body, name=name, out_shape=jax.ShapeDtypeStruct((N_DEV, R, C), shard.dtype),
        in_specs=[_HBM], out_specs=_HBM,
        scratch_shapes=[pltpu.SemaphoreType.DMA((7,)), pltpu.SemaphoreType.DMA((7,)), pltpu.SemaphoreType.DMA],
    )(shard)


def _sibling_exchange(name, x):
    def body(x_ref, out_ref, send_sem, recv_sem):
        sibling = (lax.axis_index("x"), lax.axis_index("y"), 1 - lax.axis_index("c"))
        cp = pltpu.make_async_remote_copy(src_ref=x_ref, dst_ref=out_ref, send_sem=send_sem, recv_sem=recv_sem,
                                          device_id=sibling, device_id_type=_MESH)
        cp.start()
        cp.wait()

    return _pcall(
        body, name=name, out_shape=jax.ShapeDtypeStruct(x.shape, x.dtype), in_specs=[_HBM], out_specs=_HBM,
        scratch_shapes=[pltpu.SemaphoreType.DMA, pltpu.SemaphoreType.DMA],
    )(x)


def _chip_exchange(name, x):
    _, R, C = x.shape

    def body(x_ref, out_ref, send_sems, recv_sems):
        mx, my, mc = lax.axis_index("x"), lax.axis_index("y"), lax.axis_index("c")
        chips = [(1 - mx, my), (mx, 1 - my), (1 - mx, 1 - my)]
        cps = [pltpu.make_async_remote_copy(src_ref=x_ref.at[2 * px + py], dst_ref=out_ref.at[j],
                                            send_sem=send_sems.at[j], recv_sem=recv_sems.at[j],
                                            device_id=(px, py, mc), device_id_type=_MESH)
               for j, (px, py) in enumerate(chips)]
        for cp in cps:
            cp.start()
        for cp in cps:
            cp.wait()

    return _pcall(
        body, name=name, out_shape=jax.ShapeDtypeStruct((3, R, C), x.dtype), in_specs=[_HBM], out_specs=_HBM,
        scratch_shapes=[pltpu.SemaphoreType.DMA((3,)), pltpu.SemaphoreType.DMA((3,))],
    )(x)


def _pair_sum(name, a, b):
    nb, R, C = a.shape
    tm = _tile(R, 512, 16)

    def fn(i, ab, bb):
        s = ab.astype(F32) + bb.astype(F32)
        return s, s

    spec = lambda arr: (arr, (1, tm, C), lambda i: (i // (R // tm), i % (R // tm), 0))
    out = lambda dt: ((nb, R, C), dt, (1, tm, C), lambda i: (i // (R // tm), i % (R // tm), 0))
    return _rowwise(name, fn, nb * (R // tm), [spec(a), spec(b)], [out(F32), out(BF16)])


def _sum8(name, g):
    n, R, C = g.shape
    tm = _tile(R, 256)

    def fn(i, gb):
        s = gb[0]
        for d in range(1, n):
            s = s + gb[d]
        return (s,)

    return _rowwise(name, fn, R // tm, [(g, (n, tm, C), lambda i: (0, i, 0))], [_orows(R, C, F32, tm)])[0]


def _adamw(name, w, m, v, grads, tm=512):
    R, C = w.shape
    tm = _tile(R, tm, 16)
    ng = len(grads)

    def fn(i, wb, mb, vb, *gs):
        g = gs[0].astype(F32)
        for t in gs[1:]:
            g = g + t.astype(F32)
        mn = ADAM_B1 * mb + (1.0 - ADAM_B1) * g
        vn = ADAM_B2 * vb + (1.0 - ADAM_B2) * (g * g)
        m_hat = mn / (1.0 - ADAM_B1 ** ADAM_STEP)
        v_hat = vn / (1.0 - ADAM_B2 ** ADAM_STEP)
        delta = -ADAM_LR * (m_hat / (jnp.sqrt(v_hat) + ADAM_EPS) + ADAM_WD * wb)
        return g, delta, mn, vn

    ins = [_rows(w, tm), _rows(m, tm), _rows(v, tm)] + [_rows(g, tm) for g in grads]
    return _rowwise(name, fn, R // tm, ins, [_orows(R, C, F32, tm) for _ in range(4)])


def _big_rows(D, Fh):
    return {'ssm_w_in': D // 8, 'ssm_w_glu': D // 4, 'kv_w': D // 4, 'attn_w_q': D // 8, 'attn_w_o': D // 8,
            'ffn_w_up': Fh // 2, 'ffn_w_down': Fh // 4, 'ffn_conv_w': CONV_ROWS}


def _pack_shards(d, D, wire=False):
    parts = []
    for n in BIG:
        a = d[n]
        if n == 'ffn_conv_w':
            flat = a.reshape(-1)
            if wire:
                flat = lax.bitcast_convert_type(flat, BF16).reshape(-1)
            a = jnp.pad(flat, (0, CONV_ROWS * D - flat.shape[0]))
        elif wire:
            a = a.astype(BF16)
        parts.append(a.reshape(-1, D))
    return jnp.concatenate(parts, axis=0)


def _unpack_shards(pack, shapes, D, Fh):
    rows = _big_rows(D, Fh)
    out, off = {}, 0
    for n in BIG:
        seg = pack[off:off + rows[n]]
        off += rows[n]
        size = math.prod(shapes[n])
        out[n] = seg.reshape(-1)[:size].reshape(shapes[n])
    return out


def _full_weights(gath, D, Fh):
    rows = _big_rows(D, Fh)
    seg, off = {}, 0
    for n in BIG:
        seg[n] = gath[:, off:off + rows[n]]
        off += rows[n]
    n8 = N_DEV
    w = {}
    w['ssm_w_in'] = seg['ssm_w_in'].reshape(D, D)
    w['attn_w_q'] = seg['attn_w_q'].reshape(D, D)
    w['attn_w_o'] = seg['attn_w_o'].reshape(D, D)
    w['ssm_w_glu'] = seg['ssm_w_glu'].reshape(n8, D, 2 * D // n8).transpose(1, 0, 2).reshape(D, 2 * D)
    w['kv_w'] = seg['kv_w'].reshape(n8, D, 2 * D // n8).transpose(1, 0, 2).reshape(D, 2 * D)
    w['ffn_w_up'] = seg['ffn_w_up'].reshape(n8, 2, D, 2 * Fh // n8).transpose(1, 2, 0, 3).reshape(2, D, 2 * Fh)
    w['ffn_w_down'] = seg['ffn_w_down'].reshape(n8, 2, Fh // n8, D).transpose(1, 0, 2, 3).reshape(2, Fh, D)
    cw = seg['ffn_conv_w'].reshape(n8, CONV_ROWS * D)[:, :12 * Fh // n8].reshape(n8, 6 * Fh // n8, 2)
    cw = lax.bitcast_convert_type(cw, F32)
    w['ffn_conv_w'] = cw.reshape(n8, 2, 3, Fh // n8).transpose(1, 2, 0, 3).reshape(2, 3, Fh)
    return w


def _grad_pack(g, D, Fh):
    n8 = N_DEV
    parts = {
        'ssm_w_in': g['ssm_w_in'].reshape(n8, D // n8, D),
        'attn_w_q': g['attn_w_q'].reshape(n8, D // n8, D),
        'attn_w_o': g['attn_w_o'].reshape(n8, D // n8, D),
        'ssm_w_glu': g['ssm_w_glu'].reshape(D, n8, 2 * D // n8).transpose(1, 0, 2).reshape(n8, D // 4, D),
        'kv_w': g['kv_w'].reshape(D, n8, 2 * D // n8).transpose(1, 0, 2).reshape(n8, D // 4, D),
        'ffn_w_up': g['ffn_w_up'].reshape(2, D, n8, 2 * Fh // n8).transpose(2, 0, 1, 3).reshape(n8, Fh // 2, D),
        'ffn_w_down': g['ffn_w_down'].reshape(2, n8, Fh // n8, D).transpose(1, 0, 2, 3).reshape(n8, Fh // 4, D),
    }
    cw = g['ffn_conv_w'].reshape(2, 3, n8, Fh // n8).transpose(2, 0, 1, 3).reshape(n8, 6 * Fh // n8)
    parts['ffn_conv_w'] = jnp.pad(cw, ((0, 0), (0, CONV_ROWS * D - cw.shape[1]))).reshape(n8, CONV_ROWS, D)
    return jnp.concatenate([parts[n] for n in BIG], axis=1)


def _small_layout(shapes, D):
    lay, off = {}, 0
    for n in SMALL:
        r = -(-math.prod(shapes[n]) // D)
        lay[n] = (off, r)
        off += r
    lay['loss'] = (off, 1)
    off += 1
    return lay, -(-off // 8) * 8


def _pack_small(d, lay, total, D):
    parts = []
    for n in SMALL + ['loss']:
        if n not in d:
            parts.append(jnp.zeros((lay[n][1], D), F32))
            continue
        flat = d[n].reshape(-1).astype(F32)
        parts.append(jnp.pad(flat, (0, lay[n][1] * D - flat.shape[0])).reshape(lay[n][1], D))
    used = sum(lay[n][1] for n in SMALL + ['loss'])
    if total > used:
        parts.append(jnp.zeros((total - used, D), F32))
    return jnp.concatenate(parts, axis=0)


def _unpack_small(pack, lay, shapes, D):
    out = {}
    for n in SMALL:
        off, r = lay[n]
        out[n] = pack[off:off + r].reshape(-1)[:math.prod(shapes[n])].reshape(shapes[n])
    return out


def _ffn_fwd(tag, x, g_norm, w_up, conv_w, conv_b, w_down):
    h = _rms_fwd(f"{tag}_norm", x, g_norm)
    gu = _mm(f"{tag}_up", h, w_up, 'nn', F32)
    a = _gate_fwd(f"{tag}_gate", gu, conv_w, conv_b)
    return _mm(f"{tag}_down", a, w_down, 'nn', F32, resid=x), (h, gu, a)


def _ffn_bwd(tag, dres, dres_bf, x, saved, g_norm, w_up, conv_w, conv_b, w_down):
    h, gu, a = saved
    da = _mm(f"{tag}_dgate", dres_bf, w_down, 'nt', F32)
    d_w_down = _mm(f"{tag}_dwdown", a, dres_bf, 'tn', F32)
    dgc, du, dw0, dw1, dw2, dcb = _gate_bwd1(f"{tag}_gate_b1", gu, da, conv_w, conv_b)
    dgu = _gate_bwd2(f"{tag}_gate_b2", dgc, du, conv_w)
    d_w_up = _mm(f"{tag}_dwup", h, dgu, 'tn', F32)
    dh = _mm(f"{tag}_dh", dgu, w_up, 'nt', F32)
    dres, dres_bf, dg = _rms_bwd(f"{tag}_norm_b", x, dres, [(g_norm, dh)])
    return dres, dres_bf, dg, d_w_up, jnp.concatenate([dw0, dw1, dw2], axis=0), dcb[0], d_w_down


def kernel(x, norm_mix, norm_ffn, norm_kv, norm_final, ssm_w_in, ssm_a_re, ssm_a_im, ssm_log_dt, ssm_b_re, ssm_b_im, ssm_c_re, ssm_c_im, ssm_d, ssm_w_glu, kv_w, attn_w_q, attn_w_o, ffn_w_up, ffn_conv_w, ffn_conv_b, ffn_w_down, loss_target, m_norm_mix, m_norm_ffn, m_norm_kv, m_norm_final, m_ssm_w_in, m_ssm_a_re, m_ssm_a_im, m_ssm_log_dt, m_ssm_b_re, m_ssm_b_im, m_ssm_c_re, m_ssm_c_im, m_ssm_d, m_ssm_w_glu, m_kv_w, m_attn_w_q, m_attn_w_o, m_ffn_w_up, m_ffn_conv_w, m_ffn_conv_b, m_ffn_w_down, v_norm_mix, v_norm_ffn, v_norm_kv, v_norm_final, v_ssm_w_in, v_ssm_a_re, v_ssm_a_im, v_ssm_log_dt, v_ssm_b_re, v_ssm_b_im, v_ssm_c_re, v_ssm_c_im, v_ssm_d, v_ssm_w_glu, v_kv_w, v_attn_w_q, v_attn_w_o, v_ffn_w_up, v_ffn_conv_w, v_ffn_conv_b, v_ffn_w_down):
    wts = dict(zip(W_NAMES, (norm_mix, norm_ffn, norm_kv, norm_final, ssm_w_in, ssm_a_re, ssm_a_im, ssm_log_dt,
                             ssm_b_re, ssm_b_im, ssm_c_re, ssm_c_im, ssm_d, ssm_w_glu, kv_w, attn_w_q, attn_w_o,
                             ffn_w_up, ffn_conv_w, ffn_conv_b, ffn_w_down)))
    mom = dict(zip(W_NAMES, (m_norm_mix, m_norm_ffn, m_norm_kv, m_norm_final, m_ssm_w_in, m_ssm_a_re, m_ssm_a_im,
                             m_ssm_log_dt, m_ssm_b_re, m_ssm_b_im, m_ssm_c_re, m_ssm_c_im, m_ssm_d, m_ssm_w_glu,
                             m_kv_w, m_attn_w_q, m_attn_w_o, m_ffn_w_up, m_ffn_conv_w, m_ffn_conv_b, m_ffn_w_down)))
    vel = dict(zip(W_NAMES, (v_norm_mix, v_norm_ffn, v_norm_kv, v_norm_final, v_ssm_w_in, v_ssm_a_re, v_ssm_a_im,
                             v_ssm_log_dt, v_ssm_b_re, v_ssm_b_im, v_ssm_c_re, v_ssm_c_im, v_ssm_d, v_ssm_w_glu,
                             v_kv_w, v_attn_w_q, v_attn_w_o, v_ffn_w_up, v_ffn_conv_w, v_ffn_conv_b, v_ffn_w_down)))
    shapes = {n: wts[n].shape for n in W_NAMES}
    _, L, D = x.shape
    Fh = ffn_conv_b.shape[1]
    G, P = ssm_a_re.shape[1], ssm_a_re.shape[2]
    Hg = ssm_d.shape[2]
    x0 = x[0]
    target = loss_target[0]
    scale = HEAD_DIM ** -0.5

    gath = _all_gather("gather_weights", _pack_shards(wts, D, wire=True))
    w = _full_weights(gath, D, Fh)
    nm = [norm_mix[l:l + 1] for l in range(2)]
    nf = [norm_ffn[l:l + 1] for l in range(2)]
    nkv = norm_kv[None]
    nfin = norm_final[None]
    cb = [ffn_conv_b[l:l + 1] for l in range(2)]
    cwt = [w['ffn_conv_w'][l] for l in range(2)]

    s5p = (ssm_a_re[0], ssm_a_im[0], ssm_log_dt[0], ssm_b_re[0], ssm_b_im[0], ssm_c_re[0], ssm_c_im[0], ssm_d[0])
    (m_mat, n_mat, o_mat, lam1, lam2), s5_vjp = jax.vjp(_s5_build, *s5p)
    m_bf, n_bf, o_bf = m_mat.astype(BF16), n_mat.astype(BF16), o_mat.astype(BF16)

    h0 = _rms_fwd("l0_norm", x0, nm[0])
    u = _mm("l0_win", h0, w['ssm_w_in'], 'nn', BF16)
    ug = _to_groups(u, G, Hg)
    vloc = _bmm("s5_local_state", [(ug, n_bf, 'nt')], F32)
    st = _chunk_scan("s5_scan", vloc, lam1, lam2, reverse=False)
    yraw, yg_g = _bmm("s5_out", [(ug, m_bf, 'nt'), (st, o_bf, 'nt')], (F32, BF16),
                      post=lambda acc: (acc, _gelu(acc)))
    yg = _from_groups(yg_g, Hg)
    z = _mm("l0_wglu", yg, w['ssm_w_glu'], 'nn', F32)
    x1 = _glu_fwd("l0_glu", x0, z)
    x2, ffn0 = _ffn_fwd("f0", x1, nf[0], w['ffn_w_up'][0], cwt[0], cb[0], w['ffn_w_down'][0])

    hkv = _rms_fwd("kv_norm", x2, nkv)
    kvp = _mm("kv_proj", hkv, w['kv_w'], 'nn', BF16)
    h2 = _rms_fwd("l1_norm", x2, nm[1])
    qn = _mm("l1_wq", h2, w['attn_w_q'], 'nn', BF16, scale=scale)
    qh, kh, vh = _heads(qn), _heads(kvp[:, :D]), _heads(kvp[:, D:])
    oh, rc = _attn_fwd("attn_fwd", qh, kh, vh, ATT_TQ, ATT_TK, ATT_HB_FWD)
    o = _unheads(oh)
    x3 = _mm("l1_wo", o, w['attn_w_o'], 'nn', F32, resid=x2)
    x4, ffn1 = _ffn_fwd("f1", x3, nf[1], w['ffn_w_up'][1], cwt[1], cb[1], w['ffn_w_down'][1])

    dres, dres_bf, dg_final, loss_part = _final_loss("loss_head", x4, nfin, target)

    gw = {}
    dres, dres_bf, dg_nf1, gup1, gcw1, gcb1, gdn1 = _ffn_bwd("f1", dres, dres_bf, x3, ffn1, nf[1], w['ffn_w_up'][1],
                                                             cwt[1], cb[1], w['ffn_w_down'][1])
    do = _mm("l1_do", dres_bf, w['attn_w_o'], 'nt', BF16)
    gw['attn_w_o'] = _mm("l1_dwo", o, dres_bf, 'tn', F32)
    dqh, dkh, dvh = _attn_bwd("attn_bwd", qh, kh, vh, _heads(do), rc, ATT_TQ, ATT_TK, scale, ATT_HB_BWD)
    dq = _unheads(dqh)
    dkv = jnp.concatenate([_unheads(dkh), _unheads(dvh)], axis=1).astype(BF16)
    gw['attn_w_q'] = _mm("l1_dwq", h2, dq, 'tn', F32)
    dh2 = _mm("l1_dh", dq, w['attn_w_q'], 'nt', F32)
    gw['kv_w'] = _mm("kv_dw", hkv, dkv, 'tn', F32)
    dhkv = _mm("kv_dh", dkv, w['kv_w'], 'nt', F32)
    dres, dres_bf, dg_nm1, dg_nkv = _rms_bwd("l1_norm_b", x2, dres, [(nm[1], dh2), (nkv, dhkv)])
    dres, dres_bf, dg_nf0, gup0, gcw0, gcb0, gdn0 = _ffn_bwd("f0", dres, dres_bf, x1, ffn0, nf[0], w['ffn_w_up'][0],
                                                             cwt[0], cb[0], w['ffn_w_down'][0])
    dz1, dz2 = _glu_bwd("l0_glu_b", dres, z)
    dz = jnp.concatenate([dz1, dz2], axis=1)
    gw['ssm_w_glu'] = _mm("l0_dwglu", yg, dz, 'tn', F32)
    dyg = _mm("l0_dyg", dz, w['ssm_w_glu'], 'nt', F32)
    dyg_g = _to_groups(dyg, G, Hg)
    gshape = yraw.shape
    dy = _rowwise("s5_gelu_b", lambda i, a, b: (a * _gelu_grad(b),), G,
                  [(dyg_g, (1,) + gshape[1:], lambda i: (i, 0, 0)), (yraw, (1,) + gshape[1:], lambda i: (i, 0, 0))],
                  [(gshape, BF16, (1,) + gshape[1:], lambda i: (i, 0, 0))])[0]
    ds = _bmm("s5_dstate", [(dy, o_bf, 'nn')], F32)
    dv_loc, dlam1, dlam2 = _chunk_scan("s5_scan_b", ds, lam1, lam2, reverse=True, s_fwd=st)
    du_g = _bmm("s5_du", [(dy, m_bf, 'nn'), (dv_loc, n_bf, 'nn')], BF16)
    d_m = _bmm("s5_dm", [(dy, ug, 'tn')], F32)
    d_o = _bmm("s5_do", [(dy, st, 'tn')], F32)
    d_n = _bmm("s5_dn", [(dv_loc, ug, 'tn')], F32)
    s5g = s5_vjp((d_m, d_n, d_o, dlam1, dlam2))
    du = _from_groups(du_g, Hg)
    gw['ssm_w_in'] = _mm("l0_dwin", h0, du, 'tn', F32)
    dh0 = _mm("l0_dh", du, w['ssm_w_in'], 'nt', F32)
    grad_x, _, dg_nm0 = _rms_bwd("l0_norm_b", x0, dres, [(nm[0], dh0)])

    gw['ffn_w_up'] = jnp.stack([gup0, gup1])
    gw['ffn_w_down'] = jnp.stack([gdn0, gdn1])
    gw['ffn_conv_w'] = jnp.stack([gcw0, gcw1])

    small_g = {
        'norm_mix': jnp.concatenate([dg_nm0, dg_nm1], axis=0), 'norm_ffn': jnp.concatenate([dg_nf0, dg_nf1], axis=0),
        'norm_kv': dg_nkv, 'norm_final': dg_final, 'ffn_conv_b': jnp.stack([gcb0, gcb1]),
        'ssm_a_re': s5g[0], 'ssm_a_im': s5g[1], 'ssm_log_dt': s5g[2], 'ssm_b_re': s5g[3], 'ssm_b_im': s5g[4],
        'ssm_c_re': s5g[5], 'ssm_c_im': s5g[6], 'ssm_d': s5g[7], 'loss': loss_part[0, 0:1],
    }
    lay, rs = _small_layout(shapes, D)
    small_sum = _sum8("small_sum", _all_gather("gather_small", _pack_small(small_g, lay, rs, D)))
    loss = small_sum[lay['loss'][0], 0]
    sg, sdelta, sm, sv = _adamw("adamw_small", _pack_small(wts, lay, rs, D), _pack_small(mom, lay, rs, D),
                                _pack_small(vel, lay, rs, D), [small_sum])
    small_out = [_unpack_small(t, lay, shapes, D) for t in (sg, sdelta, sm, sv)]

    gp = _grad_pack(gw, D, Fh).astype(BF16)
    rows = gp.shape[1]
    gp = gp.reshape(4, 2, rows, D)
    c = lax.axis_index("c")
    chip = 2 * lax.axis_index("x") + lax.axis_index("y")
    mine = lax.dynamic_index_in_dim(gp, c, axis=1, keepdims=False)
    theirs = lax.dynamic_index_in_dim(gp, 1 - c, axis=1, keepdims=False)
    got = _sibling_exchange("rs_sibling", theirs)
    chip_f32, chip_bf = _pair_sum("rs_pair_sum", mine, got)
    from_chips = _chip_exchange("rs_chips", chip_bf)
    own = lax.dynamic_index_in_dim(chip_f32, chip, axis=0, keepdims=False)
    bg, bdelta, bm, bv = _adamw("adamw_big", _pack_shards(wts, D), _pack_shards(mom, D), _pack_shards(vel, D),
                                [own, from_chips[0], from_chips[1], from_chips[2]])
    big_out = [_unpack_shards(t, shapes, D, Fh) for t in (bg, bdelta, bm, bv)]

    outs = [loss, grad_x[None]]
    for k in range(4):
        for n in W_NAMES:
            outs.append(big_out[k][n] if n in BIG else small_out[k][n])
    return tuple(outs)
```

```python
import math

import jax
import jax.numpy as jnp
from jax import lax
from jax.experimental import pallas as pl
from jax.experimental.pallas import tpu as pltpu

F32 = jnp.float32
BF16 = jnp.bfloat16

EPS = 1e-6
HEAD_DIM = 64
CHUNK = 16
N_DEV = 8
ADAM_LR = 0.001
ADAM_B1 = 0.9
ADAM_B2 = 0.999
ADAM_EPS = 1e-08
ADAM_WD = 0.01
ADAM_STEP = 10
VMEM_LIMIT = 48 * 1024 * 1024
LANES = 128
ATT_TQ = 128
ATT_TK = 256
ATT_HB_FWD = 4
ATT_HB_BWD = 4

W_NAMES = ['norm_mix', 'norm_ffn', 'norm_kv', 'norm_final', 'ssm_w_in', 'ssm_a_re', 'ssm_a_im', 'ssm_log_dt',
           'ssm_b_re', 'ssm_b_im', 'ssm_c_re', 'ssm_c_im', 'ssm_d', 'ssm_w_glu', 'kv_w', 'attn_w_q', 'attn_w_o',
           'ffn_w_up', 'ffn_conv_w', 'ffn_conv_b', 'ffn_w_down']
BIG = ['ssm_w_in', 'ssm_w_glu', 'kv_w', 'attn_w_q', 'attn_w_o', 'ffn_w_up', 'ffn_w_down', 'ffn_conv_w']
SMALL = [n for n in W_NAMES if n not in BIG]
CONV_ROWS = 32


def _pcall(body, **kw):
    return pl.pallas_call(body, **kw)


def _params(sem=None):
    if sem is None:
        return pltpu.CompilerParams(vmem_limit_bytes=VMEM_LIMIT)
    return pltpu.CompilerParams(dimension_semantics=sem, vmem_limit_bytes=VMEM_LIMIT)


def _tile(n, pref, mult=8):
    best = None
    for t in range(mult, min(n, pref) + 1, mult):
        if n % t == 0:
            best = t
    return n if best is None else best


def _mm(name, a, b, mode, out_dtype=F32, tm=512, tn=512, scale=None, resid=None):
    if mode == 'nn':
        (M, K), (K2, N) = a.shape, b.shape
    elif mode == 'nt':
        (M, K), (N, K2) = a.shape, b.shape
    else:
        (K, M), (K2, N) = a.shape, b.shape
    assert K == K2, (name, a.shape, b.shape)
    tm, tn = _tile(M, tm, LANES), _tile(N, tn, LANES)
    if mode == 'tn':
        a_spec = pl.BlockSpec((K, tm), lambda i, j: (0, i))
    else:
        a_spec = pl.BlockSpec((tm, K), lambda i, j: (i, 0))
    if mode == 'nt':
        b_spec = pl.BlockSpec((tn, K), lambda i, j: (j, 0))
    else:
        b_spec = pl.BlockSpec((K, tn), lambda i, j: (0, j))
    o_spec = pl.BlockSpec((tm, tn), lambda i, j: (i, j))
    dn = {'nn': ((1,), (0,)), 'nt': ((1,), (1,)), 'tn': ((0,), (0,))}[mode]

    def body(*refs):
        a_ref, b_ref, o_ref = refs[0], refs[1], refs[-1]
        acc = lax.dot_general(a_ref[...].astype(BF16), b_ref[...].astype(BF16), (dn, ((), ())),
                              preferred_element_type=F32)
        if scale is not None:
            acc = acc * scale
        if resid is not None:
            acc = acc + refs[2][...]
        o_ref[...] = acc.astype(o_ref.dtype)

    return _pcall(
        body, name=name, grid=(M // tm, N // tn),
        in_specs=[a_spec, b_spec] + ([o_spec] if resid is not None else []),
        out_specs=o_spec,
        out_shape=jax.ShapeDtypeStruct((M, N), out_dtype),
        compiler_params=_params(("parallel", "parallel")),
    )(*([a, b] + ([resid] if resid is not None else [])))


def _bmm(name, terms, out_dtype, gb=8, post=None, n_out=1):
    G = terms[0][0].shape[0]
    gb = _tile(G, gb, 1)
    dns = {'nn': ((1,), (0,)), 'nt': ((1,), (1,)), 'tn': ((0,), (0,))}

    def oshape(a, b, mode):
        m = a.shape[2] if mode == 'tn' else a.shape[1]
        n = b.shape[1] if mode == 'nt' else b.shape[2]
        return m, n

    m, n = oshape(*terms[0])
    out_dtypes = out_dtype if isinstance(out_dtype, (tuple, list)) else (out_dtype,)
    n_in = 2 * len(terms)

    def body(*refs):
        ins, outs = refs[:n_in], refs[n_in:]
        for gi in range(gb):
            acc = None
            for t, (_, _, mode) in enumerate(terms):
                part = lax.dot_general(ins[2 * t][gi].astype(BF16), ins[2 * t + 1][gi].astype(BF16),
                                       (dns[mode], ((), ())), preferred_element_type=F32)
                acc = part if acc is None else acc + part
            vals = (acc,) if post is None else post(acc)
            for o_ref, v in zip(outs, vals):
                o_ref[gi] = v.astype(o_ref.dtype)

    in_specs, args = [], []
    for a, b, _ in terms:
        in_specs += [pl.BlockSpec((gb,) + a.shape[1:], lambda g: (g, 0, 0)),
                     pl.BlockSpec((gb,) + b.shape[1:], lambda g: (g, 0, 0))]
        args += [a, b]
    res = _pcall(
        body, name=name, grid=(G // gb,), in_specs=in_specs,
        out_specs=[pl.BlockSpec((gb, m, n), lambda g: (g, 0, 0)) for _ in out_dtypes],
        out_shape=[jax.ShapeDtypeStruct((G, m, n), dt) for dt in out_dtypes],
        compiler_params=_params(("parallel",)),
    )(*args)
    return res[0] if len(out_dtypes) == 1 else res


def _rowwise(name, fn, n_steps, ins, outs, n_acc=0):
    n_in, n_out = len(ins), len(outs)

    def body(*refs):
        i = pl.program_id(0)
        vals = fn(i, *[r[...] for r in refs[:n_in]])
        o_refs = refs[n_in:]
        for j in range(n_out - n_acc):
            o_refs[j][...] = vals[j].astype(o_refs[j].dtype)
        if n_acc:
            @pl.when(i == 0)
            def _():
                for j in range(n_out - n_acc, n_out):
                    o_refs[j][...] = vals[j].astype(o_refs[j].dtype)

            @pl.when(i > 0)
            def _():
                for j in range(n_out - n_acc, n_out):
                    o_refs[j][...] += vals[j].astype(o_refs[j].dtype)

    res = _pcall(
        body, name=name, grid=(n_steps,),
        in_specs=[pl.BlockSpec(blk, im) for _, blk, im in ins],
        out_specs=[pl.BlockSpec(blk, im) for _, _, blk, im in outs],
        out_shape=[jax.ShapeDtypeStruct(s, d) for s, d, _, _ in outs],
        compiler_params=_params(("arbitrary",)),
    )(*[a for a, _, _ in ins])
    return res


def _rows(a, tm):
    return (a, (tm, a.shape[1]), lambda i: (i, 0))


def _whole(a):
    nd = a.ndim
    return (a, a.shape, lambda i: (0,) * nd)


def _orows(L, n, dtype, tm):
    return ((L, n), dtype, (tm, n), lambda i: (i, 0))


def _oacc(r, n):
    return ((r, n), F32, (r, n), lambda i: (0, 0))


def _rms_fwd(name, x, g, tm=512):
    L, D = x.shape
    tm = _tile(L, tm)

    def fn(i, xb, gb):
        r = lax.rsqrt(jnp.mean(xb * xb, axis=-1, keepdims=True) + EPS)
        return (xb * r * gb,)

    return _rowwise(name, fn, L // tm, [_rows(x, tm), _whole(g)], [_orows(L, D, BF16, tm)])[0]


def _rms_bwd(name, x, dres, branches, tm=256):
    L, D = x.shape
    tm = _tile(L, tm)
    nb = len(branches)

    def fn(i, xb, db, *rest):
        r = lax.rsqrt(jnp.mean(xb * xb, axis=-1, keepdims=True) + EPS)
        xh = xb * r
        dx = db
        dgs = []
        for b in range(nb):
            gb, dyb = rest[2 * b], rest[2 * b + 1].astype(F32)
            dxh = dyb * gb
            dx = dx + r * (dxh - xh * jnp.mean(dxh * xh, axis=-1, keepdims=True))
            dgs.append(jnp.sum(dyb * xh, axis=0, keepdims=True))
        return (dx, dx, *dgs)

    ins = [_rows(x, tm), _rows(dres, tm)]
    for g, dy in branches:
        ins += [_whole(g), _rows(dy, tm)]
    outs = [_orows(L, D, F32, tm), _orows(L, D, BF16, tm)] + [_oacc(1, D) for _ in range(nb)]
    return _rowwise(name, fn, L // tm, ins, outs, n_acc=nb)


def _final_loss(name, x, g, target, tm=256):
    L, D = x.shape
    tm = _tile(L, tm)

    def fn(i, xb, gb, tb):
        r = lax.rsqrt(jnp.mean(xb * xb, axis=-1, keepdims=True) + EPS)
        xh = xb * r
        err = xh * gb - tb
        dy = err * (1.0 / D)
        dxh = dy * gb
        dx = r * (dxh - xh * jnp.mean(dxh * xh, axis=-1, keepdims=True))
        dg = jnp.sum(dy * xh, axis=0, keepdims=True)
        per_row = jnp.mean(err * err, axis=-1, keepdims=True)
        loss = 0.5 * jnp.sum(per_row, axis=0, keepdims=True)
        return dx, dx, dg, jnp.broadcast_to(loss, (1, LANES))

    return _rowwise(name, fn, L // tm, [_rows(x, tm), _whole(g), _rows(target, tm)],
                    [_orows(L, D, F32, tm), _orows(L, D, BF16, tm), _oacc(1, D), _oacc(1, LANES)], n_acc=2)


def _glu_fwd(name, x, z, tm=256):
    L, D = x.shape
    tm = _tile(L, tm)

    def fn(i, xb, zb):
        return (xb + zb[:, :D] * jax.nn.sigmoid(zb[:, D:]),)

    return _rowwise(name, fn, L // tm, [_rows(x, tm), _rows(z, tm)], [_orows(L, D, F32, tm)])[0]


def _glu_bwd(name, dx, z, tm=256):
    L, D = dx.shape
    tm = _tile(L, tm)

    def fn(i, db, zb):
        z1, sg = zb[:, :D], jax.nn.sigmoid(zb[:, D:])
        return db * sg, db * z1 * sg * (1.0 - sg)

    d1, d2 = _rowwise(name, fn, L // tm, [_rows(dx, tm), _rows(z, tm)],
                      [_orows(L, D, BF16, tm), _orows(L, D, BF16, tm)])
    return d1, d2


def _halo_prev(a, tm):
    return (a, (8, a.shape[1]), lambda i: (jnp.maximum(i * (tm // 8) - 1, 0), 0))


def _halo_next(a, tm):
    last = a.shape[0] // 8 - 1
    return (a, (8, a.shape[1]), lambda i: (jnp.minimum((i + 1) * (tm // 8), last), 0))


def _shift_down(cur, halo, k, first):
    tm = cur.shape[0]
    rolled = pltpu.roll(cur, k, 0)
    tail = pltpu.roll(halo, k, 0)
    tail = jnp.where(first, 0.0, tail)
    row = lax.broadcasted_iota(jnp.int32, (tm, 1), 0)
    head = jnp.concatenate([tail, jnp.zeros((tm - 8, cur.shape[1]), cur.dtype)], axis=0) if tm > 8 else tail
    return jnp.where(row < k, head, rolled)


def _shift_up(cur, halo, k, last):
    tm = cur.shape[0]
    rolled = pltpu.roll(cur, tm - k, 0)
    head = pltpu.roll(halo, 8 - k, 0)
    head = jnp.where(last, 0.0, head)
    row = lax.broadcasted_iota(jnp.int32, (tm, 1), 0)
    tail = jnp.concatenate([jnp.zeros((tm - 8, cur.shape[1]), cur.dtype), head], axis=0) if tm > 8 else head
    return jnp.where(row >= tm - k, tail, rolled)


def _conv_pre(gb, hb, cw, cb, first):
    g1 = _shift_down(gb, hb, 1, first)
    g2 = _shift_down(gb, hb, 2, first)
    return cw[0:1] * g2 + cw[1:2] * g1 + cw[2:3] * gb + cb, g1, g2


def _gate_fwd(name, gu, cw, cb, tm=256):
    L, F2 = gu.shape
    Fh = F2 // 2
    tm = _tile(L, tm)

    def fn(i, gub, halo, cwb, cbb):
        gc, _, _ = _conv_pre(gub[:, :Fh], halo[:, :Fh], cwb, cbb, i == 0)
        return (gc * jax.nn.sigmoid(gc) * gub[:, Fh:],)

    return _rowwise(name, fn, L // tm, [_rows(gu, tm), _halo_prev(gu, tm), _whole(cw), _whole(cb)],
                    [_orows(L, Fh, BF16, tm)])[0]


def _gate_bwd1(name, gu, da, cw, cb, tm=256):
    L, F2 = gu.shape
    Fh = F2 // 2
    tm = _tile(L, tm)

    def fn(i, gub, halo, dab, cwb, cbb):
        gb, ub = gub[:, :Fh], gub[:, Fh:]
        gc, g1, g2 = _conv_pre(gb, halo[:, :Fh], cwb, cbb, i == 0)
        sg = jax.nn.sigmoid(gc)
        dab = dab.astype(F32)
        du = dab * gc * sg
        dgc = dab * ub * sg * (1.0 + gc * (1.0 - sg))
        return (dgc, du, jnp.sum(dgc * g2, axis=0, keepdims=True), jnp.sum(dgc * g1, axis=0, keepdims=True),
                jnp.sum(dgc * gb, axis=0, keepdims=True), jnp.sum(dgc, axis=0, keepdims=True))

    return _rowwise(name, fn, L // tm,
                    [_rows(gu, tm), _halo_prev(gu, tm), _rows(da, tm), _whole(cw), _whole(cb)],
                    [_orows(L, Fh, F32, tm), _orows(L, Fh, BF16, tm)] + [_oacc(1, Fh) for _ in range(4)], n_acc=4)


def _gate_bwd2(name, dgc, du, cw, tm=256):
    L, Fh = dgc.shape
    tm = _tile(L, tm)
    n = L // tm

    def fn(i, db, halo, dub, cwb):
        d1 = _shift_up(db, halo, 1, i == n - 1)
        d2 = _shift_up(db, halo, 2, i == n - 1)
        dg = cwb[2:3] * db + cwb[1:2] * d1 + cwb[0:1] * d2
        return (jnp.concatenate([dg.astype(BF16), dub], axis=1),)

    return _rowwise(name, fn, n, [_rows(dgc, tm), _halo_next(dgc, tm), _rows(du, tm), _whole(cw)],
                    [_orows(L, 2 * Fh, BF16, tm)])[0]


def _s5_build(a_re, a_im, log_dt, b_re, b_im, c_re, c_im, d):
    T = CHUNK
    G, P = a_re.shape
    H = d.shape[1]
    hi = lax.Precision.HIGHEST
    dt = jnp.exp(log_dt)[:, None]
    mag = jnp.exp(a_re * dt)
    ab_re = mag * jnp.cos(a_im * dt)
    ab_im = mag * jnp.sin(a_im * dt)
    den = a_re * a_re + a_im * a_im
    f_re = ((ab_re - 1.0) * a_re + ab_im * a_im) / den
    f_im = (ab_im * a_re - (ab_re - 1.0) * a_im) / den
    bb_re = f_re[..., None] * b_re - f_im[..., None] * b_im
    bb_im = f_re[..., None] * b_im + f_im[..., None] * b_re
    tau = jnp.arange(T + 1, dtype=F32)[:, None, None]
    pmag = jnp.exp(tau * (a_re * dt)[None])
    pang = tau * (a_im * dt)[None]
    pw_re = pmag * jnp.cos(pang)
    pw_im = pmag * jnp.sin(pang)
    cp_re = c_re[None] * pw_re[:, :, None, :] - c_im[None] * pw_im[:, :, None, :]
    cp_im = c_re[None] * pw_im[:, :, None, :] + c_im[None] * pw_re[:, :, None, :]
    kt = (jnp.einsum('tghp,gpk->tghk', cp_re[:T], bb_re, precision=hi)
          - jnp.einsum('tghp,gpk->tghk', cp_im[:T], bb_im, precision=hi))
    kt = kt.at[0].add(d[:, :, None] * jnp.eye(H, dtype=F32)[None])
    kp = jnp.concatenate([jnp.zeros((T - 1,) + kt.shape[1:], F32), kt], axis=0)
    kg = jnp.stack([kp[t:t + T][::-1] for t in range(T)], axis=0)
    m_mat = kg.transpose(2, 0, 3, 1, 4).reshape(G, T * H, T * H)
    rev = T - 1 - jnp.arange(T)
    pr, pi = pw_re[rev], pw_im[rev]
    n_re = pr[..., None] * bb_re[None] - pi[..., None] * bb_im[None]
    n_im = pr[..., None] * bb_im[None] + pi[..., None] * bb_re[None]
    n_mat = jnp.concatenate([n_re, n_im], axis=2).transpose(1, 2, 0, 3).reshape(G, 2 * P, T * H)
    o_mat = jnp.concatenate([cp_re[1:], -cp_im[1:]], axis=-1).transpose(1, 0, 2, 3).reshape(G, T * H, 2 * P)
    lam1 = jnp.concatenate([pw_re[T], pw_re[T]], axis=-1)[:, None, :]
    lam2 = jnp.concatenate([-pw_im[T], pw_im[T]], axis=-1)[:, None, :]
    return m_mat, n_mat, o_mat, lam1, lam2


def _chunk_scan(name, v, lam1, lam2, reverse, s_fwd=None, gb=32):
    G, nc, W = v.shape
    gb = _tile(G, gb, 1)
    half = W // 2
    ntile = nc // 8

    def body(*refs):
        if reverse:
            v_ref, l1_ref, l2_ref, s_ref, o_ref, d1_ref, d2_ref = refs
        else:
            v_ref, l1_ref, l2_ref, o_ref = refs
        l1 = jnp.broadcast_to(l1_ref[...], (gb, 8, W))
        l2 = jnp.broadcast_to(l2_ref[...], (gb, 8, W))
        if reverse:
            l2 = -l2
        row = lax.broadcasted_iota(jnp.int32, (gb, 8, W), 1)

        def tile_step(n, carry):
            if reverse:
                st, a1, a2 = carry
                base = pl.multiple_of((ntile - 1 - n) * 8, 8)
            else:
                st = carry
                base = pl.multiple_of(n * 8, 8)
            vt = v_ref[:, pl.ds(base, 8), :]
            out = jnp.zeros((gb, 8, W), F32)
            order = range(7, -1, -1) if reverse else range(8)
            for r in order:
                out = jnp.where(row == r, st, out)
                vr = jnp.broadcast_to(vt[:, r:r + 1, :], (gb, 8, W))
                st = l1 * st + l2 * pltpu.roll(st, half, 2) + vr
            o_ref[:, pl.ds(base, 8), :] = out
            if reverse:
                sv = s_ref[:, pl.ds(base, 8), :]
                a1 = a1 + out * sv
                a2 = a2 + out * pltpu.roll(sv, half, 2)
                return st, a1, a2
            return st

        zero = jnp.zeros((gb, 8, W), F32)
        if reverse:
            _, a1, a2 = lax.fori_loop(0, ntile, tile_step, (zero, zero, zero))
            d1_ref[...] = jnp.sum(a1, axis=1, keepdims=True)
            d2_ref[...] = jnp.sum(a2, axis=1, keepdims=True)
        else:
            lax.fori_loop(0, ntile, tile_step, zero)

    big = pl.BlockSpec((gb, nc, W), lambda g: (g, 0, 0))
    vec = pl.BlockSpec((gb, 1, W), lambda g: (g, 0, 0))
    if reverse:
        return _pcall(body, name=name, grid=(G // gb,), in_specs=[big, vec, vec, big],
                      out_specs=[big, vec, vec],
                      out_shape=[jax.ShapeDtypeStruct((G, nc, W), F32), jax.ShapeDtypeStruct((G, 1, W), F32),
                                 jax.ShapeDtypeStruct((G, 1, W), F32)],
                      compiler_params=_params(("parallel",)))(v, lam1, lam2, s_fwd)
    return _pcall(body, name=name, grid=(G // gb,), in_specs=[big, vec, vec], out_specs=big,
                  out_shape=jax.ShapeDtypeStruct((G, nc, W), F32),
                  compiler_params=_params(("parallel",)))(v, lam1, lam2)


_GELU_C = math.sqrt(2.0 / math.pi)


def _gelu(y):
    return 0.5 * y * (1.0 + jnp.tanh(_GELU_C * (y + 0.044715 * y * y * y)))


def _gelu_grad(y):
    t = jnp.tanh(_GELU_C * (y + 0.044715 * y * y * y))
    return 0.5 * (1.0 + t) + 0.5 * y * (1.0 - t * t) * _GELU_C * (1.0 + 3.0 * 0.044715 * y * y)


def _to_groups(a, G, Hg):
    L = a.shape[0]
    return a.reshape(L // CHUNK, CHUNK, G, Hg).transpose(2, 0, 1, 3).reshape(G, L // CHUNK, CHUNK * Hg)


def _from_groups(a, Hg):
    G, nc, _ = a.shape
    return a.reshape(G, nc, CHUNK, Hg).transpose(1, 2, 0, 3).reshape(nc * CHUNK, G * Hg)


def _split_dot(x, u):
    hi = x.astype(BF16)
    lo = (x - hi.astype(F32)).astype(BF16)
    return (jnp.dot(hi, u, preferred_element_type=F32) + jnp.dot(lo, u, preferred_element_type=F32))


def _col_to_row(col, n):
    return jnp.broadcast_to(col, (n, LANES)).T[0:1, :]


def _row_to_col(row, n):
    return jnp.broadcast_to(row, (LANES, n)).T[:, 0:1]


def _attn_fwd(name, qw, kv, tq, tk, hb):
    H, L, dw = qw.shape
    nq, nkb = L // tq, L // tk
    hs = range(hb)
    assert tk == 2 * tq

    def body(q_ref, kv_ref, o_ref, rc_ref):
        i = pl.program_id(1)
        diff = (lax.broadcasted_iota(jnp.int32, (tq, tk), 1) - lax.broadcasted_iota(jnp.int32, (tq, tk), 0))
        u_suf = (lax.broadcasted_iota(jnp.int32, (tk, tk), 0)
                 > lax.broadcasted_iota(jnp.int32, (tk, tk), 1)).astype(BF16)
        nfull = i // 2
        rc_ref[...] = jnp.zeros_like(rc_ref)

        def block(kb, carry, masked):
            runs, accs = carry
            ks = pl.multiple_of(kb * tk, tk)
            kvb = [kv_ref[h, pl.ds(ks, tk), :] for h in hs]
            z = [lax.dot_general(q_ref[h], kvb[h], (((1,), (1,)), ((), ())), preferred_element_type=F32)
                 for h in hs]
            e = [jnp.exp(-jnp.abs(z[h])) for h in hs]
            sp = [jnp.maximum(z[h], 0.0) + jnp.log(1.0 + e[h]) for h in hs]
            if masked:
                causal = diff < (i * tq - ks)
                lom = [jnp.where(causal, -sp[h], 0.0) for h in hs]
            else:
                lom = [-sp[h] for h in hs]
            rem = [runs[h] + _split_dot(lom[h], u_suf) for h in hs]
            w = [jnp.exp(z[h] - sp[h] + rem[h]) for h in hs]
            if masked:
                w = [jnp.where(causal, w[h], 0.0) for h in hs]
            accs = tuple(accs[h] + jnp.dot(w[h].astype(BF16), kvb[h], preferred_element_type=F32) for h in hs)
            for h in hs:
                rc_ref[h, kb] = _col_to_row(runs[h], tq)
            runs = tuple(runs[h] + jnp.sum(lom[h], axis=1, keepdims=True) for h in hs)
            return runs, accs

        init = (tuple(jnp.zeros((tq, 1), F32) for _ in hs), tuple(jnp.zeros((tq, dw), F32) for _ in hs))
        carry = block(nfull, init, True)
        _, accs = lax.fori_loop(0, nfull, lambda n, c: block(nfull - 1 - n, c, False), carry)
        for h in hs:
            o_ref[h] = accs[h].astype(o_ref.dtype)

    qspec = pl.BlockSpec((hb, tq, dw), lambda h, i: (h, i, 0))
    kspec = pl.BlockSpec((hb, L, dw), lambda h, i: (h, 0, 0))
    return _pcall(
        body, name=name, grid=(H // hb, nq), in_specs=[qspec, kspec],
        out_specs=[qspec, pl.BlockSpec((hb, None, nkb, 1, tq), lambda h, i: (h, i, 0, 0, 0))],
        out_shape=[jax.ShapeDtypeStruct((H, L, dw), BF16), jax.ShapeDtypeStruct((H, nq, nkb, 1, tq), F32)],
        compiler_params=_params(("parallel", "arbitrary")),
    )(qw, kv)


def _attn_bwd(name, qw, kv, dow, rc, tq, tk, scale, hb):
    H, L, dw_ = qw.shape
    nq, nkb = L // tq, L // tk
    hs = range(hb)
    assert tk == 2 * tq

    def body(q_ref, kv_ref, do_ref, rc_ref, dq_ref, dkv_ref):
        i = pl.program_id(1)

        @pl.when(i == 0)
        def _():
            dkv_ref[...] = jnp.zeros_like(dkv_ref)

        diff = (lax.broadcasted_iota(jnp.int32, (tq, tk), 1) - lax.broadcasted_iota(jnp.int32, (tq, tk), 0))
        r_io = lax.broadcasted_iota(jnp.int32, (tk, tk), 0)
        c_io = lax.broadcasted_iota(jnp.int32, (tk, tk), 1)
        u_suf = (r_io > c_io).astype(BF16)
        u_pre = (r_io < c_io).astype(BF16)
        nfull = i // 2

        def block(kb, carry, masked):
            pres, dqs = carry
            ks = pl.multiple_of(kb * tk, tk)
            kvb = [kv_ref[h, pl.ds(ks, tk), :] for h in hs]
            qb = [q_ref[h] for h in hs]
            dob = [do_ref[h] for h in hs]
            z = [lax.dot_general(qb[h], kvb[h], (((1,), (1,)), ((), ())), preferred_element_type=F32) for h in hs]
            dw = [lax.dot_general(dob[h], kvb[h], (((1,), (1,)), ((), ())), preferred_element_type=F32)
                  for h in hs]
            e = [jnp.exp(-jnp.abs(z[h])) for h in hs]
            sp = [jnp.maximum(z[h], 0.0) + jnp.log(1.0 + e[h]) for h in hs]
            if masked:
                causal = diff < (i * tq - ks)
                lom = [jnp.where(causal, -sp[h], 0.0) for h in hs]
            else:
                lom = [-sp[h] for h in hs]
            rem = [_row_to_col(rc_ref[h, kb], tq) + _split_dot(lom[h], u_suf) for h in hs]
            logb = [z[h] - sp[h] for h in hs]
            w = [jnp.exp(logb[h] + rem[h]) for h in hs]
            if masked:
                w = [jnp.where(causal, w[h], 0.0) for h in hs]
            da = [dw[h] * w[h] for h in hs]
            p = [pres[h] + _split_dot(da[h], u_pre) for h in hs]
            dz = [da[h] - jnp.exp(logb[h]) * (da[h] + p[h]) for h in hs]
            if masked:
                dz = [jnp.where(causal, dz[h], 0.0) for h in hs]
            dqs = tuple(dqs[h] + jnp.dot(dz[h].astype(BF16), kvb[h], preferred_element_type=F32) for h in hs)
            for h in hs:
                lhs = jnp.concatenate([dz[h].T, w[h].T], axis=1).astype(BF16)
                rhs = jnp.concatenate([qb[h], dob[h]], axis=0)
                dkv_ref[h, pl.ds(ks, tk), :] += jnp.dot(lhs, rhs, preferred_element_type=F32)
            pres = tuple(pres[h] + jnp.sum(da[h], axis=1, keepdims=True) for h in hs)
            return pres, dqs

        init = (tuple(jnp.zeros((tq, 1), F32) for _ in hs), tuple(jnp.zeros((tq, dw_), F32) for _ in hs))
        carry = lax.fori_loop(0, nfull, lambda kb, c: block(kb, c, False), init)
        _, dqs = block(nfull, carry, True)
        for h in hs:
            dq_ref[h] = (dqs[h] * scale).astype(dq_ref.dtype)

    qspec = pl.BlockSpec((hb, tq, dw_), lambda h, i: (h, i, 0))
    kspec = pl.BlockSpec((hb, L, dw_), lambda h, i: (h, 0, 0))
    return _pcall(
        body, name=name, grid=(H // hb, nq),
        in_specs=[qspec, kspec, qspec, pl.BlockSpec((hb, None, nkb, 1, tq), lambda h, i: (h, i, 0, 0, 0))],
        out_specs=[qspec, kspec],
        out_shape=[jax.ShapeDtypeStruct((H, L, dw_), BF16), jax.ShapeDtypeStruct((H, L, dw_), F32)],
        compiler_params=_params(("parallel", "arbitrary")),
    )(qw, kv, dow, rc)


def _heads(a):
    L, D = a.shape
    return a.reshape(L, D // HEAD_DIM, HEAD_DIM).transpose(1, 0, 2)


def _unheads(a):
    H, L, dh = a.shape
    return a.transpose(1, 0, 2).reshape(L, H * dh)


_MESH = pl.DeviceIdType.MESH
_HBM = pl.BlockSpec(memory_space=pltpu.HBM)


def _all_gather(name, shard):
    R, C = shard.shape

    def body(x_ref, out_ref, send_sems, recv_sems, local_sem):
        x, y, c = lax.axis_index("x"), lax.axis_index("y"), lax.axis_index("c")
        me, sibling = (x, y, c), (x, y, 1 - c)
        chips = [(1 - x, y), (x, 1 - y), (1 - x, 1 - y)]

        def slot(px, py, pc):
            return out_ref.at[4 * px + 2 * py + pc]

        def copy(k, block, to, src=None):
            return pltpu.make_async_remote_copy(
                src_ref=slot(*block) if src is None else src, dst_ref=slot(*block),
                send_sem=send_sems.at[k], recv_sem=recv_sems.at[k], device_id=to, device_id_type=_MESH)

        mine = pltpu.make_async_copy(x_ref, slot(*me), local_sem)
        mine.start()
        first = [copy(0, me, sibling, src=x_ref)]
        first += [copy(1 + j, me, (*chip, c), src=x_ref) for j, chip in enumerate(chips)]
        for cp in first:
            cp.start()
        passed = [copy(4 + j, (*chip, c), sibling) for j, chip in enumerate(chips)]
        for j, chip in enumerate(chips):
            copy(1 + j, (*chip, c), me).wait_recv()
            passed[j].start()
        copy(0, sibling, me).wait_recv()
        for j, chip in enumerate(chips):
            copy(4 + j, (*chip, 1 - c), me).wait_recv()
        for cp in first + passed:
            cp.wait_send()
        mine.wait()

    return _pcall(
        body, name=name, out_shape=jax.ShapeDtypeStruct((N_DEV, R, C), shard.dtype),
        in_specs=[_HBM], out_specs=_HBM,
        scratch_shapes=[pltpu.SemaphoreType.DMA((7,)), pltpu.SemaphoreType.DMA((7,)), pltpu.SemaphoreType.DMA],
    )(shard)


def _sibling_exchange(name, x):
    def body(x_ref, out_ref, send_sem, recv_sem):
        sibling = (lax.axis_index("x"), lax.axis_index("y"), 1 - lax.axis_index("c"))
        cp = pltpu.make_async_remote_copy(src_ref=x_ref, dst_ref=out_ref, send_sem=send_sem, recv_sem=recv_sem,
                                          device_id=sibling, device_id_type=_MESH)
        cp.start()
        cp.wait()

    return _pcall(
        body, name=name, out_shape=jax.ShapeDtypeStruct(x.shape, x.dtype), in_specs=[_HBM], out_specs=_HBM,
        scratch_shapes=[pltpu.SemaphoreType.DMA, pltpu.SemaphoreType.DMA],
    )(x)


def _chip_exchange(name, x):
    _, R, C = x.shape

    def body(x_ref, out_ref, send_sems, recv_sems):
        mx, my, mc = lax.axis_index("x"), lax.axis_index("y"), lax.axis_index("c")
        chips = [(1 - mx, my), (mx, 1 - my), (1 - mx, 1 - my)]
        cps = [pltpu.make_async_remote_copy(src_ref=x_ref.at[2 * px + py], dst_ref=out_ref.at[j],
                                            send_sem=send_sems.at[j], recv_sem=recv_sems.at[j],
                                            device_id=(px, py, mc), device_id_type=_MESH)
               for j, (px, py) in enumerate(chips)]
        for cp in cps:
            cp.start()
        for cp in cps:
            cp.wait()

    return _pcall(
        body, name=name, out_shape=jax.ShapeDtypeStruct((3, R, C), x.dtype), in_specs=[_HBM], out_specs=_HBM,
        scratch_shapes=[pltpu.SemaphoreType.DMA((3,)), pltpu.SemaphoreType.DMA((3,))],
    )(x)


def _pair_sum(name, a, b):
    nb, R, C = a.shape
    tm = _tile(R, 512, 16)

    def fn(i, ab, bb):
        s = ab.astype(F32) + bb.astype(F32)
        return s, s

    spec = lambda arr: (arr, (1, tm, C), lambda i: (i // (R // tm), i % (R // tm), 0))
    out = lambda dt: ((nb, R, C), dt, (1, tm, C), lambda i: (i // (R // tm), i % (R // tm), 0))
    return _rowwise(name, fn, nb * (R // tm), [spec(a), spec(b)], [out(F32), out(BF16)])


def _sum8(name, g):
    n, R, C = g.shape
    tm = _tile(R, 256)

    def fn(i, gb):
        s = gb[0]
        for d in range(1, n):
            s = s + gb[d]
        return (s,)

    return _rowwise(name, fn, R // tm, [(g, (n, tm, C), lambda i: (0, i, 0))], [_orows(R, C, F32, tm)])[0]


def _adamw(name, w, m, v, grads, tm=512):
    R, C = w.shape
    tm = _tile(R, tm, 16)
    ng = len(grads)

    def fn(i, wb, mb, vb, *gs):
        g = gs[0].astype(F32)
        for t in gs[1:]:
            g = g + t.astype(F32)
        mn = ADAM_B1 * mb + (1.0 - ADAM_B1) * g
        vn = ADAM_B2 * vb + (1.0 - ADAM_B2) * (g * g)
        m_hat = mn / (1.0 - ADAM_B1 ** ADAM_STEP)
        v_hat = vn / (1.0 - ADAM_B2 ** ADAM_STEP)
        delta = -ADAM_LR * (m_hat / (jnp.sqrt(v_hat) + ADAM_EPS) + ADAM_WD * wb)
        return g, delta, mn, vn

    ins = [_rows(w, tm), _rows(m, tm), _rows(v, tm)] + [_rows(g, tm) for g in grads]
    return _rowwise(name, fn, R // tm, ins, [_orows(R, C, F32, tm) for _ in range(4)])


def _big_rows(D, Fh):
    return {'ssm_w_in': D // 8, 'ssm_w_glu': D // 4, 'kv_w': D // 4, 'attn_w_q': D // 8, 'attn_w_o': D // 8,
            'ffn_w_up': Fh // 2, 'ffn_w_down': Fh // 4, 'ffn_conv_w': CONV_ROWS}


def _pack_shards(d, D, wire=False):
    parts = []
    for n in BIG:
        a = d[n]
        if n == 'ffn_conv_w':
            flat = a.reshape(-1)
            if wire:
                flat = lax.bitcast_convert_type(flat, BF16).reshape(-1)
            a = jnp.pad(flat, (0, CONV_ROWS * D - flat.shape[0]))
        elif wire:
            a = a.astype(BF16)
        parts.append(a.reshape(-1, D))
    return jnp.concatenate(parts, axis=0)


def _unpack_shards(pack, shapes, D, Fh):
    rows = _big_rows(D, Fh)
    out, off = {}, 0
    for n in BIG:
        seg = pack[off:off + rows[n]]
        off += rows[n]
        size = math.prod(shapes[n])
        out[n] = seg.reshape(-1)[:size].reshape(shapes[n])
    return out


def _full_weights(gath, D, Fh):
    rows = _big_rows(D, Fh)
    seg, off = {}, 0
    for n in BIG:
        seg[n] = gath[:, off:off + rows[n]]
        off += rows[n]
    n8 = N_DEV
    w = {}
    w['ssm_w_in'] = seg['ssm_w_in'].reshape(D, D)
    w['attn_w_q'] = seg['attn_w_q'].reshape(D, D)
    w['attn_w_o'] = seg['attn_w_o'].reshape(D, D)
    w['ssm_w_glu'] = seg['ssm_w_glu'].reshape(n8, D, 2 * D // n8).transpose(1, 0, 2).reshape(D, 2 * D)
    w['kv_w'] = seg['kv_w'].reshape(n8, D, 2 * D // n8).transpose(1, 0, 2).reshape(D, 2 * D)
    w['ffn_w_up'] = seg['ffn_w_up'].reshape(n8, 2, D, 2 * Fh // n8).transpose(1, 2, 0, 3).reshape(2, D, 2 * Fh)
    w['ffn_w_down'] = seg['ffn_w_down'].reshape(n8, 2, Fh // n8, D).transpose(1, 0, 2, 3).reshape(2, Fh, D)
    cw = seg['ffn_conv_w'].reshape(n8, CONV_ROWS * D)[:, :12 * Fh // n8].reshape(n8, 6 * Fh // n8, 2)
    cw = lax.bitcast_convert_type(cw, F32)
    w['ffn_conv_w'] = cw.reshape(n8, 2, 3, Fh // n8).transpose(1, 2, 0, 3).reshape(2, 3, Fh)
    return w


def _grad_pack(g, D, Fh):
    n8 = N_DEV
    parts = {
        'ssm_w_in': g['ssm_w_in'].reshape(n8, D // n8, D),
        'attn_w_q': g['attn_w_q'].reshape(n8, D // n8, D),
        'attn_w_o': g['attn_w_o'].reshape(n8, D // n8, D),
        'ssm_w_glu': g['ssm_w_glu'].reshape(D, n8, 2 * D // n8).transpose(1, 0, 2).reshape(n8, D // 4, D),
        'kv_w': g['kv_w'].reshape(D, n8, 2 * D // n8).transpose(1, 0, 2).reshape(n8, D // 4, D),
        'ffn_w_up': g['ffn_w_up'].reshape(2, D, n8, 2 * Fh // n8).transpose(2, 0, 1, 3).reshape(n8, Fh // 2, D),
        'ffn_w_down': g['ffn_w_down'].reshape(2, n8, Fh // n8, D).transpose(1, 0, 2, 3).reshape(n8, Fh // 4, D),
    }
    cw = g['ffn_conv_w'].reshape(2, 3, n8, Fh // n8).transpose(2, 0, 1, 3).reshape(n8, 6 * Fh // n8)
    parts['ffn_conv_w'] = jnp.pad(cw, ((0, 0), (0, CONV_ROWS * D - cw.shape[1]))).reshape(n8, CONV_ROWS, D)
    return jnp.concatenate([parts[n] for n in BIG], axis=1)


def _small_layout(shapes, D):
    lay, off = {}, 0
    for n in SMALL:
        r = -(-math.prod(shapes[n]) // D)
        lay[n] = (off, r)
        off += r
    lay['loss'] = (off, 1)
    off += 1
    return lay, -(-off // 8) * 8


def _pack_small(d, lay, total, D):
    parts = []
    for n in SMALL + ['loss']:
        if n not in d:
            parts.append(jnp.zeros((lay[n][1], D), F32))
            continue
        flat = d[n].reshape(-1).astype(F32)
        parts.append(jnp.pad(flat, (0, lay[n][1] * D - flat.shape[0])).reshape(lay[n][1], D))
    used = sum(lay[n][1] for n in SMALL + ['loss'])
    if total > used:
        parts.append(jnp.zeros((total - used, D), F32))
    return jnp.concatenate(parts, axis=0)


def _unpack_small(pack, lay, shapes, D):
    out = {}
    for n in SMALL:
        off, r = lay[n]
        out[n] = pack[off:off + r].reshape(-1)[:math.prod(shapes[n])].reshape(shapes[n])
    return out


def _ffn_fwd(tag, x, g_norm, w_up, conv_w, conv_b, w_down):
    h = _rms_fwd(f"{tag}_norm", x, g_norm)
    gu = _mm(f"{tag}_up", h, w_up, 'nn', F32)
    a = _gate_fwd(f"{tag}_gate", gu, conv_w, conv_b)
    return _mm(f"{tag}_down", a, w_down, 'nn', F32, resid=x), (h, gu, a)


def _ffn_bwd(tag, dres, dres_bf, x, saved, g_norm, w_up, conv_w, conv_b, w_down):
    h, gu, a = saved
    da = _mm(f"{tag}_dgate", dres_bf, w_down, 'nt', F32)
    d_w_down = _mm(f"{tag}_dwdown", a, dres_bf, 'tn', F32)
    dgc, du, dw0, dw1, dw2, dcb = _gate_bwd1(f"{tag}_gate_b1", gu, da, conv_w, conv_b)
    dgu = _gate_bwd2(f"{tag}_gate_b2", dgc, du, conv_w)
    d_w_up = _mm(f"{tag}_dwup", h, dgu, 'tn', F32)
    dh = _mm(f"{tag}_dh", dgu, w_up, 'nt', F32)
    dres, dres_bf, dg = _rms_bwd(f"{tag}_norm_b", x, dres, [(g_norm, dh)])
    return dres, dres_bf, dg, d_w_up, jnp.concatenate([dw0, dw1, dw2], axis=0), dcb[0], d_w_down


def kernel(x, norm_mix, norm_ffn, norm_kv, norm_final, ssm_w_in, ssm_a_re, ssm_a_im, ssm_log_dt, ssm_b_re, ssm_b_im, ssm_c_re, ssm_c_im, ssm_d, ssm_w_glu, kv_w, attn_w_q, attn_w_o, ffn_w_up, ffn_conv_w, ffn_conv_b, ffn_w_down, loss_target, m_norm_mix, m_norm_ffn, m_norm_kv, m_norm_final, m_ssm_w_in, m_ssm_a_re, m_ssm_a_im, m_ssm_log_dt, m_ssm_b_re, m_ssm_b_im, m_ssm_c_re, m_ssm_c_im, m_ssm_d, m_ssm_w_glu, m_kv_w, m_attn_w_q, m_attn_w_o, m_ffn_w_up, m_ffn_conv_w, m_ffn_conv_b, m_ffn_w_down, v_norm_mix, v_norm_ffn, v_norm_kv, v_norm_final, v_ssm_w_in, v_ssm_a_re, v_ssm_a_im, v_ssm_log_dt, v_ssm_b_re, v_ssm_b_im, v_ssm_c_re, v_ssm_c_im, v_ssm_d, v_ssm_w_glu, v_kv_w, v_attn_w_q, v_attn_w_o, v_ffn_w_up, v_ffn_conv_w, v_ffn_conv_b, v_ffn_w_down):
    wts = dict(zip(W_NAMES, (norm_mix, norm_ffn, norm_kv, norm_final, ssm_w_in, ssm_a_re, ssm_a_im, ssm_log_dt,
                             ssm_b_re, ssm_b_im, ssm_c_re, ssm_c_im, ssm_d, ssm_w_glu, kv_w, attn_w_q, attn_w_o,
                             ffn_w_up, ffn_conv_w, ffn_conv_b, ffn_w_down)))
    mom = dict(zip(W_NAMES, (m_norm_mix, m_norm_ffn, m_norm_kv, m_norm_final, m_ssm_w_in, m_ssm_a_re, m_ssm_a_im,
                             m_ssm_log_dt, m_ssm_b_re, m_ssm_b_im, m_ssm_c_re, m_ssm_c_im, m_ssm_d, m_ssm_w_glu,
                             m_kv_w, m_attn_w_q, m_attn_w_o, m_ffn_w_up, m_ffn_conv_w, m_ffn_conv_b, m_ffn_w_down)))
    vel = dict(zip(W_NAMES, (v_norm_mix, v_norm_ffn, v_norm_kv, v_norm_final, v_ssm_w_in, v_ssm_a_re, v_ssm_a_im,
                             v_ssm_log_dt, v_ssm_b_re, v_ssm_b_im, v_ssm_c_re, v_ssm_c_im, v_ssm_d, v_ssm_w_glu,
                             v_kv_w, v_attn_w_q, v_attn_w_o, v_ffn_w_up, v_ffn_conv_w, v_ffn_conv_b, v_ffn_w_down)))
    shapes = {n: wts[n].shape for n in W_NAMES}
    _, L, D = x.shape
    Fh = ffn_conv_b.shape[1]
    G, P = ssm_a_re.shape[1], ssm_a_re.shape[2]
    Hg = ssm_d.shape[2]
    x0 = x[0]
    target = loss_target[0]
    scale = HEAD_DIM ** -0.5

    gath = _all_gather("gather_weights", _pack_shards(wts, D, wire=True))
    w = _full_weights(gath, D, Fh)
    nm = [norm_mix[l:l + 1] for l in range(2)]
    nf = [norm_ffn[l:l + 1] for l in range(2)]
    nkv = norm_kv[None]
    nfin = norm_final[None]
    cb = [ffn_conv_b[l:l + 1] for l in range(2)]
    cwt = [w['ffn_conv_w'][l] for l in range(2)]

    s5p = (ssm_a_re[0], ssm_a_im[0], ssm_log_dt[0], ssm_b_re[0], ssm_b_im[0], ssm_c_re[0], ssm_c_im[0], ssm_d[0])
    (m_mat, n_mat, o_mat, lam1, lam2), s5_vjp = jax.vjp(_s5_build, *s5p)
    m_bf, n_bf, o_bf = m_mat.astype(BF16), n_mat.astype(BF16), o_mat.astype(BF16)

    h0 = _rms_fwd("l0_norm", x0, nm[0])
    u = _mm("l0_win", h0, w['ssm_w_in'], 'nn', BF16)
    ug = _to_groups(u, G, Hg)
    vloc = _bmm("s5_local_state", [(ug, n_bf, 'nt')], F32)
    st = _chunk_scan("s5_scan", vloc, lam1, lam2, reverse=False)
    yraw, yg_g = _bmm("s5_out", [(ug, m_bf, 'nt'), (st, o_bf, 'nt')], (F32, BF16),
                      post=lambda acc: (acc, _gelu(acc)))
    yg = _from_groups(yg_g, Hg)
    z = _mm("l0_wglu", yg, w['ssm_w_glu'], 'nn', F32)
    x1 = _glu_fwd("l0_glu", x0, z)
    x2, ffn0 = _ffn_fwd("f0", x1, nf[0], w['ffn_w_up'][0], cwt[0], cb[0], w['ffn_w_down'][0])

    hkv = _rms_fwd("kv_norm", x2, nkv)
    kvp = _mm("kv_proj", hkv, w['kv_w'], 'nn', BF16)
    h2 = _rms_fwd("l1_norm", x2, nm[1])
    qn = _mm("l1_wq", h2, w['attn_w_q'], 'nn', BF16, scale=scale)
    nh = D // HEAD_DIM
    zeros_h = jnp.zeros((nh, L, HEAD_DIM), BF16)
    qw = jnp.concatenate([_heads(qn), zeros_h], axis=-1)
    kv = kvp.reshape(L, 2, nh, HEAD_DIM).transpose(2, 0, 1, 3).reshape(nh, L, 2 * HEAD_DIM)
    ow, rc = _attn_fwd("attn_fwd", qw, kv, ATT_TQ, ATT_TK, ATT_HB_FWD)
    o = _unheads(ow[:, :, HEAD_DIM:])
    x3 = _mm("l1_wo", o, w['attn_w_o'], 'nn', F32, resid=x2)
    x4, ffn1 = _ffn_fwd("f1", x3, nf[1], w['ffn_w_up'][1], cwt[1], cb[1], w['ffn_w_down'][1])

    dres, dres_bf, dg_final, loss_part = _final_loss("loss_head", x4, nfin, target)

    gw = {}
    dres, dres_bf, dg_nf1, gup1, gcw1, gcb1, gdn1 = _ffn_bwd("f1", dres, dres_bf, x3, ffn1, nf[1], w['ffn_w_up'][1],
                                                             cwt[1], cb[1], w['ffn_w_down'][1])
    do = _mm("l1_do", dres_bf, w['attn_w_o'], 'nt', BF16)
    gw['attn_w_o'] = _mm("l1_dwo", o, dres_bf, 'tn', F32)
    dow = jnp.concatenate([zeros_h, _heads(do)], axis=-1)
    dqw, dkv_h = _attn_bwd("attn_bwd", qw, kv, dow, rc, ATT_TQ, ATT_TK, scale, ATT_HB_BWD)
    dq = _unheads(dqw[:, :, :HEAD_DIM])
    dkv = dkv_h.reshape(nh, L, 2, HEAD_DIM).transpose(1, 2, 0, 3).reshape(L, 2 * D).astype(BF16)
    gw['attn_w_q'] = _mm("l1_dwq", h2, dq, 'tn', F32)
    dh2 = _mm("l1_dh", dq, w['attn_w_q'], 'nt', F32)
    gw['kv_w'] = _mm("kv_dw", hkv, dkv, 'tn', F32)
    dhkv = _mm("kv_dh", dkv, w['kv_w'], 'nt', F32)
    dres, dres_bf, dg_nm1, dg_nkv = _rms_bwd("l1_norm_b", x2, dres, [(nm[1], dh2), (nkv, dhkv)])
    dres, dres_bf, dg_nf0, gup0, gcw0, gcb0, gdn0 = _ffn_bwd("f0", dres, dres_bf, x1, ffn0, nf[0], w['ffn_w_up'][0],
                                                             cwt[0], cb[0], w['ffn_w_down'][0])
    dz1, dz2 = _glu_bwd("l0_glu_b", dres, z)
    dz = jnp.concatenate([dz1, dz2], axis=1)
    gw['ssm_w_glu'] = _mm("l0_dwglu", yg, dz, 'tn', F32)
    dyg = _mm("l0_dyg", dz, w['ssm_w_glu'], 'nt', F32)
    dyg_g = _to_groups(dyg, G, Hg)
    gshape = yraw.shape
    dy = _rowwise("s5_gelu_b", lambda i, a, b: (a * _gelu_grad(b),), G,
                  [(dyg_g, (1,) + gshape[1:], lambda i: (i, 0, 0)), (yraw, (1,) + gshape[1:], lambda i: (i, 0, 0))],
                  [(gshape, BF16, (1,) + gshape[1:], lambda i: (i, 0, 0))])[0]
    ds = _bmm("s5_dstate", [(dy, o_bf, 'nn')], F32)
    dv_loc, dlam1, dlam2 = _chunk_scan("s5_scan_b", ds, lam1, lam2, reverse=True, s_fwd=st)
    du_g = _bmm("s5_du", [(dy, m_bf, 'nn'), (dv_loc, n_bf, 'nn')], BF16)
    d_m = _bmm("s5_dm", [(dy, ug, 'tn')], F32)
    d_o = _bmm("s5_do", [(dy, st, 'tn')], F32)
    d_n = _bmm("s5_dn", [(dv_loc, ug, 'tn')], F32)
    s5g = s5_vjp((d_m, d_n, d_o, dlam1, dlam2))
    du = _from_groups(du_g, Hg)
    gw['ssm_w_in'] = _mm("l0_dwin", h0, du, 'tn', F32)
    dh0 = _mm("l0_dh", du, w['ssm_w_in'], 'nt', F32)
    grad_x, _, dg_nm0 = _rms_bwd("l0_norm_b", x0, dres, [(nm[0], dh0)])

    gw['ffn_w_up'] = jnp.stack([gup0, gup1])
    gw['ffn_w_down'] = jnp.stack([gdn0, gdn1])
    gw['ffn_conv_w'] = jnp.stack([gcw0, gcw1])

    small_g = {
        'norm_mix': jnp.concatenate([dg_nm0, dg_nm1], axis=0), 'norm_ffn': jnp.concatenate([dg_nf0, dg_nf1], axis=0),
        'norm_kv': dg_nkv, 'norm_final': dg_final, 'ffn_conv_b': jnp.stack([gcb0, gcb1]),
        'ssm_a_re': s5g[0], 'ssm_a_im': s5g[1], 'ssm_log_dt': s5g[2], 'ssm_b_re': s5g[3], 'ssm_b_im': s5g[4],
        'ssm_c_re': s5g[5], 'ssm_c_im': s5g[6], 'ssm_d': s5g[7], 'loss': loss_part[0, 0:1],
    }
    lay, rs = _small_layout(shapes, D)
    small_sum = _sum8("small_sum", _all_gather("gather_small", _pack_small(small_g, lay, rs, D)))
    loss = small_sum[lay['loss'][0], 0]
    sg, sdelta, sm, sv = _adamw("adamw_small", _pack_small(wts, lay, rs, D), _pack_small(mom, lay, rs, D),
                                _pack_small(vel, lay, rs, D), [small_sum])
    small_out = [_unpack_small(t, lay, shapes, D) for t in (sg, sdelta, sm, sv)]

    gp = _grad_pack(gw, D, Fh).astype(BF16)
    rows = gp.shape[1]
    gp = gp.reshape(4, 2, rows, D)
    c = lax.axis_index("c")
    chip = 2 * lax.axis_index("x") + lax.axis_index("y")
    mine = lax.dynamic_index_in_dim(gp, c, axis=1, keepdims=False)
    theirs = lax.dynamic_index_in_dim(gp, 1 - c, axis=1, keepdims=False)
    got = _sibling_exchange("rs_sibling", theirs)
    chip_f32, chip_bf = _pair_sum("rs_pair_sum", mine, got)
    from_chips = _chip_exchange("rs_chips", chip_bf)
    own = lax.dynamic_index_in_dim(chip_f32, chip, axis=0, keepdims=False)
    bg, bdelta, bm, bv = _adamw("adamw_big", _pack_shards(wts, D), _pack_shards(mom, D), _pack_shards(vel, D),
                                [own, from_chips[0], from_chips[1], from_chips[2]])
    big_out = [_unpack_shards(t, shapes, D, Fh) for t in (bg, bdelta, bm, bv)]

    outs = [loss, grad_x[None]]
    for k in range(4):
        for n in W_NAMES:
            outs.append(big_out[k][n] if n in BIG else small_out[k][n])
    return tuple(outs)
```

```python
import math

import jax
import jax.numpy as jnp
from jax import lax
from jax.experimental import pallas as pl
from jax.experimental.pallas import tpu as pltpu

F32 = jnp.float32
BF16 = jnp.bfloat16

EPS = 1e-6
HEAD_DIM = 64
CHUNK = 16
N_DEV = 8
ADAM_LR = 0.001
ADAM_B1 = 0.9
ADAM_B2 = 0.999
ADAM_EPS = 1e-08
ADAM_WD = 0.01
ADAM_STEP = 10
VMEM_LIMIT = 48 * 1024 * 1024
LANES = 128
ATT_TQ = 128
ATT_TK = 256
ATT_HB_FWD = 4
ATT_HB_BWD = 4

W_NAMES = ['norm_mix', 'norm_ffn', 'norm_kv', 'norm_final', 'ssm_w_in', 'ssm_a_re', 'ssm_a_im', 'ssm_log_dt',
           'ssm_b_re', 'ssm_b_im', 'ssm_c_re', 'ssm_c_im', 'ssm_d', 'ssm_w_glu', 'kv_w', 'attn_w_q', 'attn_w_o',
           'ffn_w_up', 'ffn_conv_w', 'ffn_conv_b', 'ffn_w_down']
BIG = ['ssm_w_in', 'ssm_w_glu', 'kv_w', 'attn_w_q', 'attn_w_o', 'ffn_w_up', 'ffn_w_down', 'ffn_conv_w']
SMALL = [n for n in W_NAMES if n not in BIG]
CONV_ROWS = 32


def _pcall(body, **kw):
    return pl.pallas_call(body, **kw)


def _params(sem=None):
    if sem is None:
        return pltpu.CompilerParams(vmem_limit_bytes=VMEM_LIMIT)
    return pltpu.CompilerParams(dimension_semantics=sem, vmem_limit_bytes=VMEM_LIMIT)


def _tile(n, pref, mult=8):
    best = None
    for t in range(mult, min(n, pref) + 1, mult):
        if n % t == 0:
            best = t
    return n if best is None else best


def _mm(name, a, b, mode, out_dtype=F32, tm=512, tn=512, scale=None, resid=None):
    if mode == 'nn':
        (M, K), (K2, N) = a.shape, b.shape
    elif mode == 'nt':
        (M, K), (N, K2) = a.shape, b.shape
    else:
        (K, M), (K2, N) = a.shape, b.shape
    assert K == K2, (name, a.shape, b.shape)
    tm, tn = _tile(M, tm, LANES), _tile(N, tn, LANES)
    if mode == 'tn':
        a_spec = pl.BlockSpec((K, tm), lambda i, j: (0, i))
    else:
        a_spec = pl.BlockSpec((tm, K), lambda i, j: (i, 0))
    if mode == 'nt':
        b_spec = pl.BlockSpec((tn, K), lambda i, j: (j, 0))
    else:
        b_spec = pl.BlockSpec((K, tn), lambda i, j: (0, j))
    o_spec = pl.BlockSpec((tm, tn), lambda i, j: (i, j))
    dn = {'nn': ((1,), (0,)), 'nt': ((1,), (1,)), 'tn': ((0,), (0,))}[mode]

    def body(*refs):
        a_ref, b_ref, o_ref = refs[0], refs[1], refs[-1]
        acc = lax.dot_general(a_ref[...].astype(BF16), b_ref[...].astype(BF16), (dn, ((), ())),
                              preferred_element_type=F32)
        if scale is not None:
            acc = acc * scale
        if resid is not None:
            acc = acc + refs[2][...]
        o_ref[...] = acc.astype(o_ref.dtype)

    return _pcall(
        body, name=name, grid=(M // tm, N // tn),
        in_specs=[a_spec, b_spec] + ([o_spec] if resid is not None else []),
        out_specs=o_spec,
        out_shape=jax.ShapeDtypeStruct((M, N), out_dtype),
        compiler_params=_params(("parallel", "parallel")),
    )(*([a, b] + ([resid] if resid is not None else [])))


def _bmm(name, terms, out_dtype, gb=8, post=None, n_out=1):
    G = terms[0][0].shape[0]
    gb = _tile(G, gb, 1)
    dns = {'nn': ((1,), (0,)), 'nt': ((1,), (1,)), 'tn': ((0,), (0,))}

    def oshape(a, b, mode):
        m = a.shape[2] if mode == 'tn' else a.shape[1]
        n = b.shape[1] if mode == 'nt' else b.shape[2]
        return m, n

    m, n = oshape(*terms[0])
    out_dtypes = out_dtype if isinstance(out_dtype, (tuple, list)) else (out_dtype,)
    n_in = 2 * len(terms)

    def body(*refs):
        ins, outs = refs[:n_in], refs[n_in:]
        for gi in range(gb):
            acc = None
            for t, (_, _, mode) in enumerate(terms):
                part = lax.dot_general(ins[2 * t][gi].astype(BF16), ins[2 * t + 1][gi].astype(BF16),
                                       (dns[mode], ((), ())), preferred_element_type=F32)
                acc = part if acc is None else acc + part
            vals = (acc,) if post is None else post(acc)
            for o_ref, v in zip(outs, vals):
                o_ref[gi] = v.astype(o_ref.dtype)

    in_specs, args = [], []
    for a, b, _ in terms:
        in_specs += [pl.BlockSpec((gb,) + a.shape[1:], lambda g: (g, 0, 0)),
                     pl.BlockSpec((gb,) + b.shape[1:], lambda g: (g, 0, 0))]
        args += [a, b]
    res = _pcall(
        body, name=name, grid=(G // gb,), in_specs=in_specs,
        out_specs=[pl.BlockSpec((gb, m, n), lambda g: (g, 0, 0)) for _ in out_dtypes],
        out_shape=[jax.ShapeDtypeStruct((G, m, n), dt) for dt in out_dtypes],
        compiler_params=_params(("parallel",)),
    )(*args)
    return res[0] if len(out_dtypes) == 1 else res


def _rowwise(name, fn, n_steps, ins, outs, n_acc=0):
    n_in, n_out = len(ins), len(outs)

    def body(*refs):
        i = pl.program_id(0)
        vals = fn(i, *[r[...] for r in refs[:n_in]])
        o_refs = refs[n_in:]
        for j in range(n_out - n_acc):
            o_refs[j][...] = vals[j].astype(o_refs[j].dtype)
        if n_acc:
            @pl.when(i == 0)
            def _():
                for j in range(n_out - n_acc, n_out):
                    o_refs[j][...] = vals[j].astype(o_refs[j].dtype)

            @pl.when(i > 0)
            def _():
                for j in range(n_out - n_acc, n_out):
                    o_refs[j][...] += vals[j].astype(o_refs[j].dtype)

    res = _pcall(
        body, name=name, grid=(n_steps,),
        in_specs=[pl.BlockSpec(blk, im) for _, blk, im in ins],
        out_specs=[pl.BlockSpec(blk, im) for _, _, blk, im in outs],
        out_shape=[jax.ShapeDtypeStruct(s, d) for s, d, _, _ in outs],
        compiler_params=_params(("arbitrary",)),
    )(*[a for a, _, _ in ins])
    return res


def _rows(a, tm):
    return (a, (tm, a.shape[1]), lambda i: (i, 0))


def _whole(a):
    nd = a.ndim
    return (a, a.shape, lambda i: (0,) * nd)


def _orows(L, n, dtype, tm):
    return ((L, n), dtype, (tm, n), lambda i: (i, 0))


def _oacc(r, n):
    return ((r, n), F32, (r, n), lambda i: (0, 0))


def _rms_fwd(name, x, g, tm=512):
    L, D = x.shape
    tm = _tile(L, tm)

    def fn(i, xb, gb):
        r = lax.rsqrt(jnp.mean(xb * xb, axis=-1, keepdims=True) + EPS)
        return (xb * r * gb,)

    return _rowwise(name, fn, L // tm, [_rows(x, tm), _whole(g)], [_orows(L, D, BF16, tm)])[0]


def _rms_bwd(name, x, dres, branches, tm=256):
    L, D = x.shape
    tm = _tile(L, tm)
    nb = len(branches)

    def fn(i, xb, db, *rest):
        r = lax.rsqrt(jnp.mean(xb * xb, axis=-1, keepdims=True) + EPS)
        xh = xb * r
        dx = db
        dgs = []
        for b in range(nb):
            gb, dyb = rest[2 * b], rest[2 * b + 1].astype(F32)
            dxh = dyb * gb
            dx = dx + r * (dxh - xh * jnp.mean(dxh * xh, axis=-1, keepdims=True))
            dgs.append(jnp.sum(dyb * xh, axis=0, keepdims=True))
        return (dx, dx, *dgs)

    ins = [_rows(x, tm), _rows(dres, tm)]
    for g, dy in branches:
        ins += [_whole(g), _rows(dy, tm)]
    outs = [_orows(L, D, F32, tm), _orows(L, D, BF16, tm)] + [_oacc(1, D) for _ in range(nb)]
    return _rowwise(name, fn, L // tm, ins, outs, n_acc=nb)


def _final_loss(name, x, g, target, tm=256):
    L, D = x.shape
    tm = _tile(L, tm)

    def fn(i, xb, gb, tb):
        r = lax.rsqrt(jnp.mean(xb * xb, axis=-1, keepdims=True) + EPS)
        xh = xb * r
        err = xh * gb - tb
        dy = err * (1.0 / D)
        dxh = dy * gb
        dx = r * (dxh - xh * jnp.mean(dxh * xh, axis=-1, keepdims=True))
        dg = jnp.sum(dy * xh, axis=0, keepdims=True)
        per_row = jnp.mean(err * err, axis=-1, keepdims=True)
        loss = 0.5 * jnp.sum(per_row, axis=0, keepdims=True)
        return dx, dx, dg, jnp.broadcast_to(loss, (1, LANES))

    return _rowwise(name, fn, L // tm, [_rows(x, tm), _whole(g), _rows(target, tm)],
                    [_orows(L, D, F32, tm), _orows(L, D, BF16, tm), _oacc(1, D), _oacc(1, LANES)], n_acc=2)


def _glu_fwd(name, x, z, tm=256):
    L, D = x.shape
    tm = _tile(L, tm)

    def fn(i, xb, zb):
        return (xb + zb[:, :D] * jax.nn.sigmoid(zb[:, D:]),)

    return _rowwise(name, fn, L // tm, [_rows(x, tm), _rows(z, tm)], [_orows(L, D, F32, tm)])[0]


def _glu_bwd(name, dx, z, tm=256):
    L, D = dx.shape
    tm = _tile(L, tm)

    def fn(i, db, zb):
        z1, sg = zb[:, :D], jax.nn.sigmoid(zb[:, D:])
        return (jnp.concatenate([(db * sg).astype(BF16), (db * z1 * sg * (1.0 - sg)).astype(BF16)], axis=1),)

    return _rowwise(name, fn, L // tm, [_rows(dx, tm), _rows(z, tm)], [_orows(L, 2 * D, BF16, tm)])[0]


def _halo_prev(a, tm):
    return (a, (8, a.shape[1]), lambda i: (jnp.maximum(i * (tm // 8) - 1, 0), 0))


def _halo_next(a, tm):
    last = a.shape[0] // 8 - 1
    return (a, (8, a.shape[1]), lambda i: (jnp.minimum((i + 1) * (tm // 8), last), 0))


def _shift_down(cur, halo, k, first):
    tm = cur.shape[0]
    rolled = pltpu.roll(cur, k, 0)
    tail = pltpu.roll(halo, k, 0)
    tail = jnp.where(first, 0.0, tail)
    row = lax.broadcasted_iota(jnp.int32, (tm, 1), 0)
    head = jnp.concatenate([tail, jnp.zeros((tm - 8, cur.shape[1]), cur.dtype)], axis=0) if tm > 8 else tail
    return jnp.where(row < k, head, rolled)


def _shift_up(cur, halo, k, last):
    tm = cur.shape[0]
    rolled = pltpu.roll(cur, tm - k, 0)
    head = pltpu.roll(halo, 8 - k, 0)
    head = jnp.where(last, 0.0, head)
    row = lax.broadcasted_iota(jnp.int32, (tm, 1), 0)
    tail = jnp.concatenate([jnp.zeros((tm - 8, cur.shape[1]), cur.dtype), head], axis=0) if tm > 8 else head
    return jnp.where(row >= tm - k, tail, rolled)


def _conv_pre(gb, hb, cw, cb, first):
    g1 = _shift_down(gb, hb, 1, first)
    g2 = _shift_down(gb, hb, 2, first)
    return cw[0:1] * g2 + cw[1:2] * g1 + cw[2:3] * gb + cb, g1, g2


def _gate_fwd(name, gu, cw, cb, tm=256):
    L, F2 = gu.shape
    Fh = F2 // 2
    tm = _tile(L, tm)

    def fn(i, gub, halo, cwb, cbb):
        gc, _, _ = _conv_pre(gub[:, :Fh], halo[:, :Fh], cwb, cbb, i == 0)
        return (gc * jax.nn.sigmoid(gc) * gub[:, Fh:],)

    return _rowwise(name, fn, L // tm, [_rows(gu, tm), _halo_prev(gu, tm), _whole(cw), _whole(cb)],
                    [_orows(L, Fh, BF16, tm)])[0]


def _gate_bwd1(name, gu, da, cw, cb, tm=256):
    L, F2 = gu.shape
    Fh = F2 // 2
    tm = _tile(L, tm)

    def fn(i, gub, halo, dab, cwb, cbb):
        gb, ub = gub[:, :Fh], gub[:, Fh:]
        gc, g1, g2 = _conv_pre(gb, halo[:, :Fh], cwb, cbb, i == 0)
        sg = jax.nn.sigmoid(gc)
        dab = dab.astype(F32)
        du = dab * gc * sg
        dgc = dab * ub * sg * (1.0 + gc * (1.0 - sg))
        return (dgc, du, jnp.sum(dgc * g2, axis=0, keepdims=True), jnp.sum(dgc * g1, axis=0, keepdims=True),
                jnp.sum(dgc * gb, axis=0, keepdims=True), jnp.sum(dgc, axis=0, keepdims=True))

    return _rowwise(name, fn, L // tm,
                    [_rows(gu, tm), _halo_prev(gu, tm), _rows(da, tm), _whole(cw), _whole(cb)],
                    [_orows(L, Fh, F32, tm), _orows(L, Fh, BF16, tm)] + [_oacc(1, Fh) for _ in range(4)], n_acc=4)


def _gate_bwd2(name, dgc, du, cw, tm=256):
    L, Fh = dgc.shape
    tm = _tile(L, tm)
    n = L // tm

    def fn(i, db, halo, dub, cwb):
        d1 = _shift_up(db, halo, 1, i == n - 1)
        d2 = _shift_up(db, halo, 2, i == n - 1)
        dg = cwb[2:3] * db + cwb[1:2] * d1 + cwb[0:1] * d2
        return (jnp.concatenate([dg.astype(BF16), dub], axis=1),)

    return _rowwise(name, fn, n, [_rows(dgc, tm), _halo_next(dgc, tm), _rows(du, tm), _whole(cw)],
                    [_orows(L, 2 * Fh, BF16, tm)])[0]


def _s5_build(a_re, a_im, log_dt, b_re, b_im, c_re, c_im, d):
    T = CHUNK
    G, P = a_re.shape
    H = d.shape[1]
    hi = lax.Precision.HIGHEST
    dt = jnp.exp(log_dt)[:, None]
    mag = jnp.exp(a_re * dt)
    ab_re = mag * jnp.cos(a_im * dt)
    ab_im = mag * jnp.sin(a_im * dt)
    den = a_re * a_re + a_im * a_im
    f_re = ((ab_re - 1.0) * a_re + ab_im * a_im) / den
    f_im = (ab_im * a_re - (ab_re - 1.0) * a_im) / den
    bb_re = f_re[..., None] * b_re - f_im[..., None] * b_im
    bb_im = f_re[..., None] * b_im + f_im[..., None] * b_re
    tau = jnp.arange(T + 1, dtype=F32)[None, :, None]
    pmag = jnp.exp(tau * (a_re * dt)[:, None, :])
    pang = tau * (a_im * dt)[:, None, :]
    pw_re = pmag * jnp.cos(pang)
    pw_im = pmag * jnp.sin(pang)
    cp_re = c_re[:, :, None, :] * pw_re[:, None] - c_im[:, :, None, :] * pw_im[:, None]
    cp_im = c_re[:, :, None, :] * pw_im[:, None] + c_im[:, :, None, :] * pw_re[:, None]
    cp_cat = jnp.concatenate([cp_re, -cp_im], axis=-1)
    bb_cat = jnp.concatenate([bb_re, bb_im], axis=1)
    kt = jnp.einsum('ghtq,gqk->ghtk', cp_cat[:, :, :T], bb_cat, precision=hi)
    kt = kt.at[:, :, 0, :].add(d[:, :, None] * jnp.eye(H, dtype=F32)[None])
    kp = jnp.concatenate([kt[:, :, ::-1, :], jnp.zeros((G, H, T - 1, H), F32)], axis=2).reshape(G, H, (2 * T - 1) * H)
    m_mat = jnp.stack([kp[:, :, (T - 1 - t) * H:(2 * T - 1 - t) * H] for t in range(T)], axis=1)
    m_mat = m_mat.reshape(G, T * H, T * H)
    pr = jnp.repeat(pw_re[:, :T][:, ::-1].transpose(0, 2, 1), H, axis=2)
    pi = jnp.repeat(pw_im[:, :T][:, ::-1].transpose(0, 2, 1), H, axis=2)
    br, bi = jnp.tile(bb_re, (1, 1, T)), jnp.tile(bb_im, (1, 1, T))
    n_mat = jnp.concatenate([pr * br - pi * bi, pr * bi + pi * br], axis=1)
    o_mat = cp_cat[:, :, 1:].transpose(0, 2, 1, 3).reshape(G, T * H, 2 * P)
    lam1 = jnp.concatenate([pw_re[:, T], pw_re[:, T]], axis=-1)[:, None, :]
    lam2 = jnp.concatenate([-pw_im[:, T], pw_im[:, T]], axis=-1)[:, None, :]
    return m_mat, n_mat, o_mat, lam1, lam2


def _chunk_scan(name, v, lam1, lam2, reverse, s_fwd=None, gb=32):
    G, nc, W = v.shape
    gb = _tile(G, gb, 1)
    half = W // 2
    ntile = nc // 8

    def body(*refs):
        if reverse:
            v_ref, l1_ref, l2_ref, s_ref, o_ref, d1_ref, d2_ref = refs
        else:
            v_ref, l1_ref, l2_ref, o_ref = refs
        l1 = jnp.broadcast_to(l1_ref[...], (gb, 8, W))
        l2 = jnp.broadcast_to(l2_ref[...], (gb, 8, W))
        if reverse:
            l2 = -l2
        row = lax.broadcasted_iota(jnp.int32, (gb, 8, W), 1)

        def tile_step(n, carry):
            if reverse:
                st, a1, a2 = carry
                base = pl.multiple_of((ntile - 1 - n) * 8, 8)
            else:
                st = carry
                base = pl.multiple_of(n * 8, 8)
            vt = v_ref[:, pl.ds(base, 8), :]
            out = jnp.zeros((gb, 8, W), F32)
            order = range(7, -1, -1) if reverse else range(8)
            for r in order:
                out = jnp.where(row == r, st, out)
                vr = jnp.broadcast_to(vt[:, r:r + 1, :], (gb, 8, W))
                st = l1 * st + l2 * pltpu.roll(st, half, 2) + vr
            o_ref[:, pl.ds(base, 8), :] = out
            if reverse:
                sv = s_ref[:, pl.ds(base, 8), :]
                a1 = a1 + out * sv
                a2 = a2 + out * pltpu.roll(sv, half, 2)
                return st, a1, a2
            return st

        zero = jnp.zeros((gb, 8, W), F32)
        if reverse:
            _, a1, a2 = lax.fori_loop(0, ntile, tile_step, (zero, zero, zero))
            d1_ref[...] = jnp.sum(a1, axis=1, keepdims=True)
            d2_ref[...] = jnp.sum(a2, axis=1, keepdims=True)
        else:
            lax.fori_loop(0, ntile, tile_step, zero)

    big = pl.BlockSpec((gb, nc, W), lambda g: (g, 0, 0))
    vec = pl.BlockSpec((gb, 1, W), lambda g: (g, 0, 0))
    if reverse:
        return _pcall(body, name=name, grid=(G // gb,), in_specs=[big, vec, vec, big],
                      out_specs=[big, vec, vec],
                      out_shape=[jax.ShapeDtypeStruct((G, nc, W), F32), jax.ShapeDtypeStruct((G, 1, W), F32),
                                 jax.ShapeDtypeStruct((G, 1, W), F32)],
                      compiler_params=_params(("parallel",)))(v, lam1, lam2, s_fwd)
    return _pcall(body, name=name, grid=(G // gb,), in_specs=[big, vec, vec], out_specs=big,
                  out_shape=jax.ShapeDtypeStruct((G, nc, W), F32),
                  compiler_params=_params(("parallel",)))(v, lam1, lam2)


_GELU_C = math.sqrt(2.0 / math.pi)


def _gelu(y):
    return 0.5 * y * (1.0 + jnp.tanh(_GELU_C * (y + 0.044715 * y * y * y)))


def _gelu_grad(y):
    t = jnp.tanh(_GELU_C * (y + 0.044715 * y * y * y))
    return 0.5 * (1.0 + t) + 0.5 * y * (1.0 - t * t) * _GELU_C * (1.0 + 3.0 * 0.044715 * y * y)


def _lane_perm(Hg):
    gpb = LANES // Hg
    r = jnp.arange(CHUNK * LANES)
    t, gl, h = r // LANES, (r % LANES) // Hg, r % Hg
    target = gl * (CHUNK * Hg) + t * Hg + h
    return (target[:, None] == r[None, :]).astype(BF16), gpb


def _to_groups(name, a, G, Hg, out_dtype):
    L, D = a.shape
    nc = L // CHUNK
    perm, gpb = _lane_perm(Hg)
    gw = CHUNK * Hg

    def body(x_ref, p_ref, o_ref):
        x = jnp.concatenate([x_ref[:, t, :] for t in range(CHUNK)], axis=1)
        if x.dtype == BF16:
            z = jnp.dot(x, p_ref[...], preferred_element_type=F32)
        else:
            z = _split_dot(x, p_ref[...])
        for gl in range(gpb):
            o_ref[gl] = z[:, gl * gw:(gl + 1) * gw].astype(o_ref.dtype)

    return _pcall(
        body, name=name, grid=(G // gpb,),
        in_specs=[pl.BlockSpec((nc, CHUNK, LANES), lambda g: (0, 0, g)),
                  pl.BlockSpec(perm.shape, lambda g: (0, 0))],
        out_specs=pl.BlockSpec((gpb, nc, gw), lambda g: (g, 0, 0)),
        out_shape=jax.ShapeDtypeStruct((G, nc, gw), out_dtype),
        compiler_params=_params(("parallel",)),
    )(a.reshape(nc, CHUNK, D), perm)


def _from_groups(name, a, Hg):
    G, nc, gw = a.shape
    perm, gpb = _lane_perm(Hg)

    def body(x_ref, p_ref, o_ref):
        z = jnp.concatenate([x_ref[gl] for gl in range(gpb)], axis=1)
        x = lax.dot_general(z, p_ref[...], (((1,), (1,)), ((), ())), preferred_element_type=F32)
        for t in range(CHUNK):
            o_ref[:, t, :] = x[:, t * LANES:(t + 1) * LANES].astype(o_ref.dtype)

    out = _pcall(
        body, name=name, grid=(G // gpb,),
        in_specs=[pl.BlockSpec((gpb, nc, gw), lambda g: (g, 0, 0)),
                  pl.BlockSpec(perm.shape, lambda g: (0, 0))],
        out_specs=pl.BlockSpec((nc, CHUNK, LANES), lambda g: (0, 0, g)),
        out_shape=jax.ShapeDtypeStruct((nc, CHUNK, G * Hg), a.dtype),
        compiler_params=_params(("parallel",)),
    )(a, perm)
    return out.reshape(nc * CHUNK, G * Hg)


def _split_dot(x, u):
    hi = x.astype(BF16)
    lo = (x - hi.astype(F32)).astype(BF16)
    return (jnp.dot(hi, u, preferred_element_type=F32) + jnp.dot(lo, u, preferred_element_type=F32))


def _col_to_row(col, n):
    return jnp.broadcast_to(col, (n, LANES)).T[0:1, :]


def _row_to_col(row, n):
    return jnp.broadcast_to(row, (LANES, n)).T[:, 0:1]


def _attn_fwd(name, qw, kv, tq, tk, hb):
    H, L, dw = qw.shape
    nq, nkb = L // tq, L // tk
    hs = range(hb)
    assert tk == 2 * tq

    def body(q_ref, kv_ref, o_ref, rc_ref):
        i = pl.program_id(1)
        diff = (lax.broadcasted_iota(jnp.int32, (tq, tk), 1) - lax.broadcasted_iota(jnp.int32, (tq, tk), 0))
        u_suf = (lax.broadcasted_iota(jnp.int32, (tk, tk), 0)
                 > lax.broadcasted_iota(jnp.int32, (tk, tk), 1)).astype(BF16)
        nfull = i // 2
        rc_ref[...] = jnp.zeros_like(rc_ref)

        def block(kb, carry, masked):
            runs, accs = carry
            ks = pl.multiple_of(kb * tk, tk)
            kvb = [kv_ref[h, pl.ds(ks, tk), :] for h in hs]
            z = [lax.dot_general(q_ref[h], kvb[h], (((1,), (1,)), ((), ())), preferred_element_type=F32)
                 for h in hs]
            e = [jnp.exp(-jnp.abs(z[h])) for h in hs]
            sp = [jnp.maximum(z[h], 0.0) + jnp.log(1.0 + e[h]) for h in hs]
            if masked:
                causal = diff < (i * tq - ks)
                lom = [jnp.where(causal, -sp[h], 0.0) for h in hs]
            else:
                lom = [-sp[h] for h in hs]
            rem = [runs[h] + _split_dot(lom[h], u_suf) for h in hs]
            w = [jnp.exp(z[h] - sp[h] + rem[h]) for h in hs]
            if masked:
                w = [jnp.where(causal, w[h], 0.0) for h in hs]
            accs = tuple(accs[h] + jnp.dot(w[h].astype(BF16), kvb[h], preferred_element_type=F32) for h in hs)
            for h in hs:
                rc_ref[h, kb] = _col_to_row(runs[h], tq)
            runs = tuple(runs[h] + jnp.sum(lom[h], axis=1, keepdims=True) for h in hs)
            return runs, accs

        init = (tuple(jnp.zeros((tq, 1), F32) for _ in hs), tuple(jnp.zeros((tq, dw), F32) for _ in hs))
        carry = block(nfull, init, True)
        _, accs = lax.fori_loop(0, nfull, lambda n, c: block(nfull - 1 - n, c, False), carry)
        for h in hs:
            o_ref[h] = accs[h].astype(o_ref.dtype)

    qspec = pl.BlockSpec((hb, tq, dw), lambda h, i: (h, i, 0))
    kspec = pl.BlockSpec((hb, L, dw), lambda h, i: (h, 0, 0))
    return _pcall(
        body, name=name, grid=(H // hb, nq), in_specs=[qspec, kspec],
        out_specs=[qspec, pl.BlockSpec((hb, None, nkb, 1, tq), lambda h, i: (h, i, 0, 0, 0))],
        out_shape=[jax.ShapeDtypeStruct((H, L, dw), BF16), jax.ShapeDtypeStruct((H, nq, nkb, 1, tq), F32)],
        compiler_params=_params(("parallel", "arbitrary")),
    )(qw, kv)


def _attn_bwd(name, qw, kv, dow, rc, tq, tk, scale, hb):
    H, L, dw_ = qw.shape
    nq, nkb = L // tq, L // tk
    hs = range(hb)
    assert tk == 2 * tq

    def body(q_ref, kv_ref, do_ref, rc_ref, dq_ref, dkv_ref):
        i = pl.program_id(1)

        @pl.when(i == 0)
        def _():
            dkv_ref[...] = jnp.zeros_like(dkv_ref)

        diff = (lax.broadcasted_iota(jnp.int32, (tq, tk), 1) - lax.broadcasted_iota(jnp.int32, (tq, tk), 0))
        r_io = lax.broadcasted_iota(jnp.int32, (tk, tk), 0)
        c_io = lax.broadcasted_iota(jnp.int32, (tk, tk), 1)
        u_suf = (r_io > c_io).astype(BF16)
        u_pre = (r_io < c_io).astype(BF16)
        nfull = i // 2

        def block(kb, carry, masked):
            pres, dqs = carry
            ks = pl.multiple_of(kb * tk, tk)
            kvb = [kv_ref[h, pl.ds(ks, tk), :] for h in hs]
            qb = [q_ref[h] for h in hs]
            dob = [do_ref[h] for h in hs]
            z = [lax.dot_general(qb[h], kvb[h], (((1,), (1,)), ((), ())), preferred_element_type=F32) for h in hs]
            dw = [lax.dot_general(dob[h], kvb[h], (((1,), (1,)), ((), ())), preferred_element_type=F32)
                  for h in hs]
            e = [jnp.exp(-jnp.abs(z[h])) for h in hs]
            sp = [jnp.maximum(z[h], 0.0) + jnp.log(1.0 + e[h]) for h in hs]
            if masked:
                causal = diff < (i * tq - ks)
                lom = [jnp.where(causal, -sp[h], 0.0) for h in hs]
            else:
                lom = [-sp[h] for h in hs]
            rem = [_row_to_col(rc_ref[h, kb], tq) + _split_dot(lom[h], u_suf) for h in hs]
            logb = [z[h] - sp[h] for h in hs]
            w = [jnp.exp(logb[h] + rem[h]) for h in hs]
            if masked:
                w = [jnp.where(causal, w[h], 0.0) for h in hs]
            da = [dw[h] * w[h] for h in hs]
            p = [pres[h] + _split_dot(da[h], u_pre) for h in hs]
            dz = [da[h] - jnp.exp(logb[h]) * (da[h] + p[h]) for h in hs]
            if masked:
                dz = [jnp.where(causal, dz[h], 0.0) for h in hs]
            dqs = tuple(dqs[h] + jnp.dot(dz[h].astype(BF16), kvb[h], preferred_element_type=F32) for h in hs)
            for h in hs:
                lhs = jnp.concatenate([dz[h].T, w[h].T], axis=1).astype(BF16)
                rhs = jnp.concatenate([qb[h], dob[h]], axis=0)
                dkv_ref[h, pl.ds(ks, tk), :] += jnp.dot(lhs, rhs, preferred_element_type=F32)
            pres = tuple(pres[h] + jnp.sum(da[h], axis=1, keepdims=True) for h in hs)
            return pres, dqs

        init = (tuple(jnp.zeros((tq, 1), F32) for _ in hs), tuple(jnp.zeros((tq, dw_), F32) for _ in hs))
        carry = lax.fori_loop(0, nfull, lambda kb, c: block(kb, c, False), init)
        _, dqs = block(nfull, carry, True)
        for h in hs:
            dq_ref[h] = (dqs[h] * scale).astype(dq_ref.dtype)

    qspec = pl.BlockSpec((hb, tq, dw_), lambda h, i: (h, i, 0))
    kspec = pl.BlockSpec((hb, L, dw_), lambda h, i: (h, 0, 0))
    return _pcall(
        body, name=name, grid=(H // hb, nq),
        in_specs=[qspec, kspec, qspec, pl.BlockSpec((hb, None, nkb, 1, tq), lambda h, i: (h, i, 0, 0, 0))],
        out_specs=[qspec, kspec],
        out_shape=[jax.ShapeDtypeStruct((H, L, dw_), BF16), jax.ShapeDtypeStruct((H, L, dw_), F32)],
        compiler_params=_params(("parallel", "arbitrary")),
    )(qw, kv, dow, rc)


def _heads(a):
    L, D = a.shape
    return a.reshape(L, D // HEAD_DIM, HEAD_DIM).transpose(1, 0, 2)


def _unheads(a):
    H, L, dh = a.shape
    return a.transpose(1, 0, 2).reshape(L, H * dh)


_MESH = pl.DeviceIdType.MESH
_HBM = pl.BlockSpec(memory_space=pltpu.HBM)


def _all_gather(name, shard):
    R, C = shard.shape

    def body(x_ref, out_ref, send_sems, recv_sems, local_sem):
        x, y, c = lax.axis_index("x"), lax.axis_index("y"), lax.axis_index("c")
        me, sibling = (x, y, c), (x, y, 1 - c)
        chips = [(1 - x, y), (x, 1 - y), (1 - x, 1 - y)]

        def slot(px, py, pc):
            return out_ref.at[4 * px + 2 * py + pc]

        def copy(k, block, to, src=None):
            return pltpu.make_async_remote_copy(
                src_ref=slot(*block) if src is None else src, dst_ref=slot(*block),
                send_sem=send_sems.at[k], recv_sem=recv_sems.at[k], device_id=to, device_id_type=_MESH)

        mine = pltpu.make_async_copy(x_ref, slot(*me), local_sem)
        mine.start()
        first = [copy(0, me, sibling, src=x_ref)]
        first += [copy(1 + j, me, (*chip, c), src=x_ref) for j, chip in enumerate(chips)]
        for cp in first:
            cp.start()
        passed = [copy(4 + j, (*chip, c), sibling) for j, chip in enumerate(chips)]
        for j, chip in enumerate(chips):
            copy(1 + j, (*chip, c), me).wait_recv()
            passed[j].start()
        copy(0, sibling, me).wait_recv()
        for j, chip in enumerate(chips):
            copy(4 + j, (*chip, 1 - c), me).wait_recv()
        for cp in first + passed:
            cp.wait_send()
        mine.wait()

    return _pcall(
        body, name=name, out_shape=jax.ShapeDtypeStruct((N_DEV, R, C), shard.dtype),
        in_specs=[_HBM], out_specs=_HBM,
        scratch_shapes=[pltpu.SemaphoreType.DMA((7,)), pltpu.SemaphoreType.DMA((7,)), pltpu.SemaphoreType.DMA],
    )(shard)


def _sibling_exchange(name, x):
    def body(x_ref, out_ref, send_sem, recv_sem):
        sibling = (lax.axis_index("x"), lax.axis_index("y"), 1 - lax.axis_index("c"))
        cp = pltpu.make_async_remote_copy(src_ref=x_ref, dst_ref=out_ref, send_sem=send_sem, recv_sem=recv_sem,
                                          device_id=sibling, device_id_type=_MESH)
        cp.start()
        cp.wait()

    return _pcall(
        body, name=name, out_shape=jax.ShapeDtypeStruct(x.shape, x.dtype), in_specs=[_HBM], out_specs=_HBM,
        scratch_shapes=[pltpu.SemaphoreType.DMA, pltpu.SemaphoreType.DMA],
    )(x)


def _chip_exchange(name, x):
    _, R, C = x.shape

    def body(x_ref, out_ref, send_sems, recv_sems):
        mx, my, mc = lax.axis_index("x"), lax.axis_index("y"), lax.axis_index("c")
        chips = [(1 - mx, my), (mx, 1 - my), (1 - mx, 1 - my)]
        cps = [pltpu.make_async_remote_copy(src_ref=x_ref.at[2 * px + py], dst_ref=out_ref.at[j],
                                            send_sem=send_sems.at[j], recv_sem=recv_sems.at[j],
                                            device_id=(px, py, mc), device_id_type=_MESH)
               for j, (px, py) in enumerate(chips)]
        for cp in cps:
            cp.start()
        for cp in cps:
            cp.wait()

    return _pcall(
        body, name=name, out_shape=jax.ShapeDtypeStruct((3, R, C), x.dtype), in_specs=[_HBM], out_specs=_HBM,
        scratch_shapes=[pltpu.SemaphoreType.DMA((3,)), pltpu.SemaphoreType.DMA((3,))],
    )(x)


def _pair_sum(name, a, b):
    nb, R, C = a.shape
    tm = _tile(R, 512, 16)

    def fn(i, ab, bb):
        s = ab.astype(F32) + bb.astype(F32)
        return s, s

    spec = lambda arr: (arr, (1, tm, C), lambda i: (i // (R // tm), i % (R // tm), 0))
    out = lambda dt: ((nb, R, C), dt, (1, tm, C), lambda i: (i // (R // tm), i % (R // tm), 0))
    return _rowwise(name, fn, nb * (R // tm), [spec(a), spec(b)], [out(F32), out(BF16)])


def _sum8(name, g):
    n, R, C = g.shape
    tm = _tile(R, 256)

    def fn(i, gb):
        s = gb[0]
        for d in range(1, n):
            s = s + gb[d]
        return (s,)

    return _rowwise(name, fn, R // tm, [(g, (n, tm, C), lambda i: (0, i, 0))], [_orows(R, C, F32, tm)])[0]


def _adamw(name, w, m, v, grads, tm=512):
    R, C = w.shape
    tm = _tile(R, tm, 16)
    ng = len(grads)

    def fn(i, wb, mb, vb, *gs):
        g = gs[0].astype(F32)
        for t in gs[1:]:
            g = g + t.astype(F32)
        mn = ADAM_B1 * mb + (1.0 - ADAM_B1) * g
        vn = ADAM_B2 * vb + (1.0 - ADAM_B2) * (g * g)
        m_hat = mn / (1.0 - ADAM_B1 ** ADAM_STEP)
        v_hat = vn / (1.0 - ADAM_B2 ** ADAM_STEP)
        delta = -ADAM_LR * (m_hat / (jnp.sqrt(v_hat) + ADAM_EPS) + ADAM_WD * wb)
        return g, delta, mn, vn

    ins = [_rows(w, tm), _rows(m, tm), _rows(v, tm)] + [_rows(g, tm) for g in grads]
    return _rowwise(name, fn, R // tm, ins, [_orows(R, C, F32, tm) for _ in range(4)])


def _big_rows(D, Fh):
    return {'ssm_w_in': D // 8, 'ssm_w_glu': D // 4, 'kv_w': D // 4, 'attn_w_q': D // 8, 'attn_w_o': D // 8,
            'ffn_w_up': Fh // 2, 'ffn_w_down': Fh // 4, 'ffn_conv_w': CONV_ROWS}


def _pack_shards(d, D, wire=False):
    parts = []
    for n in BIG:
        a = d[n]
        if n == 'ffn_conv_w':
            flat = a.reshape(-1)
            if wire:
                flat = lax.bitcast_convert_type(flat, BF16).reshape(-1)
            a = jnp.pad(flat, (0, CONV_ROWS * D - flat.shape[0]))
        elif wire:
            a = a.astype(BF16)
        parts.append(a.reshape(-1, D))
    return jnp.concatenate(parts, axis=0)


def _unpack_shards(pack, shapes, D, Fh):
    rows = _big_rows(D, Fh)
    out, off = {}, 0
    for n in BIG:
        seg = pack[off:off + rows[n]]
        off += rows[n]
        size = math.prod(shapes[n])
        out[n] = seg.reshape(-1)[:size].reshape(shapes[n])
    return out


def _full_weights(gath, D, Fh):
    rows = _big_rows(D, Fh)
    seg, off = {}, 0
    for n in BIG:
        seg[n] = gath[:, off:off + rows[n]]
        off += rows[n]
    n8 = N_DEV
    w = {}
    w['ssm_w_in'] = seg['ssm_w_in'].reshape(D, D)
    w['attn_w_q'] = seg['attn_w_q'].reshape(D, D)
    w['attn_w_o'] = seg['attn_w_o'].reshape(D, D)
    w['ssm_w_glu'] = seg['ssm_w_glu'].reshape(n8, D, 2 * D // n8).transpose(1, 0, 2).reshape(D, 2 * D)
    w['kv_w'] = seg['kv_w'].reshape(n8, D, 2 * D // n8).transpose(1, 0, 2).reshape(D, 2 * D)
    w['ffn_w_up'] = seg['ffn_w_up'].reshape(n8, 2, D, 2 * Fh // n8).transpose(1, 2, 0, 3).reshape(2, D, 2 * Fh)
    w['ffn_w_down'] = seg['ffn_w_down'].reshape(n8, 2, Fh // n8, D).transpose(1, 0, 2, 3).reshape(2, Fh, D)
    cw = seg['ffn_conv_w'].reshape(n8, CONV_ROWS * D)[:, :12 * Fh // n8].reshape(n8, 6 * Fh // n8, 2)
    cw = lax.bitcast_convert_type(cw, F32)
    w['ffn_conv_w'] = cw.reshape(n8, 2, 3, Fh // n8).transpose(1, 2, 0, 3).reshape(2, 3, Fh)
    return w


def _grad_pack(g, D, Fh):
    n8 = N_DEV
    parts = {
        'ssm_w_in': g['ssm_w_in'].reshape(n8, D // n8, D),
        'attn_w_q': g['attn_w_q'].reshape(n8, D // n8, D),
        'attn_w_o': g['attn_w_o'].reshape(n8, D // n8, D),
        'ssm_w_glu': g['ssm_w_glu'].reshape(D, n8, 2 * D // n8).transpose(1, 0, 2).reshape(n8, D // 4, D),
        'kv_w': g['kv_w'].reshape(D, n8, 2 * D // n8).transpose(1, 0, 2).reshape(n8, D // 4, D),
        'ffn_w_up': g['ffn_w_up'].reshape(2, D, n8, 2 * Fh // n8).transpose(2, 0, 1, 3).reshape(n8, Fh // 2, D),
        'ffn_w_down': g['ffn_w_down'].reshape(2, n8, Fh // n8, D).transpose(1, 0, 2, 3).reshape(n8, Fh // 4, D),
    }
    cw = g['ffn_conv_w'].reshape(2, 3, n8, Fh // n8).transpose(2, 0, 1, 3).reshape(n8, 6 * Fh // n8)
    parts['ffn_conv_w'] = jnp.pad(cw, ((0, 0), (0, CONV_ROWS * D - cw.shape[1]))).reshape(n8, CONV_ROWS, D)
    return jnp.concatenate([parts[n] for n in BIG], axis=1)


def _small_layout(shapes, D):
    lay, off = {}, 0
    for n in SMALL:
        r = -(-math.prod(shapes[n]) // D)
        lay[n] = (off, r)
        off += r
    lay['loss'] = (off, 1)
    off += 1
    return lay, -(-off // 8) * 8


def _pack_small(d, lay, total, D):
    parts = []
    for n in SMALL + ['loss']:
        if n not in d:
            parts.append(jnp.zeros((lay[n][1], D), F32))
            continue
        flat = d[n].reshape(-1).astype(F32)
        parts.append(jnp.pad(flat, (0, lay[n][1] * D - flat.shape[0])).reshape(lay[n][1], D))
    used = sum(lay[n][1] for n in SMALL + ['loss'])
    if total > used:
        parts.append(jnp.zeros((total - used, D), F32))
    return jnp.concatenate(parts, axis=0)


def _unpack_small(pack, lay, shapes, D):
    out = {}
    for n in SMALL:
        off, r = lay[n]
        out[n] = pack[off:off + r].reshape(-1)[:math.prod(shapes[n])].reshape(shapes[n])
    return out


def _ffn_fwd(tag, x, g_norm, w_up, conv_w, conv_b, w_down):
    h = _rms_fwd(f"{tag}_norm", x, g_norm)
    gu = _mm(f"{tag}_up", h, w_up, 'nn', F32)
    a = _gate_fwd(f"{tag}_gate", gu, conv_w, conv_b)
    return _mm(f"{tag}_down", a, w_down, 'nn', F32, resid=x), (h, gu, a)


def _ffn_bwd(tag, dres, dres_bf, x, saved, g_norm, w_up, conv_w, conv_b, w_down):
    h, gu, a = saved
    da = _mm(f"{tag}_dgate", dres_bf, w_down, 'nt', F32)
    d_w_down = _mm(f"{tag}_dwdown", a, dres_bf, 'tn', F32)
    dgc, du, dw0, dw1, dw2, dcb = _gate_bwd1(f"{tag}_gate_b1", gu, da, conv_w, conv_b)
    dgu = _gate_bwd2(f"{tag}_gate_b2", dgc, du, conv_w)
    d_w_up = _mm(f"{tag}_dwup", h, dgu, 'tn', F32)
    dh = _mm(f"{tag}_dh", dgu, w_up, 'nt', F32)
    dres, dres_bf, dg = _rms_bwd(f"{tag}_norm_b", x, dres, [(g_norm, dh)])
    return dres, dres_bf, dg, d_w_up, jnp.concatenate([dw0, dw1, dw2], axis=0), dcb[0], d_w_down


def kernel(x, norm_mix, norm_ffn, norm_kv, norm_final, ssm_w_in, ssm_a_re, ssm_a_im, ssm_log_dt, ssm_b_re, ssm_b_im, ssm_c_re, ssm_c_im, ssm_d, ssm_w_glu, kv_w, attn_w_q, attn_w_o, ffn_w_up, ffn_conv_w, ffn_conv_b, ffn_w_down, loss_target, m_norm_mix, m_norm_ffn, m_norm_kv, m_norm_final, m_ssm_w_in, m_ssm_a_re, m_ssm_a_im, m_ssm_log_dt, m_ssm_b_re, m_ssm_b_im, m_ssm_c_re, m_ssm_c_im, m_ssm_d, m_ssm_w_glu, m_kv_w, m_attn_w_q, m_attn_w_o, m_ffn_w_up, m_ffn_conv_w, m_ffn_conv_b, m_ffn_w_down, v_norm_mix, v_norm_ffn, v_norm_kv, v_norm_final, v_ssm_w_in, v_ssm_a_re, v_ssm_a_im, v_ssm_log_dt, v_ssm_b_re, v_ssm_b_im, v_ssm_c_re, v_ssm_c_im, v_ssm_d, v_ssm_w_glu, v_kv_w, v_attn_w_q, v_attn_w_o, v_ffn_w_up, v_ffn_conv_w, v_ffn_conv_b, v_ffn_w_down):
    wts = dict(zip(W_NAMES, (norm_mix, norm_ffn, norm_kv, norm_final, ssm_w_in, ssm_a_re, ssm_a_im, ssm_log_dt,
                             ssm_b_re, ssm_b_im, ssm_c_re, ssm_c_im, ssm_d, ssm_w_glu, kv_w, attn_w_q, attn_w_o,
                             ffn_w_up, ffn_conv_w, ffn_conv_b, ffn_w_down)))
    mom = dict(zip(W_NAMES, (m_norm_mix, m_norm_ffn, m_norm_kv, m_norm_final, m_ssm_w_in, m_ssm_a_re, m_ssm_a_im,
                             m_ssm_log_dt, m_ssm_b_re, m_ssm_b_im, m_ssm_c_re, m_ssm_c_im, m_ssm_d, m_ssm_w_glu,
                             m_kv_w, m_attn_w_q, m_attn_w_o, m_ffn_w_up, m_ffn_conv_w, m_ffn_conv_b, m_ffn_w_down)))
    vel = dict(zip(W_NAMES, (v_norm_mix, v_norm_ffn, v_norm_kv, v_norm_final, v_ssm_w_in, v_ssm_a_re, v_ssm_a_im,
                             v_ssm_log_dt, v_ssm_b_re, v_ssm_b_im, v_ssm_c_re, v_ssm_c_im, v_ssm_d, v_ssm_w_glu,
                             v_kv_w, v_attn_w_q, v_attn_w_o, v_ffn_w_up, v_ffn_conv_w, v_ffn_conv_b, v_ffn_w_down)))
    shapes = {n: wts[n].shape for n in W_NAMES}
    _, L, D = x.shape
    Fh = ffn_conv_b.shape[1]
    G, P = ssm_a_re.shape[1], ssm_a_re.shape[2]
    Hg = ssm_d.shape[2]
    x0 = x[0]
    target = loss_target[0]
    scale = HEAD_DIM ** -0.5

    gath = _all_gather("gather_weights", _pack_shards(wts, D, wire=True))
    w = _full_weights(gath, D, Fh)
    nm = [norm_mix[l:l + 1] for l in range(2)]
    nf = [norm_ffn[l:l + 1] for l in range(2)]
    nkv = norm_kv[None]
    nfin = norm_final[None]
    cb = [ffn_conv_b[l:l + 1] for l in range(2)]
    cwt = [w['ffn_conv_w'][l] for l in range(2)]

    s5p = (ssm_a_re[0], ssm_a_im[0], ssm_log_dt[0], ssm_b_re[0], ssm_b_im[0], ssm_c_re[0], ssm_c_im[0], ssm_d[0])
    (m_mat, n_mat, o_mat, lam1, lam2), s5_vjp = jax.vjp(_s5_build, *s5p)
    m_bf, n_bf, o_bf = m_mat.astype(BF16), n_mat.astype(BF16), o_mat.astype(BF16)

    h0 = _rms_fwd("l0_norm", x0, nm[0])
    u = _mm("l0_win", h0, w['ssm_w_in'], 'nn', BF16)
    ug = _to_groups("s5_group_u", u, G, Hg, BF16)
    vloc = _bmm("s5_local_state", [(ug, n_bf, 'nt')], F32)
    st = _chunk_scan("s5_scan", vloc, lam1, lam2, reverse=False)
    yraw, yg_g = _bmm("s5_out", [(ug, m_bf, 'nt'), (st, o_bf, 'nt')], (F32, BF16),
                      post=lambda acc: (acc, _gelu(acc)))
    yg = _from_groups("s5_ungroup_y", yg_g, Hg)
    z = _mm("l0_wglu", yg, w['ssm_w_glu'], 'nn', F32)
    x1 = _glu_fwd("l0_glu", x0, z)
    x2, ffn0 = _ffn_fwd("f0", x1, nf[0], w['ffn_w_up'][0], cwt[0], cb[0], w['ffn_w_down'][0])

    hkv = _rms_fwd("kv_norm", x2, nkv)
    kvp = _mm("kv_proj", hkv, w['kv_w'], 'nn', BF16)
    h2 = _rms_fwd("l1_norm", x2, nm[1])
    qn = _mm("l1_wq", h2, w['attn_w_q'], 'nn', BF16, scale=scale)
    nh = D // HEAD_DIM
    zeros_h = jnp.zeros((nh, L, HEAD_DIM), BF16)
    qw = jnp.concatenate([_heads(qn), zeros_h], axis=-1)
    kv = kvp.reshape(L, 2, nh, HEAD_DIM).transpose(2, 0, 1, 3).reshape(nh, L, 2 * HEAD_DIM)
    ow, rc = _attn_fwd("attn_fwd", qw, kv, ATT_TQ, ATT_TK, ATT_HB_FWD)
    o = _unheads(ow[:, :, HEAD_DIM:])
    x3 = _mm("l1_wo", o, w['attn_w_o'], 'nn', F32, resid=x2)
    x4, ffn1 = _ffn_fwd("f1", x3, nf[1], w['ffn_w_up'][1], cwt[1], cb[1], w['ffn_w_down'][1])

    dres, dres_bf, dg_final, loss_part = _final_loss("loss_head", x4, nfin, target)

    gw = {}
    dres, dres_bf, dg_nf1, gup1, gcw1, gcb1, gdn1 = _ffn_bwd("f1", dres, dres_bf, x3, ffn1, nf[1], w['ffn_w_up'][1],
                                                             cwt[1], cb[1], w['ffn_w_down'][1])
    do = _mm("l1_do", dres_bf, w['attn_w_o'], 'nt', BF16)
    gw['attn_w_o'] = _mm("l1_dwo", o, dres_bf, 'tn', F32)
    dow = jnp.concatenate([zeros_h, _heads(do)], axis=-1)
    dqw, dkv_h = _attn_bwd("attn_bwd", qw, kv, dow, rc, ATT_TQ, ATT_TK, scale, ATT_HB_BWD)
    dq = _unheads(dqw[:, :, :HEAD_DIM])
    dkv = dkv_h.reshape(nh, L, 2, HEAD_DIM).transpose(1, 2, 0, 3).reshape(L, 2 * D).astype(BF16)
    gw['attn_w_q'] = _mm("l1_dwq", h2, dq, 'tn', F32)
    dh2 = _mm("l1_dh", dq, w['attn_w_q'], 'nt', F32)
    gw['kv_w'] = _mm("kv_dw", hkv, dkv, 'tn', F32)
    dhkv = _mm("kv_dh", dkv, w['kv_w'], 'nt', F32)
    dres, dres_bf, dg_nm1, dg_nkv = _rms_bwd("l1_norm_b", x2, dres, [(nm[1], dh2), (nkv, dhkv)])
    dres, dres_bf, dg_nf0, gup0, gcw0, gcb0, gdn0 = _ffn_bwd("f0", dres, dres_bf, x1, ffn0, nf[0], w['ffn_w_up'][0],
                                                             cwt[0], cb[0], w['ffn_w_down'][0])
    dz = _glu_bwd("l0_glu_b", dres, z)
    gw['ssm_w_glu'] = _mm("l0_dwglu", yg, dz, 'tn', F32)
    dyg = _mm("l0_dyg", dz, w['ssm_w_glu'], 'nt', F32)
    dyg_g = _to_groups("s5_group_dy", dyg, G, Hg, F32)
    gshape = yraw.shape
    dy = _rowwise("s5_gelu_b", lambda i, a, b: (a * _gelu_grad(b),), G,
                  [(dyg_g, (1,) + gshape[1:], lambda i: (i, 0, 0)), (yraw, (1,) + gshape[1:], lambda i: (i, 0, 0))],
                  [(gshape, BF16, (1,) + gshape[1:], lambda i: (i, 0, 0))])[0]
    ds = _bmm("s5_dstate", [(dy, o_bf, 'nn')], F32)
    dv_loc, dlam1, dlam2 = _chunk_scan("s5_scan_b", ds, lam1, lam2, reverse=True, s_fwd=st)
    du_g = _bmm("s5_du", [(dy, m_bf, 'nn'), (dv_loc, n_bf, 'nn')], BF16)
    d_m = _bmm("s5_dm", [(dy, ug, 'tn')], F32)
    d_o = _bmm("s5_do", [(dy, st, 'tn')], F32)
    d_n = _bmm("s5_dn", [(dv_loc, ug, 'tn')], F32)
    s5g = s5_vjp((d_m, d_n, d_o, dlam1, dlam2))
    du = _from_groups("s5_ungroup_du", du_g, Hg)
    gw['ssm_w_in'] = _mm("l0_dwin", h0, du, 'tn', F32)
    dh0 = _mm("l0_dh", du, w['ssm_w_in'], 'nt', F32)
    grad_x, _, dg_nm0 = _rms_bwd("l0_norm_b", x0, dres, [(nm[0], dh0)])

    gw['ffn_w_up'] = jnp.stack([gup0, gup1])
    gw['ffn_w_down'] = jnp.stack([gdn0, gdn1])
    gw['ffn_conv_w'] = jnp.stack([gcw0, gcw1])

    small_g = {
        'norm_mix': jnp.concatenate([dg_nm0, dg_nm1], axis=0), 'norm_ffn': jnp.concatenate([dg_nf0, dg_nf1], axis=0),
        'norm_kv': dg_nkv, 'norm_final': dg_final, 'ffn_conv_b': jnp.stack([gcb0, gcb1]),
        'ssm_a_re': s5g[0], 'ssm_a_im': s5g[1], 'ssm_log_dt': s5g[2], 'ssm_b_re': s5g[3], 'ssm_b_im': s5g[4],
        'ssm_c_re': s5g[5], 'ssm_c_im': s5g[6], 'ssm_d': s5g[7], 'loss': loss_part[0, 0:1],
    }
    lay, rs = _small_layout(shapes, D)
    small_sum = _sum8("small_sum", _all_gather("gather_small", _pack_small(small_g, lay, rs, D)))
    loss = small_sum[lay['loss'][0], 0]
    sg, sdelta, sm, sv = _adamw("adamw_small", _pack_small(wts, lay, rs, D), _pack_small(mom, lay, rs, D),
                                _pack_small(vel, lay, rs, D), [small_sum])
    small_out = [_unpack_small(t, lay, shapes, D) for t in (sg, sdelta, sm, sv)]

    gp = _grad_pack(gw, D, Fh).astype(BF16)
    rows = gp.shape[1]
    gp = gp.reshape(4, 2, rows, D)
    c = lax.axis_index("c")
    chip = 2 * lax.axis_index("x") + lax.axis_index("y")
    mine = lax.dynamic_index_in_dim(gp, c, axis=1, keepdims=False)
    theirs = lax.dynamic_index_in_dim(gp, 1 - c, axis=1, keepdims=False)
    got = _sibling_exchange("rs_sibling", theirs)
    chip_f32, chip_bf = _pair_sum("rs_pair_sum", mine, got)
    from_chips = _chip_exchange("rs_chips", chip_bf)
    own = lax.dynamic_index_in_dim(chip_f32, chip, axis=0, keepdims=False)
    bg, bdelta, bm, bv = _adamw("adamw_big", _pack_shards(wts, D), _pack_shards(mom, D), _pack_shards(vel, D),
                                [own, from_chips[0], from_chips[1], from_chips[2]])
    big_out = [_unpack_shards(t, shapes, D, Fh) for t in (bg, bdelta, bm, bv)]

    outs = [loss, grad_x[None]]
    for k in range(4):
        for n in W_NAMES:
            outs.append(big_out[k][n] if n in BIG else small_out[k][n])
    return tuple(outs)
```

```python
import math

import jax
import jax.numpy as jnp
from jax import lax
from jax.experimental import pallas as pl
from jax.experimental.pallas import tpu as pltpu

F32 = jnp.float32
BF16 = jnp.bfloat16

EPS = 1e-6
HEAD_DIM = 64
CHUNK = 16
N_DEV = 8
ADAM_LR = 0.001
ADAM_B1 = 0.9
ADAM_B2 = 0.999
ADAM_EPS = 1e-08
ADAM_WD = 0.01
ADAM_STEP = 10
VMEM_LIMIT = 48 * 1024 * 1024
LANES = 128
ATT_TQ = 128
ATT_TK = 256
ATT_HB_FWD = 4
ATT_HB_BWD = 4

W_NAMES = ['norm_mix', 'norm_ffn', 'norm_kv', 'norm_final', 'ssm_w_in', 'ssm_a_re', 'ssm_a_im', 'ssm_log_dt',
           'ssm_b_re', 'ssm_b_im', 'ssm_c_re', 'ssm_c_im', 'ssm_d', 'ssm_w_glu', 'kv_w', 'attn_w_q', 'attn_w_o',
           'ffn_w_up', 'ffn_conv_w', 'ffn_conv_b', 'ffn_w_down']
BIG = ['ssm_w_in', 'ssm_w_glu', 'kv_w', 'attn_w_q', 'attn_w_o', 'ffn_w_up', 'ffn_w_down', 'ffn_conv_w']
SMALL = [n for n in W_NAMES if n not in BIG]


def _pcall(body, **kw):
    return pl.pallas_call(body, **kw)


def _params(sem=None):
    if sem is None:
        return pltpu.CompilerParams(vmem_limit_bytes=VMEM_LIMIT)
    return pltpu.CompilerParams(dimension_semantics=sem, vmem_limit_bytes=VMEM_LIMIT)


def _tile(n, pref, mult=8):
    best = None
    for t in range(mult, min(n, pref) + 1, mult):
        if n % t == 0:
            best = t
    return n if best is None else best


def _mm(name, a, b, mode, out_dtype=F32, tm=512, tn=512, scale=None, resid=None):
    if mode == 'nn':
        (M, K), (K2, N) = a.shape, b.shape
    elif mode == 'nt':
        (M, K), (N, K2) = a.shape, b.shape
    else:
        (K, M), (K2, N) = a.shape, b.shape
    assert K == K2, (name, a.shape, b.shape)
    tm, tn = _tile(M, tm, LANES), _tile(N, tn, LANES)
    if mode == 'tn':
        a_spec = pl.BlockSpec((K, tm), lambda i, j: (0, i))
    else:
        a_spec = pl.BlockSpec((tm, K), lambda i, j: (i, 0))
    if mode == 'nt':
        b_spec = pl.BlockSpec((tn, K), lambda i, j: (j, 0))
    else:
        b_spec = pl.BlockSpec((K, tn), lambda i, j: (0, j))
    o_spec = pl.BlockSpec((tm, tn), lambda i, j: (i, j))
    dn = {'nn': ((1,), (0,)), 'nt': ((1,), (1,)), 'tn': ((0,), (0,))}[mode]

    def body(*refs):
        a_ref, b_ref, o_ref = refs[0], refs[1], refs[-1]
        acc = lax.dot_general(a_ref[...].astype(BF16), b_ref[...].astype(BF16), (dn, ((), ())),
                              preferred_element_type=F32)
        if scale is not None:
            acc = acc * scale
        if resid is not None:
            acc = acc + refs[2][...]
        o_ref[...] = acc.astype(o_ref.dtype)

    return _pcall(
        body, name=name, grid=(M // tm, N // tn),
        in_specs=[a_spec, b_spec] + ([o_spec] if resid is not None else []),
        out_specs=o_spec,
        out_shape=jax.ShapeDtypeStruct((M, N), out_dtype),
        compiler_params=_params(("parallel", "parallel")),
    )(*([a, b] + ([resid] if resid is not None else [])))


def _bmm(name, terms, out_dtype, gb=8, post=None, n_out=1):
    G = terms[0][0].shape[0]
    gb = _tile(G, gb, 1)
    dns = {'nn': ((1,), (0,)), 'nt': ((1,), (1,)), 'tn': ((0,), (0,))}

    def oshape(a, b, mode):
        m = a.shape[2] if mode == 'tn' else a.shape[1]
        n = b.shape[1] if mode == 'nt' else b.shape[2]
        return m, n

    m, n = oshape(*terms[0])
    out_dtypes = out_dtype if isinstance(out_dtype, (tuple, list)) else (out_dtype,)
    n_in = 2 * len(terms)

    def body(*refs):
        ins, outs = refs[:n_in], refs[n_in:]
        for gi in range(gb):
            acc = None
            for t, (_, _, mode) in enumerate(terms):
                part = lax.dot_general(ins[2 * t][gi].astype(BF16), ins[2 * t + 1][gi].astype(BF16),
                                       (dns[mode], ((), ())), preferred_element_type=F32)
                acc = part if acc is None else acc + part
            vals = (acc,) if post is None else post(acc)
            for o_ref, v in zip(outs, vals):
                o_ref[gi] = v.astype(o_ref.dtype)

    in_specs, args = [], []
    for a, b, _ in terms:
        in_specs += [pl.BlockSpec((gb,) + a.shape[1:], lambda g: (g, 0, 0)),
                     pl.BlockSpec((gb,) + b.shape[1:], lambda g: (g, 0, 0))]
        args += [a, b]
    res = _pcall(
        body, name=name, grid=(G // gb,), in_specs=in_specs,
        out_specs=[pl.BlockSpec((gb, m, n), lambda g: (g, 0, 0)) for _ in out_dtypes],
        out_shape=[jax.ShapeDtypeStruct((G, m, n), dt) for dt in out_dtypes],
        compiler_params=_params(("parallel",)),
    )(*args)
    return res[0] if len(out_dtypes) == 1 else res


def _rowwise(name, fn, n_steps, ins, outs, n_acc=0):
    n_in, n_out = len(ins), len(outs)

    def body(*refs):
        i = pl.program_id(0)
        vals = fn(i, *[r[...] for r in refs[:n_in]])
        o_refs = refs[n_in:]
        for j in range(n_out - n_acc):
            o_refs[j][...] = vals[j].astype(o_refs[j].dtype)
        if n_acc:
            @pl.when(i == 0)
            def _():
                for j in range(n_out - n_acc, n_out):
                    o_refs[j][...] = vals[j].astype(o_refs[j].dtype)

            @pl.when(i > 0)
            def _():
                for j in range(n_out - n_acc, n_out):
                    o_refs[j][...] += vals[j].astype(o_refs[j].dtype)

    res = _pcall(
        body, name=name, grid=(n_steps,),
        in_specs=[pl.BlockSpec(blk, im) for _, blk, im in ins],
        out_specs=[pl.BlockSpec(blk, im) for _, _, blk, im in outs],
        out_shape=[jax.ShapeDtypeStruct(s, d) for s, d, _, _ in outs],
        compiler_params=_params(("arbitrary",)),
    )(*[a for a, _, _ in ins])
    return res


def _rows(a, tm):
    return (a, (tm, a.shape[1]), lambda i: (i, 0))


def _whole(a):
    nd = a.ndim
    return (a, a.shape, lambda i: (0,) * nd)


def _orows(L, n, dtype, tm):
    return ((L, n), dtype, (tm, n), lambda i: (i, 0))


def _oacc(r, n):
    return ((r, n), F32, (r, n), lambda i: (0, 0))


def _rms_fwd(name, x, g, tm=512):
    L, D = x.shape
    tm = _tile(L, tm)

    def fn(i, xb, gb):
        r = lax.rsqrt(jnp.mean(xb * xb, axis=-1, keepdims=True) + EPS)
        return (xb * r * gb,)

    return _rowwise(name, fn, L // tm, [_rows(x, tm), _whole(g)], [_orows(L, D, BF16, tm)])[0]


def _rms_bwd(name, x, dres, branches, tm=256):
    L, D = x.shape
    tm = _tile(L, tm)
    nb = len(branches)

    def fn(i, xb, db, *rest):
        r = lax.rsqrt(jnp.mean(xb * xb, axis=-1, keepdims=True) + EPS)
        xh = xb * r
        dx = db
        dgs = []
        for b in range(nb):
            gb, dyb = rest[2 * b], rest[2 * b + 1].astype(F32)
            dxh = dyb * gb
            dx = dx + r * (dxh - xh * jnp.mean(dxh * xh, axis=-1, keepdims=True))
            dgs.append(jnp.sum(dyb * xh, axis=0, keepdims=True))
        return (dx, dx, *dgs)

    ins = [_rows(x, tm), _rows(dres, tm)]
    for g, dy in branches:
        ins += [_whole(g), _rows(dy, tm)]
    outs = [_orows(L, D, F32, tm), _orows(L, D, BF16, tm)] + [_oacc(1, D) for _ in range(nb)]
    return _rowwise(name, fn, L // tm, ins, outs, n_acc=nb)


def _final_loss(name, x, g, target, tm=256):
    L, D = x.shape
    tm = _tile(L, tm)

    def fn(i, xb, gb, tb):
        r = lax.rsqrt(jnp.mean(xb * xb, axis=-1, keepdims=True) + EPS)
        xh = xb * r
        err = xh * gb - tb
        dy = err * (1.0 / D)
        dxh = dy * gb
        dx = r * (dxh - xh * jnp.mean(dxh * xh, axis=-1, keepdims=True))
        dg = jnp.sum(dy * xh, axis=0, keepdims=True)
        per_row = jnp.mean(err * err, axis=-1, keepdims=True)
        loss = 0.5 * jnp.sum(per_row, axis=0, keepdims=True)
        return dx, dx, dg, jnp.broadcast_to(loss, (1, LANES))

    return _rowwise(name, fn, L // tm, [_rows(x, tm), _whole(g), _rows(target, tm)],
                    [_orows(L, D, F32, tm), _orows(L, D, BF16, tm), _oacc(1, D), _oacc(1, LANES)], n_acc=2)


def _glu_fwd(name, x, z, tm=256):
    L, D = x.shape
    tm = _tile(L, tm)

    def fn(i, xb, zb):
        return (xb + zb[:, :D] * jax.nn.sigmoid(zb[:, D:]),)

    return _rowwise(name, fn, L // tm, [_rows(x, tm), _rows(z, tm)], [_orows(L, D, F32, tm)])[0]


def _glu_bwd(name, dx, z, tm=256):
    L, D = dx.shape
    tm = _tile(L, tm)

    def fn(i, db, zb):
        z1, sg = zb[:, :D], jax.nn.sigmoid(zb[:, D:])
        return (jnp.concatenate([(db * sg).astype(BF16), (db * z1 * sg * (1.0 - sg)).astype(BF16)], axis=1),)

    return _rowwise(name, fn, L // tm, [_rows(dx, tm), _rows(z, tm)], [_orows(L, 2 * D, BF16, tm)])[0]


HALO = 16


def _halo_prev(a, tm):
    return (a, (HALO, a.shape[1]), lambda i: (jnp.maximum(i * (tm // HALO) - 1, 0), 0))


def _halo_next(a, tm):
    last = a.shape[0] // HALO - 1
    return (a, (HALO, a.shape[1]), lambda i: (jnp.minimum((i + 1) * (tm // HALO), last), 0))


def _shift_down(cur, halo, k, first):
    tm = cur.shape[0]
    rolled = pltpu.roll(cur, k, 0)
    tail = pltpu.roll(halo, k, 0)
    tail = jnp.where(first, 0.0, tail)
    row = lax.broadcasted_iota(jnp.int32, (tm, 1), 0)
    head = jnp.concatenate([tail, jnp.zeros((tm - 8, cur.shape[1]), cur.dtype)], axis=0) if tm > 8 else tail
    return jnp.where(row < k, head, rolled)


def _shift_up(cur, halo, k, last):
    tm = cur.shape[0]
    rolled = pltpu.roll(cur, tm - k, 0)
    head = pltpu.roll(halo, 8 - k, 0)
    head = jnp.where(last, 0.0, head)
    row = lax.broadcasted_iota(jnp.int32, (tm, 1), 0)
    tail = jnp.concatenate([jnp.zeros((tm - 8, cur.shape[1]), cur.dtype), head], axis=0) if tm > 8 else head
    return jnp.where(row >= tm - k, tail, rolled)


def _conv_pre(gb, hb, cw, cb, first):
    g1 = _shift_down(gb, hb, 1, first)
    g2 = _shift_down(gb, hb, 2, first)
    return cw[0:1] * g2 + cw[1:2] * g1 + cw[2:3] * gb + cb, g1, g2


def _gate_fwd(name, gu, cw, cb, tm=256):
    L, F2 = gu.shape
    Fh = F2 // 2
    tm = _tile(L, tm)

    def fn(i, gub, halo, cwb, cbb):
        gub, halo = gub.astype(F32), halo.astype(F32)[HALO - 8:]
        gc, _, _ = _conv_pre(gub[:, :Fh], halo[:, :Fh], cwb, cbb, i == 0)
        return (gc * jax.nn.sigmoid(gc) * gub[:, Fh:],)

    return _rowwise(name, fn, L // tm, [_rows(gu, tm), _halo_prev(gu, tm), _whole(cw), _whole(cb)],
                    [_orows(L, Fh, BF16, tm)])[0]


def _gate_bwd1(name, gu, da, cw, cb, tm=256):
    L, F2 = gu.shape
    Fh = F2 // 2
    tm = _tile(L, tm)

    def fn(i, gub, halo, dab, cwb, cbb):
        gub, halo = gub.astype(F32), halo.astype(F32)[HALO - 8:]
        gb, ub = gub[:, :Fh], gub[:, Fh:]
        gc, g1, g2 = _conv_pre(gb, halo[:, :Fh], cwb, cbb, i == 0)
        sg = jax.nn.sigmoid(gc)
        dab = dab.astype(F32)
        du = dab * gc * sg
        dgc = dab * ub * sg * (1.0 + gc * (1.0 - sg))
        return (dgc, du, jnp.sum(dgc * g2, axis=0, keepdims=True), jnp.sum(dgc * g1, axis=0, keepdims=True),
                jnp.sum(dgc * gb, axis=0, keepdims=True), jnp.sum(dgc, axis=0, keepdims=True))

    return _rowwise(name, fn, L // tm,
                    [_rows(gu, tm), _halo_prev(gu, tm), _rows(da, tm), _whole(cw), _whole(cb)],
                    [_orows(L, Fh, BF16, tm), _orows(L, Fh, BF16, tm)] + [_oacc(1, Fh) for _ in range(4)], n_acc=4)


def _gate_bwd2(name, dgc, du, cw, tm=256):
    L, Fh = dgc.shape
    tm = _tile(L, tm)
    n = L // tm

    def fn(i, db, halo, dub, cwb):
        db, halo = db.astype(F32), halo.astype(F32)[:8]
        d1 = _shift_up(db, halo, 1, i == n - 1)
        d2 = _shift_up(db, halo, 2, i == n - 1)
        dg = cwb[2:3] * db + cwb[1:2] * d1 + cwb[0:1] * d2
        return (jnp.concatenate([dg.astype(BF16), dub], axis=1),)

    return _rowwise(name, fn, n, [_rows(dgc, tm), _halo_next(dgc, tm), _rows(du, tm), _whole(cw)],
                    [_orows(L, 2 * Fh, BF16, tm)])[0]


def _s5_build(a_re, a_im, log_dt, b_re, b_im, c_re, c_im, d):
    T = CHUNK
    G, P = a_re.shape
    H = d.shape[1]
    hi = lax.Precision.HIGHEST
    dt = jnp.exp(log_dt)[:, None]
    mag = jnp.exp(a_re * dt)
    ab_re = mag * jnp.cos(a_im * dt)
    ab_im = mag * jnp.sin(a_im * dt)
    den = a_re * a_re + a_im * a_im
    f_re = ((ab_re - 1.0) * a_re + ab_im * a_im) / den
    f_im = (ab_im * a_re - (ab_re - 1.0) * a_im) / den
    bb_re = f_re[..., None] * b_re - f_im[..., None] * b_im
    bb_im = f_re[..., None] * b_im + f_im[..., None] * b_re
    tau = jnp.arange(T + 1, dtype=F32)[None, :, None]
    pmag = jnp.exp(tau * (a_re * dt)[:, None, :])
    pang = tau * (a_im * dt)[:, None, :]
    pw_re = pmag * jnp.cos(pang)
    pw_im = pmag * jnp.sin(pang)
    cp_re = c_re[:, :, None, :] * pw_re[:, None] - c_im[:, :, None, :] * pw_im[:, None]
    cp_im = c_re[:, :, None, :] * pw_im[:, None] + c_im[:, :, None, :] * pw_re[:, None]
    cp_cat = jnp.concatenate([cp_re, -cp_im], axis=-1)
    bb_cat = jnp.concatenate([bb_re, bb_im], axis=1)
    kt = jnp.einsum('ghtq,gqk->ghtk', cp_cat[:, :, :T], bb_cat, precision=hi)
    kt = kt.at[:, :, 0, :].add(d[:, :, None] * jnp.eye(H, dtype=F32)[None])
    kp = jnp.concatenate([kt[:, :, ::-1, :], jnp.zeros((G, H, T - 1, H), F32)], axis=2).reshape(G, H, (2 * T - 1) * H)
    m_mat = jnp.stack([kp[:, :, (T - 1 - t) * H:(2 * T - 1 - t) * H] for t in range(T)], axis=1)
    m_mat = m_mat.reshape(G, T * H, T * H)
    pr = jnp.repeat(pw_re[:, :T][:, ::-1].transpose(0, 2, 1), H, axis=2)
    pi = jnp.repeat(pw_im[:, :T][:, ::-1].transpose(0, 2, 1), H, axis=2)
    br, bi = jnp.tile(bb_re, (1, 1, T)), jnp.tile(bb_im, (1, 1, T))
    n_mat = jnp.concatenate([pr * br - pi * bi, pr * bi + pi * br], axis=1)
    o_mat = cp_cat[:, :, 1:].transpose(0, 2, 1, 3).reshape(G, T * H, 2 * P)
    lam1 = jnp.concatenate([pw_re[:, T], pw_re[:, T]], axis=-1)[:, None, :]
    lam2 = jnp.concatenate([-pw_im[:, T], pw_im[:, T]], axis=-1)[:, None, :]
    return m_mat, n_mat, o_mat, lam1, lam2


def _chunk_scan(name, v, lam1, lam2, reverse, s_fwd=None, gb=32):
    G, nc, W = v.shape
    gb = _tile(G, gb, 1)
    half = W // 2
    ntile = nc // 8

    def body(*refs):
        if reverse:
            v_ref, l1_ref, l2_ref, s_ref, o_ref, d1_ref, d2_ref = refs
        else:
            v_ref, l1_ref, l2_ref, o_ref = refs
        l1 = jnp.broadcast_to(l1_ref[...], (gb, 8, W))
        l2 = jnp.broadcast_to(l2_ref[...], (gb, 8, W))
        if reverse:
            l2 = -l2
        row = lax.broadcasted_iota(jnp.int32, (gb, 8, W), 1)

        def tile_step(n, carry):
            if reverse:
                st, a1, a2 = carry
                base = pl.multiple_of((ntile - 1 - n) * 8, 8)
            else:
                st = carry
                base = pl.multiple_of(n * 8, 8)
            vt = v_ref[:, pl.ds(base, 8), :]
            out = jnp.zeros((gb, 8, W), F32)
            order = range(7, -1, -1) if reverse else range(8)
            for r in order:
                out = jnp.where(row == r, st, out)
                vr = jnp.broadcast_to(vt[:, r:r + 1, :], (gb, 8, W))
                st = l1 * st + l2 * pltpu.roll(st, half, 2) + vr
            o_ref[:, pl.ds(base, 8), :] = out
            if reverse:
                sv = s_ref[:, pl.ds(base, 8), :]
                a1 = a1 + out * sv
                a2 = a2 + out * pltpu.roll(sv, half, 2)
                return st, a1, a2
            return st

        zero = jnp.zeros((gb, 8, W), F32)
        if reverse:
            _, a1, a2 = lax.fori_loop(0, ntile, tile_step, (zero, zero, zero))
            d1_ref[...] = jnp.sum(a1, axis=1, keepdims=True)
            d2_ref[...] = jnp.sum(a2, axis=1, keepdims=True)
        else:
            lax.fori_loop(0, ntile, tile_step, zero)

    big = pl.BlockSpec((gb, nc, W), lambda g: (g, 0, 0))
    vec = pl.BlockSpec((gb, 1, W), lambda g: (g, 0, 0))
    if reverse:
        return _pcall(body, name=name, grid=(G // gb,), in_specs=[big, vec, vec, big],
                      out_specs=[big, vec, vec],
                      out_shape=[jax.ShapeDtypeStruct((G, nc, W), F32), jax.ShapeDtypeStruct((G, 1, W), F32),
                                 jax.ShapeDtypeStruct((G, 1, W), F32)],
                      compiler_params=_params(("parallel",)))(v, lam1, lam2, s_fwd)
    return _pcall(body, name=name, grid=(G // gb,), in_specs=[big, vec, vec], out_specs=big,
                  out_shape=jax.ShapeDtypeStruct((G, nc, W), F32),
                  compiler_params=_params(("parallel",)))(v, lam1, lam2)


_GELU_C = math.sqrt(2.0 / math.pi)


def _gelu(y):
    return 0.5 * y * (1.0 + jnp.tanh(_GELU_C * (y + 0.044715 * y * y * y)))


def _gelu_grad(y):
    t = jnp.tanh(_GELU_C * (y + 0.044715 * y * y * y))
    return 0.5 * (1.0 + t) + 0.5 * y * (1.0 - t * t) * _GELU_C * (1.0 + 3.0 * 0.044715 * y * y)


def _lane_perm(Hg):
    gpb = LANES // Hg
    r = jnp.arange(CHUNK * LANES)
    t, gl, h = r // LANES, (r % LANES) // Hg, r % Hg
    target = gl * (CHUNK * Hg) + t * Hg + h
    return (target[:, None] == r[None, :]).astype(BF16), gpb


def _to_groups(name, a, G, Hg, out_dtype):
    L, D = a.shape
    nc = L // CHUNK
    perm, gpb = _lane_perm(Hg)
    gw = CHUNK * Hg

    def body(x_ref, p_ref, o_ref):
        x = jnp.concatenate([x_ref[:, t, :] for t in range(CHUNK)], axis=1)
        if x.dtype == BF16:
            z = jnp.dot(x, p_ref[...], preferred_element_type=F32)
        else:
            z = _split_dot(x, p_ref[...])
        for gl in range(gpb):
            o_ref[gl] = z[:, gl * gw:(gl + 1) * gw].astype(o_ref.dtype)

    return _pcall(
        body, name=name, grid=(G // gpb,),
        in_specs=[pl.BlockSpec((nc, CHUNK, LANES), lambda g: (0, 0, g)),
                  pl.BlockSpec(perm.shape, lambda g: (0, 0))],
        out_specs=pl.BlockSpec((gpb, nc, gw), lambda g: (g, 0, 0)),
        out_shape=jax.ShapeDtypeStruct((G, nc, gw), out_dtype),
        compiler_params=_params(("parallel",)),
    )(a.reshape(nc, CHUNK, D), perm)


def _from_groups(name, a, Hg):
    G, nc, gw = a.shape
    perm, gpb = _lane_perm(Hg)

    def body(x_ref, p_ref, o_ref):
        z = jnp.concatenate([x_ref[gl] for gl in range(gpb)], axis=1)
        x = lax.dot_general(z, p_ref[...], (((1,), (1,)), ((), ())), preferred_element_type=F32)
        for t in range(CHUNK):
            o_ref[:, t, :] = x[:, t * LANES:(t + 1) * LANES].astype(o_ref.dtype)

    out = _pcall(
        body, name=name, grid=(G // gpb,),
        in_specs=[pl.BlockSpec((gpb, nc, gw), lambda g: (g, 0, 0)),
                  pl.BlockSpec(perm.shape, lambda g: (0, 0))],
        out_specs=pl.BlockSpec((nc, CHUNK, LANES), lambda g: (0, 0, g)),
        out_shape=jax.ShapeDtypeStruct((nc, CHUNK, G * Hg), a.dtype),
        compiler_params=_params(("parallel",)),
    )(a, perm)
    return out.reshape(nc * CHUNK, G * Hg)


def _split_dot(x, u):
    hi = x.astype(BF16)
    lo = (x - hi.astype(F32)).astype(BF16)
    return (jnp.dot(hi, u, preferred_element_type=F32) + jnp.dot(lo, u, preferred_element_type=F32))


def _col_to_row(col, n):
    return jnp.broadcast_to(col, (n, LANES)).T[0:1, :]


def _row_to_col(row, n):
    return jnp.broadcast_to(row, (LANES, n)).T[:, 0:1]


def _attn_fwd(name, qw, kv, tq, tk, hb):
    H, L, dw = qw.shape
    nq, nkb = L // tq, L // tk
    hs = range(hb)
    assert tk == 2 * tq

    def body(q_ref, kv_ref, o_ref, rc_ref):
        i = pl.program_id(1)
        diff = (lax.broadcasted_iota(jnp.int32, (tq, tk), 1) - lax.broadcasted_iota(jnp.int32, (tq, tk), 0))
        u_suf = (lax.broadcasted_iota(jnp.int32, (tk, tk), 0)
                 > lax.broadcasted_iota(jnp.int32, (tk, tk), 1)).astype(BF16)
        nfull = i // 2
        rc_ref[...] = jnp.zeros_like(rc_ref)

        def block(kb, carry, masked):
            runs, accs = carry
            ks = pl.multiple_of(kb * tk, tk)
            kvb = [kv_ref[h, pl.ds(ks, tk), :] for h in hs]
            z = [lax.dot_general(q_ref[h], kvb[h], (((1,), (1,)), ((), ())), preferred_element_type=F32)
                 for h in hs]
            e = [jnp.exp(-jnp.abs(z[h])) for h in hs]
            sp = [jnp.maximum(z[h], 0.0) + jnp.log(1.0 + e[h]) for h in hs]
            if masked:
                causal = diff < (i * tq - ks)
                lom = [jnp.where(causal, -sp[h], 0.0) for h in hs]
            else:
                lom = [-sp[h] for h in hs]
            rem = [runs[h] + _split_dot(lom[h], u_suf) for h in hs]
            w = [jnp.exp(z[h] - sp[h] + rem[h]) for h in hs]
            if masked:
                w = [jnp.where(causal, w[h], 0.0) for h in hs]
            accs = tuple(accs[h] + jnp.dot(w[h].astype(BF16), kvb[h], preferred_element_type=F32) for h in hs)
            for h in hs:
                rc_ref[h, kb] = _col_to_row(runs[h], tq)
            runs = tuple(runs[h] + jnp.sum(lom[h], axis=1, keepdims=True) for h in hs)
            return runs, accs

        init = (tuple(jnp.zeros((tq, 1), F32) for _ in hs), tuple(jnp.zeros((tq, dw), F32) for _ in hs))
        carry = block(nfull, init, True)
        _, accs = lax.fori_loop(0, nfull, lambda n, c: block(nfull - 1 - n, c, False), carry)
        for h in hs:
            o_ref[h] = accs[h].astype(o_ref.dtype)

    qspec = pl.BlockSpec((hb, tq, dw), lambda h, i: (h, i, 0))
    kspec = pl.BlockSpec((hb, L, dw), lambda h, i: (h, 0, 0))
    return _pcall(
        body, name=name, grid=(H // hb, nq), in_specs=[qspec, kspec],
        out_specs=[qspec, pl.BlockSpec((hb, None, nkb, 1, tq), lambda h, i: (h, i, 0, 0, 0))],
        out_shape=[jax.ShapeDtypeStruct((H, L, dw), BF16), jax.ShapeDtypeStruct((H, nq, nkb, 1, tq), F32)],
        compiler_params=_params(("parallel", "arbitrary")),
    )(qw, kv)


def _attn_bwd(name, qw, kv, dow, rc, tq, tk, scale, hb):
    H, L, dw_ = qw.shape
    nq, nkb = L // tq, L // tk
    hs = range(hb)
    assert tk == 2 * tq

    def body(q_ref, kv_ref, do_ref, rc_ref, dq_ref, dkv_ref):
        i = pl.program_id(1)

        @pl.when(i == 0)
        def _():
            dkv_ref[...] = jnp.zeros_like(dkv_ref)

        diff = (lax.broadcasted_iota(jnp.int32, (tq, tk), 1) - lax.broadcasted_iota(jnp.int32, (tq, tk), 0))
        r_io = lax.broadcasted_iota(jnp.int32, (tk, tk), 0)
        c_io = lax.broadcasted_iota(jnp.int32, (tk, tk), 1)
        u_suf = (r_io > c_io).astype(BF16)
        u_pre = (r_io < c_io).astype(BF16)
        nfull = i // 2

        def block(kb, carry, masked):
            pres, dqs = carry
            ks = pl.multiple_of(kb * tk, tk)
            kvb = [kv_ref[h, pl.ds(ks, tk), :] for h in hs]
            qb = [q_ref[h] for h in hs]
            dob = [do_ref[h] for h in hs]
            z = [lax.dot_general(qb[h], kvb[h], (((1,), (1,)), ((), ())), preferred_element_type=F32) for h in hs]
            dw = [lax.dot_general(dob[h], kvb[h], (((1,), (1,)), ((), ())), preferred_element_type=F32)
                  for h in hs]
            e = [jnp.exp(-jnp.abs(z[h])) for h in hs]
            sp = [jnp.maximum(z[h], 0.0) + jnp.log(1.0 + e[h]) for h in hs]
            if masked:
                causal = diff < (i * tq - ks)
                lom = [jnp.where(causal, -sp[h], 0.0) for h in hs]
            else:
                lom = [-sp[h] for h in hs]
            rem = [_row_to_col(rc_ref[h, kb], tq) + _split_dot(lom[h], u_suf) for h in hs]
            logb = [z[h] - sp[h] for h in hs]
            w = [jnp.exp(logb[h] + rem[h]) for h in hs]
            if masked:
                w = [jnp.where(causal, w[h], 0.0) for h in hs]
            da = [dw[h] * w[h] for h in hs]
            p = [pres[h] + _split_dot(da[h], u_pre) for h in hs]
            dz = [da[h] - jnp.exp(logb[h]) * (da[h] + p[h]) for h in hs]
            if masked:
                dz = [jnp.where(causal, dz[h], 0.0) for h in hs]
            dqs = tuple(dqs[h] + jnp.dot(dz[h].astype(BF16), kvb[h], preferred_element_type=F32) for h in hs)
            for h in hs:
                lhs = jnp.concatenate([dz[h].T, w[h].T], axis=1).astype(BF16)
                rhs = jnp.concatenate([qb[h], dob[h]], axis=0)
                dkv_ref[h, pl.ds(ks, tk), :] += jnp.dot(lhs, rhs, preferred_element_type=F32)
            pres = tuple(pres[h] + jnp.sum(da[h], axis=1, keepdims=True) for h in hs)
            return pres, dqs

        init = (tuple(jnp.zeros((tq, 1), F32) for _ in hs), tuple(jnp.zeros((tq, dw_), F32) for _ in hs))
        carry = lax.fori_loop(0, nfull, lambda kb, c: block(kb, c, False), init)
        _, dqs = block(nfull, carry, True)
        for h in hs:
            dq_ref[h] = (dqs[h] * scale).astype(dq_ref.dtype)

    qspec = pl.BlockSpec((hb, tq, dw_), lambda h, i: (h, i, 0))
    kspec = pl.BlockSpec((hb, L, dw_), lambda h, i: (h, 0, 0))
    return _pcall(
        body, name=name, grid=(H // hb, nq),
        in_specs=[qspec, kspec, qspec, pl.BlockSpec((hb, None, nkb, 1, tq), lambda h, i: (h, i, 0, 0, 0))],
        out_specs=[qspec, kspec],
        out_shape=[jax.ShapeDtypeStruct((H, L, dw_), BF16), jax.ShapeDtypeStruct((H, L, dw_), F32)],
        compiler_params=_params(("parallel", "arbitrary")),
    )(qw, kv, dow, rc)


def _heads(a):
    L, D = a.shape
    return a.reshape(L, D // HEAD_DIM, HEAD_DIM).transpose(1, 0, 2)


def _unheads(a):
    H, L, dh = a.shape
    return a.transpose(1, 0, 2).reshape(L, H * dh)


_MESH = pl.DeviceIdType.MESH
_HBM = pl.BlockSpec(memory_space=pltpu.HBM)


def _all_gather(name, shards):
    n = len(shards)

    def body(*refs):
        x_refs, out_refs = refs[:n], refs[n:2 * n]
        send_sems, recv_sems, local_sems = refs[2 * n:]
        x, y, c = lax.axis_index("x"), lax.axis_index("y"), lax.axis_index("c")
        me, sibling = (x, y, c), (x, y, 1 - c)
        chips = [(1 - x, y), (x, 1 - y), (1 - x, 1 - y)]

        def slot(a, px, py, pc):
            return out_refs[a].at[4 * px + 2 * py + pc]

        def copy(a, k, block, to, src=None):
            return pltpu.make_async_remote_copy(
                src_ref=slot(a, *block) if src is None else src, dst_ref=slot(a, *block),
                send_sem=send_sems.at[7 * a + k], recv_sem=recv_sems.at[7 * a + k], device_id=to,
                device_id_type=_MESH)

        mine = [pltpu.make_async_copy(x_refs[a], slot(a, *me), local_sems.at[a]) for a in range(n)]
        for cp in mine:
            cp.start()
        first = []
        for a in range(n):
            first.append(copy(a, 0, me, sibling, src=x_refs[a]))
            first += [copy(a, 1 + j, me, (*chip, c), src=x_refs[a]) for j, chip in enumerate(chips)]
        for cp in first:
            cp.start()
        passed = []
        for j, chip in enumerate(chips):
            for a in range(n):
                copy(a, 1 + j, (*chip, c), me).wait_recv()
                passed.append(copy(a, 4 + j, (*chip, c), sibling))
                passed[-1].start()
        for a in range(n):
            copy(a, 0, sibling, me).wait_recv()
            for j, chip in enumerate(chips):
                copy(a, 4 + j, (*chip, 1 - c), me).wait_recv()
        for cp in first + passed:
            cp.wait_send()
        for cp in mine:
            cp.wait()

    return _pcall(
        body, name=name, out_shape=[jax.ShapeDtypeStruct((N_DEV,) + s.shape, s.dtype) for s in shards],
        in_specs=[_HBM] * n, out_specs=[_HBM] * n,
        scratch_shapes=[pltpu.SemaphoreType.DMA((7 * n,)), pltpu.SemaphoreType.DMA((7 * n,)),
                        pltpu.SemaphoreType.DMA((n,))],
    )(*shards)


def _sibling_exchange(name, xs):
    n = len(xs)

    def body(*refs):
        x_refs, out_refs, send_sems, recv_sems = refs[:n], refs[n:2 * n], refs[2 * n], refs[2 * n + 1]
        sibling = (lax.axis_index("x"), lax.axis_index("y"), 1 - lax.axis_index("c"))
        cps = [pltpu.make_async_remote_copy(src_ref=x_refs[a], dst_ref=out_refs[a], send_sem=send_sems.at[a],
                                            recv_sem=recv_sems.at[a], device_id=sibling, device_id_type=_MESH)
               for a in range(n)]
        for cp in cps:
            cp.start()
        for cp in cps:
            cp.wait()

    return _pcall(
        body, name=name, out_shape=[jax.ShapeDtypeStruct(x.shape, x.dtype) for x in xs],
        in_specs=[_HBM] * n, out_specs=[_HBM] * n,
        scratch_shapes=[pltpu.SemaphoreType.DMA((n,)), pltpu.SemaphoreType.DMA((n,))],
    )(*xs)


def _chip_exchange(name, xs):
    n = len(xs)

    def body(*refs):
        x_refs, out_refs, send_sems, recv_sems = refs[:n], refs[n:2 * n], refs[2 * n], refs[2 * n + 1]
        mx, my, mc = lax.axis_index("x"), lax.axis_index("y"), lax.axis_index("c")
        chips = [(1 - mx, my), (mx, 1 - my), (1 - mx, 1 - my)]
        cps = [pltpu.make_async_remote_copy(src_ref=x_refs[a].at[2 * px + py], dst_ref=out_refs[a].at[j],
                                            send_sem=send_sems.at[3 * a + j], recv_sem=recv_sems.at[3 * a + j],
                                            device_id=(px, py, mc), device_id_type=_MESH)
               for a in range(n) for j, (px, py) in enumerate(chips)]
        for cp in cps:
            cp.start()
        for cp in cps:
            cp.wait()

    return _pcall(
        body, name=name, out_shape=[jax.ShapeDtypeStruct((3,) + x.shape[1:], x.dtype) for x in xs],
        in_specs=[_HBM] * n, out_specs=[_HBM] * n,
        scratch_shapes=[pltpu.SemaphoreType.DMA((3 * n,)), pltpu.SemaphoreType.DMA((3 * n,))],
    )(*xs)


def _pair_sum(name, a, b):
    nb, R, C = a.shape
    tm = _tile(R, 512, 16)

    def fn(i, ab, bb):
        s = ab.astype(F32) + bb.astype(F32)
        return s, s

    spec = lambda arr: (arr, (1, tm, C), lambda i: (i // (R // tm), i % (R // tm), 0))
    out = lambda dt: ((nb, R, C), dt, (1, tm, C), lambda i: (i // (R // tm), i % (R // tm), 0))
    return _rowwise(name, fn, nb * (R // tm), [spec(a), spec(b)], [out(F32), out(a.dtype)])


def _sum8(name, g):
    n, R, C = g.shape
    tm = _tile(R, 256)

    def fn(i, gb):
        s = gb[0]
        for d in range(1, n):
            s = s + gb[d]
        return (s,)

    return _rowwise(name, fn, R // tm, [(g, (n, tm, C), lambda i: (0, i, 0))], [_orows(R, C, F32, tm)])[0]


def _adamw(name, w, m, v, grads, tm=512):
    R, C = w.shape
    tm = _tile(R, tm, 16)
    ng = len(grads)

    def fn(i, wb, mb, vb, *gs):
        g = gs[0].astype(F32)
        for t in gs[1:]:
            g = g + t.astype(F32)
        mn = ADAM_B1 * mb + (1.0 - ADAM_B1) * g
        vn = ADAM_B2 * vb + (1.0 - ADAM_B2) * (g * g)
        m_hat = mn / (1.0 - ADAM_B1 ** ADAM_STEP)
        v_hat = vn / (1.0 - ADAM_B2 ** ADAM_STEP)
        delta = -ADAM_LR * (m_hat / (jnp.sqrt(v_hat) + ADAM_EPS) + ADAM_WD * wb)
        return g, delta, mn, vn

    ins = [_rows(w, tm), _rows(m, tm), _rows(v, tm)] + [_rows(g, tm) for g in grads]
    return _rowwise(name, fn, R // tm, ins, [_orows(R, C, F32, tm) for _ in range(4)])


CLASSES = [
    (BF16, True, [('ssm_w_in', None), ('attn_w_q', None), ('attn_w_o', None), ('ffn_w_down', 0), ('ffn_w_down', 1)]),
    (BF16, False, [('ssm_w_glu', None), ('kv_w', None)]),
    (BF16, False, [('ffn_w_up', 0), ('ffn_w_up', 1)]),
    (F32, False, [('ffn_conv_w', 0), ('ffn_conv_w', 1)]),
]


def _shard2d(a, name, layer):
    if name == 'ffn_conv_w':
        return a[layer, :, 0, :]
    if name == 'kv_w':
        return a
    return a[0 if layer is None else layer]


def _pack_class(d, members, dtype):
    return jnp.concatenate([_shard2d(d[n], n, l).astype(dtype) for n, l in members], axis=0)


def _full_weights(gaths, d):
    w = {}
    for (_, row_sharded, members), g in zip(CLASSES, gaths):
        off = 0
        for n, l in members:
            r = _shard2d(d[n], n, l).shape[0]
            blk = g[:, off:off + r]
            if row_sharded:
                w[(n, l)] = blk.reshape(N_DEV * r, g.shape[2])
            else:
                w[(n, l)] = blk.transpose(1, 0, 2).reshape(r, N_DEV * g.shape[2])
            off += r
    return w


def _grad_classes(gw):
    out = []
    for dtype, row_sharded, members in CLASSES:
        parts = []
        for n, l in members:
            g = gw[(n, l)].astype(dtype)
            if row_sharded:
                parts.append(g.reshape(N_DEV, g.shape[0] // N_DEV, g.shape[1]))
            else:
                parts.append(g.reshape(g.shape[0], N_DEV, g.shape[1] // N_DEV).transpose(1, 0, 2))
        out.append(jnp.concatenate(parts, axis=1))
    return out


def _unpack_class(buf, members, d):
    out, off = {}, 0
    for n, l in members:
        r = _shard2d(d[n], n, l).shape[0]
        out[(n, l)] = buf[off:off + r]
        off += r
    return out


def _restore(parts, name, shape):
    layers = sorted(k[1] for k in parts if k[0] == name and k[1] is not None)
    if layers:
        return jnp.stack([parts[(name, l)] for l in layers]).reshape(shape)
    return parts[(name, None)].reshape(shape)


def _small_layout(shapes, D):
    lay, off = {}, 0
    for n in SMALL:
        r = -(-math.prod(shapes[n]) // D)
        lay[n] = (off, r)
        off += r
    lay['loss'] = (off, 1)
    off += 1
    return lay, -(-off // 8) * 8


def _pack_small(d, lay, total, D):
    parts = []
    for n in SMALL + ['loss']:
        if n not in d:
            parts.append(jnp.zeros((lay[n][1], D), F32))
            continue
        flat = d[n].reshape(-1).astype(F32)
        parts.append(jnp.pad(flat, (0, lay[n][1] * D - flat.shape[0])).reshape(lay[n][1], D))
    used = sum(lay[n][1] for n in SMALL + ['loss'])
    if total > used:
        parts.append(jnp.zeros((total - used, D), F32))
    return jnp.concatenate(parts, axis=0)


def _unpack_small(pack, lay, shapes, D):
    out = {}
    for n in SMALL:
        off, r = lay[n]
        out[n] = pack[off:off + r].reshape(-1)[:math.prod(shapes[n])].reshape(shapes[n])
    return out


def _ffn_fwd(tag, x, g_norm, w_up, conv_w, conv_b, w_down):
    h = _rms_fwd(f"{tag}_norm", x, g_norm)
    gu = _mm(f"{tag}_up", h, w_up, 'nn', BF16)
    a = _gate_fwd(f"{tag}_gate", gu, conv_w, conv_b)
    return _mm(f"{tag}_down", a, w_down, 'nn', F32, resid=x), (h, gu, a)


def _ffn_bwd(tag, dres, dres_bf, x, saved, g_norm, w_up, conv_w, conv_b, w_down):
    h, gu, a = saved
    da = _mm(f"{tag}_dgate", dres_bf, w_down, 'nt', BF16)
    d_w_down = _mm(f"{tag}_dwdown", a, dres_bf, 'tn', BF16)
    dgc, du, dw0, dw1, dw2, dcb = _gate_bwd1(f"{tag}_gate_b1", gu, da, conv_w, conv_b)
    dgu = _gate_bwd2(f"{tag}_gate_b2", dgc, du, conv_w)
    d_w_up = _mm(f"{tag}_dwup", h, dgu, 'tn', BF16)
    dh = _mm(f"{tag}_dh", dgu, w_up, 'nt', F32)
    dres, dres_bf, dg = _rms_bwd(f"{tag}_norm_b", x, dres, [(g_norm, dh)])
    return dres, dres_bf, dg, d_w_up, jnp.concatenate([dw0, dw1, dw2], axis=0), dcb[0], d_w_down


def kernel(x, norm_mix, norm_ffn, norm_kv, norm_final, ssm_w_in, ssm_a_re, ssm_a_im, ssm_log_dt, ssm_b_re, ssm_b_im, ssm_c_re, ssm_c_im, ssm_d, ssm_w_glu, kv_w, attn_w_q, attn_w_o, ffn_w_up, ffn_conv_w, ffn_conv_b, ffn_w_down, loss_target, m_norm_mix, m_norm_ffn, m_norm_kv, m_norm_final, m_ssm_w_in, m_ssm_a_re, m_ssm_a_im, m_ssm_log_dt, m_ssm_b_re, m_ssm_b_im, m_ssm_c_re, m_ssm_c_im, m_ssm_d, m_ssm_w_glu, m_kv_w, m_attn_w_q, m_attn_w_o, m_ffn_w_up, m_ffn_conv_w, m_ffn_conv_b, m_ffn_w_down, v_norm_mix, v_norm_ffn, v_norm_kv, v_norm_final, v_ssm_w_in, v_ssm_a_re, v_ssm_a_im, v_ssm_log_dt, v_ssm_b_re, v_ssm_b_im, v_ssm_c_re, v_ssm_c_im, v_ssm_d, v_ssm_w_glu, v_kv_w, v_attn_w_q, v_attn_w_o, v_ffn_w_up, v_ffn_conv_w, v_ffn_conv_b, v_ffn_w_down):
    wts = dict(zip(W_NAMES, (norm_mix, norm_ffn, norm_kv, norm_final, ssm_w_in, ssm_a_re, ssm_a_im, ssm_log_dt,
                             ssm_b_re, ssm_b_im, ssm_c_re, ssm_c_im, ssm_d, ssm_w_glu, kv_w, attn_w_q, attn_w_o,
                             ffn_w_up, ffn_conv_w, ffn_conv_b, ffn_w_down)))
    mom = dict(zip(W_NAMES, (m_norm_mix, m_norm_ffn, m_norm_kv, m_norm_final, m_ssm_w_in, m_ssm_a_re, m_ssm_a_im,
                             m_ssm_log_dt, m_ssm_b_re, m_ssm_b_im, m_ssm_c_re, m_ssm_c_im, m_ssm_d, m_ssm_w_glu,
                             m_kv_w, m_attn_w_q, m_attn_w_o, m_ffn_w_up, m_ffn_conv_w, m_ffn_conv_b, m_ffn_w_down)))
    vel = dict(zip(W_NAMES, (v_norm_mix, v_norm_ffn, v_norm_kv, v_norm_final, v_ssm_w_in, v_ssm_a_re, v_ssm_a_im,
                             v_ssm_log_dt, v_ssm_b_re, v_ssm_b_im, v_ssm_c_re, v_ssm_c_im, v_ssm_d, v_ssm_w_glu,
                             v_kv_w, v_attn_w_q, v_attn_w_o, v_ffn_w_up, v_ffn_conv_w, v_ffn_conv_b, v_ffn_w_down)))
    shapes = {n: wts[n].shape for n in W_NAMES}
    _, L, D = x.shape
    Fh = ffn_conv_b.shape[1]
    G, P = ssm_a_re.shape[1], ssm_a_re.shape[2]
    Hg = ssm_d.shape[2]
    x0 = x[0]
    target = loss_target[0]
    scale = HEAD_DIM ** -0.5

    gaths = _all_gather("gather_weights", [_pack_class(wts, members, dt) for dt, _, members in CLASSES])
    w = _full_weights(gaths, wts)
    nm = [norm_mix[l:l + 1] for l in range(2)]
    nf = [norm_ffn[l:l + 1] for l in range(2)]
    nkv = norm_kv[None]
    nfin = norm_final[None]
    cb = [ffn_conv_b[l:l + 1] for l in range(2)]
    cwt = [w[('ffn_conv_w', l)] for l in range(2)]

    s5p = (ssm_a_re[0], ssm_a_im[0], ssm_log_dt[0], ssm_b_re[0], ssm_b_im[0], ssm_c_re[0], ssm_c_im[0], ssm_d[0])
    (m_mat, n_mat, o_mat, lam1, lam2), s5_vjp = jax.vjp(_s5_build, *s5p)
    m_bf, n_bf, o_bf = m_mat.astype(BF16), n_mat.astype(BF16), o_mat.astype(BF16)

    h0 = _rms_fwd("l0_norm", x0, nm[0])
    u = _mm("l0_win", h0, w[('ssm_w_in', None)], 'nn', BF16)
    ug = _to_groups("s5_group_u", u, G, Hg, BF16)
    vloc = _bmm("s5_local_state", [(ug, n_bf, 'nt')], F32)
    st = _chunk_scan("s5_scan", vloc, lam1, lam2, reverse=False)
    yraw, yg_g = _bmm("s5_out", [(ug, m_bf, 'nt'), (st, o_bf, 'nt')], (F32, BF16),
                      post=lambda acc: (acc, _gelu(acc)))
    yg = _from_groups("s5_ungroup_y", yg_g, Hg)
    z = _mm("l0_wglu", yg, w[('ssm_w_glu', None)], 'nn', F32)
    x1 = _glu_fwd("l0_glu", x0, z)
    x2, ffn0 = _ffn_fwd("f0", x1, nf[0], w[('ffn_w_up', 0)], cwt[0], cb[0], w[('ffn_w_down', 0)])

    hkv = _rms_fwd("kv_norm", x2, nkv)
    kvp = _mm("kv_proj", hkv, w[('kv_w', None)], 'nn', BF16)
    h2 = _rms_fwd("l1_norm", x2, nm[1])
    qn = _mm("l1_wq", h2, w[('attn_w_q', None)], 'nn', BF16, scale=scale)
    nh = D // HEAD_DIM
    zeros_h = jnp.zeros((nh, L, HEAD_DIM), BF16)
    qw = jnp.concatenate([_heads(qn), zeros_h], axis=-1)
    kv = kvp.reshape(L, 2, nh, HEAD_DIM).transpose(2, 0, 1, 3).reshape(nh, L, 2 * HEAD_DIM)
    ow, rc = _attn_fwd("attn_fwd", qw, kv, ATT_TQ, ATT_TK, ATT_HB_FWD)
    o = _unheads(ow[:, :, HEAD_DIM:])
    x3 = _mm("l1_wo", o, w[('attn_w_o', None)], 'nn', F32, resid=x2)
    x4, ffn1 = _ffn_fwd("f1", x3, nf[1], w[('ffn_w_up', 1)], cwt[1], cb[1], w[('ffn_w_down', 1)])

    dres, dres_bf, dg_final, loss_part = _final_loss("loss_head", x4, nfin, target)

    gw = {}
    dres, dres_bf, dg_nf1, gup1, gcw1, gcb1, gdn1 = _ffn_bwd("f1", dres, dres_bf, x3, ffn1, nf[1], w[('ffn_w_up', 1)],
                                                             cwt[1], cb[1], w[('ffn_w_down', 1)])
    do = _mm("l1_do", dres_bf, w[('attn_w_o', None)], 'nt', BF16)
    gw[('attn_w_o', None)] = _mm("l1_dwo", o, dres_bf, 'tn', BF16)
    dow = jnp.concatenate([zeros_h, _heads(do)], axis=-1)
    dqw, dkv_h = _attn_bwd("attn_bwd", qw, kv, dow, rc, ATT_TQ, ATT_TK, scale, ATT_HB_BWD)
    dq = _unheads(dqw[:, :, :HEAD_DIM])
    dkv = dkv_h.reshape(nh, L, 2, HEAD_DIM).transpose(1, 2, 0, 3).reshape(L, 2 * D).astype(BF16)
    gw[('attn_w_q', None)] = _mm("l1_dwq", h2, dq, 'tn', BF16)
    dh2 = _mm("l1_dh", dq, w[('attn_w_q', None)], 'nt', F32)
    gw[('kv_w', None)] = _mm("kv_dw", hkv, dkv, 'tn', BF16)
    dhkv = _mm("kv_dh", dkv, w[('kv_w', None)], 'nt', F32)
    dres, dres_bf, dg_nm1, dg_nkv = _rms_bwd("l1_norm_b", x2, dres, [(nm[1], dh2), (nkv, dhkv)])
    dres, dres_bf, dg_nf0, gup0, gcw0, gcb0, gdn0 = _ffn_bwd("f0", dres, dres_bf, x1, ffn0, nf[0], w[('ffn_w_up', 0)],
                                                             cwt[0], cb[0], w[('ffn_w_down', 0)])
    dz = _glu_bwd("l0_glu_b", dres, z)
    gw[('ssm_w_glu', None)] = _mm("l0_dwglu", yg, dz, 'tn', BF16)
    dyg = _mm("l0_dyg", dz, w[('ssm_w_glu', None)], 'nt', F32)
    dyg_g = _to_groups("s5_group_dy", dyg, G, Hg, F32)
    gshape = yraw.shape
    dy = _rowwise("s5_gelu_b", lambda i, a, b: (a * _gelu_grad(b),), G,
                  [(dyg_g, (1,) + gshape[1:], lambda i: (i, 0, 0)), (yraw, (1,) + gshape[1:], lambda i: (i, 0, 0))],
                  [(gshape, BF16, (1,) + gshape[1:], lambda i: (i, 0, 0))])[0]
    ds = _bmm("s5_dstate", [(dy, o_bf, 'nn')], F32)
    dv_loc, dlam1, dlam2 = _chunk_scan("s5_scan_b", ds, lam1, lam2, reverse=True, s_fwd=st)
    du_g = _bmm("s5_du", [(dy, m_bf, 'nn'), (dv_loc, n_bf, 'nn')], BF16)
    d_m = _bmm("s5_dm", [(dy, ug, 'tn')], F32)
    d_o = _bmm("s5_do", [(dy, st, 'tn')], F32)
    d_n = _bmm("s5_dn", [(dv_loc, ug, 'tn')], F32)
    s5g = s5_vjp((d_m, d_n, d_o, dlam1, dlam2))
    du = _from_groups("s5_ungroup_du", du_g, Hg)
    gw[('ssm_w_in', None)] = _mm("l0_dwin", h0, du, 'tn', BF16)
    dh0 = _mm("l0_dh", du, w[('ssm_w_in', None)], 'nt', F32)
    grad_x, _, dg_nm0 = _rms_bwd("l0_norm_b", x0, dres, [(nm[0], dh0)])

    gw.update({('ffn_w_up', 0): gup0, ('ffn_w_up', 1): gup1, ('ffn_w_down', 0): gdn0, ('ffn_w_down', 1): gdn1,
               ('ffn_conv_w', 0): gcw0, ('ffn_conv_w', 1): gcw1})

    small_g = {
        'norm_mix': jnp.concatenate([dg_nm0, dg_nm1], axis=0), 'norm_ffn': jnp.concatenate([dg_nf0, dg_nf1], axis=0),
        'norm_kv': dg_nkv, 'norm_final': dg_final, 'ffn_conv_b': jnp.stack([gcb0, gcb1]),
        'ssm_a_re': s5g[0], 'ssm_a_im': s5g[1], 'ssm_log_dt': s5g[2], 'ssm_b_re': s5g[3], 'ssm_b_im': s5g[4],
        'ssm_c_re': s5g[5], 'ssm_c_im': s5g[6], 'ssm_d': s5g[7], 'loss': loss_part[0, 0:1],
    }
    lay, rs = _small_layout(shapes, D)
    small_sum = _sum8("small_sum", _all_gather("gather_small", [_pack_small(small_g, lay, rs, D)])[0])
    loss = small_sum[lay['loss'][0], 0]
    sg, sdelta, sm, sv = _adamw("adamw_small", _pack_small(wts, lay, rs, D), _pack_small(mom, lay, rs, D),
                                _pack_small(vel, lay, rs, D), [small_sum])
    small_out = [_unpack_small(t, lay, shapes, D) for t in (sg, sdelta, sm, sv)]

    c = lax.axis_index("c")
    chip = 2 * lax.axis_index("x") + lax.axis_index("y")
    gcls = [g.reshape((4, 2) + g.shape[1:]) for g in _grad_classes(gw)]
    mine = [lax.dynamic_index_in_dim(g, c, axis=1, keepdims=False) for g in gcls]
    theirs = [lax.dynamic_index_in_dim(g, 1 - c, axis=1, keepdims=False) for g in gcls]
    got = _sibling_exchange("rs_sibling", theirs)
    sums = [_pair_sum(f"rs_pair_sum{k}", a, b) for k, (a, b) in enumerate(zip(mine, got))]
    from_chips = _chip_exchange("rs_chips", [on_wire for _, on_wire in sums])
    big_parts = [{}, {}, {}, {}]
    for k, ((_, _, members), (chip_f32, _), fc) in enumerate(zip(CLASSES, sums, from_chips)):
        own = lax.dynamic_index_in_dim(chip_f32, chip, axis=0, keepdims=False)
        res = _adamw(f"adamw_big{k}", _pack_class(wts, members, F32), _pack_class(mom, members, F32),
                     _pack_class(vel, members, F32), [own, fc[0], fc[1], fc[2]])
        for q in range(4):
            big_parts[q].update(_unpack_class(res[q], members, wts))
    big_out = [{n: _restore(big_parts[q], n, shapes[n]) for n in BIG} for q in range(4)]

    outs = [loss, grad_x[None]]
    for k in range(4):
        for n in W_NAMES:
            outs.append(big_out[k][n] if n in BIG else small_out[k][n])
    return tuple(outs)
```

```python
import math

import jax
import jax.numpy as jnp
from jax import lax
from jax.experimental import pallas as pl
from jax.experimental.pallas import tpu as pltpu

F32 = jnp.float32
BF16 = jnp.bfloat16

EPS = 1e-6
HEAD_DIM = 64
CHUNK = 16
N_DEV = 8
ADAM_LR = 0.001
ADAM_B1 = 0.9
ADAM_B2 = 0.999
ADAM_EPS = 1e-08
ADAM_WD = 0.01
ADAM_STEP = 10
VMEM_LIMIT = 48 * 1024 * 1024
LANES = 128
ATT_TQ = 128
ATT_TK = 256
ATT_HB_FWD = 4
ATT_HB_BWD = 4

W_NAMES = ['norm_mix', 'norm_ffn', 'norm_kv', 'norm_final', 'ssm_w_in', 'ssm_a_re', 'ssm_a_im', 'ssm_log_dt',
           'ssm_b_re', 'ssm_b_im', 'ssm_c_re', 'ssm_c_im', 'ssm_d', 'ssm_w_glu', 'kv_w', 'attn_w_q', 'attn_w_o',
           'ffn_w_up', 'ffn_conv_w', 'ffn_conv_b', 'ffn_w_down']
BIG = ['ssm_w_in', 'ssm_w_glu', 'kv_w', 'attn_w_q', 'attn_w_o', 'ffn_w_up', 'ffn_w_down', 'ffn_conv_w']
SMALL = [n for n in W_NAMES if n not in BIG]


def _pcall(body, **kw):
    return pl.pallas_call(body, **kw)


def _params(sem=None):
    if sem is None:
        return pltpu.CompilerParams(vmem_limit_bytes=VMEM_LIMIT)
    return pltpu.CompilerParams(dimension_semantics=sem, vmem_limit_bytes=VMEM_LIMIT)


def _tile(n, pref, mult=8):
    best = None
    for t in range(mult, min(n, pref) + 1, mult):
        if n % t == 0:
            best = t
    return n if best is None else best


def _mm(name, a, b, mode, out_dtype=F32, tm=512, tn=512, scale=None, resid=None):
    if mode == 'nn':
        (M, K), (K2, N) = a.shape, b.shape
    elif mode == 'nt':
        (M, K), (N, K2) = a.shape, b.shape
    else:
        (K, M), (K2, N) = a.shape, b.shape
    assert K == K2, (name, a.shape, b.shape)
    tm, tn = _tile(M, tm, LANES), _tile(N, tn, LANES)
    if mode == 'tn':
        a_spec = pl.BlockSpec((K, tm), lambda i, j: (0, i))
    else:
        a_spec = pl.BlockSpec((tm, K), lambda i, j: (i, 0))
    if mode == 'nt':
        b_spec = pl.BlockSpec((tn, K), lambda i, j: (j, 0))
    else:
        b_spec = pl.BlockSpec((K, tn), lambda i, j: (0, j))
    o_spec = pl.BlockSpec((tm, tn), lambda i, j: (i, j))
    dn = {'nn': ((1,), (0,)), 'nt': ((1,), (1,)), 'tn': ((0,), (0,))}[mode]

    def body(*refs):
        a_ref, b_ref, o_ref = refs[0], refs[1], refs[-1]
        acc = lax.dot_general(a_ref[...].astype(BF16), b_ref[...].astype(BF16), (dn, ((), ())),
                              preferred_element_type=F32)
        if scale is not None:
            acc = acc * scale
        if resid is not None:
            acc = acc + refs[2][...]
        o_ref[...] = acc.astype(o_ref.dtype)

    return _pcall(
        body, name=name, grid=(M // tm, N // tn),
        in_specs=[a_spec, b_spec] + ([o_spec] if resid is not None else []),
        out_specs=o_spec,
        out_shape=jax.ShapeDtypeStruct((M, N), out_dtype),
        compiler_params=_params(("parallel", "parallel")),
    )(*([a, b] + ([resid] if resid is not None else [])))


def _bmm(name, terms, out_dtype, gb=8, post=None, n_out=1):
    G = terms[0][0].shape[0]
    gb = _tile(G, gb, 1)
    dns = {'nn': ((1,), (0,)), 'nt': ((1,), (1,)), 'tn': ((0,), (0,))}

    def oshape(a, b, mode):
        m = a.shape[2] if mode == 'tn' else a.shape[1]
        n = b.shape[1] if mode == 'nt' else b.shape[2]
        return m, n

    m, n = oshape(*terms[0])
    out_dtypes = out_dtype if isinstance(out_dtype, (tuple, list)) else (out_dtype,)
    n_in = 2 * len(terms)

    def body(*refs):
        ins, outs = refs[:n_in], refs[n_in:]
        for gi in range(gb):
            acc = None
            for t, (_, _, mode) in enumerate(terms):
                part = lax.dot_general(ins[2 * t][gi].astype(BF16), ins[2 * t + 1][gi].astype(BF16),
                                       (dns[mode], ((), ())), preferred_element_type=F32)
                acc = part if acc is None else acc + part
            vals = (acc,) if post is None else post(acc)
            for o_ref, v in zip(outs, vals):
                o_ref[gi] = v.astype(o_ref.dtype)

    in_specs, args = [], []
    for a, b, _ in terms:
        in_specs += [pl.BlockSpec((gb,) + a.shape[1:], lambda g: (g, 0, 0)),
                     pl.BlockSpec((gb,) + b.shape[1:], lambda g: (g, 0, 0))]
        args += [a, b]
    res = _pcall(
        body, name=name, grid=(G // gb,), in_specs=in_specs,
        out_specs=[pl.BlockSpec((gb, m, n), lambda g: (g, 0, 0)) for _ in out_dtypes],
        out_shape=[jax.ShapeDtypeStruct((G, m, n), dt) for dt in out_dtypes],
        compiler_params=_params(("parallel",)),
    )(*args)
    return res[0] if len(out_dtypes) == 1 else res


def _rowwise(name, fn, n_steps, ins, outs, n_acc=0):
    n_in, n_out = len(ins), len(outs)

    def body(*refs):
        i = pl.program_id(0)
        vals = fn(i, *[r[...] for r in refs[:n_in]])
        o_refs = refs[n_in:]
        for j in range(n_out - n_acc):
            o_refs[j][...] = vals[j].astype(o_refs[j].dtype)
        if n_acc:
            @pl.when(i == 0)
            def _():
                for j in range(n_out - n_acc, n_out):
                    o_refs[j][...] = vals[j].astype(o_refs[j].dtype)

            @pl.when(i > 0)
            def _():
                for j in range(n_out - n_acc, n_out):
                    o_refs[j][...] += vals[j].astype(o_refs[j].dtype)

    res = _pcall(
        body, name=name, grid=(n_steps,),
        in_specs=[pl.BlockSpec(blk, im) for _, blk, im in ins],
        out_specs=[pl.BlockSpec(blk, im) for _, _, blk, im in outs],
        out_shape=[jax.ShapeDtypeStruct(s, d) for s, d, _, _ in outs],
        compiler_params=_params(("arbitrary",)),
    )(*[a for a, _, _ in ins])
    return res


def _rows(a, tm):
    return (a, (tm, a.shape[1]), lambda i: (i, 0))


def _whole(a):
    nd = a.ndim
    return (a, a.shape, lambda i: (0,) * nd)


def _orows(L, n, dtype, tm):
    return ((L, n), dtype, (tm, n), lambda i: (i, 0))


def _oacc(r, n):
    return ((r, n), F32, (r, n), lambda i: (0, 0))


def _rms_fwd(name, x, g, tm=512):
    L, D = x.shape
    tm = _tile(L, tm)

    def fn(i, xb, gb):
        r = lax.rsqrt(jnp.mean(xb * xb, axis=-1, keepdims=True) + EPS)
        return (xb * r * gb,)

    return _rowwise(name, fn, L // tm, [_rows(x, tm), _whole(g)], [_orows(L, D, BF16, tm)])[0]


def _rms_bwd(name, x, dres, branches, tm=256):
    L, D = x.shape
    tm = _tile(L, tm)
    nb = len(branches)

    def fn(i, xb, db, *rest):
        r = lax.rsqrt(jnp.mean(xb * xb, axis=-1, keepdims=True) + EPS)
        xh = xb * r
        dx = db
        dgs = []
        for b in range(nb):
            gb, dyb = rest[2 * b], rest[2 * b + 1].astype(F32)
            dxh = dyb * gb
            dx = dx + r * (dxh - xh * jnp.mean(dxh * xh, axis=-1, keepdims=True))
            dgs.append(jnp.sum(dyb * xh, axis=0, keepdims=True))
        return (dx, dx, *dgs)

    ins = [_rows(x, tm), _rows(dres, tm)]
    for g, dy in branches:
        ins += [_whole(g), _rows(dy, tm)]
    outs = [_orows(L, D, F32, tm), _orows(L, D, BF16, tm)] + [_oacc(1, D) for _ in range(nb)]
    return _rowwise(name, fn, L // tm, ins, outs, n_acc=nb)


def _final_loss(name, x, g, target, tm=256):
    L, D = x.shape
    tm = _tile(L, tm)

    def fn(i, xb, gb, tb):
        r = lax.rsqrt(jnp.mean(xb * xb, axis=-1, keepdims=True) + EPS)
        xh = xb * r
        err = xh * gb - tb
        dy = err * (1.0 / D)
        dxh = dy * gb
        dx = r * (dxh - xh * jnp.mean(dxh * xh, axis=-1, keepdims=True))
        dg = jnp.sum(dy * xh, axis=0, keepdims=True)
        per_row = jnp.mean(err * err, axis=-1, keepdims=True)
        loss = 0.5 * jnp.sum(per_row, axis=0, keepdims=True)
        return dx, dx, dg, jnp.broadcast_to(loss, (1, LANES))

    return _rowwise(name, fn, L // tm, [_rows(x, tm), _whole(g), _rows(target, tm)],
                    [_orows(L, D, F32, tm), _orows(L, D, BF16, tm), _oacc(1, D), _oacc(1, LANES)], n_acc=2)


def _glu_fwd(name, x, z, tm=256):
    L, D = x.shape
    tm = _tile(L, tm)

    def fn(i, xb, zb):
        return (xb + zb[:, :D] * jax.nn.sigmoid(zb[:, D:]),)

    return _rowwise(name, fn, L // tm, [_rows(x, tm), _rows(z, tm)], [_orows(L, D, F32, tm)])[0]


def _glu_bwd(name, dx, z, tm=256):
    L, D = dx.shape
    tm = _tile(L, tm)

    def fn(i, db, zb):
        z1, sg = zb[:, :D], jax.nn.sigmoid(zb[:, D:])
        return (jnp.concatenate([(db * sg).astype(BF16), (db * z1 * sg * (1.0 - sg)).astype(BF16)], axis=1),)

    return _rowwise(name, fn, L // tm, [_rows(dx, tm), _rows(z, tm)], [_orows(L, 2 * D, BF16, tm)])[0]


HALO = 16


def _halo_prev(a, tm):
    return (a, (HALO, a.shape[1]), lambda i: (jnp.maximum(i * (tm // HALO) - 1, 0), 0))


def _halo_next(a, tm):
    last = a.shape[0] // HALO - 1
    return (a, (HALO, a.shape[1]), lambda i: (jnp.minimum((i + 1) * (tm // HALO), last), 0))


def _shift_down(cur, halo, k, first):
    tm = cur.shape[0]
    rolled = pltpu.roll(cur, k, 0)
    tail = pltpu.roll(halo, k, 0)
    tail = jnp.where(first, 0.0, tail)
    row = lax.broadcasted_iota(jnp.int32, (tm, 1), 0)
    head = jnp.concatenate([tail, jnp.zeros((tm - 8, cur.shape[1]), cur.dtype)], axis=0) if tm > 8 else tail
    return jnp.where(row < k, head, rolled)


def _shift_up(cur, halo, k, last):
    tm = cur.shape[0]
    rolled = pltpu.roll(cur, tm - k, 0)
    head = pltpu.roll(halo, 8 - k, 0)
    head = jnp.where(last, 0.0, head)
    row = lax.broadcasted_iota(jnp.int32, (tm, 1), 0)
    tail = jnp.concatenate([jnp.zeros((tm - 8, cur.shape[1]), cur.dtype), head], axis=0) if tm > 8 else head
    return jnp.where(row >= tm - k, tail, rolled)


def _conv_pre(gb, hb, cw, cb, first):
    g1 = _shift_down(gb, hb, 1, first)
    g2 = _shift_down(gb, hb, 2, first)
    return cw[0:1] * g2 + cw[1:2] * g1 + cw[2:3] * gb + cb, g1, g2


def _gate_fwd(name, gu, cw, cb, tm=256):
    L, F2 = gu.shape
    Fh = F2 // 2
    tm = _tile(L, tm)

    def fn(i, gub, halo, cwb, cbb):
        gub, halo = gub.astype(F32), halo.astype(F32)[HALO - 8:]
        gc, _, _ = _conv_pre(gub[:, :Fh], halo[:, :Fh], cwb, cbb, i == 0)
        return (gc * jax.nn.sigmoid(gc) * gub[:, Fh:],)

    return _rowwise(name, fn, L // tm, [_rows(gu, tm), _halo_prev(gu, tm), _whole(cw), _whole(cb)],
                    [_orows(L, Fh, BF16, tm)])[0]


def _gate_bwd1(name, gu, da, cw, cb, tm=256):
    L, F2 = gu.shape
    Fh = F2 // 2
    tm = _tile(L, tm)

    def fn(i, gub, halo, dab, cwb, cbb):
        gub, halo = gub.astype(F32), halo.astype(F32)[HALO - 8:]
        gb, ub = gub[:, :Fh], gub[:, Fh:]
        gc, g1, g2 = _conv_pre(gb, halo[:, :Fh], cwb, cbb, i == 0)
        sg = jax.nn.sigmoid(gc)
        dab = dab.astype(F32)
        du = dab * gc * sg
        dgc = dab * ub * sg * (1.0 + gc * (1.0 - sg))
        return (dgc, du, jnp.sum(dgc * g2, axis=0, keepdims=True), jnp.sum(dgc * g1, axis=0, keepdims=True),
                jnp.sum(dgc * gb, axis=0, keepdims=True), jnp.sum(dgc, axis=0, keepdims=True))

    return _rowwise(name, fn, L // tm,
                    [_rows(gu, tm), _halo_prev(gu, tm), _rows(da, tm), _whole(cw), _whole(cb)],
                    [_orows(L, Fh, BF16, tm), _orows(L, Fh, BF16, tm)] + [_oacc(1, Fh) for _ in range(4)], n_acc=4)


def _gate_bwd2(name, dgc, du, cw, tm=256):
    L, Fh = dgc.shape
    tm = _tile(L, tm)
    n = L // tm

    def fn(i, db, halo, dub, cwb):
        db, halo = db.astype(F32), halo.astype(F32)[:8]
        d1 = _shift_up(db, halo, 1, i == n - 1)
        d2 = _shift_up(db, halo, 2, i == n - 1)
        dg = cwb[2:3] * db + cwb[1:2] * d1 + cwb[0:1] * d2
        return (jnp.concatenate([dg.astype(BF16), dub], axis=1),)

    return _rowwise(name, fn, n, [_rows(dgc, tm), _halo_next(dgc, tm), _rows(du, tm), _whole(cw)],
                    [_orows(L, 2 * Fh, BF16, tm)])[0]


def _s5_build(a_re, a_im, log_dt, b_re, b_im, c_re, c_im, d):
    T = CHUNK
    G, P = a_re.shape
    H = d.shape[1]
    hi = lax.Precision.HIGHEST
    dt = jnp.exp(log_dt)[:, None]
    mag = jnp.exp(a_re * dt)
    ab_re = mag * jnp.cos(a_im * dt)
    ab_im = mag * jnp.sin(a_im * dt)
    den = a_re * a_re + a_im * a_im
    f_re = ((ab_re - 1.0) * a_re + ab_im * a_im) / den
    f_im = (ab_im * a_re - (ab_re - 1.0) * a_im) / den
    bb_re = f_re[..., None] * b_re - f_im[..., None] * b_im
    bb_im = f_re[..., None] * b_im + f_im[..., None] * b_re
    tau = jnp.arange(T + 1, dtype=F32)[None, :, None]
    pmag = jnp.exp(tau * (a_re * dt)[:, None, :])
    pang = tau * (a_im * dt)[:, None, :]
    pw_re = pmag * jnp.cos(pang)
    pw_im = pmag * jnp.sin(pang)
    cp_re = c_re[:, :, None, :] * pw_re[:, None] - c_im[:, :, None, :] * pw_im[:, None]
    cp_im = c_re[:, :, None, :] * pw_im[:, None] + c_im[:, :, None, :] * pw_re[:, None]
    cp_cat = jnp.concatenate([cp_re, -cp_im], axis=-1)
    bb_cat = jnp.concatenate([bb_re, bb_im], axis=1)
    kt = jnp.einsum('ghtq,gqk->ghtk', cp_cat[:, :, :T], bb_cat, precision=hi)
    kt = kt.at[:, :, 0, :].add(d[:, :, None] * jnp.eye(H, dtype=F32)[None])
    kp = jnp.concatenate([kt[:, :, ::-1, :], jnp.zeros((G, H, T - 1, H), F32)], axis=2).reshape(G, H, (2 * T - 1) * H)
    m_mat = jnp.stack([kp[:, :, (T - 1 - t) * H:(2 * T - 1 - t) * H] for t in range(T)], axis=1)
    m_mat = m_mat.reshape(G, T * H, T * H)
    pr = jnp.repeat(pw_re[:, :T][:, ::-1].transpose(0, 2, 1), H, axis=2)
    pi = jnp.repeat(pw_im[:, :T][:, ::-1].transpose(0, 2, 1), H, axis=2)
    br, bi = jnp.tile(bb_re, (1, 1, T)), jnp.tile(bb_im, (1, 1, T))
    n_mat = jnp.concatenate([pr * br - pi * bi, pr * bi + pi * br], axis=1)
    o_mat = cp_cat[:, :, 1:].transpose(0, 2, 1, 3).reshape(G, T * H, 2 * P)
    lam1 = jnp.concatenate([pw_re[:, T], pw_re[:, T]], axis=-1)[:, None, :]
    lam2 = jnp.concatenate([-pw_im[:, T], pw_im[:, T]], axis=-1)[:, None, :]
    return m_mat, n_mat, o_mat, lam1, lam2


def _chunk_scan(name, v, lam1, lam2, reverse, s_fwd=None, gb=32):
    G, nc, W = v.shape
    gb = _tile(G, gb, 1)
    half = W // 2
    ntile = nc // 8

    def body(*refs):
        if reverse:
            v_ref, l1_ref, l2_ref, s_ref, o_ref, d1_ref, d2_ref = refs
        else:
            v_ref, l1_ref, l2_ref, o_ref = refs
        l1 = jnp.broadcast_to(l1_ref[...], (gb, 8, W))
        l2 = jnp.broadcast_to(l2_ref[...], (gb, 8, W))
        if reverse:
            l2 = -l2
        row = lax.broadcasted_iota(jnp.int32, (gb, 8, W), 1)

        def tile_step(n, carry):
            if reverse:
                st, a1, a2 = carry
                base = pl.multiple_of((ntile - 1 - n) * 8, 8)
            else:
                st = carry
                base = pl.multiple_of(n * 8, 8)
            vt = v_ref[:, pl.ds(base, 8), :]
            out = jnp.zeros((gb, 8, W), F32)
            order = range(7, -1, -1) if reverse else range(8)
            for r in order:
                out = jnp.where(row == r, st, out)
                vr = jnp.broadcast_to(vt[:, r:r + 1, :], (gb, 8, W))
                st = l1 * st + l2 * pltpu.roll(st, half, 2) + vr
            o_ref[:, pl.ds(base, 8), :] = out
            if reverse:
                sv = s_ref[:, pl.ds(base, 8), :]
                a1 = a1 + out * sv
                a2 = a2 + out * pltpu.roll(sv, half, 2)
                return st, a1, a2
            return st

        zero = jnp.zeros((gb, 8, W), F32)
        if reverse:
            _, a1, a2 = lax.fori_loop(0, ntile, tile_step, (zero, zero, zero))
            d1_ref[...] = jnp.sum(a1, axis=1, keepdims=True)
            d2_ref[...] = jnp.sum(a2, axis=1, keepdims=True)
        else:
            lax.fori_loop(0, ntile, tile_step, zero)

    big = pl.BlockSpec((gb, nc, W), lambda g: (g, 0, 0))
    vec = pl.BlockSpec((gb, 1, W), lambda g: (g, 0, 0))
    if reverse:
        return _pcall(body, name=name, grid=(G // gb,), in_specs=[big, vec, vec, big],
                      out_specs=[big, vec, vec],
                      out_shape=[jax.ShapeDtypeStruct((G, nc, W), F32), jax.ShapeDtypeStruct((G, 1, W), F32),
                                 jax.ShapeDtypeStruct((G, 1, W), F32)],
                      compiler_params=_params(("parallel",)))(v, lam1, lam2, s_fwd)
    return _pcall(body, name=name, grid=(G // gb,), in_specs=[big, vec, vec], out_specs=big,
                  out_shape=jax.ShapeDtypeStruct((G, nc, W), F32),
                  compiler_params=_params(("parallel",)))(v, lam1, lam2)


_GELU_C = math.sqrt(2.0 / math.pi)


def _gelu(y):
    return 0.5 * y * (1.0 + jnp.tanh(_GELU_C * (y + 0.044715 * y * y * y)))


def _gelu_grad(y):
    t = jnp.tanh(_GELU_C * (y + 0.044715 * y * y * y))
    return 0.5 * (1.0 + t) + 0.5 * y * (1.0 - t * t) * _GELU_C * (1.0 + 3.0 * 0.044715 * y * y)


def _lane_perm(Hg):
    gpb = LANES // Hg
    r = jnp.arange(CHUNK * LANES)
    t, gl, h = r // LANES, (r % LANES) // Hg, r % Hg
    target = gl * (CHUNK * Hg) + t * Hg + h
    return (target[:, None] == r[None, :]).astype(BF16), gpb


def _to_groups(name, a, G, Hg, out_dtype):
    L, D = a.shape
    nc = L // CHUNK
    perm, gpb = _lane_perm(Hg)
    gw = CHUNK * Hg

    def body(x_ref, p_ref, o_ref):
        x = jnp.concatenate([x_ref[:, t, :] for t in range(CHUNK)], axis=1)
        if x.dtype == BF16:
            z = jnp.dot(x, p_ref[...], preferred_element_type=F32)
        else:
            z = _split_dot(x, p_ref[...])
        for gl in range(gpb):
            o_ref[gl] = z[:, gl * gw:(gl + 1) * gw].astype(o_ref.dtype)

    return _pcall(
        body, name=name, grid=(G // gpb,),
        in_specs=[pl.BlockSpec((nc, CHUNK, LANES), lambda g: (0, 0, g)),
                  pl.BlockSpec(perm.shape, lambda g: (0, 0))],
        out_specs=pl.BlockSpec((gpb, nc, gw), lambda g: (g, 0, 0)),
        out_shape=jax.ShapeDtypeStruct((G, nc, gw), out_dtype),
        compiler_params=_params(("parallel",)),
    )(a.reshape(nc, CHUNK, D), perm)


def _from_groups(name, a, Hg):
    G, nc, gw = a.shape
    perm, gpb = _lane_perm(Hg)

    def body(x_ref, p_ref, o_ref):
        z = jnp.concatenate([x_ref[gl] for gl in range(gpb)], axis=1)
        x = lax.dot_general(z, p_ref[...], (((1,), (1,)), ((), ())), preferred_element_type=F32)
        for t in range(CHUNK):
            o_ref[:, t, :] = x[:, t * LANES:(t + 1) * LANES].astype(o_ref.dtype)

    out = _pcall(
        body, name=name, grid=(G // gpb,),
        in_specs=[pl.BlockSpec((gpb, nc, gw), lambda g: (g, 0, 0)),
                  pl.BlockSpec(perm.shape, lambda g: (0, 0))],
        out_specs=pl.BlockSpec((nc, CHUNK, LANES), lambda g: (0, 0, g)),
        out_shape=jax.ShapeDtypeStruct((nc, CHUNK, G * Hg), a.dtype),
        compiler_params=_params(("parallel",)),
    )(a, perm)
    return out.reshape(nc * CHUNK, G * Hg)


def _split_dot(x, u):
    hi = x.astype(BF16)
    lo = (x - hi.astype(F32)).astype(BF16)
    return (jnp.dot(hi, u, preferred_element_type=F32) + jnp.dot(lo, u, preferred_element_type=F32))


def _col_to_row(col, n):
    return jnp.broadcast_to(col, (n, LANES)).T[0:1, :]


def _row_to_col(row, n):
    return jnp.broadcast_to(row, (LANES, n)).T[:, 0:1]


def _head_in(pair, h, upper):
    x = pair.astype(F32)
    lane = lax.broadcasted_iota(jnp.int32, x.shape, 1)
    if (h % 2 == 1) != upper:
        x = pltpu.roll(x, HEAD_DIM, 1)
    keep = (lane >= HEAD_DIM) if upper else (lane < HEAD_DIM)
    return jnp.where(keep, x, 0.0).astype(BF16)


def _pair_out(even, odd, upper):
    lane = lax.broadcasted_iota(jnp.int32, even.shape, 1)
    if upper:
        return jnp.where(lane < HEAD_DIM, pltpu.roll(even, HEAD_DIM, 1), odd)
    return jnp.where(lane < HEAD_DIM, even, pltpu.roll(odd, HEAD_DIM, 1))


def _attn_fwd(name, q, kv, tq, tk, hb):
    L = q.shape[0]
    H = q.shape[1] // HEAD_DIM
    dw = 2 * HEAD_DIM
    nq, nkb = L // tq, L // tk
    hs = range(hb)
    assert tk % tq == 0 and hb % 2 == 0

    def body(q_ref, kv_ref, o_ref, rc_ref):
        i = pl.program_id(1)
        diff = (lax.broadcasted_iota(jnp.int32, (tq, tk), 1) - lax.broadcasted_iota(jnp.int32, (tq, tk), 0))
        u_suf = (lax.broadcasted_iota(jnp.int32, (tk, tk), 0)
                 > lax.broadcasted_iota(jnp.int32, (tk, tk), 1)).astype(BF16)
        nfull = (i * tq) // tk
        rc_ref[...] = jnp.zeros_like(rc_ref)
        qw = [_head_in(q_ref[:, dw * (h // 2):dw * (h // 2 + 1)], h, False) for h in hs]

        def block(kb, carry, masked):
            runs, accs = carry
            ks = pl.multiple_of(kb * tk, tk)
            kvb = [kv_ref[pl.ds(ks, tk), dw * h:dw * (h + 1)] for h in hs]
            z = [lax.dot_general(qw[h], kvb[h], (((1,), (1,)), ((), ())), preferred_element_type=F32)
                 for h in hs]
            e = [jnp.exp(-jnp.abs(z[h])) for h in hs]
            sp = [jnp.maximum(z[h], 0.0) + jnp.log(1.0 + e[h]) for h in hs]
            if masked:
                causal = diff < (i * tq - ks)
                lom = [jnp.where(causal, -sp[h], 0.0) for h in hs]
            else:
                lom = [-sp[h] for h in hs]
            rem = [runs[h] + _split_dot(lom[h], u_suf) for h in hs]
            w = [jnp.exp(z[h] - sp[h] + rem[h]) for h in hs]
            if masked:
                w = [jnp.where(causal, w[h], 0.0) for h in hs]
            accs = tuple(accs[h] + jnp.dot(w[h].astype(BF16), kvb[h], preferred_element_type=F32) for h in hs)
            for h in hs:
                rc_ref[h, kb] = _col_to_row(runs[h], tq)
            runs = tuple(runs[h] + jnp.sum(lom[h], axis=1, keepdims=True) for h in hs)
            return runs, accs

        init = (tuple(jnp.zeros((tq, 1), F32) for _ in hs), tuple(jnp.zeros((tq, dw), F32) for _ in hs))
        carry = block(nfull, init, True)
        _, accs = lax.fori_loop(0, nfull, lambda n, c: block(nfull - 1 - n, c, False), carry)
        for p in range(hb // 2):
            o_ref[:, dw * p:dw * (p + 1)] = _pair_out(accs[2 * p], accs[2 * p + 1], True).astype(o_ref.dtype)

    qspec = pl.BlockSpec((tq, hb * HEAD_DIM), lambda h, i: (i, h))
    kspec = pl.BlockSpec((L, hb * dw), lambda h, i: (0, h))
    return _pcall(
        body, name=name, grid=(H // hb, nq), in_specs=[qspec, kspec],
        out_specs=[qspec, pl.BlockSpec((hb, None, nkb, 1, tq), lambda h, i: (h, i, 0, 0, 0))],
        out_shape=[jax.ShapeDtypeStruct((L, H * HEAD_DIM), BF16), jax.ShapeDtypeStruct((H, nq, nkb, 1, tq), F32)],
        compiler_params=_params(("parallel", "arbitrary")),
    )(q, kv)


def _attn_bwd(name, q, kv, do, rc, tq, tk, scale, hb):
    L = q.shape[0]
    H = q.shape[1] // HEAD_DIM
    dw_ = 2 * HEAD_DIM
    nq, nkb = L // tq, L // tk
    hs = range(hb)
    assert tk % tq == 0 and hb % 2 == 0

    def body(q_ref, kv_ref, do_ref, rc_ref, dq_ref, dkv_ref):
        i = pl.program_id(1)

        @pl.when(i == 0)
        def _():
            dkv_ref[...] = jnp.zeros_like(dkv_ref)

        diff = (lax.broadcasted_iota(jnp.int32, (tq, tk), 1) - lax.broadcasted_iota(jnp.int32, (tq, tk), 0))
        r_io = lax.broadcasted_iota(jnp.int32, (tk, tk), 0)
        c_io = lax.broadcasted_iota(jnp.int32, (tk, tk), 1)
        u_suf = (r_io > c_io).astype(BF16)
        u_pre = (r_io < c_io).astype(BF16)
        nfull = (i * tq) // tk
        qb = [_head_in(q_ref[:, dw_ * (h // 2):dw_ * (h // 2 + 1)], h, False) for h in hs]
        dob = [_head_in(do_ref[:, dw_ * (h // 2):dw_ * (h // 2 + 1)], h, True) for h in hs]

        def block(kb, carry, masked):
            pres, dqs = carry
            ks = pl.multiple_of(kb * tk, tk)
            kvb = [kv_ref[pl.ds(ks, tk), dw_ * h:dw_ * (h + 1)] for h in hs]
            z = [lax.dot_general(qb[h], kvb[h], (((1,), (1,)), ((), ())), preferred_element_type=F32) for h in hs]
            dw = [lax.dot_general(dob[h], kvb[h], (((1,), (1,)), ((), ())), preferred_element_type=F32)
                  for h in hs]
            e = [jnp.exp(-jnp.abs(z[h])) for h in hs]
            sp = [jnp.maximum(z[h], 0.0) + jnp.log(1.0 + e[h]) for h in hs]
            if masked:
                causal = diff < (i * tq - ks)
                lom = [jnp.where(causal, -sp[h], 0.0) for h in hs]
            else:
                lom = [-sp[h] for h in hs]
            rem = [_row_to_col(rc_ref[h, kb], tq) + _split_dot(lom[h], u_suf) for h in hs]
            logb = [z[h] - sp[h] for h in hs]
            w = [jnp.exp(logb[h] + rem[h]) for h in hs]
            if masked:
                w = [jnp.where(causal, w[h], 0.0) for h in hs]
            da = [dw[h] * w[h] for h in hs]
            p = [pres[h] + jnp.dot(da[h].astype(BF16), u_pre, preferred_element_type=F32) for h in hs]
            dz = [da[h] - jnp.exp(logb[h]) * (da[h] + p[h]) for h in hs]
            if masked:
                dz = [jnp.where(causal, dz[h], 0.0) for h in hs]
            dqs = tuple(dqs[h] + jnp.dot(dz[h].astype(BF16), kvb[h], preferred_element_type=F32) for h in hs)
            for h in hs:
                lhs = jnp.concatenate([dz[h].T, w[h].T], axis=1).astype(BF16)
                rhs = jnp.concatenate([qb[h], dob[h]], axis=0)
                dkv_ref[pl.ds(ks, tk), dw_ * h:dw_ * (h + 1)] += jnp.dot(lhs, rhs, preferred_element_type=F32)
            pres = tuple(pres[h] + jnp.sum(da[h], axis=1, keepdims=True) for h in hs)
            return pres, dqs

        init = (tuple(jnp.zeros((tq, 1), F32) for _ in hs), tuple(jnp.zeros((tq, dw_), F32) for _ in hs))
        carry = lax.fori_loop(0, nfull, lambda kb, c: block(kb, c, False), init)
        _, dqs = block(nfull, carry, True)
        for p in range(hb // 2):
            dq_ref[:, dw_ * p:dw_ * (p + 1)] = (_pair_out(dqs[2 * p], dqs[2 * p + 1], False)
                                                * scale).astype(dq_ref.dtype)

    qspec = pl.BlockSpec((tq, hb * HEAD_DIM), lambda h, i: (i, h))
    kspec = pl.BlockSpec((L, hb * dw_), lambda h, i: (0, h))
    return _pcall(
        body, name=name, grid=(H // hb, nq),
        in_specs=[qspec, kspec, qspec, pl.BlockSpec((hb, None, nkb, 1, tq), lambda h, i: (h, i, 0, 0, 0))],
        out_specs=[qspec, kspec],
        out_shape=[jax.ShapeDtypeStruct((L, H * HEAD_DIM), BF16), jax.ShapeDtypeStruct((L, 2 * H * HEAD_DIM), F32)],
        compiler_params=_params(("parallel", "arbitrary")),
    )(q, kv, do, rc)


_MESH = pl.DeviceIdType.MESH
_HBM = pl.BlockSpec(memory_space=pltpu.HBM)


def _all_gather(name, shards):
    n = len(shards)

    def body(*refs):
        x_refs, out_refs = refs[:n], refs[n:2 * n]
        send_sems, recv_sems, local_sems = refs[2 * n:]
        x, y, c = lax.axis_index("x"), lax.axis_index("y"), lax.axis_index("c")
        me, sibling = (x, y, c), (x, y, 1 - c)
        chips = [(1 - x, y), (x, 1 - y), (1 - x, 1 - y)]

        def slot(a, px, py, pc):
            return out_refs[a].at[4 * px + 2 * py + pc]

        def copy(a, k, block, to, src=None):
            return pltpu.make_async_remote_copy(
                src_ref=slot(a, *block) if src is None else src, dst_ref=slot(a, *block),
                send_sem=send_sems.at[7 * a + k], recv_sem=recv_sems.at[7 * a + k], device_id=to,
                device_id_type=_MESH)

        mine = [pltpu.make_async_copy(x_refs[a], slot(a, *me), local_sems.at[a]) for a in range(n)]
        for cp in mine:
            cp.start()
        first = []
        for a in range(n):
            first.append(copy(a, 0, me, sibling, src=x_refs[a]))
            first += [copy(a, 1 + j, me, (*chip, c), src=x_refs[a]) for j, chip in enumerate(chips)]
        for cp in first:
            cp.start()
        passed = []
        for j, chip in enumerate(chips):
            for a in range(n):
                copy(a, 1 + j, (*chip, c), me).wait_recv()
                passed.append(copy(a, 4 + j, (*chip, c), sibling))
                passed[-1].start()
        for a in range(n):
            copy(a, 0, sibling, me).wait_recv()
            for j, chip in enumerate(chips):
                copy(a, 4 + j, (*chip, 1 - c), me).wait_recv()
        for cp in first + passed:
            cp.wait_send()
        for cp in mine:
            cp.wait()

    return _pcall(
        body, name=name, out_shape=[jax.ShapeDtypeStruct((N_DEV,) + s.shape, s.dtype) for s in shards],
        in_specs=[_HBM] * n, out_specs=[_HBM] * n,
        scratch_shapes=[pltpu.SemaphoreType.DMA((7 * n,)), pltpu.SemaphoreType.DMA((7 * n,)),
                        pltpu.SemaphoreType.DMA((n,))],
    )(*shards)


def _sibling_exchange(name, xs):
    n = len(xs)

    def body(*refs):
        x_refs, out_refs, send_sems, recv_sems = refs[:n], refs[n:2 * n], refs[2 * n], refs[2 * n + 1]
        c = lax.axis_index("c")
        sibling = (lax.axis_index("x"), lax.axis_index("y"), 1 - c)
        cps = [pltpu.make_async_remote_copy(src_ref=x_refs[a].at[:, 1 - c], dst_ref=out_refs[a],
                                            send_sem=send_sems.at[a], recv_sem=recv_sems.at[a], device_id=sibling,
                                            device_id_type=_MESH)
               for a in range(n)]
        for cp in cps:
            cp.start()
        for cp in cps:
            cp.wait()

    return _pcall(
        body, name=name, out_shape=[jax.ShapeDtypeStruct(x.shape[:1] + x.shape[2:], x.dtype) for x in xs],
        in_specs=[_HBM] * n, out_specs=[_HBM] * n,
        scratch_shapes=[pltpu.SemaphoreType.DMA((n,)), pltpu.SemaphoreType.DMA((n,))],
    )(*xs)


def _chip_exchange(name, xs):
    n = len(xs)

    def body(*refs):
        x_refs, out_refs, send_sems, recv_sems = refs[:n], refs[n:2 * n], refs[2 * n], refs[2 * n + 1]
        mx, my, mc = lax.axis_index("x"), lax.axis_index("y"), lax.axis_index("c")
        chips = [(1 - mx, my), (mx, 1 - my), (1 - mx, 1 - my)]
        cps = [pltpu.make_async_remote_copy(src_ref=x_refs[a].at[2 * px + py], dst_ref=out_refs[a].at[j],
                                            send_sem=send_sems.at[3 * a + j], recv_sem=recv_sems.at[3 * a + j],
                                            device_id=(px, py, mc), device_id_type=_MESH)
               for a in range(n) for j, (px, py) in enumerate(chips)]
        for cp in cps:
            cp.start()
        for cp in cps:
            cp.wait()

    return _pcall(
        body, name=name, out_shape=[jax.ShapeDtypeStruct((3,) + x.shape[1:], x.dtype) for x in xs],
        in_specs=[_HBM] * n, out_specs=[_HBM] * n,
        scratch_shapes=[pltpu.SemaphoreType.DMA((3 * n,)), pltpu.SemaphoreType.DMA((3 * n,))],
    )(*xs)


def _pair_sum(name, a, b):
    nb, R, C = a.shape
    tm = _tile(R, 512, 16)

    def fn(i, ab, bb):
        s = ab.astype(F32) + bb.astype(F32)
        return s, s

    spec = lambda arr: (arr, (1, tm, C), lambda i: (i // (R // tm), i % (R // tm), 0))
    out = lambda dt: ((nb, R, C), dt, (1, tm, C), lambda i: (i // (R // tm), i % (R // tm), 0))
    return _rowwise(name, fn, nb * (R // tm), [spec(a), spec(b)], [out(F32), out(a.dtype)])


def _sum8(name, g):
    n, R, C = g.shape
    tm = _tile(R, 256)

    def fn(i, gb):
        s = gb[0]
        for d in range(1, n):
            s = s + gb[d]
        return (s,)

    return _rowwise(name, fn, R // tm, [(g, (n, tm, C), lambda i: (0, i, 0))], [_orows(R, C, F32, tm)])[0]


def _adamw(name, w, m, v, grads, tm=512):
    R, C = w.shape
    tm = _tile(R, tm, 16)
    ng = len(grads)

    def fn(i, wb, mb, vb, *gs):
        g = gs[0].astype(F32)
        for t in gs[1:]:
            g = g + t.astype(F32)
        mn = ADAM_B1 * mb + (1.0 - ADAM_B1) * g
        vn = ADAM_B2 * vb + (1.0 - ADAM_B2) * (g * g)
        m_hat = mn / (1.0 - ADAM_B1 ** ADAM_STEP)
        v_hat = vn / (1.0 - ADAM_B2 ** ADAM_STEP)
        delta = -ADAM_LR * (m_hat / (jnp.sqrt(v_hat) + ADAM_EPS) + ADAM_WD * wb)
        return g, delta, mn, vn

    ins = [_rows(w, tm), _rows(m, tm), _rows(v, tm)] + [_rows(g, tm) for g in grads]
    return _rowwise(name, fn, R // tm, ins, [_orows(R, C, F32, tm) for _ in range(4)])


CLASSES = [
    (BF16, True, [('ssm_w_in', None), ('attn_w_q', None), ('attn_w_o', None), ('ffn_w_down', 0), ('ffn_w_down', 1)]),
    (BF16, False, [('ssm_w_glu', None), ('kv_w', None)]),
    (BF16, False, [('ffn_w_up', 0), ('ffn_w_up', 1)]),
    (F32, False, [('ffn_conv_w', 0), ('ffn_conv_w', 1)]),
]


def _shard2d(a, name, layer):
    if name == 'ffn_conv_w':
        return a[layer, :, 0, :]
    if name == 'kv_w':
        return a
    return a[0 if layer is None else layer]


def _weights_of(members):
    names = []
    for n, _ in members:
        if not names or names[-1] != n:
            names.append(n)
    return names


def _pack_class(d, members, dtype):
    width = _shard2d(d[members[0][0]], *members[0]).shape[1]
    parts = [d[n].reshape(-1, width).astype(dtype) for n in _weights_of(members)]
    return parts[0] if len(parts) == 1 else jnp.concatenate(parts, axis=0)


def _full_weights(gaths, d):
    w = {}
    for (_, row_sharded, members), g in zip(CLASSES, gaths):
        off = 0
        for n, l in members:
            r = _shard2d(d[n], n, l).shape[0]
            blk = g[:, off:off + r]
            if row_sharded:
                w[(n, l)] = blk.reshape(N_DEV * r, g.shape[2])
            else:
                w[(n, l)] = blk.transpose(1, 0, 2).reshape(r, N_DEV * g.shape[2])
            off += r
    return w


def _grad_classes(gw):
    out = []
    for dtype, row_sharded, members in CLASSES:
        parts = []
        for n, l in members:
            g = gw[(n, l)].astype(dtype)
            if row_sharded:
                parts.append(g.reshape(N_DEV, g.shape[0] // N_DEV, g.shape[1]))
            else:
                parts.append(g.reshape(g.shape[0], N_DEV, g.shape[1] // N_DEV).transpose(1, 0, 2))
        out.append(jnp.concatenate(parts, axis=1))
    return out


def _unpack_class(buf, members, shapes):
    out, off = {}, 0
    for n in _weights_of(members):
        r = math.prod(shapes[n]) // buf.shape[1]
        out[n] = buf[off:off + r].reshape(shapes[n])
        off += r
    return out


def _small_layout(shapes, D):
    lay, off = {}, 0
    for n in SMALL:
        r = -(-math.prod(shapes[n]) // D)
        lay[n] = (off, r)
        off += r
    lay['loss'] = (off, 1)
    off += 1
    return lay, -(-off // 8) * 8


def _pack_small(d, lay, total, D):
    parts = []
    for n in SMALL + ['loss']:
        if n not in d:
            parts.append(jnp.zeros((lay[n][1], D), F32))
            continue
        flat = d[n].reshape(-1).astype(F32)
        parts.append(jnp.pad(flat, (0, lay[n][1] * D - flat.shape[0])).reshape(lay[n][1], D))
    used = sum(lay[n][1] for n in SMALL + ['loss'])
    if total > used:
        parts.append(jnp.zeros((total - used, D), F32))
    return jnp.concatenate(parts, axis=0)


def _unpack_small(pack, lay, shapes, D):
    out = {}
    for n in SMALL:
        off, r = lay[n]
        out[n] = pack[off:off + r].reshape(-1)[:math.prod(shapes[n])].reshape(shapes[n])
    return out


def _ffn_fwd(tag, x, g_norm, w_up, conv_w, conv_b, w_down):
    h = _rms_fwd(f"{tag}_norm", x, g_norm)
    gu = _mm(f"{tag}_up", h, w_up, 'nn', BF16)
    a = _gate_fwd(f"{tag}_gate", gu, conv_w, conv_b)
    return _mm(f"{tag}_down", a, w_down, 'nn', F32, resid=x), (h, gu, a)


def _ffn_bwd(tag, dres, dres_bf, x, saved, g_norm, w_up, conv_w, conv_b, w_down):
    h, gu, a = saved
    da = _mm(f"{tag}_dgate", dres_bf, w_down, 'nt', BF16)
    d_w_down = _mm(f"{tag}_dwdown", a, dres_bf, 'tn', BF16)
    dgc, du, dw0, dw1, dw2, dcb = _gate_bwd1(f"{tag}_gate_b1", gu, da, conv_w, conv_b)
    dgu = _gate_bwd2(f"{tag}_gate_b2", dgc, du, conv_w)
    d_w_up = _mm(f"{tag}_dwup", h, dgu, 'tn', BF16)
    dh = _mm(f"{tag}_dh", dgu, w_up, 'nt', F32)
    dres, dres_bf, dg = _rms_bwd(f"{tag}_norm_b", x, dres, [(g_norm, dh)])
    return dres, dres_bf, dg, d_w_up, jnp.concatenate([dw0, dw1, dw2], axis=0), dcb[0], d_w_down


def kernel(x, norm_mix, norm_ffn, norm_kv, norm_final, ssm_w_in, ssm_a_re, ssm_a_im, ssm_log_dt, ssm_b_re, ssm_b_im, ssm_c_re, ssm_c_im, ssm_d, ssm_w_glu, kv_w, attn_w_q, attn_w_o, ffn_w_up, ffn_conv_w, ffn_conv_b, ffn_w_down, loss_target, m_norm_mix, m_norm_ffn, m_norm_kv, m_norm_final, m_ssm_w_in, m_ssm_a_re, m_ssm_a_im, m_ssm_log_dt, m_ssm_b_re, m_ssm_b_im, m_ssm_c_re, m_ssm_c_im, m_ssm_d, m_ssm_w_glu, m_kv_w, m_attn_w_q, m_attn_w_o, m_ffn_w_up, m_ffn_conv_w, m_ffn_conv_b, m_ffn_w_down, v_norm_mix, v_norm_ffn, v_norm_kv, v_norm_final, v_ssm_w_in, v_ssm_a_re, v_ssm_a_im, v_ssm_log_dt, v_ssm_b_re, v_ssm_b_im, v_ssm_c_re, v_ssm_c_im, v_ssm_d, v_ssm_w_glu, v_kv_w, v_attn_w_q, v_attn_w_o, v_ffn_w_up, v_ffn_conv_w, v_ffn_conv_b, v_ffn_w_down):
    wts = dict(zip(W_NAMES, (norm_mix, norm_ffn, norm_kv, norm_final, ssm_w_in, ssm_a_re, ssm_a_im, ssm_log_dt,
                             ssm_b_re, ssm_b_im, ssm_c_re, ssm_c_im, ssm_d, ssm_w_glu, kv_w, attn_w_q, attn_w_o,
                             ffn_w_up, ffn_conv_w, ffn_conv_b, ffn_w_down)))
    mom = dict(zip(W_NAMES, (m_norm_mix, m_norm_ffn, m_norm_kv, m_norm_final, m_ssm_w_in, m_ssm_a_re, m_ssm_a_im,
                             m_ssm_log_dt, m_ssm_b_re, m_ssm_b_im, m_ssm_c_re, m_ssm_c_im, m_ssm_d, m_ssm_w_glu,
                             m_kv_w, m_attn_w_q, m_attn_w_o, m_ffn_w_up, m_ffn_conv_w, m_ffn_conv_b, m_ffn_w_down)))
    vel = dict(zip(W_NAMES, (v_norm_mix, v_norm_ffn, v_norm_kv, v_norm_final, v_ssm_w_in, v_ssm_a_re, v_ssm_a_im,
                             v_ssm_log_dt, v_ssm_b_re, v_ssm_b_im, v_ssm_c_re, v_ssm_c_im, v_ssm_d, v_ssm_w_glu,
                             v_kv_w, v_attn_w_q, v_attn_w_o, v_ffn_w_up, v_ffn_conv_w, v_ffn_conv_b, v_ffn_w_down)))
    shapes = {n: wts[n].shape for n in W_NAMES}
    _, L, D = x.shape
    Fh = ffn_conv_b.shape[1]
    G, P = ssm_a_re.shape[1], ssm_a_re.shape[2]
    Hg = ssm_d.shape[2]
    x0 = x[0]
    target = loss_target[0]
    scale = HEAD_DIM ** -0.5

    gaths = _all_gather("gather_weights", [_pack_class(wts, members, dt) for dt, _, members in CLASSES])
    w = _full_weights(gaths, wts)
    nm = [norm_mix[l:l + 1] for l in range(2)]
    nf = [norm_ffn[l:l + 1] for l in range(2)]
    nkv = norm_kv[None]
    nfin = norm_final[None]
    cb = [ffn_conv_b[l:l + 1] for l in range(2)]
    cwt = [w[('ffn_conv_w', l)] for l in range(2)]

    s5p = (ssm_a_re[0], ssm_a_im[0], ssm_log_dt[0], ssm_b_re[0], ssm_b_im[0], ssm_c_re[0], ssm_c_im[0], ssm_d[0])
    (m_mat, n_mat, o_mat, lam1, lam2), s5_vjp = jax.vjp(_s5_build, *s5p)
    m_bf, n_bf, o_bf = m_mat.astype(BF16), n_mat.astype(BF16), o_mat.astype(BF16)

    h0 = _rms_fwd("l0_norm", x0, nm[0])
    u = _mm("l0_win", h0, w[('ssm_w_in', None)], 'nn', BF16)
    ug = _to_groups("s5_group_u", u, G, Hg, BF16)
    vloc = _bmm("s5_local_state", [(ug, n_bf, 'nt')], F32)
    st = _chunk_scan("s5_scan", vloc, lam1, lam2, reverse=False)
    yraw, yg_g = _bmm("s5_out", [(ug, m_bf, 'nt'), (st, o_bf, 'nt')], (F32, BF16),
                      post=lambda acc: (acc, _gelu(acc)))
    yg = _from_groups("s5_ungroup_y", yg_g, Hg)
    z = _mm("l0_wglu", yg, w[('ssm_w_glu', None)], 'nn', F32)
    x1 = _glu_fwd("l0_glu", x0, z)
    x2, ffn0 = _ffn_fwd("f0", x1, nf[0], w[('ffn_w_up', 0)], cwt[0], cb[0], w[('ffn_w_down', 0)])

    hkv = _rms_fwd("kv_norm", x2, nkv)
    nh = D // HEAD_DIM
    w_kv = w[('kv_w', None)].reshape(D, 2, nh, HEAD_DIM).transpose(0, 2, 1, 3).reshape(D, 2 * D)
    kvp = _mm("kv_proj", hkv, w_kv, 'nn', BF16)
    h2 = _rms_fwd("l1_norm", x2, nm[1])
    qn = _mm("l1_wq", h2, w[('attn_w_q', None)], 'nn', BF16, scale=scale)
    o, rc = _attn_fwd("attn_fwd", qn, kvp, ATT_TQ, ATT_TK, ATT_HB_FWD)
    x3 = _mm("l1_wo", o, w[('attn_w_o', None)], 'nn', F32, resid=x2)
    x4, ffn1 = _ffn_fwd("f1", x3, nf[1], w[('ffn_w_up', 1)], cwt[1], cb[1], w[('ffn_w_down', 1)])

    dres, dres_bf, dg_final, loss_part = _final_loss("loss_head", x4, nfin, target)

    gw = {}
    dres, dres_bf, dg_nf1, gup1, gcw1, gcb1, gdn1 = _ffn_bwd("f1", dres, dres_bf, x3, ffn1, nf[1], w[('ffn_w_up', 1)],
                                                             cwt[1], cb[1], w[('ffn_w_down', 1)])
    do = _mm("l1_do", dres_bf, w[('attn_w_o', None)], 'nt', BF16)
    gw[('attn_w_o', None)] = _mm("l1_dwo", o, dres_bf, 'tn', BF16)
    dq, dkv = _attn_bwd("attn_bwd", qn, kvp, do, rc, ATT_TQ, ATT_TK, scale, ATT_HB_BWD)
    gw[('attn_w_q', None)] = _mm("l1_dwq", h2, dq, 'tn', BF16)
    dh2 = _mm("l1_dh", dq, w[('attn_w_q', None)], 'nt', F32)
    g_kv = _mm("kv_dw", hkv, dkv, 'tn', BF16)
    gw[('kv_w', None)] = g_kv.reshape(D, nh, 2, HEAD_DIM).transpose(0, 2, 1, 3).reshape(D, 2 * D)
    dhkv = _mm("kv_dh", dkv, w_kv, 'nt', F32)
    dres, dres_bf, dg_nm1, dg_nkv = _rms_bwd("l1_norm_b", x2, dres, [(nm[1], dh2), (nkv, dhkv)])
    dres, dres_bf, dg_nf0, gup0, gcw0, gcb0, gdn0 = _ffn_bwd("f0", dres, dres_bf, x1, ffn0, nf[0], w[('ffn_w_up', 0)],
                                                             cwt[0], cb[0], w[('ffn_w_down', 0)])
    dz = _glu_bwd("l0_glu_b", dres, z)
    gw[('ssm_w_glu', None)] = _mm("l0_dwglu", yg, dz, 'tn', BF16)
    dyg = _mm("l0_dyg", dz, w[('ssm_w_glu', None)], 'nt', F32)
    dyg_g = _to_groups("s5_group_dy", dyg, G, Hg, F32)
    gshape = yraw.shape
    dy = _rowwise("s5_gelu_b", lambda i, a, b: (a * _gelu_grad(b),), G,
                  [(dyg_g, (1,) + gshape[1:], lambda i: (i, 0, 0)), (yraw, (1,) + gshape[1:], lambda i: (i, 0, 0))],
                  [(gshape, BF16, (1,) + gshape[1:], lambda i: (i, 0, 0))])[0]
    ds = _bmm("s5_dstate", [(dy, o_bf, 'nn')], F32)
    dv_loc, dlam1, dlam2 = _chunk_scan("s5_scan_b", ds, lam1, lam2, reverse=True, s_fwd=st)
    du_g = _bmm("s5_du", [(dy, m_bf, 'nn'), (dv_loc, n_bf, 'nn')], BF16)
    d_m = _bmm("s5_dm", [(dy, ug, 'tn')], F32)
    d_o = _bmm("s5_do", [(dy, st, 'tn')], F32)
    d_n = _bmm("s5_dn", [(dv_loc, ug, 'tn')], F32)
    s5g = s5_vjp((d_m, d_n, d_o, dlam1, dlam2))
    du = _from_groups("s5_ungroup_du", du_g, Hg)
    gw[('ssm_w_in', None)] = _mm("l0_dwin", h0, du, 'tn', BF16)
    dh0 = _mm("l0_dh", du, w[('ssm_w_in', None)], 'nt', F32)
    grad_x, _, dg_nm0 = _rms_bwd("l0_norm_b", x0, dres, [(nm[0], dh0)])

    gw.update({('ffn_w_up', 0): gup0, ('ffn_w_up', 1): gup1, ('ffn_w_down', 0): gdn0, ('ffn_w_down', 1): gdn1,
               ('ffn_conv_w', 0): gcw0, ('ffn_conv_w', 1): gcw1})

    small_g = {
        'norm_mix': jnp.concatenate([dg_nm0, dg_nm1], axis=0), 'norm_ffn': jnp.concatenate([dg_nf0, dg_nf1], axis=0),
        'norm_kv': dg_nkv, 'norm_final': dg_final, 'ffn_conv_b': jnp.stack([gcb0, gcb1]),
        'ssm_a_re': s5g[0], 'ssm_a_im': s5g[1], 'ssm_log_dt': s5g[2], 'ssm_b_re': s5g[3], 'ssm_b_im': s5g[4],
        'ssm_c_re': s5g[5], 'ssm_c_im': s5g[6], 'ssm_d': s5g[7], 'loss': loss_part[0, 0:1],
    }
    lay, rs = _small_layout(shapes, D)
    small_sum = _sum8("small_sum", _all_gather("gather_small", [_pack_small(small_g, lay, rs, D)])[0])
    loss = small_sum[lay['loss'][0], 0]
    sg, sdelta, sm, sv = _adamw("adamw_small", _pack_small(wts, lay, rs, D), _pack_small(mom, lay, rs, D),
                                _pack_small(vel, lay, rs, D), [small_sum])
    small_out = [_unpack_small(t, lay, shapes, D) for t in (sg, sdelta, sm, sv)]

    c = lax.axis_index("c")
    chip = 2 * lax.axis_index("x") + lax.axis_index("y")
    gcls = [g.reshape((4, 2) + g.shape[1:]) for g in _grad_classes(gw)]
    mine = [lax.dynamic_index_in_dim(g, c, axis=1, keepdims=False) for g in gcls]
    got = _sibling_exchange("rs_sibling", gcls)
    sums = [_pair_sum(f"rs_pair_sum{k}", a, b) for k, (a, b) in enumerate(zip(mine, got))]
    from_chips = _chip_exchange("rs_chips", [on_wire for _, on_wire in sums])
    big_parts = [{}, {}, {}, {}]
    for k, ((_, _, members), (chip_f32, _), fc) in enumerate(zip(CLASSES, sums, from_chips)):
        own = lax.dynamic_index_in_dim(chip_f32, chip, axis=0, keepdims=False)
        res = _adamw(f"adamw_big{k}", _pack_class(wts, members, F32), _pack_class(mom, members, F32),
                     _pack_class(vel, members, F32), [own, fc[0], fc[1], fc[2]])
        for q in range(4):
            big_parts[q].update(_unpack_class(res[q], members, shapes))
    big_out = big_parts

    outs = [loss, grad_x[None]]
    for k in range(4):
        for n in W_NAMES:
            outs.append(big_out[k][n] if n in BIG else small_out[k][n])
    return tuple(outs)
```

```python
import math

import jax
import jax.numpy as jnp
from jax import lax
from jax.experimental import pallas as pl
from jax.experimental.pallas import tpu as pltpu

F32 = jnp.float32
BF16 = jnp.bfloat16

EPS = 1e-6
HEAD_DIM = 64
CHUNK = 16
N_DEV = 8
ADAM_LR = 0.001
ADAM_B1 = 0.9
ADAM_B2 = 0.999
ADAM_EPS = 1e-08
ADAM_WD = 0.01
ADAM_STEP = 10
VMEM_LIMIT = 48 * 1024 * 1024
MM_BLOCK_BYTES = 28 * 1024 * 1024
MM_TM_MAX = 1024
MM_TM_MAX_T = 512
MM_TN_MAX = 1536
LANES = 128
ATT_TQ = 128
ATT_TK = 256
ATT_HB_FWD = 4
ATT_HB_BWD = 4

W_NAMES = ['norm_mix', 'norm_ffn', 'norm_kv', 'norm_final', 'ssm_w_in', 'ssm_a_re', 'ssm_a_im', 'ssm_log_dt',
           'ssm_b_re', 'ssm_b_im', 'ssm_c_re', 'ssm_c_im', 'ssm_d', 'ssm_w_glu', 'kv_w', 'attn_w_q', 'attn_w_o',
           'ffn_w_up', 'ffn_conv_w', 'ffn_conv_b', 'ffn_w_down']
BIG = ['ssm_w_in', 'ssm_w_glu', 'kv_w', 'attn_w_q', 'attn_w_o', 'ffn_w_up', 'ffn_w_down', 'ffn_conv_w']
SMALL = [n for n in W_NAMES if n not in BIG]


def _pcall(body, **kw):
    return pl.pallas_call(body, **kw)


def _params(sem=None):
    if sem is None:
        return pltpu.CompilerParams(vmem_limit_bytes=VMEM_LIMIT)
    return pltpu.CompilerParams(dimension_semantics=sem, vmem_limit_bytes=VMEM_LIMIT)


def _tile(n, pref, mult=8):
    best = None
    for t in range(mult, min(n, pref) + 1, mult):
        if n % t == 0:
            best = t
    return n if best is None else best


def _mm_tiles(M, N, K, mode, a_bytes, b_bytes, o_bytes):
    def cands(n, cap):
        c = [t for t in range(LANES, min(n, cap) + 1, LANES) if n % t == 0]
        return c or [n]
    best = None
    for tm in cands(M, MM_TM_MAX_T if mode == 'tn' else MM_TM_MAX):
        for tn in cands(N, MM_TN_MAX):
            need = 2 * (tm * K * a_bytes + K * tn * b_bytes + tm * tn * o_bytes)
            if need <= MM_BLOCK_BYTES and (best is None or tm * tn > best[0]):
                best = (tm * tn, tm, tn)
    assert best is not None, (M, N, K)
    return best[1], best[2]


def _mm(name, a, b, mode, out_dtype=F32, scale=None, resid=None):
    if mode == 'nn':
        (M, K), (K2, N) = a.shape, b.shape
    elif mode == 'nt':
        (M, K), (N, K2) = a.shape, b.shape
    else:
        (K, M), (K2, N) = a.shape, b.shape
    assert K == K2, (name, a.shape, b.shape)
    o_bytes = jnp.dtype(out_dtype).itemsize + (resid.dtype.itemsize if resid is not None else 0)
    tm, tn = _mm_tiles(M, N, K, mode, a.dtype.itemsize, b.dtype.itemsize, o_bytes)
    if mode == 'tn':
        a_spec = pl.BlockSpec((K, tm), lambda i, j: (0, i))
    else:
        a_spec = pl.BlockSpec((tm, K), lambda i, j: (i, 0))
    if mode == 'nt':
        b_spec = pl.BlockSpec((tn, K), lambda i, j: (j, 0))
    else:
        b_spec = pl.BlockSpec((K, tn), lambda i, j: (0, j))
    o_spec = pl.BlockSpec((tm, tn), lambda i, j: (i, j))
    dn = {'nn': ((1,), (0,)), 'nt': ((1,), (1,)), 'tn': ((0,), (0,))}[mode]

    def body(*refs):
        a_ref, b_ref, o_ref = refs[0], refs[1], refs[-1]
        acc = lax.dot_general(a_ref[...].astype(BF16), b_ref[...].astype(BF16), (dn, ((), ())),
                              preferred_element_type=F32)
        if scale is not None:
            acc = acc * scale
        if resid is not None:
            acc = acc + refs[2][...]
        o_ref[...] = acc.astype(o_ref.dtype)

    return _pcall(
        body, name=name, grid=(M // tm, N // tn),
        in_specs=[a_spec, b_spec] + ([o_spec] if resid is not None else []),
        out_specs=o_spec,
        out_shape=jax.ShapeDtypeStruct((M, N), out_dtype),
        compiler_params=_params(("parallel", "parallel")),
    )(*([a, b] + ([resid] if resid is not None else [])))


def _bmm(name, terms, out_dtype, gb=8, post=None, ungroup=None):
    G = terms[0][0].shape[0]
    gb = _tile(G, gb, 1)
    dns = {'nn': ((1,), (0,)), 'nt': ((1,), (1,)), 'tn': ((0,), (0,))}

    def oshape(a, b, mode):
        m = a.shape[2] if mode == 'tn' else a.shape[1]
        n = b.shape[1] if mode == 'nt' else b.shape[2]
        return m, n

    m, n = oshape(*terms[0])
    out_dtypes = out_dtype if isinstance(out_dtype, (tuple, list)) else (out_dtype,)
    n_in = 2 * len(terms)
    if ungroup is not None:
        perm, gpb = _lane_perm(ungroup[1])
        assert gpb == gb and n == CHUNK * ungroup[1]

    def body(*refs):
        ins, outs = refs[:n_in], refs[len(refs) - len(out_dtypes):]
        kept = []
        for gi in range(gb):
            acc = None
            for t, (_, _, mode) in enumerate(terms):
                part = lax.dot_general(ins[2 * t][gi].astype(BF16), ins[2 * t + 1][gi].astype(BF16),
                                       (dns[mode], ((), ())), preferred_element_type=F32)
                acc = part if acc is None else acc + part
            vals = (acc,) if post is None else post(acc)
            for k, (o_ref, v) in enumerate(zip(outs, vals)):
                if ungroup is not None and k == ungroup[0]:
                    kept.append(v.astype(BF16))
                else:
                    o_ref[gi] = v.astype(o_ref.dtype)
        if ungroup is not None:
            x = lax.dot_general(jnp.concatenate(kept, axis=1), refs[n_in][...], (((1,), (1,)), ((), ())),
                                preferred_element_type=F32)
            o_ref = outs[ungroup[0]]
            for t in range(CHUNK):
                o_ref[:, t, :] = x[:, t * LANES:(t + 1) * LANES].astype(o_ref.dtype)

    in_specs, args = [], []
    for a, b, _ in terms:
        in_specs += [pl.BlockSpec((gb,) + a.shape[1:], lambda g: (g, 0, 0)),
                     pl.BlockSpec((gb,) + b.shape[1:], lambda g: (g, 0, 0))]
        args += [a, b]
    out_specs = [pl.BlockSpec((gb, m, n), lambda g: (g, 0, 0)) for _ in out_dtypes]
    out_shape = [jax.ShapeDtypeStruct((G, m, n), dt) for dt in out_dtypes]
    if ungroup is not None:
        in_specs.append(pl.BlockSpec(perm.shape, lambda g: (0, 0)))
        args.append(perm)
        out_specs[ungroup[0]] = pl.BlockSpec((m, CHUNK, LANES), lambda g: (0, 0, g))
        out_shape[ungroup[0]] = jax.ShapeDtypeStruct((m, CHUNK, G * ungroup[1]), out_dtypes[ungroup[0]])
    res = _pcall(
        body, name=name, grid=(G // gb,), in_specs=in_specs, out_specs=out_specs, out_shape=out_shape,
        compiler_params=_params(("parallel",)),
    )(*args)
    if ungroup is not None:
        res = list(res)
        res[ungroup[0]] = res[ungroup[0]].reshape(m * CHUNK, G * ungroup[1])
    return res[0] if len(out_dtypes) == 1 else res


def _rowwise(name, fn, n_steps, ins, outs, n_acc=0):
    n_in, n_out = len(ins), len(outs)

    def body(*refs):
        i = pl.program_id(0)
        vals = fn(i, *[r[...] for r in refs[:n_in]])
        o_refs = refs[n_in:]
        for j in range(n_out - n_acc):
            o_refs[j][...] = vals[j].astype(o_refs[j].dtype)
        if n_acc:
            @pl.when(i == 0)
            def _():
                for j in range(n_out - n_acc, n_out):
                    o_refs[j][...] = vals[j].astype(o_refs[j].dtype)

            @pl.when(i > 0)
            def _():
                for j in range(n_out - n_acc, n_out):
                    o_refs[j][...] += vals[j].astype(o_refs[j].dtype)

    res = _pcall(
        body, name=name, grid=(n_steps,),
        in_specs=[pl.BlockSpec(blk, im) for _, blk, im in ins],
        out_specs=[pl.BlockSpec(blk, im) for _, _, blk, im in outs],
        out_shape=[jax.ShapeDtypeStruct(s, d) for s, d, _, _ in outs],
        compiler_params=_params(("arbitrary",)),
    )(*[a for a, _, _ in ins])
    return res


def _rows(a, tm):
    return (a, (tm, a.shape[1]), lambda i: (i, 0))


def _whole(a):
    nd = a.ndim
    return (a, a.shape, lambda i: (0,) * nd)


def _orows(L, n, dtype, tm):
    return ((L, n), dtype, (tm, n), lambda i: (i, 0))


def _oacc(r, n):
    return ((r, n), F32, (r, n), lambda i: (0, 0))


def _rms_fwd(name, x, g, tm=512):
    L, D = x.shape
    tm = _tile(L, tm)

    def fn(i, xb, gb):
        r = lax.rsqrt(jnp.mean(xb * xb, axis=-1, keepdims=True) + EPS)
        return (xb * r * gb,)

    return _rowwise(name, fn, L // tm, [_rows(x, tm), _whole(g)], [_orows(L, D, BF16, tm)])[0]


def _rms_bwd(name, x, dres, branches, tm=256):
    L, D = x.shape
    tm = _tile(L, tm)
    nb = len(branches)

    def fn(i, xb, db, *rest):
        r = lax.rsqrt(jnp.mean(xb * xb, axis=-1, keepdims=True) + EPS)
        xh = xb * r
        dx = db
        dgs = []
        for b in range(nb):
            gb, dyb = rest[2 * b], rest[2 * b + 1].astype(F32)
            dxh = dyb * gb
            dx = dx + r * (dxh - xh * jnp.mean(dxh * xh, axis=-1, keepdims=True))
            dgs.append(jnp.sum(dyb * xh, axis=0, keepdims=True))
        return (dx, dx, *dgs)

    ins = [_rows(x, tm), _rows(dres, tm)]
    for g, dy in branches:
        ins += [_whole(g), _rows(dy, tm)]
    outs = [_orows(L, D, F32, tm), _orows(L, D, BF16, tm)] + [_oacc(1, D) for _ in range(nb)]
    return _rowwise(name, fn, L // tm, ins, outs, n_acc=nb)


def _final_loss(name, x, g, target, tm=256):
    L, D = x.shape
    tm = _tile(L, tm)

    def fn(i, xb, gb, tb):
        r = lax.rsqrt(jnp.mean(xb * xb, axis=-1, keepdims=True) + EPS)
        xh = xb * r
        err = xh * gb - tb
        dy = err * (1.0 / D)
        dxh = dy * gb
        dx = r * (dxh - xh * jnp.mean(dxh * xh, axis=-1, keepdims=True))
        dg = jnp.sum(dy * xh, axis=0, keepdims=True)
        per_row = jnp.mean(err * err, axis=-1, keepdims=True)
        loss = 0.5 * jnp.sum(per_row, axis=0, keepdims=True)
        return dx, dx, dg, jnp.broadcast_to(loss, (1, LANES))

    return _rowwise(name, fn, L // tm, [_rows(x, tm), _whole(g), _rows(target, tm)],
                    [_orows(L, D, F32, tm), _orows(L, D, BF16, tm), _oacc(1, D), _oacc(1, LANES)], n_acc=2)


def _glu_fwd(name, x, z, tm=256):
    L, D = x.shape
    tm = _tile(L, tm)

    def fn(i, xb, zb):
        return (xb + zb[:, :D] * jax.nn.sigmoid(zb[:, D:]),)

    return _rowwise(name, fn, L // tm, [_rows(x, tm), _rows(z, tm)], [_orows(L, D, F32, tm)])[0]


def _glu_bwd(name, dx, z, tm=256):
    L, D = dx.shape
    tm = _tile(L, tm)

    def fn(i, db, zb):
        z1, sg = zb[:, :D], jax.nn.sigmoid(zb[:, D:])
        return (jnp.concatenate([(db * sg).astype(BF16), (db * z1 * sg * (1.0 - sg)).astype(BF16)], axis=1),)

    return _rowwise(name, fn, L // tm, [_rows(dx, tm), _rows(z, tm)], [_orows(L, 2 * D, BF16, tm)])[0]


HALO = 16


def _halo_prev(a, tm):
    return (a, (HALO, a.shape[1]), lambda i: (jnp.maximum(i * (tm // HALO) - 1, 0), 0))


def _halo_next(a, tm):
    last = a.shape[0] // HALO - 1
    return (a, (HALO, a.shape[1]), lambda i: (jnp.minimum((i + 1) * (tm // HALO), last), 0))


def _shift_down(cur, halo, k, first):
    tm = cur.shape[0]
    rolled = pltpu.roll(cur, k, 0)
    tail = pltpu.roll(halo, k, 0)
    tail = jnp.where(first, 0.0, tail)
    row = lax.broadcasted_iota(jnp.int32, (tm, 1), 0)
    head = jnp.concatenate([tail, jnp.zeros((tm - 8, cur.shape[1]), cur.dtype)], axis=0) if tm > 8 else tail
    return jnp.where(row < k, head, rolled)


def _shift_up(cur, halo, k, last):
    tm = cur.shape[0]
    rolled = pltpu.roll(cur, tm - k, 0)
    head = pltpu.roll(halo, 8 - k, 0)
    head = jnp.where(last, 0.0, head)
    row = lax.broadcasted_iota(jnp.int32, (tm, 1), 0)
    tail = jnp.concatenate([jnp.zeros((tm - 8, cur.shape[1]), cur.dtype), head], axis=0) if tm > 8 else head
    return jnp.where(row >= tm - k, tail, rolled)


def _conv_pre(gb, hb, cw, cb, first):
    g1 = _shift_down(gb, hb, 1, first)
    g2 = _shift_down(gb, hb, 2, first)
    return cw[0:1] * g2 + cw[1:2] * g1 + cw[2:3] * gb + cb, g1, g2


def _gate_fwd(name, gu, cw, cb, tm=256):
    L, F2 = gu.shape
    Fh = F2 // 2
    tm = _tile(L, tm)

    def fn(i, gub, halo, cwb, cbb):
        gub, halo = gub.astype(F32), halo.astype(F32)[HALO - 8:]
        gc, _, _ = _conv_pre(gub[:, :Fh], halo[:, :Fh], cwb, cbb, i == 0)
        return (gc * jax.nn.sigmoid(gc) * gub[:, Fh:],)

    return _rowwise(name, fn, L // tm, [_rows(gu, tm), _halo_prev(gu, tm), _whole(cw), _whole(cb)],
                    [_orows(L, Fh, BF16, tm)])[0]


def _gate_bwd1(name, gu, da, cw, cb, tm=256):
    L, F2 = gu.shape
    Fh = F2 // 2
    tm = _tile(L, tm)

    def fn(i, gub, halo, dab, cwb, cbb):
        gub, halo = gub.astype(F32), halo.astype(F32)[HALO - 8:]
        gb, ub = gub[:, :Fh], gub[:, Fh:]
        gc, g1, g2 = _conv_pre(gb, halo[:, :Fh], cwb, cbb, i == 0)
        sg = jax.nn.sigmoid(gc)
        dab = dab.astype(F32)
        du = dab * gc * sg
        dgc = dab * ub * sg * (1.0 + gc * (1.0 - sg))
        return (dgc, du, jnp.sum(dgc * g2, axis=0, keepdims=True), jnp.sum(dgc * g1, axis=0, keepdims=True),
                jnp.sum(dgc * gb, axis=0, keepdims=True), jnp.sum(dgc, axis=0, keepdims=True))

    return _rowwise(name, fn, L // tm,
                    [_rows(gu, tm), _halo_prev(gu, tm), _rows(da, tm), _whole(cw), _whole(cb)],
                    [_orows(L, Fh, BF16, tm), _orows(L, Fh, BF16, tm)] + [_oacc(1, Fh) for _ in range(4)], n_acc=4)


def _gate_bwd2(name, dgc, du, cw, tm=256):
    L, Fh = dgc.shape
    tm = _tile(L, tm)
    n = L // tm

    def fn(i, db, halo, dub, cwb):
        db, halo = db.astype(F32), halo.astype(F32)[:8]
        d1 = _shift_up(db, halo, 1, i == n - 1)
        d2 = _shift_up(db, halo, 2, i == n - 1)
        dg = cwb[2:3] * db + cwb[1:2] * d1 + cwb[0:1] * d2
        return (jnp.concatenate([dg.astype(BF16), dub], axis=1),)

    return _rowwise(name, fn, n, [_rows(dgc, tm), _halo_next(dgc, tm), _rows(du, tm), _whole(cw)],
                    [_orows(L, 2 * Fh, BF16, tm)])[0]


def _s5_build(a_re, a_im, log_dt, b_re, b_im, c_re, c_im, d):
    T = CHUNK
    G, P = a_re.shape
    H = d.shape[1]
    hi = lax.Precision.HIGHEST
    dt = jnp.exp(log_dt)[:, None]
    mag = jnp.exp(a_re * dt)
    ab_re = mag * jnp.cos(a_im * dt)
    ab_im = mag * jnp.sin(a_im * dt)
    den = a_re * a_re + a_im * a_im
    f_re = ((ab_re - 1.0) * a_re + ab_im * a_im) / den
    f_im = (ab_im * a_re - (ab_re - 1.0) * a_im) / den
    bb_re = f_re[..., None] * b_re - f_im[..., None] * b_im
    bb_im = f_re[..., None] * b_im + f_im[..., None] * b_re
    tau = jnp.arange(T + 1, dtype=F32)[None, :, None]
    pmag = jnp.exp(tau * (a_re * dt)[:, None, :])
    pang = tau * (a_im * dt)[:, None, :]
    pw_re = pmag * jnp.cos(pang)
    pw_im = pmag * jnp.sin(pang)
    cp_re = c_re[:, :, None, :] * pw_re[:, None] - c_im[:, :, None, :] * pw_im[:, None]
    cp_im = c_re[:, :, None, :] * pw_im[:, None] + c_im[:, :, None, :] * pw_re[:, None]
    cp_cat = jnp.concatenate([cp_re, -cp_im], axis=-1)
    bb_cat = jnp.concatenate([bb_re, bb_im], axis=1)
    kt = jnp.einsum('ghtq,gqk->ghtk', cp_cat[:, :, :T], bb_cat, precision=hi)
    kt = kt.at[:, :, 0, :].add(d[:, :, None] * jnp.eye(H, dtype=F32)[None])
    kp = jnp.concatenate([kt[:, :, ::-1, :], jnp.zeros((G, H, T - 1, H), F32)], axis=2).reshape(G, H, (2 * T - 1) * H)
    m_mat = jnp.stack([kp[:, :, (T - 1 - t) * H:(2 * T - 1 - t) * H] for t in range(T)], axis=1)
    m_mat = m_mat.reshape(G, T * H, T * H)
    pr = jnp.repeat(pw_re[:, :T][:, ::-1].transpose(0, 2, 1), H, axis=2)
    pi = jnp.repeat(pw_im[:, :T][:, ::-1].transpose(0, 2, 1), H, axis=2)
    br, bi = jnp.tile(bb_re, (1, 1, T)), jnp.tile(bb_im, (1, 1, T))
    n_mat = jnp.concatenate([pr * br - pi * bi, pr * bi + pi * br], axis=1)
    o_mat = cp_cat[:, :, 1:].transpose(0, 2, 1, 3).reshape(G, T * H, 2 * P)
    lam1 = jnp.concatenate([pw_re[:, T], pw_re[:, T]], axis=-1)[:, None, :]
    lam2 = jnp.concatenate([-pw_im[:, T], pw_im[:, T]], axis=-1)[:, None, :]
    return m_mat, n_mat, o_mat, lam1, lam2


def _chunk_scan(name, v, lam1, lam2, reverse, s_fwd=None, gb=32):
    G, nc, W = v.shape
    gb = _tile(G, gb, 1)
    half = W // 2
    ntile = nc // 8

    def body(*refs):
        if reverse:
            v_ref, l1_ref, l2_ref, s_ref, o_ref, d1_ref, d2_ref = refs
        else:
            v_ref, l1_ref, l2_ref, o_ref = refs
        l1 = jnp.broadcast_to(l1_ref[...], (gb, 8, W))
        l2 = jnp.broadcast_to(l2_ref[...], (gb, 8, W))
        if reverse:
            l2 = -l2
        row = lax.broadcasted_iota(jnp.int32, (gb, 8, W), 1)

        def tile_step(n, carry):
            if reverse:
                st, a1, a2 = carry
                base = pl.multiple_of((ntile - 1 - n) * 8, 8)
            else:
                st = carry
                base = pl.multiple_of(n * 8, 8)
            vt = v_ref[:, pl.ds(base, 8), :]
            out = jnp.zeros((gb, 8, W), F32)
            order = range(7, -1, -1) if reverse else range(8)
            for r in order:
                out = jnp.where(row == r, st, out)
                vr = jnp.broadcast_to(vt[:, r:r + 1, :], (gb, 8, W))
                st = l1 * st + l2 * pltpu.roll(st, half, 2) + vr
            o_ref[:, pl.ds(base, 8), :] = out
            if reverse:
                sv = s_ref[:, pl.ds(base, 8), :]
                a1 = a1 + out * sv
                a2 = a2 + out * pltpu.roll(sv, half, 2)
                return st, a1, a2
            return st

        zero = jnp.zeros((gb, 8, W), F32)
        if reverse:
            _, a1, a2 = lax.fori_loop(0, ntile, tile_step, (zero, zero, zero))
            d1_ref[...] = jnp.sum(a1, axis=1, keepdims=True)
            d2_ref[...] = jnp.sum(a2, axis=1, keepdims=True)
        else:
            lax.fori_loop(0, ntile, tile_step, zero)

    big = pl.BlockSpec((gb, nc, W), lambda g: (g, 0, 0))
    vec = pl.BlockSpec((gb, 1, W), lambda g: (g, 0, 0))
    if reverse:
        return _pcall(body, name=name, grid=(G // gb,), in_specs=[big, vec, vec, big],
                      out_specs=[big, vec, vec],
                      out_shape=[jax.ShapeDtypeStruct((G, nc, W), F32), jax.ShapeDtypeStruct((G, 1, W), F32),
                                 jax.ShapeDtypeStruct((G, 1, W), F32)],
                      compiler_params=_params(("parallel",)))(v, lam1, lam2, s_fwd)
    return _pcall(body, name=name, grid=(G // gb,), in_specs=[big, vec, vec], out_specs=big,
                  out_shape=jax.ShapeDtypeStruct((G, nc, W), F32),
                  compiler_params=_params(("parallel",)))(v, lam1, lam2)


_GELU_C = math.sqrt(2.0 / math.pi)


def _gelu(y):
    return 0.5 * y * (1.0 + jnp.tanh(_GELU_C * (y + 0.044715 * y * y * y)))


def _gelu_grad(y):
    t = jnp.tanh(_GELU_C * (y + 0.044715 * y * y * y))
    return 0.5 * (1.0 + t) + 0.5 * y * (1.0 - t * t) * _GELU_C * (1.0 + 3.0 * 0.044715 * y * y)


def _lane_perm(Hg):
    gpb = LANES // Hg
    r = jnp.arange(CHUNK * LANES)
    t, gl, h = r // LANES, (r % LANES) // Hg, r % Hg
    target = gl * (CHUNK * Hg) + t * Hg + h
    return (target[:, None] == r[None, :]).astype(BF16), gpb


def _to_groups(name, a, G, Hg, out_dtype, gelu_arg=None):
    L, D = a.shape
    nc = L // CHUNK
    perm, gpb = _lane_perm(Hg)
    gw = CHUNK * Hg

    def body(*refs):
        x_ref, p_ref, o_ref = refs[0], refs[1], refs[-1]
        x = jnp.concatenate([x_ref[:, t, :] for t in range(CHUNK)], axis=1)
        if x.dtype == BF16:
            z = jnp.dot(x, p_ref[...], preferred_element_type=F32)
        else:
            z = _split_dot(x, p_ref[...])
        for gl in range(gpb):
            piece = z[:, gl * gw:(gl + 1) * gw]
            if gelu_arg is not None:
                piece = piece * _gelu_grad(refs[2][gl])
            o_ref[gl] = piece.astype(o_ref.dtype)

    grouped = pl.BlockSpec((gpb, nc, gw), lambda g: (g, 0, 0))
    return _pcall(
        body, name=name, grid=(G // gpb,),
        in_specs=[pl.BlockSpec((nc, CHUNK, LANES), lambda g: (0, 0, g)),
                  pl.BlockSpec(perm.shape, lambda g: (0, 0))] + ([grouped] if gelu_arg is not None else []),
        out_specs=grouped,
        out_shape=jax.ShapeDtypeStruct((G, nc, gw), out_dtype),
        compiler_params=_params(("parallel",)),
    )(*([a.reshape(nc, CHUNK, D), perm] + ([gelu_arg] if gelu_arg is not None else [])))


def _split_dot(x, u):
    hi = x.astype(BF16)
    lo = (x - hi.astype(F32)).astype(BF16)
    return (jnp.dot(hi, u, preferred_element_type=F32) + jnp.dot(lo, u, preferred_element_type=F32))


def _col_to_row(col, n):
    return jnp.broadcast_to(col, (n, LANES)).T[0:1, :]


def _row_to_col(row, n):
    return jnp.broadcast_to(row, (LANES, n)).T[:, 0:1]


def _head_in(pair, h, upper):
    x = pair.astype(F32)
    lane = lax.broadcasted_iota(jnp.int32, x.shape, 1)
    if (h % 2 == 1) != upper:
        x = pltpu.roll(x, HEAD_DIM, 1)
    keep = (lane >= HEAD_DIM) if upper else (lane < HEAD_DIM)
    return jnp.where(keep, x, 0.0).astype(BF16)


def _pair_out(even, odd, upper):
    lane = lax.broadcasted_iota(jnp.int32, even.shape, 1)
    if upper:
        return jnp.where(lane < HEAD_DIM, pltpu.roll(even, HEAD_DIM, 1), odd)
    return jnp.where(lane < HEAD_DIM, even, pltpu.roll(odd, HEAD_DIM, 1))


def _attn_fwd(name, q, kv, tq, tk, hb):
    L = q.shape[0]
    H = q.shape[1] // HEAD_DIM
    dw = 2 * HEAD_DIM
    nq, nkb = L // tq, L // tk
    hs = range(hb)
    assert tk % tq == 0 and hb % 2 == 0

    def body(q_ref, kv_ref, o_ref, rc_ref):
        i = pl.program_id(1)
        diff = (lax.broadcasted_iota(jnp.int32, (tq, tk), 1) - lax.broadcasted_iota(jnp.int32, (tq, tk), 0))
        u_suf = (lax.broadcasted_iota(jnp.int32, (tk, tk), 0)
                 > lax.broadcasted_iota(jnp.int32, (tk, tk), 1)).astype(BF16)
        nfull = (i * tq) // tk
        rc_ref[...] = jnp.zeros_like(rc_ref)
        qw = [_head_in(q_ref[:, dw * (h // 2):dw * (h // 2 + 1)], h, False) for h in hs]

        def block(kb, carry, masked):
            runs, accs = carry
            ks = pl.multiple_of(kb * tk, tk)
            kvb = [kv_ref[pl.ds(ks, tk), dw * h:dw * (h + 1)] for h in hs]
            z = [lax.dot_general(qw[h], kvb[h], (((1,), (1,)), ((), ())), preferred_element_type=F32)
                 for h in hs]
            e = [jnp.exp(-jnp.abs(z[h])) for h in hs]
            sp = [jnp.maximum(z[h], 0.0) + jnp.log(1.0 + e[h]) for h in hs]
            if masked:
                causal = diff < (i * tq - ks)
                lom = [jnp.where(causal, -sp[h], 0.0) for h in hs]
            else:
                lom = [-sp[h] for h in hs]
            rem = [runs[h] + _split_dot(lom[h], u_suf) for h in hs]
            w = [jnp.exp(z[h] - sp[h] + rem[h]) for h in hs]
            if masked:
                w = [jnp.where(causal, w[h], 0.0) for h in hs]
            accs = tuple(accs[h] + jnp.dot(w[h].astype(BF16), kvb[h], preferred_element_type=F32) for h in hs)
            for h in hs:
                rc_ref[h, kb] = _col_to_row(runs[h], tq)
            runs = tuple(runs[h] + jnp.sum(lom[h], axis=1, keepdims=True) for h in hs)
            return runs, accs

        init = (tuple(jnp.zeros((tq, 1), F32) for _ in hs), tuple(jnp.zeros((tq, dw), F32) for _ in hs))
        carry = block(nfull, init, True)
        _, accs = lax.fori_loop(0, nfull, lambda n, c: block(nfull - 1 - n, c, False), carry)
        for p in range(hb // 2):
            o_ref[:, dw * p:dw * (p + 1)] = _pair_out(accs[2 * p], accs[2 * p + 1], True).astype(o_ref.dtype)

    qspec = pl.BlockSpec((tq, hb * HEAD_DIM), lambda h, i: (i, h))
    kspec = pl.BlockSpec((L, hb * dw), lambda h, i: (0, h))
    return _pcall(
        body, name=name, grid=(H // hb, nq), in_specs=[qspec, kspec],
        out_specs=[qspec, pl.BlockSpec((hb, None, nkb, 1, tq), lambda h, i: (h, i, 0, 0, 0))],
        out_shape=[jax.ShapeDtypeStruct((L, H * HEAD_DIM), BF16), jax.ShapeDtypeStruct((H, nq, nkb, 1, tq), F32)],
        compiler_params=_params(("parallel", "arbitrary")),
    )(q, kv)


def _attn_bwd(name, q, kv, do, rc, tq, tk, scale, hb):
    L = q.shape[0]
    H = q.shape[1] // HEAD_DIM
    dw_ = 2 * HEAD_DIM
    nq, nkb = L // tq, L // tk
    hs = range(hb)
    assert tk % tq == 0 and hb % 2 == 0

    def body(q_ref, kv_ref, do_ref, rc_ref, dq_ref, dkv_ref):
        i = pl.program_id(1)

        @pl.when(i == 0)
        def _():
            dkv_ref[...] = jnp.zeros_like(dkv_ref)

        diff = (lax.broadcasted_iota(jnp.int32, (tq, tk), 1) - lax.broadcasted_iota(jnp.int32, (tq, tk), 0))
        r_io = lax.broadcasted_iota(jnp.int32, (tk, tk), 0)
        c_io = lax.broadcasted_iota(jnp.int32, (tk, tk), 1)
        u_suf = (r_io > c_io).astype(BF16)
        u_pre = (r_io < c_io).astype(BF16)
        nfull = (i * tq) // tk
        qb = [_head_in(q_ref[:, dw_ * (h // 2):dw_ * (h // 2 + 1)], h, False) for h in hs]
        dob = [_head_in(do_ref[:, dw_ * (h // 2):dw_ * (h // 2 + 1)], h, True) for h in hs]

        def block(kb, carry, masked):
            pres, dqs = carry
            ks = pl.multiple_of(kb * tk, tk)
            kvb = [kv_ref[pl.ds(ks, tk), dw_ * h:dw_ * (h + 1)] for h in hs]
            z = [lax.dot_general(qb[h], kvb[h], (((1,), (1,)), ((), ())), preferred_element_type=F32) for h in hs]
            dw = [lax.dot_general(dob[h], kvb[h], (((1,), (1,)), ((), ())), preferred_element_type=F32)
                  for h in hs]
            e = [jnp.exp(-jnp.abs(z[h])) for h in hs]
            sp = [jnp.maximum(z[h], 0.0) + jnp.log(1.0 + e[h]) for h in hs]
            if masked:
                causal = diff < (i * tq - ks)
                lom = [jnp.where(causal, -sp[h], 0.0) for h in hs]
            else:
                lom = [-sp[h] for h in hs]
            rem = [_row_to_col(rc_ref[h, kb], tq) + _split_dot(lom[h], u_suf) for h in hs]
            logb = [z[h] - sp[h] for h in hs]
            w = [jnp.exp(logb[h] + rem[h]) for h in hs]
            if masked:
                w = [jnp.where(causal, w[h], 0.0) for h in hs]
            da = [dw[h] * w[h] for h in hs]
            p = [pres[h] + jnp.dot(da[h].astype(BF16), u_pre, preferred_element_type=F32) for h in hs]
            dz = [da[h] - jnp.exp(logb[h]) * (da[h] + p[h]) for h in hs]
            if masked:
                dz = [jnp.where(causal, dz[h], 0.0) for h in hs]
            dqs = tuple(dqs[h] + jnp.dot(dz[h].astype(BF16), kvb[h], preferred_element_type=F32) for h in hs)
            for h in hs:
                lhs = jnp.concatenate([dz[h].T, w[h].T], axis=1).astype(BF16)
                rhs = jnp.concatenate([qb[h], dob[h]], axis=0)
                dkv_ref[pl.ds(ks, tk), dw_ * h:dw_ * (h + 1)] += jnp.dot(lhs, rhs, preferred_element_type=F32)
            pres = tuple(pres[h] + jnp.sum(da[h], axis=1, keepdims=True) for h in hs)
            return pres, dqs

        init = (tuple(jnp.zeros((tq, 1), F32) for _ in hs), tuple(jnp.zeros((tq, dw_), F32) for _ in hs))
        carry = lax.fori_loop(0, nfull, lambda kb, c: block(kb, c, False), init)
        _, dqs = block(nfull, carry, True)
        for p in range(hb // 2):
            dq_ref[:, dw_ * p:dw_ * (p + 1)] = (_pair_out(dqs[2 * p], dqs[2 * p + 1], False)
                                                * scale).astype(dq_ref.dtype)

    qspec = pl.BlockSpec((tq, hb * HEAD_DIM), lambda h, i: (i, h))
    kspec = pl.BlockSpec((L, hb * dw_), lambda h, i: (0, h))
    return _pcall(
        body, name=name, grid=(H // hb, nq),
        in_specs=[qspec, kspec, qspec, pl.BlockSpec((hb, None, nkb, 1, tq), lambda h, i: (h, i, 0, 0, 0))],
        out_specs=[qspec, kspec],
        out_shape=[jax.ShapeDtypeStruct((L, H * HEAD_DIM), BF16), jax.ShapeDtypeStruct((L, 2 * H * HEAD_DIM), F32)],
        compiler_params=_params(("parallel", "arbitrary")),
    )(q, kv, do, rc)


_MESH = pl.DeviceIdType.MESH
_HBM = pl.BlockSpec(memory_space=pltpu.HBM)


def _all_gather(name, shards):
    n = len(shards)

    def body(*refs):
        x_refs, out_refs = refs[:n], refs[n:2 * n]
        send_sems, recv_sems, local_sems = refs[2 * n:]
        x, y, c = lax.axis_index("x"), lax.axis_index("y"), lax.axis_index("c")
        me, sibling = (x, y, c), (x, y, 1 - c)
        chips = [(1 - x, y), (x, 1 - y), (1 - x, 1 - y)]

        def slot(a, px, py, pc):
            return out_refs[a].at[4 * px + 2 * py + pc]

        def copy(a, k, block, to, src=None):
            return pltpu.make_async_remote_copy(
                src_ref=slot(a, *block) if src is None else src, dst_ref=slot(a, *block),
                send_sem=send_sems.at[7 * a + k], recv_sem=recv_sems.at[7 * a + k], device_id=to,
                device_id_type=_MESH)

        mine = [pltpu.make_async_copy(x_refs[a], slot(a, *me), local_sems.at[a]) for a in range(n)]
        for cp in mine:
            cp.start()
        first = []
        for a in range(n):
            first.append(copy(a, 0, me, sibling, src=x_refs[a]))
            first += [copy(a, 1 + j, me, (*chip, c), src=x_refs[a]) for j, chip in enumerate(chips)]
        for cp in first:
            cp.start()
        passed = []
        for j, chip in enumerate(chips):
            for a in range(n):
                copy(a, 1 + j, (*chip, c), me).wait_recv()
                passed.append(copy(a, 4 + j, (*chip, c), sibling))
                passed[-1].start()
        for a in range(n):
            copy(a, 0, sibling, me).wait_recv()
            for j, chip in enumerate(chips):
                copy(a, 4 + j, (*chip, 1 - c), me).wait_recv()
        for cp in first + passed:
            cp.wait_send()
        for cp in mine:
            cp.wait()

    return _pcall(
        body, name=name, out_shape=[jax.ShapeDtypeStruct((N_DEV,) + s.shape, s.dtype) for s in shards],
        in_specs=[_HBM] * n, out_specs=[_HBM] * n,
        scratch_shapes=[pltpu.SemaphoreType.DMA((7 * n,)), pltpu.SemaphoreType.DMA((7 * n,)),
                        pltpu.SemaphoreType.DMA((n,))],
    )(*shards)


def _sibling_exchange(name, xs):
    n = len(xs)

    def body(*refs):
        x_refs, out_refs, send_sems, recv_sems = refs[:n], refs[n:2 * n], refs[2 * n], refs[2 * n + 1]
        c = lax.axis_index("c")
        sibling = (lax.axis_index("x"), lax.axis_index("y"), 1 - c)
        cps = [pltpu.make_async_remote_copy(src_ref=x_refs[a].at[:, 1 - c], dst_ref=out_refs[a],
                                            send_sem=send_sems.at[a], recv_sem=recv_sems.at[a], device_id=sibling,
                                            device_id_type=_MESH)
               for a in range(n)]
        for cp in cps:
            cp.start()
        for cp in cps:
            cp.wait()

    return _pcall(
        body, name=name, out_shape=[jax.ShapeDtypeStruct(x.shape[:1] + x.shape[2:], x.dtype) for x in xs],
        in_specs=[_HBM] * n, out_specs=[_HBM] * n,
        scratch_shapes=[pltpu.SemaphoreType.DMA((n,)), pltpu.SemaphoreType.DMA((n,))],
    )(*xs)


def _chip_exchange(name, xs):
    n = len(xs)

    def body(*refs):
        x_refs, out_refs, send_sems, recv_sems = refs[:n], refs[n:2 * n], refs[2 * n], refs[2 * n + 1]
        mx, my, mc = lax.axis_index("x"), lax.axis_index("y"), lax.axis_index("c")
        chips = [(1 - mx, my), (mx, 1 - my), (1 - mx, 1 - my)]
        cps = [pltpu.make_async_remote_copy(src_ref=x_refs[a].at[2 * px + py], dst_ref=out_refs[a].at[j],
                                            send_sem=send_sems.at[3 * a + j], recv_sem=recv_sems.at[3 * a + j],
                                            device_id=(px, py, mc), device_id_type=_MESH)
               for a in range(n) for j, (px, py) in enumerate(chips)]
        for cp in cps:
            cp.start()
        for cp in cps:
            cp.wait()

    return _pcall(
        body, name=name, out_shape=[jax.ShapeDtypeStruct((3,) + x.shape[1:], x.dtype) for x in xs],
        in_specs=[_HBM] * n, out_specs=[_HBM] * n,
        scratch_shapes=[pltpu.SemaphoreType.DMA((3 * n,)), pltpu.SemaphoreType.DMA((3 * n,))],
    )(*xs)


def _pair_sum(name, a, b):
    nb, R, C = a.shape
    tm = _tile(R, 512, 16)

    def fn(i, ab, bb):
        s = ab.astype(F32) + bb.astype(F32)
        return s, s

    spec = lambda arr: (arr, (1, tm, C), lambda i: (i // (R // tm), i % (R // tm), 0))
    out = lambda dt: ((nb, R, C), dt, (1, tm, C), lambda i: (i // (R // tm), i % (R // tm), 0))
    return _rowwise(name, fn, nb * (R // tm), [spec(a), spec(b)], [out(F32), out(a.dtype)])


def _sum8(name, g):
    n, R, C = g.shape
    tm = _tile(R, 256)

    def fn(i, gb):
        s = gb[0]
        for d in range(1, n):
            s = s + gb[d]
        return (s,)

    return _rowwise(name, fn, R // tm, [(g, (n, tm, C), lambda i: (0, i, 0))], [_orows(R, C, F32, tm)])[0]


def _adamw(name, w, m, v, grads, tm=512):
    R, C = w.shape
    tm = _tile(R, tm, 16)
    ng = len(grads)

    def fn(i, wb, mb, vb, *gs):
        g = gs[0].astype(F32)
        for t in gs[1:]:
            g = g + t.astype(F32)
        mn = ADAM_B1 * mb + (1.0 - ADAM_B1) * g
        vn = ADAM_B2 * vb + (1.0 - ADAM_B2) * (g * g)
        m_hat = mn / (1.0 - ADAM_B1 ** ADAM_STEP)
        v_hat = vn / (1.0 - ADAM_B2 ** ADAM_STEP)
        delta = -ADAM_LR * (m_hat / (jnp.sqrt(v_hat) + ADAM_EPS) + ADAM_WD * wb)
        return g, delta, mn, vn

    ins = [_rows(w, tm), _rows(m, tm), _rows(v, tm)] + [_rows(g, tm) for g in grads]
    return _rowwise(name, fn, R // tm, ins, [_orows(R, C, F32, tm) for _ in range(4)])


CLASSES = [
    (BF16, True, [('ssm_w_in', None), ('attn_w_q', None), ('attn_w_o', None), ('ffn_w_down', 0), ('ffn_w_down', 1)]),
    (BF16, False, [('ssm_w_glu', None), ('kv_w', None)]),
    (BF16, False, [('ffn_w_up', 0), ('ffn_w_up', 1)]),
    (F32, False, [('ffn_conv_w', 0), ('ffn_conv_w', 1)]),
]


def _shard2d(a, name, layer):
    if name == 'ffn_conv_w':
        return a[layer, :, 0, :]
    if name == 'kv_w':
        return a
    return a[0 if layer is None else layer]


def _weights_of(members):
    names = []
    for n, _ in members:
        if not names or names[-1] != n:
            names.append(n)
    return names


def _pack_class(d, members, dtype):
    width = _shard2d(d[members[0][0]], *members[0]).shape[1]
    parts = [d[n].reshape(-1, width).astype(dtype) for n in _weights_of(members)]
    return parts[0] if len(parts) == 1 else jnp.concatenate(parts, axis=0)


def _full_weights(gaths, d):
    w = {}
    for (_, row_sharded, members), g in zip(CLASSES, gaths):
        off = 0
        for n, l in members:
            r = _shard2d(d[n], n, l).shape[0]
            blk = g[:, off:off + r]
            if row_sharded:
                w[(n, l)] = blk.reshape(N_DEV * r, g.shape[2])
            else:
                w[(n, l)] = blk.transpose(1, 0, 2).reshape(r, N_DEV * g.shape[2])
            off += r
    return w


def _grad_classes(gw):
    out = []
    for dtype, row_sharded, members in CLASSES:
        parts = []
        for n, l in members:
            g = gw[(n, l)].astype(dtype)
            if row_sharded:
                parts.append(g.reshape(N_DEV, g.shape[0] // N_DEV, g.shape[1]))
            else:
                parts.append(g.reshape(g.shape[0], N_DEV, g.shape[1] // N_DEV).transpose(1, 0, 2))
        out.append(jnp.concatenate(parts, axis=1))
    return out


def _unpack_class(buf, members, shapes):
    out, off = {}, 0
    for n in _weights_of(members):
        r = math.prod(shapes[n]) // buf.shape[1]
        out[n] = buf[off:off + r].reshape(shapes[n])
        off += r
    return out


def _small_layout(shapes, D):
    lay, off = {}, 0
    for n in SMALL:
        r = -(-math.prod(shapes[n]) // D)
        lay[n] = (off, r)
        off += r
    lay['loss'] = (off, 1)
    off += 1
    return lay, -(-off // 8) * 8


def _pack_small(d, lay, total, D):
    parts = []
    for n in SMALL + ['loss']:
        if n not in d:
            parts.append(jnp.zeros((lay[n][1], D), F32))
            continue
        flat = d[n].reshape(-1).astype(F32)
        parts.append(jnp.pad(flat, (0, lay[n][1] * D - flat.shape[0])).reshape(lay[n][1], D))
    used = sum(lay[n][1] for n in SMALL + ['loss'])
    if total > used:
        parts.append(jnp.zeros((total - used, D), F32))
    return jnp.concatenate(parts, axis=0)


def _unpack_small(pack, lay, shapes, D):
    out = {}
    for n in SMALL:
        off, r = lay[n]
        out[n] = pack[off:off + r].reshape(-1)[:math.prod(shapes[n])].reshape(shapes[n])
    return out


def _ffn_fwd(tag, x, g_norm, w_up, conv_w, conv_b, w_down):
    h = _rms_fwd(f"{tag}_norm", x, g_norm)
    gu = _mm(f"{tag}_up", h, w_up, 'nn', BF16)
    a = _gate_fwd(f"{tag}_gate", gu, conv_w, conv_b)
    return _mm(f"{tag}_down", a, w_down, 'nn', F32, resid=x), (h, gu, a)


def _ffn_bwd(tag, dres, dres_bf, x, saved, g_norm, w_up, conv_w, conv_b, w_down):
    h, gu, a = saved
    da = _mm(f"{tag}_dgate", dres_bf, w_down, 'nt', BF16)
    d_w_down = _mm(f"{tag}_dwdown", a, dres_bf, 'tn', BF16)
    dgc, du, dw0, dw1, dw2, dcb = _gate_bwd1(f"{tag}_gate_b1", gu, da, conv_w, conv_b)
    dgu = _gate_bwd2(f"{tag}_gate_b2", dgc, du, conv_w)
    d_w_up = _mm(f"{tag}_dwup", h, dgu, 'tn', BF16)
    dh = _mm(f"{tag}_dh", dgu, w_up, 'nt', F32)
    dres, dres_bf, dg = _rms_bwd(f"{tag}_norm_b", x, dres, [(g_norm, dh)])
    return dres, dres_bf, dg, d_w_up, jnp.concatenate([dw0, dw1, dw2], axis=0), dcb[0], d_w_down


def kernel(x, norm_mix, norm_ffn, norm_kv, norm_final, ssm_w_in, ssm_a_re, ssm_a_im, ssm_log_dt, ssm_b_re, ssm_b_im, ssm_c_re, ssm_c_im, ssm_d, ssm_w_glu, kv_w, attn_w_q, attn_w_o, ffn_w_up, ffn_conv_w, ffn_conv_b, ffn_w_down, loss_target, m_norm_mix, m_norm_ffn, m_norm_kv, m_norm_final, m_ssm_w_in, m_ssm_a_re, m_ssm_a_im, m_ssm_log_dt, m_ssm_b_re, m_ssm_b_im, m_ssm_c_re, m_ssm_c_im, m_ssm_d, m_ssm_w_glu, m_kv_w, m_attn_w_q, m_attn_w_o, m_ffn_w_up, m_ffn_conv_w, m_ffn_conv_b, m_ffn_w_down, v_norm_mix, v_norm_ffn, v_norm_kv, v_norm_final, v_ssm_w_in, v_ssm_a_re, v_ssm_a_im, v_ssm_log_dt, v_ssm_b_re, v_ssm_b_im, v_ssm_c_re, v_ssm_c_im, v_ssm_d, v_ssm_w_glu, v_kv_w, v_attn_w_q, v_attn_w_o, v_ffn_w_up, v_ffn_conv_w, v_ffn_conv_b, v_ffn_w_down):
    wts = dict(zip(W_NAMES, (norm_mix, norm_ffn, norm_kv, norm_final, ssm_w_in, ssm_a_re, ssm_a_im, ssm_log_dt,
                             ssm_b_re, ssm_b_im, ssm_c_re, ssm_c_im, ssm_d, ssm_w_glu, kv_w, attn_w_q, attn_w_o,
                             ffn_w_up, ffn_conv_w, ffn_conv_b, ffn_w_down)))
    mom = dict(zip(W_NAMES, (m_norm_mix, m_norm_ffn, m_norm_kv, m_norm_final, m_ssm_w_in, m_ssm_a_re, m_ssm_a_im,
                             m_ssm_log_dt, m_ssm_b_re, m_ssm_b_im, m_ssm_c_re, m_ssm_c_im, m_ssm_d, m_ssm_w_glu,
                             m_kv_w, m_attn_w_q, m_attn_w_o, m_ffn_w_up, m_ffn_conv_w, m_ffn_conv_b, m_ffn_w_down)))
    vel = dict(zip(W_NAMES, (v_norm_mix, v_norm_ffn, v_norm_kv, v_norm_final, v_ssm_w_in, v_ssm_a_re, v_ssm_a_im,
                             v_ssm_log_dt, v_ssm_b_re, v_ssm_b_im, v_ssm_c_re, v_ssm_c_im, v_ssm_d, v_ssm_w_glu,
                             v_kv_w, v_attn_w_q, v_attn_w_o, v_ffn_w_up, v_ffn_conv_w, v_ffn_conv_b, v_ffn_w_down)))
    shapes = {n: wts[n].shape for n in W_NAMES}
    _, L, D = x.shape
    Fh = ffn_conv_b.shape[1]
    G, P = ssm_a_re.shape[1], ssm_a_re.shape[2]
    Hg = ssm_d.shape[2]
    x0 = x[0]
    target = loss_target[0]
    scale = HEAD_DIM ** -0.5

    gaths = _all_gather("gather_weights", [_pack_class(wts, members, dt) for dt, _, members in CLASSES])
    w = _full_weights(gaths, wts)
    nm = [norm_mix[l:l + 1] for l in range(2)]
    nf = [norm_ffn[l:l + 1] for l in range(2)]
    nkv = norm_kv[None]
    nfin = norm_final[None]
    cb = [ffn_conv_b[l:l + 1] for l in range(2)]
    cwt = [w[('ffn_conv_w', l)] for l in range(2)]

    s5p = (ssm_a_re[0], ssm_a_im[0], ssm_log_dt[0], ssm_b_re[0], ssm_b_im[0], ssm_c_re[0], ssm_c_im[0], ssm_d[0])
    (m_mat, n_mat, o_mat, lam1, lam2), s5_vjp = jax.vjp(_s5_build, *s5p)
    m_bf, n_bf, o_bf = m_mat.astype(BF16), n_mat.astype(BF16), o_mat.astype(BF16)

    h0 = _rms_fwd("l0_norm", x0, nm[0])
    u = _mm("l0_win", h0, w[('ssm_w_in', None)], 'nn', BF16)
    ug = _to_groups("s5_group_u", u, G, Hg, BF16)
    vloc = _bmm("s5_local_state", [(ug, n_bf, 'nt')], F32)
    st = _chunk_scan("s5_scan", vloc, lam1, lam2, reverse=False)
    yraw, yg = _bmm("s5_out", [(ug, m_bf, 'nt'), (st, o_bf, 'nt')], (F32, BF16),
                    post=lambda acc: (acc, _gelu(acc)), ungroup=(1, Hg))
    z = _mm("l0_wglu", yg, w[('ssm_w_glu', None)], 'nn', F32)
    x1 = _glu_fwd("l0_glu", x0, z)
    x2, ffn0 = _ffn_fwd("f0", x1, nf[0], w[('ffn_w_up', 0)], cwt[0], cb[0], w[('ffn_w_down', 0)])

    hkv = _rms_fwd("kv_norm", x2, nkv)
    nh = D // HEAD_DIM
    w_kv = w[('kv_w', None)].reshape(D, 2, nh, HEAD_DIM).transpose(0, 2, 1, 3).reshape(D, 2 * D)
    kvp = _mm("kv_proj", hkv, w_kv, 'nn', BF16)
    h2 = _rms_fwd("l1_norm", x2, nm[1])
    qn = _mm("l1_wq", h2, w[('attn_w_q', None)], 'nn', BF16, scale=scale)
    o, rc = _attn_fwd("attn_fwd", qn, kvp, ATT_TQ, ATT_TK, ATT_HB_FWD)
    x3 = _mm("l1_wo", o, w[('attn_w_o', None)], 'nn', F32, resid=x2)
    x4, ffn1 = _ffn_fwd("f1", x3, nf[1], w[('ffn_w_up', 1)], cwt[1], cb[1], w[('ffn_w_down', 1)])

    dres, dres_bf, dg_final, loss_part = _final_loss("loss_head", x4, nfin, target)

    gw = {}
    dres, dres_bf, dg_nf1, gup1, gcw1, gcb1, gdn1 = _ffn_bwd("f1", dres, dres_bf, x3, ffn1, nf[1], w[('ffn_w_up', 1)],
                                                             cwt[1], cb[1], w[('ffn_w_down', 1)])
    do = _mm("l1_do", dres_bf, w[('attn_w_o', None)], 'nt', BF16)
    gw[('attn_w_o', None)] = _mm("l1_dwo", o, dres_bf, 'tn', BF16)
    dq, dkv = _attn_bwd("attn_bwd", qn, kvp, do, rc, ATT_TQ, ATT_TK, scale, ATT_HB_BWD)
    gw[('attn_w_q', None)] = _mm("l1_dwq", h2, dq, 'tn', BF16)
    dh2 = _mm("l1_dh", dq, w[('attn_w_q', None)], 'nt', F32)
    g_kv = _mm("kv_dw", hkv, dkv, 'tn', BF16)
    gw[('kv_w', None)] = g_kv.reshape(D, nh, 2, HEAD_DIM).transpose(0, 2, 1, 3).reshape(D, 2 * D)
    dhkv = _mm("kv_dh", dkv, w_kv, 'nt', F32)
    dres, dres_bf, dg_nm1, dg_nkv = _rms_bwd("l1_norm_b", x2, dres, [(nm[1], dh2), (nkv, dhkv)])
    dres, dres_bf, dg_nf0, gup0, gcw0, gcb0, gdn0 = _ffn_bwd("f0", dres, dres_bf, x1, ffn0, nf[0], w[('ffn_w_up', 0)],
                                                             cwt[0], cb[0], w[('ffn_w_down', 0)])
    dz = _glu_bwd("l0_glu_b", dres, z)
    gw[('ssm_w_glu', None)] = _mm("l0_dwglu", yg, dz, 'tn', BF16)
    dyg = _mm("l0_dyg", dz, w[('ssm_w_glu', None)], 'nt', F32)
    dy = _to_groups("s5_group_dy", dyg, G, Hg, BF16, gelu_arg=yraw)
    ds = _bmm("s5_dstate", [(dy, o_bf, 'nn')], F32)
    dv_loc, dlam1, dlam2 = _chunk_scan("s5_scan_b", ds, lam1, lam2, reverse=True, s_fwd=st)
    du = _bmm("s5_du", [(dy, m_bf, 'nn'), (dv_loc, n_bf, 'nn')], BF16, ungroup=(0, Hg))
    d_m = _bmm("s5_dm", [(dy, ug, 'tn')], F32)
    d_o = _bmm("s5_do", [(dy, st, 'tn')], F32)
    d_n = _bmm("s5_dn", [(dv_loc, ug, 'tn')], F32)
    s5g = s5_vjp((d_m, d_n, d_o, dlam1, dlam2))
    gw[('ssm_w_in', None)] = _mm("l0_dwin", h0, du, 'tn', BF16)
    dh0 = _mm("l0_dh", du, w[('ssm_w_in', None)], 'nt', F32)
    grad_x, _, dg_nm0 = _rms_bwd("l0_norm_b", x0, dres, [(nm[0], dh0)])

    gw.update({('ffn_w_up', 0): gup0, ('ffn_w_up', 1): gup1, ('ffn_w_down', 0): gdn0, ('ffn_w_down', 1): gdn1,
               ('ffn_conv_w', 0): gcw0, ('ffn_conv_w', 1): gcw1})

    small_g = {
        'norm_mix': jnp.concatenate([dg_nm0, dg_nm1], axis=0), 'norm_ffn': jnp.concatenate([dg_nf0, dg_nf1], axis=0),
        'norm_kv': dg_nkv, 'norm_final': dg_final, 'ffn_conv_b': jnp.stack([gcb0, gcb1]),
        'ssm_a_re': s5g[0], 'ssm_a_im': s5g[1], 'ssm_log_dt': s5g[2], 'ssm_b_re': s5g[3], 'ssm_b_im': s5g[4],
        'ssm_c_re': s5g[5], 'ssm_c_im': s5g[6], 'ssm_d': s5g[7], 'loss': loss_part[0, 0:1],
    }
    lay, rs = _small_layout(shapes, D)
    small_sum = _sum8("small_sum", _all_gather("gather_small", [_pack_small(small_g, lay, rs, D)])[0])
    loss = small_sum[lay['loss'][0], 0]
    sg, sdelta, sm, sv = _adamw("adamw_small", _pack_small(wts, lay, rs, D), _pack_small(mom, lay, rs, D),
                                _pack_small(vel, lay, rs, D), [small_sum])
    small_out = [_unpack_small(t, lay, shapes, D) for t in (sg, sdelta, sm, sv)]

    c = lax.axis_index("c")
    chip = 2 * lax.axis_index("x") + lax.axis_index("y")
    gcls = [g.reshape((4, 2) + g.shape[1:]) for g in _grad_classes(gw)]
    mine = [lax.dynamic_index_in_dim(g, c, axis=1, keepdims=False) for g in gcls]
    got = _sibling_exchange("rs_sibling", gcls)
    sums = [_pair_sum(f"rs_pair_sum{k}", a, b) for k, (a, b) in enumerate(zip(mine, got))]
    from_chips = _chip_exchange("rs_chips", [on_wire for _, on_wire in sums])
    big_parts = [{}, {}, {}, {}]
    for k, ((_, _, members), (chip_f32, _), fc) in enumerate(zip(CLASSES, sums, from_chips)):
        own = lax.dynamic_index_in_dim(chip_f32, chip, axis=0, keepdims=False)
        res = _adamw(f"adamw_big{k}", _pack_class(wts, members, F32), _pack_class(mom, members, F32),
                     _pack_class(vel, members, F32), [own, fc[0], fc[1], fc[2]])
        for q in range(4):
            big_parts[q].update(_unpack_class(res[q], members, shapes))
    big_out = big_parts

    outs = [loss, grad_x[None]]
    for k in range(4):
        for n in W_NAMES:
            outs.append(big_out[k][n] if n in BIG else small_out[k][n])
    return tuple(outs)
```

```python
import math

import jax
import jax.numpy as jnp
from jax import lax
from jax.experimental import pallas as pl
from jax.experimental.pallas import tpu as pltpu

F32 = jnp.float32
BF16 = jnp.bfloat16

EPS = 1e-6
HEAD_DIM = 64
CHUNK = 16
N_DEV = 8
ADAM_LR = 0.001
ADAM_B1 = 0.9
ADAM_B2 = 0.999
ADAM_EPS = 1e-08
ADAM_WD = 0.01
ADAM_STEP = 10
VMEM_LIMIT = 48 * 1024 * 1024
MM_BLOCK_BYTES = 28 * 1024 * 1024
MM_TM_MAX = 1024
MM_T_MAX = 512
MM_TN_MAX = 1536
LANES = 128
ATT_TQ = 128
ATT_TK = 256
ATT_HB_FWD = 8
ATT_HB_BWD = 4

W_NAMES = ['norm_mix', 'norm_ffn', 'norm_kv', 'norm_final', 'ssm_w_in', 'ssm_a_re', 'ssm_a_im', 'ssm_log_dt',
           'ssm_b_re', 'ssm_b_im', 'ssm_c_re', 'ssm_c_im', 'ssm_d', 'ssm_w_glu', 'kv_w', 'attn_w_q', 'attn_w_o',
           'ffn_w_up', 'ffn_conv_w', 'ffn_conv_b', 'ffn_w_down']
BIG = ['ssm_w_in', 'ssm_w_glu', 'kv_w', 'attn_w_q', 'attn_w_o', 'ffn_w_up', 'ffn_w_down', 'ffn_conv_w']
SMALL = [n for n in W_NAMES if n not in BIG]


def _pcall(body, **kw):
    return pl.pallas_call(body, **kw)


def _params(sem=None):
    if sem is None:
        return pltpu.CompilerParams(vmem_limit_bytes=VMEM_LIMIT)
    return pltpu.CompilerParams(dimension_semantics=sem, vmem_limit_bytes=VMEM_LIMIT)


def _tile(n, pref, mult=8):
    best = None
    for t in range(mult, min(n, pref) + 1, mult):
        if n % t == 0:
            best = t
    return n if best is None else best


def _mm_tiles(M, N, K, mode, a_bytes, b_bytes, o_bytes):
    def cands(n, cap):
        c = [t for t in range(LANES, min(n, cap) + 1, LANES) if n % t == 0]
        return c or [n]
    best = None
    for tm in cands(M, MM_T_MAX if mode == 'tn' else MM_TM_MAX):
        for tn in cands(N, MM_TN_MAX):
            need = 2 * (tm * K * a_bytes + K * tn * b_bytes + tm * tn * o_bytes)
            score = tm * tn * (tm if mode == 'tn' else 1)
            if need <= MM_BLOCK_BYTES and (best is None or score > best[0]):
                best = (score, tm, tn)
    assert best is not None, (M, N, K)
    return best[1], best[2]


def _mm(name, a, b, mode, out_dtype=F32, scale=None, resid=None):
    if mode == 'nn':
        (M, K), (K2, N) = a.shape, b.shape
    elif mode == 'nt':
        (M, K), (N, K2) = a.shape, b.shape
    else:
        (K, M), (K2, N) = a.shape, b.shape
    assert K == K2, (name, a.shape, b.shape)
    o_bytes = jnp.dtype(out_dtype).itemsize + (resid.dtype.itemsize if resid is not None else 0)
    tm, tn = _mm_tiles(M, N, K, mode, a.dtype.itemsize, b.dtype.itemsize, o_bytes)
    if mode == 'tn':
        a_spec = pl.BlockSpec((K, tm), lambda i, j: (0, i))
    else:
        a_spec = pl.BlockSpec((tm, K), lambda i, j: (i, 0))
    if mode == 'nt':
        b_spec = pl.BlockSpec((tn, K), lambda i, j: (j, 0))
    else:
        b_spec = pl.BlockSpec((K, tn), lambda i, j: (0, j))
    o_spec = pl.BlockSpec((tm, tn), lambda i, j: (i, j))
    dn = {'nn': ((1,), (0,)), 'nt': ((1,), (1,)), 'tn': ((0,), (0,))}[mode]

    def body(*refs):
        a_ref, b_ref, o_ref = refs[0], refs[1], refs[-1]
        acc = lax.dot_general(a_ref[...].astype(BF16), b_ref[...].astype(BF16), (dn, ((), ())),
                              preferred_element_type=F32)
        if scale is not None:
            acc = acc * scale
        if resid is not None:
            acc = acc + refs[2][...]
        o_ref[...] = acc.astype(o_ref.dtype)

    return _pcall(
        body, name=name, grid=(M // tm, N // tn),
        in_specs=[a_spec, b_spec] + ([o_spec] if resid is not None else []),
        out_specs=o_spec,
        out_shape=jax.ShapeDtypeStruct((M, N), out_dtype),
        compiler_params=_params(("parallel", "parallel")),
    )(*([a, b] + ([resid] if resid is not None else [])))


def _bmm(name, terms, out_dtype, gb=8, post=None, ungroup=None):
    G = terms[0][0].shape[0]
    gb = _tile(G, gb, 1)
    dns = {'nn': ((1,), (0,)), 'nt': ((1,), (1,)), 'tn': ((0,), (0,))}

    def oshape(a, b, mode):
        m = a.shape[2] if mode == 'tn' else a.shape[1]
        n = b.shape[1] if mode == 'nt' else b.shape[2]
        return m, n

    m, n = oshape(*terms[0])
    out_dtypes = out_dtype if isinstance(out_dtype, (tuple, list)) else (out_dtype,)
    n_in = 2 * len(terms)
    if ungroup is not None:
        perm, gpb = _lane_perm(ungroup[1])
        assert gpb == gb and n == CHUNK * ungroup[1]

    def body(*refs):
        ins, outs = refs[:n_in], refs[len(refs) - len(out_dtypes):]
        kept = []
        for gi in range(gb):
            acc = None
            for t, (_, _, mode) in enumerate(terms):
                part = lax.dot_general(ins[2 * t][gi].astype(BF16), ins[2 * t + 1][gi].astype(BF16),
                                       (dns[mode], ((), ())), preferred_element_type=F32)
                acc = part if acc is None else acc + part
            vals = (acc,) if post is None else post(acc)
            for k, (o_ref, v) in enumerate(zip(outs, vals)):
                if ungroup is not None and k == ungroup[0]:
                    kept.append(v.astype(BF16))
                else:
                    o_ref[gi] = v.astype(o_ref.dtype)
        if ungroup is not None:
            x = lax.dot_general(jnp.concatenate(kept, axis=1), refs[n_in][...], (((1,), (1,)), ((), ())),
                                preferred_element_type=F32)
            o_ref = outs[ungroup[0]]
            for t in range(CHUNK):
                o_ref[:, t, :] = x[:, t * LANES:(t + 1) * LANES].astype(o_ref.dtype)

    in_specs, args = [], []
    for a, b, _ in terms:
        in_specs += [pl.BlockSpec((gb,) + a.shape[1:], lambda g: (g, 0, 0)),
                     pl.BlockSpec((gb,) + b.shape[1:], lambda g: (g, 0, 0))]
        args += [a, b]
    out_specs = [pl.BlockSpec((gb, m, n), lambda g: (g, 0, 0)) for _ in out_dtypes]
    out_shape = [jax.ShapeDtypeStruct((G, m, n), dt) for dt in out_dtypes]
    if ungroup is not None:
        in_specs.append(pl.BlockSpec(perm.shape, lambda g: (0, 0)))
        args.append(perm)
        out_specs[ungroup[0]] = pl.BlockSpec((m, CHUNK, LANES), lambda g: (0, 0, g))
        out_shape[ungroup[0]] = jax.ShapeDtypeStruct((m, CHUNK, G * ungroup[1]), out_dtypes[ungroup[0]])
    res = _pcall(
        body, name=name, grid=(G // gb,), in_specs=in_specs, out_specs=out_specs, out_shape=out_shape,
        compiler_params=_params(("parallel",)),
    )(*args)
    if ungroup is not None:
        res = list(res)
        res[ungroup[0]] = res[ungroup[0]].reshape(m * CHUNK, G * ungroup[1])
    return res[0] if len(out_dtypes) == 1 else res


def _rowwise(name, fn, n_steps, ins, outs, n_acc=0):
    n_in, n_out = len(ins), len(outs)

    def body(*refs):
        i = pl.program_id(0)
        vals = fn(i, *[r[...] for r in refs[:n_in]])
        o_refs = refs[n_in:]
        for j in range(n_out - n_acc):
            o_refs[j][...] = vals[j].astype(o_refs[j].dtype)
        if n_acc:
            @pl.when(i == 0)
            def _():
                for j in range(n_out - n_acc, n_out):
                    o_refs[j][...] = vals[j].astype(o_refs[j].dtype)

            @pl.when(i > 0)
            def _():
                for j in range(n_out - n_acc, n_out):
                    o_refs[j][...] += vals[j].astype(o_refs[j].dtype)

    res = _pcall(
        body, name=name, grid=(n_steps,),
        in_specs=[pl.BlockSpec(blk, im) for _, blk, im in ins],
        out_specs=[pl.BlockSpec(blk, im) for _, _, blk, im in outs],
        out_shape=[jax.ShapeDtypeStruct(s, d) for s, d, _, _ in outs],
        compiler_params=_params(("arbitrary",)),
    )(*[a for a, _, _ in ins])
    return res


def _rows(a, tm):
    return (a, (tm, a.shape[1]), lambda i: (i, 0))


def _whole(a):
    nd = a.ndim
    return (a, a.shape, lambda i: (0,) * nd)


def _orows(L, n, dtype, tm):
    return ((L, n), dtype, (tm, n), lambda i: (i, 0))


def _oacc(r, n):
    return ((r, n), F32, (r, n), lambda i: (0, 0))


def _rms_fwd(name, x, g, tm=512):
    L, D = x.shape
    tm = _tile(L, tm)

    def fn(i, xb, gb):
        r = lax.rsqrt(jnp.mean(xb * xb, axis=-1, keepdims=True) + EPS)
        return (xb * r * gb,)

    return _rowwise(name, fn, L // tm, [_rows(x, tm), _whole(g)], [_orows(L, D, BF16, tm)])[0]


def _rms_bwd(name, x, dres, branches, tm=256):
    L, D = x.shape
    tm = _tile(L, tm)
    nb = len(branches)

    def fn(i, xb, db, *rest):
        r = lax.rsqrt(jnp.mean(xb * xb, axis=-1, keepdims=True) + EPS)
        xh = xb * r
        dx = db
        dgs = []
        for b in range(nb):
            gb, dyb = rest[2 * b], rest[2 * b + 1].astype(F32)
            dxh = dyb * gb
            dx = dx + r * (dxh - xh * jnp.mean(dxh * xh, axis=-1, keepdims=True))
            dgs.append(jnp.sum(dyb * xh, axis=0, keepdims=True))
        return (dx, dx, *dgs)

    ins = [_rows(x, tm), _rows(dres, tm)]
    for g, dy in branches:
        ins += [_whole(g), _rows(dy, tm)]
    outs = [_orows(L, D, F32, tm), _orows(L, D, BF16, tm)] + [_oacc(1, D) for _ in range(nb)]
    return _rowwise(name, fn, L // tm, ins, outs, n_acc=nb)


def _final_loss(name, x, g, target, tm=256):
    L, D = x.shape
    tm = _tile(L, tm)

    def fn(i, xb, gb, tb):
        r = lax.rsqrt(jnp.mean(xb * xb, axis=-1, keepdims=True) + EPS)
        xh = xb * r
        err = xh * gb - tb
        dy = err * (1.0 / D)
        dxh = dy * gb
        dx = r * (dxh - xh * jnp.mean(dxh * xh, axis=-1, keepdims=True))
        dg = jnp.sum(dy * xh, axis=0, keepdims=True)
        per_row = jnp.mean(err * err, axis=-1, keepdims=True)
        loss = 0.5 * jnp.sum(per_row, axis=0, keepdims=True)
        return dx, dx, dg, jnp.broadcast_to(loss, (1, LANES))

    return _rowwise(name, fn, L // tm, [_rows(x, tm), _whole(g), _rows(target, tm)],
                    [_orows(L, D, F32, tm), _orows(L, D, BF16, tm), _oacc(1, D), _oacc(1, LANES)], n_acc=2)


def _glu_fwd(name, x, z, tm=256):
    L, D = x.shape
    tm = _tile(L, tm)

    def fn(i, xb, zb):
        return (xb + zb[:, :D] * jax.nn.sigmoid(zb[:, D:]),)

    return _rowwise(name, fn, L // tm, [_rows(x, tm), _rows(z, tm)], [_orows(L, D, F32, tm)])[0]


def _glu_bwd(name, dx, z, tm=256):
    L, D = dx.shape
    tm = _tile(L, tm)

    def fn(i, db, zb):
        z1, sg = zb[:, :D], jax.nn.sigmoid(zb[:, D:])
        return (jnp.concatenate([(db * sg).astype(BF16), (db * z1 * sg * (1.0 - sg)).astype(BF16)], axis=1),)

    return _rowwise(name, fn, L // tm, [_rows(dx, tm), _rows(z, tm)], [_orows(L, 2 * D, BF16, tm)])[0]


HALO = 16


def _halo_prev(a, tm):
    return (a, (HALO, a.shape[1]), lambda i: (jnp.maximum(i * (tm // HALO) - 1, 0), 0))


def _halo_next(a, tm):
    last = a.shape[0] // HALO - 1
    return (a, (HALO, a.shape[1]), lambda i: (jnp.minimum((i + 1) * (tm // HALO), last), 0))


def _shift_down(cur, halo, k, first):
    tm = cur.shape[0]
    rolled = pltpu.roll(cur, k, 0)
    tail = pltpu.roll(halo, k, 0)
    tail = jnp.where(first, 0.0, tail)
    row = lax.broadcasted_iota(jnp.int32, (tm, 1), 0)
    head = jnp.concatenate([tail, jnp.zeros((tm - 8, cur.shape[1]), cur.dtype)], axis=0) if tm > 8 else tail
    return jnp.where(row < k, head, rolled)


def _shift_up(cur, halo, k, last):
    tm = cur.shape[0]
    rolled = pltpu.roll(cur, tm - k, 0)
    head = pltpu.roll(halo, 8 - k, 0)
    head = jnp.where(last, 0.0, head)
    row = lax.broadcasted_iota(jnp.int32, (tm, 1), 0)
    tail = jnp.concatenate([jnp.zeros((tm - 8, cur.shape[1]), cur.dtype), head], axis=0) if tm > 8 else head
    return jnp.where(row >= tm - k, tail, rolled)


def _conv_pre(gb, hb, cw, cb, first):
    g1 = _shift_down(gb, hb, 1, first)
    g2 = _shift_down(gb, hb, 2, first)
    return cw[0:1] * g2 + cw[1:2] * g1 + cw[2:3] * gb + cb, g1, g2


def _gate_fwd(name, gu, cw, cb, tm=256):
    L, F2 = gu.shape
    Fh = F2 // 2
    tm = _tile(L, tm)

    def fn(i, gub, halo, cwb, cbb):
        gub, halo = gub.astype(F32), halo.astype(F32)[HALO - 8:]
        gc, _, _ = _conv_pre(gub[:, :Fh], halo[:, :Fh], cwb, cbb, i == 0)
        return (gc * jax.nn.sigmoid(gc) * gub[:, Fh:],)

    return _rowwise(name, fn, L // tm, [_rows(gu, tm), _halo_prev(gu, tm), _whole(cw), _whole(cb)],
                    [_orows(L, Fh, BF16, tm)])[0]


def _gate_bwd(name, gu, da, cw, cb, tm=128):
    L, F2 = gu.shape
    Fh = F2 // 2
    tm = _tile(L, tm)
    n = L // tm

    def dgate(gb, ub, dab, hb, cwb, cbb, first):
        gc, g1, g2 = _conv_pre(gb, hb, cwb, cbb, first)
        sg = jax.nn.sigmoid(gc)
        return dab * ub * sg * (1.0 + gc * (1.0 - sg)), dab * gc * sg, g1, g2

    def fn(i, gub, prev, nxt, dab, dnx, cwb, cbb):
        gub, prev, nxt = gub.astype(F32), prev.astype(F32)[HALO - 8:], nxt.astype(F32)[:8]
        dab, dnx = dab.astype(F32), dnx.astype(F32)[:8]
        gb, ub = gub[:, :Fh], gub[:, Fh:]
        dgc, du, g1, g2 = dgate(gb, ub, dab, prev[:, :Fh], cwb, cbb, i == 0)
        dgc_next, _, _, _ = dgate(nxt[:, :Fh], nxt[:, Fh:], dnx, gb[tm - 8:], cwb, cbb, False)
        d1 = _shift_up(dgc, dgc_next, 1, i == n - 1)
        d2 = _shift_up(dgc, dgc_next, 2, i == n - 1)
        dg = cwb[2:3] * dgc + cwb[1:2] * d1 + cwb[0:1] * d2
        return (jnp.concatenate([dg.astype(BF16), du.astype(BF16)], axis=1),
                jnp.sum(dgc * g2, axis=0, keepdims=True), jnp.sum(dgc * g1, axis=0, keepdims=True),
                jnp.sum(dgc * gb, axis=0, keepdims=True), jnp.sum(dgc, axis=0, keepdims=True))

    return _rowwise(name, fn, n,
                    [_rows(gu, tm), _halo_prev(gu, tm), _halo_next(gu, tm), _rows(da, tm), _halo_next(da, tm),
                     _whole(cw), _whole(cb)],
                    [_orows(L, 2 * Fh, BF16, tm)] + [_oacc(1, Fh) for _ in range(4)], n_acc=4)


def _s5_build(a_re, a_im, log_dt, b_re, b_im, c_re, c_im, d):
    T = CHUNK
    G, P = a_re.shape
    H = d.shape[1]
    hi = lax.Precision.HIGHEST
    dt = jnp.exp(log_dt)[:, None]
    mag = jnp.exp(a_re * dt)
    ab_re = mag * jnp.cos(a_im * dt)
    ab_im = mag * jnp.sin(a_im * dt)
    den = a_re * a_re + a_im * a_im
    f_re = ((ab_re - 1.0) * a_re + ab_im * a_im) / den
    f_im = (ab_im * a_re - (ab_re - 1.0) * a_im) / den
    bb_re = f_re[..., None] * b_re - f_im[..., None] * b_im
    bb_im = f_re[..., None] * b_im + f_im[..., None] * b_re
    tau = jnp.arange(T + 1, dtype=F32)[None, :, None]
    pmag = jnp.exp(tau * (a_re * dt)[:, None, :])
    pang = tau * (a_im * dt)[:, None, :]
    pw_re = pmag * jnp.cos(pang)
    pw_im = pmag * jnp.sin(pang)
    cp_re = c_re[:, :, None, :] * pw_re[:, None] - c_im[:, :, None, :] * pw_im[:, None]
    cp_im = c_re[:, :, None, :] * pw_im[:, None] + c_im[:, :, None, :] * pw_re[:, None]
    cp_cat = jnp.concatenate([cp_re, -cp_im], axis=-1)
    bb_cat = jnp.concatenate([bb_re, bb_im], axis=1)
    kt = jnp.einsum('ghtq,gqk->ghtk', cp_cat[:, :, :T], bb_cat, precision=hi)
    kt = kt.at[:, :, 0, :].add(d[:, :, None] * jnp.eye(H, dtype=F32)[None])
    kp = jnp.concatenate([kt[:, :, ::-1, :], jnp.zeros((G, H, T - 1, H), F32)], axis=2).reshape(G, H, (2 * T - 1) * H)
    m_mat = jnp.stack([kp[:, :, (T - 1 - t) * H:(2 * T - 1 - t) * H] for t in range(T)], axis=1)
    m_mat = m_mat.reshape(G, T * H, T * H)
    pr = jnp.repeat(pw_re[:, :T][:, ::-1].transpose(0, 2, 1), H, axis=2)
    pi = jnp.repeat(pw_im[:, :T][:, ::-1].transpose(0, 2, 1), H, axis=2)
    br, bi = jnp.tile(bb_re, (1, 1, T)), jnp.tile(bb_im, (1, 1, T))
    n_mat = jnp.concatenate([pr * br - pi * bi, pr * bi + pi * br], axis=1)
    o_mat = cp_cat[:, :, 1:].transpose(0, 2, 1, 3).reshape(G, T * H, 2 * P)
    lam1 = jnp.concatenate([pw_re[:, T], pw_re[:, T]], axis=-1)[:, None, :]
    lam2 = jnp.concatenate([-pw_im[:, T], pw_im[:, T]], axis=-1)[:, None, :]
    return m_mat, n_mat, o_mat, lam1, lam2


def _chunk_scan(name, v, lam1, lam2, reverse, s_fwd=None, gb=32):
    G, nc, W = v.shape
    gb = _tile(G, gb, 1)
    half = W // 2
    ntile = nc // 8

    def body(*refs):
        if reverse:
            v_ref, l1_ref, l2_ref, s_ref, o_ref, d1_ref, d2_ref = refs
        else:
            v_ref, l1_ref, l2_ref, o_ref = refs
        l1 = jnp.broadcast_to(l1_ref[...], (gb, 8, W))
        l2 = jnp.broadcast_to(l2_ref[...], (gb, 8, W))
        if reverse:
            l2 = -l2
        row = lax.broadcasted_iota(jnp.int32, (gb, 8, W), 1)

        def tile_step(n, carry):
            if reverse:
                st, a1, a2 = carry
                base = pl.multiple_of((ntile - 1 - n) * 8, 8)
            else:
                st = carry
                base = pl.multiple_of(n * 8, 8)
            vt = v_ref[:, pl.ds(base, 8), :]
            out = jnp.zeros((gb, 8, W), F32)
            order = range(7, -1, -1) if reverse else range(8)
            for r in order:
                out = jnp.where(row == r, st, out)
                vr = jnp.broadcast_to(vt[:, r:r + 1, :], (gb, 8, W))
                st = l1 * st + l2 * pltpu.roll(st, half, 2) + vr
            o_ref[:, pl.ds(base, 8), :] = out
            if reverse:
                sv = s_ref[:, pl.ds(base, 8), :]
                a1 = a1 + out * sv
                a2 = a2 + out * pltpu.roll(sv, half, 2)
                return st, a1, a2
            return st

        zero = jnp.zeros((gb, 8, W), F32)
        if reverse:
            _, a1, a2 = lax.fori_loop(0, ntile, tile_step, (zero, zero, zero))
            d1_ref[...] = jnp.sum(a1, axis=1, keepdims=True)
            d2_ref[...] = jnp.sum(a2, axis=1, keepdims=True)
        else:
            lax.fori_loop(0, ntile, tile_step, zero)

    big = pl.BlockSpec((gb, nc, W), lambda g: (g, 0, 0))
    vec = pl.BlockSpec((gb, 1, W), lambda g: (g, 0, 0))
    if reverse:
        return _pcall(body, name=name, grid=(G // gb,), in_specs=[big, vec, vec, big],
                      out_specs=[big, vec, vec],
                      out_shape=[jax.ShapeDtypeStruct((G, nc, W), F32), jax.ShapeDtypeStruct((G, 1, W), F32),
                                 jax.ShapeDtypeStruct((G, 1, W), F32)],
                      compiler_params=_params(("parallel",)))(v, lam1, lam2, s_fwd)
    return _pcall(body, name=name, grid=(G // gb,), in_specs=[big, vec, vec], out_specs=big,
                  out_shape=jax.ShapeDtypeStruct((G, nc, W), F32),
                  compiler_params=_params(("parallel",)))(v, lam1, lam2)


_GELU_C = math.sqrt(2.0 / math.pi)


def _gelu(y):
    return 0.5 * y * (1.0 + jnp.tanh(_GELU_C * (y + 0.044715 * y * y * y)))


def _gelu_grad(y):
    t = jnp.tanh(_GELU_C * (y + 0.044715 * y * y * y))
    return 0.5 * (1.0 + t) + 0.5 * y * (1.0 - t * t) * _GELU_C * (1.0 + 3.0 * 0.044715 * y * y)


def _lane_perm(Hg):
    gpb = LANES // Hg
    r = jnp.arange(CHUNK * LANES)
    t, gl, h = r // LANES, (r % LANES) // Hg, r % Hg
    target = gl * (CHUNK * Hg) + t * Hg + h
    return (target[:, None] == r[None, :]).astype(BF16), gpb


def _to_groups(name, a, G, Hg, out_dtype, gelu_arg=None):
    L, D = a.shape
    nc = L // CHUNK
    perm, gpb = _lane_perm(Hg)
    gw = CHUNK * Hg

    def body(*refs):
        x_ref, p_ref, o_ref = refs[0], refs[1], refs[-1]
        x = jnp.concatenate([x_ref[:, t, :] for t in range(CHUNK)], axis=1)
        if x.dtype == BF16:
            z = jnp.dot(x, p_ref[...], preferred_element_type=F32)
        else:
            z = _split_dot(x, p_ref[...])
        for gl in range(gpb):
            piece = z[:, gl * gw:(gl + 1) * gw]
            if gelu_arg is not None:
                piece = piece * _gelu_grad(refs[2][gl])
            o_ref[gl] = piece.astype(o_ref.dtype)

    grouped = pl.BlockSpec((gpb, nc, gw), lambda g: (g, 0, 0))
    return _pcall(
        body, name=name, grid=(G // gpb,),
        in_specs=[pl.BlockSpec((nc, CHUNK, LANES), lambda g: (0, 0, g)),
                  pl.BlockSpec(perm.shape, lambda g: (0, 0))] + ([grouped] if gelu_arg is not None else []),
        out_specs=grouped,
        out_shape=jax.ShapeDtypeStruct((G, nc, gw), out_dtype),
        compiler_params=_params(("parallel",)),
    )(*([a.reshape(nc, CHUNK, D), perm] + ([gelu_arg] if gelu_arg is not None else [])))


def _split_dot(x, u):
    hi = x.astype(BF16)
    lo = (x - hi.astype(F32)).astype(BF16)
    return (jnp.dot(hi, u, preferred_element_type=F32) + jnp.dot(lo, u, preferred_element_type=F32))


def _col_to_row(col, n):
    return jnp.broadcast_to(col, (n, LANES)).T[0:1, :]


def _row_to_col(row, n):
    return jnp.broadcast_to(row, (LANES, n)).T[:, 0:1]


def _head_in(pair, h, upper):
    x = pair.astype(F32)
    lane = lax.broadcasted_iota(jnp.int32, x.shape, 1)
    if (h % 2 == 1) != upper:
        x = pltpu.roll(x, HEAD_DIM, 1)
    keep = (lane >= HEAD_DIM) if upper else (lane < HEAD_DIM)
    return jnp.where(keep, x, 0.0).astype(BF16)


def _pair_out(even, odd, upper):
    lane = lax.broadcasted_iota(jnp.int32, even.shape, 1)
    if upper:
        return jnp.where(lane < HEAD_DIM, pltpu.roll(even, HEAD_DIM, 1), odd)
    return jnp.where(lane < HEAD_DIM, even, pltpu.roll(odd, HEAD_DIM, 1))


def _attn_fwd(name, q, kv, tq, tk, hb):
    L = q.shape[0]
    H = q.shape[1] // HEAD_DIM
    hb = min(hb, H)
    dw = 2 * HEAD_DIM
    nq, nkb = L // tq, L // tk
    hs = range(hb)
    assert tk % tq == 0 and hb % 2 == 0 and H % hb == 0

    def body(q_ref, kv_ref, o_ref, rc_ref):
        i = pl.program_id(1)
        diff = (lax.broadcasted_iota(jnp.int32, (tq, tk), 1) - lax.broadcasted_iota(jnp.int32, (tq, tk), 0))
        u_suf = (lax.broadcasted_iota(jnp.int32, (tk, tk), 0)
                 > lax.broadcasted_iota(jnp.int32, (tk, tk), 1)).astype(BF16)
        nfull = (i * tq) // tk
        rc_ref[...] = jnp.zeros_like(rc_ref)
        qw = [_head_in(q_ref[:, dw * (h // 2):dw * (h // 2 + 1)], h, False) for h in hs]

        def block(kb, carry, masked):
            runs, accs = carry
            ks = pl.multiple_of(kb * tk, tk)
            kvb = [kv_ref[pl.ds(ks, tk), dw * h:dw * (h + 1)] for h in hs]
            z = [lax.dot_general(qw[h], kvb[h], (((1,), (1,)), ((), ())), preferred_element_type=F32)
                 for h in hs]
            e = [jnp.exp(-jnp.abs(z[h])) for h in hs]
            sp = [jnp.maximum(z[h], 0.0) + jnp.log(1.0 + e[h]) for h in hs]
            if masked:
                causal = diff < (i * tq - ks)
                lom = [jnp.where(causal, -sp[h], 0.0) for h in hs]
            else:
                lom = [-sp[h] for h in hs]
            rem = [runs[h] + _split_dot(lom[h], u_suf) for h in hs]
            w = [jnp.exp(z[h] - sp[h] + rem[h]) for h in hs]
            if masked:
                w = [jnp.where(causal, w[h], 0.0) for h in hs]
            accs = tuple(accs[h] + jnp.dot(w[h].astype(BF16), kvb[h], preferred_element_type=F32) for h in hs)
            for h in hs:
                rc_ref[h, kb] = _col_to_row(runs[h], tq)
            runs = tuple(runs[h] + jnp.sum(lom[h], axis=1, keepdims=True) for h in hs)
            return runs, accs

        init = (tuple(jnp.zeros((tq, 1), F32) for _ in hs), tuple(jnp.zeros((tq, dw), F32) for _ in hs))
        carry = block(nfull, init, True)
        _, accs = lax.fori_loop(0, nfull, lambda n, c: block(nfull - 1 - n, c, False), carry)
        for p in range(hb // 2):
            o_ref[:, dw * p:dw * (p + 1)] = _pair_out(accs[2 * p], accs[2 * p + 1], True).astype(o_ref.dtype)

    qspec = pl.BlockSpec((tq, hb * HEAD_DIM), lambda h, i: (i, h))
    kspec = pl.BlockSpec((L, hb * dw), lambda h, i: (0, h))
    return _pcall(
        body, name=name, grid=(H // hb, nq), in_specs=[qspec, kspec],
        out_specs=[qspec, pl.BlockSpec((hb, None, nkb, 1, tq), lambda h, i: (h, i, 0, 0, 0))],
        out_shape=[jax.ShapeDtypeStruct((L, H * HEAD_DIM), BF16), jax.ShapeDtypeStruct((H, nq, nkb, 1, tq), F32)],
        compiler_params=_params(("parallel", "arbitrary")),
    )(q, kv)


def _attn_bwd(name, q, kv, do, rc, tq, tk, scale, hb):
    L = q.shape[0]
    H = q.shape[1] // HEAD_DIM
    hb = min(hb, H)
    dw_ = 2 * HEAD_DIM
    nq, nkb = L // tq, L // tk
    hs = range(hb)
    assert tk % tq == 0 and hb % 2 == 0 and H % hb == 0

    def body(q_ref, kv_ref, do_ref, rc_ref, dq_ref, dkv_ref):
        i = pl.program_id(1)

        @pl.when(i == 0)
        def _():
            dkv_ref[...] = jnp.zeros_like(dkv_ref)

        diff = (lax.broadcasted_iota(jnp.int32, (tq, tk), 1) - lax.broadcasted_iota(jnp.int32, (tq, tk), 0))
        r_io = lax.broadcasted_iota(jnp.int32, (tk, tk), 0)
        c_io = lax.broadcasted_iota(jnp.int32, (tk, tk), 1)
        u_suf = (r_io > c_io).astype(BF16)
        u_pre = (r_io < c_io).astype(BF16)
        nfull = (i * tq) // tk
        qb = [_head_in(q_ref[:, dw_ * (h // 2):dw_ * (h // 2 + 1)], h, False) for h in hs]
        dob = [_head_in(do_ref[:, dw_ * (h // 2):dw_ * (h // 2 + 1)], h, True) for h in hs]

        def block(kb, carry, masked):
            pres, dqs = carry
            ks = pl.multiple_of(kb * tk, tk)
            kvb = [kv_ref[pl.ds(ks, tk), dw_ * h:dw_ * (h + 1)] for h in hs]
            z = [lax.dot_general(qb[h], kvb[h], (((1,), (1,)), ((), ())), preferred_element_type=F32) for h in hs]
            dw = [lax.dot_general(dob[h], kvb[h], (((1,), (1,)), ((), ())), preferred_element_type=F32)
                  for h in hs]
            e = [jnp.exp(-jnp.abs(z[h])) for h in hs]
            sp = [jnp.maximum(z[h], 0.0) + jnp.log(1.0 + e[h]) for h in hs]
            if masked:
                causal = diff < (i * tq - ks)
                lom = [jnp.where(causal, -sp[h], 0.0) for h in hs]
            else:
                lom = [-sp[h] for h in hs]
            rem = [_row_to_col(rc_ref[h, kb], tq) + _split_dot(lom[h], u_suf) for h in hs]
            logb = [z[h] - sp[h] for h in hs]
            w = [jnp.exp(logb[h] + rem[h]) for h in hs]
            if masked:
                w = [jnp.where(causal, w[h], 0.0) for h in hs]
            da = [dw[h] * w[h] for h in hs]
            p = [pres[h] + jnp.dot(da[h].astype(BF16), u_pre, preferred_element_type=F32) for h in hs]
            dz = [da[h] - jnp.exp(logb[h]) * (da[h] + p[h]) for h in hs]
            if masked:
                dz = [jnp.where(causal, dz[h], 0.0) for h in hs]
            dqs = tuple(dqs[h] + jnp.dot(dz[h].astype(BF16), kvb[h], preferred_element_type=F32) for h in hs)
            for h in hs:
                lhs = jnp.concatenate([dz[h].T, w[h].T], axis=1).astype(BF16)
                rhs = jnp.concatenate([qb[h], dob[h]], axis=0)
                dkv_ref[pl.ds(ks, tk), dw_ * h:dw_ * (h + 1)] += jnp.dot(lhs, rhs, preferred_element_type=F32)
            pres = tuple(pres[h] + jnp.sum(da[h], axis=1, keepdims=True) for h in hs)
            return pres, dqs

        init = (tuple(jnp.zeros((tq, 1), F32) for _ in hs), tuple(jnp.zeros((tq, dw_), F32) for _ in hs))
        carry = lax.fori_loop(0, nfull, lambda kb, c: block(kb, c, False), init)
        _, dqs = block(nfull, carry, True)
        for p in range(hb // 2):
            dq_ref[:, dw_ * p:dw_ * (p + 1)] = (_pair_out(dqs[2 * p], dqs[2 * p + 1], False)
                                                * scale).astype(dq_ref.dtype)

    qspec = pl.BlockSpec((tq, hb * HEAD_DIM), lambda h, i: (i, h))
    kspec = pl.BlockSpec((L, hb * dw_), lambda h, i: (0, h))
    return _pcall(
        body, name=name, grid=(H // hb, nq),
        in_specs=[qspec, kspec, qspec, pl.BlockSpec((hb, None, nkb, 1, tq), lambda h, i: (h, i, 0, 0, 0))],
        out_specs=[qspec, kspec],
        out_shape=[jax.ShapeDtypeStruct((L, H * HEAD_DIM), BF16), jax.ShapeDtypeStruct((L, 2 * H * HEAD_DIM), F32)],
        compiler_params=_params(("parallel", "arbitrary")),
    )(q, kv, do, rc)


_MESH = pl.DeviceIdType.MESH
_HBM = pl.BlockSpec(memory_space=pltpu.HBM)


def _all_gather(name, shards):
    n = len(shards)

    def body(*refs):
        x_refs, out_refs = refs[:n], refs[n:2 * n]
        send_sems, recv_sems, local_sems = refs[2 * n:]
        x, y, c = lax.axis_index("x"), lax.axis_index("y"), lax.axis_index("c")
        me, sibling = (x, y, c), (x, y, 1 - c)
        chips = [(1 - x, y), (x, 1 - y), (1 - x, 1 - y)]

        def slot(a, px, py, pc):
            return out_refs[a].at[4 * px + 2 * py + pc]

        def copy(a, k, block, to, src=None):
            return pltpu.make_async_remote_copy(
                src_ref=slot(a, *block) if src is None else src, dst_ref=slot(a, *block),
                send_sem=send_sems.at[7 * a + k], recv_sem=recv_sems.at[7 * a + k], device_id=to,
                device_id_type=_MESH)

        mine = [pltpu.make_async_copy(x_refs[a], slot(a, *me), local_sems.at[a]) for a in range(n)]
        for cp in mine:
            cp.start()
        first = []
        for a in range(n):
            first.append(copy(a, 0, me, sibling, src=x_refs[a]))
            first += [copy(a, 1 + j, me, (*chip, c), src=x_refs[a]) for j, chip in enumerate(chips)]
        for cp in first:
            cp.start()
        passed = []
        for j, chip in enumerate(chips):
            for a in range(n):
                copy(a, 1 + j, (*chip, c), me).wait_recv()
                passed.append(copy(a, 4 + j, (*chip, c), sibling))
                passed[-1].start()
        for a in range(n):
            copy(a, 0, sibling, me).wait_recv()
            for j, chip in enumerate(chips):
                copy(a, 4 + j, (*chip, 1 - c), me).wait_recv()
        for cp in first + passed:
            cp.wait_send()
        for cp in mine:
            cp.wait()

    return _pcall(
        body, name=name, out_shape=[jax.ShapeDtypeStruct((N_DEV,) + s.shape, s.dtype) for s in shards],
        in_specs=[_HBM] * n, out_specs=[_HBM] * n,
        scratch_shapes=[pltpu.SemaphoreType.DMA((7 * n,)), pltpu.SemaphoreType.DMA((7 * n,)),
                        pltpu.SemaphoreType.DMA((n,))],
    )(*shards)


def _sibling_exchange(name, xs):
    n = len(xs)

    def body(*refs):
        x_refs, out_refs, send_sems, recv_sems = refs[:n], refs[n:2 * n], refs[2 * n], refs[2 * n + 1]
        c = lax.axis_index("c")
        sibling = (lax.axis_index("x"), lax.axis_index("y"), 1 - c)
        cps = [pltpu.make_async_remote_copy(src_ref=x_refs[a].at[:, 1 - c], dst_ref=out_refs[a],
                                            send_sem=send_sems.at[a], recv_sem=recv_sems.at[a], device_id=sibling,
                                            device_id_type=_MESH)
               for a in range(n)]
        for cp in cps:
            cp.start()
        for cp in cps:
            cp.wait()

    return _pcall(
        body, name=name, out_shape=[jax.ShapeDtypeStruct(x.shape[:1] + x.shape[2:], x.dtype) for x in xs],
        in_specs=[_HBM] * n, out_specs=[_HBM] * n,
        scratch_shapes=[pltpu.SemaphoreType.DMA((n,)), pltpu.SemaphoreType.DMA((n,))],
    )(*xs)


def _chip_exchange(name, xs):
    n = len(xs)

    def body(*refs):
        x_refs, out_refs, send_sems, recv_sems = refs[:n], refs[n:2 * n], refs[2 * n], refs[2 * n + 1]
        mx, my, mc = lax.axis_index("x"), lax.axis_index("y"), lax.axis_index("c")
        chips = [(1 - mx, my), (mx, 1 - my), (1 - mx, 1 - my)]
        cps = [pltpu.make_async_remote_copy(src_ref=x_refs[a].at[2 * px + py], dst_ref=out_refs[a].at[j],
                                            send_sem=send_sems.at[3 * a + j], recv_sem=recv_sems.at[3 * a + j],
                                            device_id=(px, py, mc), device_id_type=_MESH)
               for a in range(n) for j, (px, py) in enumerate(chips)]
        for cp in cps:
            cp.start()
        for cp in cps:
            cp.wait()

    return _pcall(
        body, name=name, out_shape=[jax.ShapeDtypeStruct((3,) + x.shape[1:], x.dtype) for x in xs],
        in_specs=[_HBM] * n, out_specs=[_HBM] * n,
        scratch_shapes=[pltpu.SemaphoreType.DMA((3 * n,)), pltpu.SemaphoreType.DMA((3 * n,))],
    )(*xs)


def _pair_sum(name, a, b):
    nb, R, C = a.shape
    tm = _tile(R, 512, 16)

    def fn(i, ab, bb):
        s = ab.astype(F32) + bb.astype(F32)
        return s, s

    spec = lambda arr: (arr, (1, tm, C), lambda i: (i // (R // tm), i % (R // tm), 0))
    out = lambda dt: ((nb, R, C), dt, (1, tm, C), lambda i: (i // (R // tm), i % (R // tm), 0))
    return _rowwise(name, fn, nb * (R // tm), [spec(a), spec(b)], [out(F32), out(a.dtype)])


def _sum8(name, g):
    n, R, C = g.shape
    tm = _tile(R, 256)

    def fn(i, gb):
        s = gb[0]
        for d in range(1, n):
            s = s + gb[d]
        return (s,)

    return _rowwise(name, fn, R // tm, [(g, (n, tm, C), lambda i: (0, i, 0))], [_orows(R, C, F32, tm)])[0]


def _adamw(name, w, m, v, grads, tm=512):
    R, C = w.shape
    tm = _tile(R, tm, 16)
    ng = len(grads)

    def fn(i, wb, mb, vb, *gs):
        g = gs[0].astype(F32)
        for t in gs[1:]:
            g = g + t.astype(F32)
        mn = ADAM_B1 * mb + (1.0 - ADAM_B1) * g
        vn = ADAM_B2 * vb + (1.0 - ADAM_B2) * (g * g)
        m_hat = mn / (1.0 - ADAM_B1 ** ADAM_STEP)
        v_hat = vn / (1.0 - ADAM_B2 ** ADAM_STEP)
        delta = -ADAM_LR * (m_hat / (jnp.sqrt(v_hat) + ADAM_EPS) + ADAM_WD * wb)
        return g, delta, mn, vn

    ins = [_rows(w, tm), _rows(m, tm), _rows(v, tm)] + [_rows(g, tm) for g in grads]
    return _rowwise(name, fn, R // tm, ins, [_orows(R, C, F32, tm) for _ in range(4)])


CLASSES = [
    (BF16, True, [('ssm_w_in', None), ('attn_w_q', None), ('attn_w_o', None), ('ffn_w_down', 0), ('ffn_w_down', 1)]),
    (BF16, False, [('ssm_w_glu', None), ('kv_w', None)]),
    (BF16, False, [('ffn_w_up', 0), ('ffn_w_up', 1)]),
    (F32, False, [('ffn_conv_w', 0), ('ffn_conv_w', 1)]),
]


def _shard2d(a, name, layer):
    if name == 'ffn_conv_w':
        return a[layer, :, 0, :]
    if name == 'kv_w':
        return a
    return a[0 if layer is None else layer]


def _weights_of(members):
    names = []
    for n, _ in members:
        if not names or names[-1] != n:
            names.append(n)
    return names


def _pack_class(d, members, dtype):
    width = _shard2d(d[members[0][0]], *members[0]).shape[1]
    parts = [d[n].reshape(-1, width).astype(dtype) for n in _weights_of(members)]
    return parts[0] if len(parts) == 1 else jnp.concatenate(parts, axis=0)


def _full_weights(gaths, d):
    w = {}
    for (_, row_sharded, members), g in zip(CLASSES, gaths):
        off = 0
        for n, l in members:
            r = _shard2d(d[n], n, l).shape[0]
            blk = g[:, off:off + r]
            if row_sharded:
                w[(n, l)] = blk.reshape(N_DEV * r, g.shape[2])
            else:
                w[(n, l)] = blk.transpose(1, 0, 2).reshape(r, N_DEV * g.shape[2])
            off += r
    return w


def _grad_classes(gw):
    out = []
    for dtype, row_sharded, members in CLASSES:
        parts = []
        for n, l in members:
            g = gw[(n, l)].astype(dtype)
            if row_sharded:
                parts.append(g.reshape(N_DEV, g.shape[0] // N_DEV, g.shape[1]))
            else:
                parts.append(g.reshape(g.shape[0], N_DEV, g.shape[1] // N_DEV).transpose(1, 0, 2))
        out.append(jnp.concatenate(parts, axis=1))
    return out


def _unpack_class(buf, members, shapes):
    out, off = {}, 0
    for n in _weights_of(members):
        r = math.prod(shapes[n]) // buf.shape[1]
        out[n] = buf[off:off + r].reshape(shapes[n])
        off += r
    return out


def _small_layout(shapes, D):
    lay, off = {}, 0
    for n in SMALL:
        r = -(-math.prod(shapes[n]) // D)
        lay[n] = (off, r)
        off += r
    lay['loss'] = (off, 1)
    off += 1
    return lay, -(-off // 8) * 8


def _pack_small(d, lay, total, D):
    parts = []
    for n in SMALL + ['loss']:
        if n not in d:
            parts.append(jnp.zeros((lay[n][1], D), F32))
            continue
        flat = d[n].reshape(-1).astype(F32)
        parts.append(jnp.pad(flat, (0, lay[n][1] * D - flat.shape[0])).reshape(lay[n][1], D))
    used = sum(lay[n][1] for n in SMALL + ['loss'])
    if total > used:
        parts.append(jnp.zeros((total - used, D), F32))
    return jnp.concatenate(parts, axis=0)


def _unpack_small(pack, lay, shapes, D):
    out = {}
    for n in SMALL:
        off, r = lay[n]
        out[n] = pack[off:off + r].reshape(-1)[:math.prod(shapes[n])].reshape(shapes[n])
    return out


def _ffn_fwd(tag, x, g_norm, w_up, conv_w, conv_b, w_down):
    h = _rms_fwd(f"{tag}_norm", x, g_norm)
    gu = _mm(f"{tag}_up", h, w_up, 'nn', BF16)
    a = _gate_fwd(f"{tag}_gate", gu, conv_w, conv_b)
    return _mm(f"{tag}_down", a, w_down, 'nn', F32, resid=x), (h, gu, a)


def _ffn_bwd(tag, dres, dres_bf, x, saved, g_norm, w_up, conv_w, conv_b, w_down):
    h, gu, a = saved
    da = _mm(f"{tag}_dgate", dres_bf, w_down, 'nt', BF16)
    d_w_down = _mm(f"{tag}_dwdown", a, dres_bf, 'tn', BF16)
    dgu, dw0, dw1, dw2, dcb = _gate_bwd(f"{tag}_gate_b", gu, da, conv_w, conv_b)
    d_w_up = _mm(f"{tag}_dwup", h, dgu, 'tn', BF16)
    dh = _mm(f"{tag}_dh", dgu, w_up, 'nt', F32)
    dres, dres_bf, dg = _rms_bwd(f"{tag}_norm_b", x, dres, [(g_norm, dh)])
    return dres, dres_bf, dg, d_w_up, jnp.concatenate([dw0, dw1, dw2], axis=0), dcb[0], d_w_down


def kernel(x, norm_mix, norm_ffn, norm_kv, norm_final, ssm_w_in, ssm_a_re, ssm_a_im, ssm_log_dt, ssm_b_re, ssm_b_im, ssm_c_re, ssm_c_im, ssm_d, ssm_w_glu, kv_w, attn_w_q, attn_w_o, ffn_w_up, ffn_conv_w, ffn_conv_b, ffn_w_down, loss_target, m_norm_mix, m_norm_ffn, m_norm_kv, m_norm_final, m_ssm_w_in, m_ssm_a_re, m_ssm_a_im, m_ssm_log_dt, m_ssm_b_re, m_ssm_b_im, m_ssm_c_re, m_ssm_c_im, m_ssm_d, m_ssm_w_glu, m_kv_w, m_attn_w_q, m_attn_w_o, m_ffn_w_up, m_ffn_conv_w, m_ffn_conv_b, m_ffn_w_down, v_norm_mix, v_norm_ffn, v_norm_kv, v_norm_final, v_ssm_w_in, v_ssm_a_re, v_ssm_a_im, v_ssm_log_dt, v_ssm_b_re, v_ssm_b_im, v_ssm_c_re, v_ssm_c_im, v_ssm_d, v_ssm_w_glu, v_kv_w, v_attn_w_q, v_attn_w_o, v_ffn_w_up, v_ffn_conv_w, v_ffn_conv_b, v_ffn_w_down):
    wts = dict(zip(W_NAMES, (norm_mix, norm_ffn, norm_kv, norm_final, ssm_w_in, ssm_a_re, ssm_a_im, ssm_log_dt,
                             ssm_b_re, ssm_b_im, ssm_c_re, ssm_c_im, ssm_d, ssm_w_glu, kv_w, attn_w_q, attn_w_o,
                             ffn_w_up, ffn_conv_w, ffn_conv_b, ffn_w_down)))
    mom = dict(zip(W_NAMES, (m_norm_mix, m_norm_ffn, m_norm_kv, m_norm_final, m_ssm_w_in, m_ssm_a_re, m_ssm_a_im,
                             m_ssm_log_dt, m_ssm_b_re, m_ssm_b_im, m_ssm_c_re, m_ssm_c_im, m_ssm_d, m_ssm_w_glu,
                             m_kv_w, m_attn_w_q, m_attn_w_o, m_ffn_w_up, m_ffn_conv_w, m_ffn_conv_b, m_ffn_w_down)))
    vel = dict(zip(W_NAMES, (v_norm_mix, v_norm_ffn, v_norm_kv, v_norm_final, v_ssm_w_in, v_ssm_a_re, v_ssm_a_im,
                             v_ssm_log_dt, v_ssm_b_re, v_ssm_b_im, v_ssm_c_re, v_ssm_c_im, v_ssm_d, v_ssm_w_glu,
                             v_kv_w, v_attn_w_q, v_attn_w_o, v_ffn_w_up, v_ffn_conv_w, v_ffn_conv_b, v_ffn_w_down)))
    shapes = {n: wts[n].shape for n in W_NAMES}
    _, L, D = x.shape
    Fh = ffn_conv_b.shape[1]
    G, P = ssm_a_re.shape[1], ssm_a_re.shape[2]
    Hg = ssm_d.shape[2]
    x0 = x[0]
    target = loss_target[0]
    scale = HEAD_DIM ** -0.5

    gaths = _all_gather("gather_weights", [_pack_class(wts, members, dt) for dt, _, members in CLASSES])
    w = _full_weights(gaths, wts)
    nm = [norm_mix[l:l + 1] for l in range(2)]
    nf = [norm_ffn[l:l + 1] for l in range(2)]
    nkv = norm_kv[None]
    nfin = norm_final[None]
    cb = [ffn_conv_b[l:l + 1] for l in range(2)]
    cwt = [w[('ffn_conv_w', l)] for l in range(2)]

    s5p = (ssm_a_re[0], ssm_a_im[0], ssm_log_dt[0], ssm_b_re[0], ssm_b_im[0], ssm_c_re[0], ssm_c_im[0], ssm_d[0])
    (m_mat, n_mat, o_mat, lam1, lam2), s5_vjp = jax.vjp(_s5_build, *s5p)
    m_bf, n_bf, o_bf = m_mat.astype(BF16), n_mat.astype(BF16), o_mat.astype(BF16)

    h0 = _rms_fwd("l0_norm", x0, nm[0])
    u = _mm("l0_win", h0, w[('ssm_w_in', None)], 'nn', BF16)
    ug = _to_groups("s5_group_u", u, G, Hg, BF16)
    vloc = _bmm("s5_local_state", [(ug, n_bf, 'nt')], F32)
    st = _chunk_scan("s5_scan", vloc, lam1, lam2, reverse=False)
    yraw, yg = _bmm("s5_out", [(ug, m_bf, 'nt'), (st, o_bf, 'nt')], (F32, BF16),
                    post=lambda acc: (acc, _gelu(acc)), ungroup=(1, Hg))
    z = _mm("l0_wglu", yg, w[('ssm_w_glu', None)], 'nn', F32)
    x1 = _glu_fwd("l0_glu", x0, z)
    x2, ffn0 = _ffn_fwd("f0", x1, nf[0], w[('ffn_w_up', 0)], cwt[0], cb[0], w[('ffn_w_down', 0)])

    hkv = _rms_fwd("kv_norm", x2, nkv)
    nh = D // HEAD_DIM
    w_kv = w[('kv_w', None)].reshape(D, 2, nh, HEAD_DIM).transpose(0, 2, 1, 3).reshape(D, 2 * D)
    kvp = _mm("kv_proj", hkv, w_kv, 'nn', BF16)
    h2 = _rms_fwd("l1_norm", x2, nm[1])
    qn = _mm("l1_wq", h2, w[('attn_w_q', None)], 'nn', BF16, scale=scale)
    o, rc = _attn_fwd("attn_fwd", qn, kvp, ATT_TQ, ATT_TK, ATT_HB_FWD)
    x3 = _mm("l1_wo", o, w[('attn_w_o', None)], 'nn', F32, resid=x2)
    x4, ffn1 = _ffn_fwd("f1", x3, nf[1], w[('ffn_w_up', 1)], cwt[1], cb[1], w[('ffn_w_down', 1)])

    dres, dres_bf, dg_final, loss_part = _final_loss("loss_head", x4, nfin, target)

    gw = {}
    dres, dres_bf, dg_nf1, gup1, gcw1, gcb1, gdn1 = _ffn_bwd("f1", dres, dres_bf, x3, ffn1, nf[1], w[('ffn_w_up', 1)],
                                                             cwt[1], cb[1], w[('ffn_w_down', 1)])
    do = _mm("l1_do", dres_bf, w[('attn_w_o', None)], 'nt', BF16)
    gw[('attn_w_o', None)] = _mm("l1_dwo", o, dres_bf, 'tn', BF16)
    dq, dkv = _attn_bwd("attn_bwd", qn, kvp, do, rc, ATT_TQ, ATT_TK, scale, ATT_HB_BWD)
    gw[('attn_w_q', None)] = _mm("l1_dwq", h2, dq, 'tn', BF16)
    dh2 = _mm("l1_dh", dq, w[('attn_w_q', None)], 'nt', F32)
    g_kv = _mm("kv_dw", hkv, dkv, 'tn', BF16)
    gw[('kv_w', None)] = g_kv.reshape(D, nh, 2, HEAD_DIM).transpose(0, 2, 1, 3).reshape(D, 2 * D)
    dhkv = _mm("kv_dh", dkv, w_kv, 'nt', F32)
    dres, dres_bf, dg_nm1, dg_nkv = _rms_bwd("l1_norm_b", x2, dres, [(nm[1], dh2), (nkv, dhkv)])
    dres, dres_bf, dg_nf0, gup0, gcw0, gcb0, gdn0 = _ffn_bwd("f0", dres, dres_bf, x1, ffn0, nf[0], w[('ffn_w_up', 0)],
                                                             cwt[0], cb[0], w[('ffn_w_down', 0)])
    dz = _glu_bwd("l0_glu_b", dres, z)
    gw[('ssm_w_glu', None)] = _mm("l0_dwglu", yg, dz, 'tn', BF16)
    dyg = _mm("l0_dyg", dz, w[('ssm_w_glu', None)], 'nt', F32)
    dy = _to_groups("s5_group_dy", dyg, G, Hg, BF16, gelu_arg=yraw)
    ds = _bmm("s5_dstate", [(dy, o_bf, 'nn')], F32)
    dv_loc, dlam1, dlam2 = _chunk_scan("s5_scan_b", ds, lam1, lam2, reverse=True, s_fwd=st)
    du = _bmm("s5_du", [(dy, m_bf, 'nn'), (dv_loc, n_bf, 'nn')], BF16, ungroup=(0, Hg))
    d_m = _bmm("s5_dm", [(dy, ug, 'tn')], F32)
    d_o = _bmm("s5_do", [(dy, st, 'tn')], F32)
    d_n = _bmm("s5_dn", [(dv_loc, ug, 'tn')], F32)
    s5g = s5_vjp((d_m, d_n, d_o, dlam1, dlam2))
    gw[('ssm_w_in', None)] = _mm("l0_dwin", h0, du, 'tn', BF16)
    dh0 = _mm("l0_dh", du, w[('ssm_w_in', None)], 'nt', F32)
    grad_x, _, dg_nm0 = _rms_bwd("l0_norm_b", x0, dres, [(nm[0], dh0)])

    gw.update({('ffn_w_up', 0): gup0, ('ffn_w_up', 1): gup1, ('ffn_w_down', 0): gdn0, ('ffn_w_down', 1): gdn1,
               ('ffn_conv_w', 0): gcw0, ('ffn_conv_w', 1): gcw1})

    small_g = {
        'norm_mix': jnp.concatenate([dg_nm0, dg_nm1], axis=0), 'norm_ffn': jnp.concatenate([dg_nf0, dg_nf1], axis=0),
        'norm_kv': dg_nkv, 'norm_final': dg_final, 'ffn_conv_b': jnp.stack([gcb0, gcb1]),
        'ssm_a_re': s5g[0], 'ssm_a_im': s5g[1], 'ssm_log_dt': s5g[2], 'ssm_b_re': s5g[3], 'ssm_b_im': s5g[4],
        'ssm_c_re': s5g[5], 'ssm_c_im': s5g[6], 'ssm_d': s5g[7], 'loss': loss_part[0, 0:1],
    }
    lay, rs = _small_layout(shapes, D)
    small_sum = _sum8("small_sum", _all_gather("gather_small", [_pack_small(small_g, lay, rs, D)])[0])
    loss = small_sum[lay['loss'][0], 0]
    sg, sdelta, sm, sv = _adamw("adamw_small", _pack_small(wts, lay, rs, D), _pack_small(mom, lay, rs, D),
                                _pack_small(vel, lay, rs, D), [small_sum])
    small_out = [_unpack_small(t, lay, shapes, D) for t in (sg, sdelta, sm, sv)]

    c = lax.axis_index("c")
    chip = 2 * lax.axis_index("x") + lax.axis_index("y")
    gcls = [g.reshape((4, 2) + g.shape[1:]) for g in _grad_classes(gw)]
    mine = [lax.dynamic_index_in_dim(g, c, axis=1, keepdims=False) for g in gcls]
    got = _sibling_exchange("rs_sibling", gcls)
    sums = [_pair_sum(f"rs_pair_sum{k}", a, b) for k, (a, b) in enumerate(zip(mine, got))]
    from_chips = _chip_exchange("rs_chips", [on_wire for _, on_wire in sums])
    big_parts = [{}, {}, {}, {}]
    for k, ((_, _, members), (chip_f32, _), fc) in enumerate(zip(CLASSES, sums, from_chips)):
        own = lax.dynamic_index_in_dim(chip_f32, chip, axis=0, keepdims=False)
        res = _adamw(f"adamw_big{k}", _pack_class(wts, members, F32), _pack_class(mom, members, F32),
                     _pack_class(vel, members, F32), [own, fc[0], fc[1], fc[2]])
        for q in range(4):
            big_parts[q].update(_unpack_class(res[q], members, shapes))
    big_out = big_parts

    outs = [loss, grad_x[None]]
    for k in range(4):
        for n in W_NAMES:
            outs.append(big_out[k][n] if n in BIG else small_out[k][n])
    return tuple(outs)
```

```python
import math

import jax
import jax.numpy as jnp
from jax import lax
from jax.experimental import pallas as pl
from jax.experimental.pallas import tpu as pltpu

F32 = jnp.float32
BF16 = jnp.bfloat16

EPS = 1e-6
HEAD_DIM = 64
CHUNK = 16
N_DEV = 8
ADAM_LR = 0.001
ADAM_B1 = 0.9
ADAM_B2 = 0.999
ADAM_EPS = 1e-08
ADAM_WD = 0.01
ADAM_STEP = 10
VMEM_LIMIT = 48 * 1024 * 1024
MM_BLOCK_BYTES = 28 * 1024 * 1024
MM_TM_MAX = 1024
MM_T_MAX = 512
MM_TN_MAX = 1536
LANES = 128
ATT_TQ = 128
ATT_TK = 256
ATT_HB_FWD = 8
ATT_HB_BWD = 8

W_NAMES = ['norm_mix', 'norm_ffn', 'norm_kv', 'norm_final', 'ssm_w_in', 'ssm_a_re', 'ssm_a_im', 'ssm_log_dt',
           'ssm_b_re', 'ssm_b_im', 'ssm_c_re', 'ssm_c_im', 'ssm_d', 'ssm_w_glu', 'kv_w', 'attn_w_q', 'attn_w_o',
           'ffn_w_up', 'ffn_conv_w', 'ffn_conv_b', 'ffn_w_down']
BIG = ['ssm_w_in', 'ssm_w_glu', 'kv_w', 'attn_w_q', 'attn_w_o', 'ffn_w_up', 'ffn_w_down', 'ffn_conv_w']
SMALL = [n for n in W_NAMES if n not in BIG]


def _pcall(body, **kw):
    return pl.pallas_call(body, **kw)


def _params(sem=None):
    if sem is None:
        return pltpu.CompilerParams(vmem_limit_bytes=VMEM_LIMIT)
    return pltpu.CompilerParams(dimension_semantics=sem, vmem_limit_bytes=VMEM_LIMIT)


def _tile(n, pref, mult=8):
    best = None
    for t in range(mult, min(n, pref) + 1, mult):
        if n % t == 0:
            best = t
    return n if best is None else best


def _mm_tiles(M, N, K, mode, a_bytes, b_bytes, o_bytes):
    def cands(n, cap):
        c = [t for t in range(LANES, min(n, cap) + 1, LANES) if n % t == 0]
        return c or [n]
    best = None
    for tm in cands(M, MM_T_MAX if mode == 'tn' else MM_TM_MAX):
        for tn in cands(N, MM_TN_MAX):
            need = 2 * (tm * K * a_bytes + K * tn * b_bytes + tm * tn * o_bytes)
            score = tm * tn * (tm if mode == 'tn' else 1)
            if need <= MM_BLOCK_BYTES and (best is None or score > best[0]):
                best = (score, tm, tn)
    assert best is not None, (M, N, K)
    return best[1], best[2]


def _mm(name, a, b, mode, out_dtype=F32, scale=None, resid=None):
    if mode == 'nn':
        (M, K), (K2, N) = a.shape, b.shape
    elif mode == 'nt':
        (M, K), (N, K2) = a.shape, b.shape
    else:
        (K, M), (K2, N) = a.shape, b.shape
    assert K == K2, (name, a.shape, b.shape)
    o_bytes = jnp.dtype(out_dtype).itemsize + (resid.dtype.itemsize if resid is not None else 0)
    tm, tn = _mm_tiles(M, N, K, mode, a.dtype.itemsize, b.dtype.itemsize, o_bytes)
    if mode == 'tn':
        a_spec = pl.BlockSpec((K, tm), lambda i, j: (0, i))
    else:
        a_spec = pl.BlockSpec((tm, K), lambda i, j: (i, 0))
    if mode == 'nt':
        b_spec = pl.BlockSpec((tn, K), lambda i, j: (j, 0))
    else:
        b_spec = pl.BlockSpec((K, tn), lambda i, j: (0, j))
    o_spec = pl.BlockSpec((tm, tn), lambda i, j: (i, j))
    dn = {'nn': ((1,), (0,)), 'nt': ((1,), (1,)), 'tn': ((0,), (0,))}[mode]

    def body(*refs):
        a_ref, b_ref, o_ref = refs[0], refs[1], refs[-1]
        acc = lax.dot_general(a_ref[...].astype(BF16), b_ref[...].astype(BF16), (dn, ((), ())),
                              preferred_element_type=F32)
        if scale is not None:
            acc = acc * scale
        if resid is not None:
            acc = acc + refs[2][...]
        o_ref[...] = acc.astype(o_ref.dtype)

    return _pcall(
        body, name=name, grid=(M // tm, N // tn),
        in_specs=[a_spec, b_spec] + ([o_spec] if resid is not None else []),
        out_specs=o_spec,
        out_shape=jax.ShapeDtypeStruct((M, N), out_dtype),
        compiler_params=_params(("parallel", "parallel")),
    )(*([a, b] + ([resid] if resid is not None else [])))


def _bmm(name, terms, out_dtype, gb=8, post=None, ungroup=None):
    G = terms[0][0].shape[0]
    gb = _tile(G, gb, 1)
    dns = {'nn': ((1,), (0,)), 'nt': ((1,), (1,)), 'tn': ((0,), (0,))}

    def oshape(a, b, mode):
        m = a.shape[2] if mode == 'tn' else a.shape[1]
        n = b.shape[1] if mode == 'nt' else b.shape[2]
        return m, n

    m, n = oshape(*terms[0])
    out_dtypes = out_dtype if isinstance(out_dtype, (tuple, list)) else (out_dtype,)
    n_in = 2 * len(terms)
    if ungroup is not None:
        perm, gpb = _lane_perm(ungroup[1])
        assert gpb == gb and n == CHUNK * ungroup[1]

    def body(*refs):
        ins, outs = refs[:n_in], refs[len(refs) - len(out_dtypes):]
        kept = []
        for gi in range(gb):
            acc = None
            for t, (_, _, mode) in enumerate(terms):
                part = lax.dot_general(ins[2 * t][gi].astype(BF16), ins[2 * t + 1][gi].astype(BF16),
                                       (dns[mode], ((), ())), preferred_element_type=F32)
                acc = part if acc is None else acc + part
            vals = (acc,) if post is None else post(acc)
            for k, (o_ref, v) in enumerate(zip(outs, vals)):
                if ungroup is not None and k == ungroup[0]:
                    kept.append(v.astype(BF16))
                else:
                    o_ref[gi] = v.astype(o_ref.dtype)
        if ungroup is not None:
            x = lax.dot_general(jnp.concatenate(kept, axis=1), refs[n_in][...], (((1,), (1,)), ((), ())),
                                preferred_element_type=F32)
            o_ref = outs[ungroup[0]]
            for t in range(CHUNK):
                o_ref[:, t, :] = x[:, t * LANES:(t + 1) * LANES].astype(o_ref.dtype)

    in_specs, args = [], []
    for a, b, _ in terms:
        in_specs += [pl.BlockSpec((gb,) + a.shape[1:], lambda g: (g, 0, 0)),
                     pl.BlockSpec((gb,) + b.shape[1:], lambda g: (g, 0, 0))]
        args += [a, b]
    out_specs = [pl.BlockSpec((gb, m, n), lambda g: (g, 0, 0)) for _ in out_dtypes]
    out_shape = [jax.ShapeDtypeStruct((G, m, n), dt) for dt in out_dtypes]
    if ungroup is not None:
        in_specs.append(pl.BlockSpec(perm.shape, lambda g: (0, 0)))
        args.append(perm)
        out_specs[ungroup[0]] = pl.BlockSpec((m, CHUNK, LANES), lambda g: (0, 0, g))
        out_shape[ungroup[0]] = jax.ShapeDtypeStruct((m, CHUNK, G * ungroup[1]), out_dtypes[ungroup[0]])
    res = _pcall(
        body, name=name, grid=(G // gb,), in_specs=in_specs, out_specs=out_specs, out_shape=out_shape,
        compiler_params=_params(("parallel",)),
    )(*args)
    if ungroup is not None:
        res = list(res)
        res[ungroup[0]] = res[ungroup[0]].reshape(m * CHUNK, G * ungroup[1])
    return res[0] if len(out_dtypes) == 1 else res


def _rowwise(name, fn, n_steps, ins, outs, n_acc=0):
    n_in, n_out = len(ins), len(outs)

    def body(*refs):
        i = pl.program_id(0)
        vals = fn(i, *[r[...] for r in refs[:n_in]])
        o_refs = refs[n_in:]
        for j in range(n_out - n_acc):
            o_refs[j][...] = vals[j].astype(o_refs[j].dtype)
        if n_acc:
            @pl.when(i == 0)
            def _():
                for j in range(n_out - n_acc, n_out):
                    o_refs[j][...] = vals[j].astype(o_refs[j].dtype)

            @pl.when(i > 0)
            def _():
                for j in range(n_out - n_acc, n_out):
                    o_refs[j][...] += vals[j].astype(o_refs[j].dtype)

    res = _pcall(
        body, name=name, grid=(n_steps,),
        in_specs=[pl.BlockSpec(blk, im) for _, blk, im in ins],
        out_specs=[pl.BlockSpec(blk, im) for _, _, blk, im in outs],
        out_shape=[jax.ShapeDtypeStruct(s, d) for s, d, _, _ in outs],
        compiler_params=_params(("arbitrary",)),
    )(*[a for a, _, _ in ins])
    return res


def _rows(a, tm):
    return (a, (tm, a.shape[1]), lambda i: (i, 0))


def _whole(a):
    nd = a.ndim
    return (a, a.shape, lambda i: (0,) * nd)


def _orows(L, n, dtype, tm):
    return ((L, n), dtype, (tm, n), lambda i: (i, 0))


def _oacc(r, n):
    return ((r, n), F32, (r, n), lambda i: (0, 0))


def _rms_fwd(name, x, g, tm=512):
    L, D = x.shape
    tm = _tile(L, tm)

    def fn(i, xb, gb):
        r = lax.rsqrt(jnp.mean(xb * xb, axis=-1, keepdims=True) + EPS)
        return (xb * r * gb,)

    return _rowwise(name, fn, L // tm, [_rows(x, tm), _whole(g)], [_orows(L, D, BF16, tm)])[0]


def _rms_bwd(name, x, dres, branches, tm=256):
    L, D = x.shape
    tm = _tile(L, tm)
    nb = len(branches)

    def fn(i, xb, db, *rest):
        r = lax.rsqrt(jnp.mean(xb * xb, axis=-1, keepdims=True) + EPS)
        xh = xb * r
        dx = db
        dgs = []
        for b in range(nb):
            gb, dyb = rest[2 * b], rest[2 * b + 1].astype(F32)
            dxh = dyb * gb
            dx = dx + r * (dxh - xh * jnp.mean(dxh * xh, axis=-1, keepdims=True))
            dgs.append(jnp.sum(dyb * xh, axis=0, keepdims=True))
        return (dx, dx, *dgs)

    ins = [_rows(x, tm), _rows(dres, tm)]
    for g, dy in branches:
        ins += [_whole(g), _rows(dy, tm)]
    outs = [_orows(L, D, F32, tm), _orows(L, D, BF16, tm)] + [_oacc(1, D) for _ in range(nb)]
    return _rowwise(name, fn, L // tm, ins, outs, n_acc=nb)


def _final_loss(name, x, g, target, tm=256):
    L, D = x.shape
    tm = _tile(L, tm)

    def fn(i, xb, gb, tb):
        r = lax.rsqrt(jnp.mean(xb * xb, axis=-1, keepdims=True) + EPS)
        xh = xb * r
        err = xh * gb - tb
        dy = err * (1.0 / D)
        dxh = dy * gb
        dx = r * (dxh - xh * jnp.mean(dxh * xh, axis=-1, keepdims=True))
        dg = jnp.sum(dy * xh, axis=0, keepdims=True)
        per_row = jnp.mean(err * err, axis=-1, keepdims=True)
        loss = 0.5 * jnp.sum(per_row, axis=0, keepdims=True)
        return dx, dx, dg, jnp.broadcast_to(loss, (1, LANES))

    return _rowwise(name, fn, L // tm, [_rows(x, tm), _whole(g), _rows(target, tm)],
                    [_orows(L, D, F32, tm), _orows(L, D, BF16, tm), _oacc(1, D), _oacc(1, LANES)], n_acc=2)


def _glu_fwd(name, x, z, tm=256):
    L, D = x.shape
    tm = _tile(L, tm)

    def fn(i, xb, zb):
        return (xb + zb[:, :D] * jax.nn.sigmoid(zb[:, D:]),)

    return _rowwise(name, fn, L // tm, [_rows(x, tm), _rows(z, tm)], [_orows(L, D, F32, tm)])[0]


def _glu_bwd(name, dx, z, tm=256):
    L, D = dx.shape
    tm = _tile(L, tm)

    def fn(i, db, zb):
        z1, sg = zb[:, :D], jax.nn.sigmoid(zb[:, D:])
        return (jnp.concatenate([(db * sg).astype(BF16), (db * z1 * sg * (1.0 - sg)).astype(BF16)], axis=1),)

    return _rowwise(name, fn, L // tm, [_rows(dx, tm), _rows(z, tm)], [_orows(L, 2 * D, BF16, tm)])[0]


HALO = 16


def _halo_prev(a, tm):
    return (a, (HALO, a.shape[1]), lambda i: (jnp.maximum(i * (tm // HALO) - 1, 0), 0))


def _halo_next(a, tm):
    last = a.shape[0] // HALO - 1
    return (a, (HALO, a.shape[1]), lambda i: (jnp.minimum((i + 1) * (tm // HALO), last), 0))


def _shift_down(cur, halo, k, first):
    tm = cur.shape[0]
    rolled = pltpu.roll(cur, k, 0)
    tail = pltpu.roll(halo, k, 0)
    tail = jnp.where(first, 0.0, tail)
    row = lax.broadcasted_iota(jnp.int32, (tm, 1), 0)
    head = jnp.concatenate([tail, jnp.zeros((tm - 8, cur.shape[1]), cur.dtype)], axis=0) if tm > 8 else tail
    return jnp.where(row < k, head, rolled)


def _shift_up(cur, halo, k, last):
    tm = cur.shape[0]
    rolled = pltpu.roll(cur, tm - k, 0)
    head = pltpu.roll(halo, 8 - k, 0)
    head = jnp.where(last, 0.0, head)
    row = lax.broadcasted_iota(jnp.int32, (tm, 1), 0)
    tail = jnp.concatenate([jnp.zeros((tm - 8, cur.shape[1]), cur.dtype), head], axis=0) if tm > 8 else head
    return jnp.where(row >= tm - k, tail, rolled)


def _conv_pre(gb, hb, cw, cb, first):
    g1 = _shift_down(gb, hb, 1, first)
    g2 = _shift_down(gb, hb, 2, first)
    return cw[0:1] * g2 + cw[1:2] * g1 + cw[2:3] * gb + cb, g1, g2


def _gate_fwd(name, gu, cw, cb, tm=256):
    L, F2 = gu.shape
    Fh = F2 // 2
    tm = _tile(L, tm)

    def fn(i, gub, halo, cwb, cbb):
        gub, halo = gub.astype(F32), halo.astype(F32)[HALO - 8:]
        gc, _, _ = _conv_pre(gub[:, :Fh], halo[:, :Fh], cwb, cbb, i == 0)
        return (gc * jax.nn.sigmoid(gc) * gub[:, Fh:],)

    return _rowwise(name, fn, L // tm, [_rows(gu, tm), _halo_prev(gu, tm), _whole(cw), _whole(cb)],
                    [_orows(L, Fh, BF16, tm)])[0]


def _gate_bwd(name, gu, da, cw, cb, tm=128):
    L, F2 = gu.shape
    Fh = F2 // 2
    tm = _tile(L, tm)
    n = L // tm

    def dgate(gb, ub, dab, hb, cwb, cbb, first):
        gc, g1, g2 = _conv_pre(gb, hb, cwb, cbb, first)
        sg = jax.nn.sigmoid(gc)
        return dab * ub * sg * (1.0 + gc * (1.0 - sg)), dab * gc * sg, g1, g2

    def fn(i, gub, prev, nxt, dab, dnx, cwb, cbb):
        gub, prev, nxt = gub.astype(F32), prev.astype(F32)[HALO - 8:], nxt.astype(F32)[:8]
        dab, dnx = dab.astype(F32), dnx.astype(F32)[:8]
        gb, ub = gub[:, :Fh], gub[:, Fh:]
        dgc, du, g1, g2 = dgate(gb, ub, dab, prev[:, :Fh], cwb, cbb, i == 0)
        dgc_next, _, _, _ = dgate(nxt[:, :Fh], nxt[:, Fh:], dnx, gb[tm - 8:], cwb, cbb, False)
        d1 = _shift_up(dgc, dgc_next, 1, i == n - 1)
        d2 = _shift_up(dgc, dgc_next, 2, i == n - 1)
        dg = cwb[2:3] * dgc + cwb[1:2] * d1 + cwb[0:1] * d2
        return (jnp.concatenate([dg.astype(BF16), du.astype(BF16)], axis=1),
                jnp.sum(dgc * g2, axis=0, keepdims=True), jnp.sum(dgc * g1, axis=0, keepdims=True),
                jnp.sum(dgc * gb, axis=0, keepdims=True), jnp.sum(dgc, axis=0, keepdims=True))

    return _rowwise(name, fn, n,
                    [_rows(gu, tm), _halo_prev(gu, tm), _halo_next(gu, tm), _rows(da, tm), _halo_next(da, tm),
                     _whole(cw), _whole(cb)],
                    [_orows(L, 2 * Fh, BF16, tm)] + [_oacc(1, Fh) for _ in range(4)], n_acc=4)


def _s5_build(a_re, a_im, log_dt, b_re, b_im, c_re, c_im, d):
    T = CHUNK
    G, P = a_re.shape
    H = d.shape[1]
    hi = lax.Precision.HIGH
    dt = jnp.exp(log_dt)[:, None]
    mag = jnp.exp(a_re * dt)
    ab_re = mag * jnp.cos(a_im * dt)
    ab_im = mag * jnp.sin(a_im * dt)
    den = a_re * a_re + a_im * a_im
    f_re = ((ab_re - 1.0) * a_re + ab_im * a_im) / den
    f_im = (ab_im * a_re - (ab_re - 1.0) * a_im) / den
    bb_re = f_re[..., None] * b_re - f_im[..., None] * b_im
    bb_im = f_re[..., None] * b_im + f_im[..., None] * b_re
    tau = jnp.arange(T + 1, dtype=F32)[None, :, None]
    pmag = jnp.exp(tau * (a_re * dt)[:, None, :])
    pang = tau * (a_im * dt)[:, None, :]
    pw_re = pmag * jnp.cos(pang)
    pw_im = pmag * jnp.sin(pang)
    cp_re = c_re[:, :, None, :] * pw_re[:, None] - c_im[:, :, None, :] * pw_im[:, None]
    cp_im = c_re[:, :, None, :] * pw_im[:, None] + c_im[:, :, None, :] * pw_re[:, None]
    cp_cat = jnp.concatenate([cp_re, -cp_im], axis=-1)
    bb_cat = jnp.concatenate([bb_re, bb_im], axis=1)
    kt = jnp.einsum('ghtq,gqk->ghtk', cp_cat[:, :, :T], bb_cat, precision=hi)
    kt = kt.at[:, :, 0, :].add(d[:, :, None] * jnp.eye(H, dtype=F32)[None])
    kp = jnp.concatenate([kt[:, :, ::-1, :], jnp.zeros((G, H, T - 1, H), F32)], axis=2).reshape(G, H, (2 * T - 1) * H)
    m_mat = jnp.stack([kp[:, :, (T - 1 - t) * H:(2 * T - 1 - t) * H] for t in range(T)], axis=1)
    m_mat = m_mat.reshape(G, T * H, T * H)
    pr = jnp.repeat(pw_re[:, :T][:, ::-1].transpose(0, 2, 1), H, axis=2)
    pi = jnp.repeat(pw_im[:, :T][:, ::-1].transpose(0, 2, 1), H, axis=2)
    br, bi = jnp.tile(bb_re, (1, 1, T)), jnp.tile(bb_im, (1, 1, T))
    n_mat = jnp.concatenate([pr * br - pi * bi, pr * bi + pi * br], axis=1)
    o_mat = cp_cat[:, :, 1:].transpose(0, 2, 1, 3).reshape(G, T * H, 2 * P)
    lam1 = jnp.concatenate([pw_re[:, T], pw_re[:, T]], axis=-1)[:, None, :]
    lam2 = jnp.concatenate([-pw_im[:, T], pw_im[:, T]], axis=-1)[:, None, :]
    return m_mat, n_mat, o_mat, lam1, lam2


def _chunk_scan(name, v, lam1, lam2, reverse, s_fwd=None, gb=32):
    G, nc, W = v.shape
    gb = _tile(G, gb, 1)
    half = W // 2
    ntile = nc // 8

    def body(*refs):
        if reverse:
            v_ref, l1_ref, l2_ref, s_ref, o_ref, d1_ref, d2_ref = refs
        else:
            v_ref, l1_ref, l2_ref, o_ref = refs
        l1 = jnp.broadcast_to(l1_ref[...], (gb, 8, W))
        l2 = jnp.broadcast_to(l2_ref[...], (gb, 8, W))
        if reverse:
            l2 = -l2
        row = lax.broadcasted_iota(jnp.int32, (gb, 8, W), 1)

        def tile_step(n, carry):
            if reverse:
                st, a1, a2 = carry
                base = pl.multiple_of((ntile - 1 - n) * 8, 8)
            else:
                st = carry
                base = pl.multiple_of(n * 8, 8)
            vt = v_ref[:, pl.ds(base, 8), :]
            out = jnp.zeros((gb, 8, W), F32)
            order = range(7, -1, -1) if reverse else range(8)
            for r in order:
                out = jnp.where(row == r, st, out)
                vr = jnp.broadcast_to(vt[:, r:r + 1, :], (gb, 8, W))
                st = l1 * st + l2 * pltpu.roll(st, half, 2) + vr
            o_ref[:, pl.ds(base, 8), :] = out
            if reverse:
                sv = s_ref[:, pl.ds(base, 8), :]
                a1 = a1 + out * sv
                a2 = a2 + out * pltpu.roll(sv, half, 2)
                return st, a1, a2
            return st

        zero = jnp.zeros((gb, 8, W), F32)
        if reverse:
            _, a1, a2 = lax.fori_loop(0, ntile, tile_step, (zero, zero, zero))
            d1_ref[...] = jnp.sum(a1, axis=1, keepdims=True)
            d2_ref[...] = jnp.sum(a2, axis=1, keepdims=True)
        else:
            lax.fori_loop(0, ntile, tile_step, zero)

    big = pl.BlockSpec((gb, nc, W), lambda g: (g, 0, 0))
    vec = pl.BlockSpec((gb, 1, W), lambda g: (g, 0, 0))
    if reverse:
        return _pcall(body, name=name, grid=(G // gb,), in_specs=[big, vec, vec, big],
                      out_specs=[big, vec, vec],
                      out_shape=[jax.ShapeDtypeStruct((G, nc, W), F32), jax.ShapeDtypeStruct((G, 1, W), F32),
                                 jax.ShapeDtypeStruct((G, 1, W), F32)],
                      compiler_params=_params(("parallel",)))(v, lam1, lam2, s_fwd)
    return _pcall(body, name=name, grid=(G // gb,), in_specs=[big, vec, vec], out_specs=big,
                  out_shape=jax.ShapeDtypeStruct((G, nc, W), F32),
                  compiler_params=_params(("parallel",)))(v, lam1, lam2)


_GELU_C = math.sqrt(2.0 / math.pi)


def _gelu(y):
    return 0.5 * y * (1.0 + jnp.tanh(_GELU_C * (y + 0.044715 * y * y * y)))


def _gelu_grad(y):
    t = jnp.tanh(_GELU_C * (y + 0.044715 * y * y * y))
    return 0.5 * (1.0 + t) + 0.5 * y * (1.0 - t * t) * _GELU_C * (1.0 + 3.0 * 0.044715 * y * y)


def _lane_perm(Hg):
    gpb = LANES // Hg
    r = jnp.arange(CHUNK * LANES)
    t, gl, h = r // LANES, (r % LANES) // Hg, r % Hg
    target = gl * (CHUNK * Hg) + t * Hg + h
    return (target[:, None] == r[None, :]).astype(BF16), gpb


def _to_groups(name, a, G, Hg, out_dtype, gelu_arg=None):
    L, D = a.shape
    nc = L // CHUNK
    perm, gpb = _lane_perm(Hg)
    gw = CHUNK * Hg

    def body(*refs):
        x_ref, p_ref, o_ref = refs[0], refs[1], refs[-1]
        x = jnp.concatenate([x_ref[:, t, :] for t in range(CHUNK)], axis=1)
        if x.dtype == BF16:
            z = jnp.dot(x, p_ref[...], preferred_element_type=F32)
        else:
            z = _split_dot(x, p_ref[...])
        for gl in range(gpb):
            piece = z[:, gl * gw:(gl + 1) * gw]
            if gelu_arg is not None:
                piece = piece * _gelu_grad(refs[2][gl])
            o_ref[gl] = piece.astype(o_ref.dtype)

    grouped = pl.BlockSpec((gpb, nc, gw), lambda g: (g, 0, 0))
    return _pcall(
        body, name=name, grid=(G // gpb,),
        in_specs=[pl.BlockSpec((nc, CHUNK, LANES), lambda g: (0, 0, g)),
                  pl.BlockSpec(perm.shape, lambda g: (0, 0))] + ([grouped] if gelu_arg is not None else []),
        out_specs=grouped,
        out_shape=jax.ShapeDtypeStruct((G, nc, gw), out_dtype),
        compiler_params=_params(("parallel",)),
    )(*([a.reshape(nc, CHUNK, D), perm] + ([gelu_arg] if gelu_arg is not None else [])))


def _split_dot(x, u):
    hi = x.astype(BF16)
    lo = (x - hi.astype(F32)).astype(BF16)
    return (jnp.dot(hi, u, preferred_element_type=F32) + jnp.dot(lo, u, preferred_element_type=F32))


def _col_to_row(col, n):
    return jnp.broadcast_to(col, (n, LANES)).T[0:1, :]


def _row_to_col(row, n):
    return jnp.broadcast_to(row, (LANES, n)).T[:, 0:1]


def _head_in(pair, h, upper):
    x = pair.astype(F32)
    lane = lax.broadcasted_iota(jnp.int32, x.shape, 1)
    if (h % 2 == 1) != upper:
        x = pltpu.roll(x, HEAD_DIM, 1)
    keep = (lane >= HEAD_DIM) if upper else (lane < HEAD_DIM)
    return jnp.where(keep, x, 0.0).astype(BF16)


def _pair_out(even, odd, upper):
    lane = lax.broadcasted_iota(jnp.int32, even.shape, 1)
    if upper:
        return jnp.where(lane < HEAD_DIM, pltpu.roll(even, HEAD_DIM, 1), odd)
    return jnp.where(lane < HEAD_DIM, even, pltpu.roll(odd, HEAD_DIM, 1))


def _attn_fwd(name, q, kv, tq, tk, hb):
    L = q.shape[0]
    H = q.shape[1] // HEAD_DIM
    hb = min(hb, H)
    dw = 2 * HEAD_DIM
    nq, nkb = L // tq, L // tk
    hs = range(hb)
    assert tk % tq == 0 and hb % 2 == 0 and H % hb == 0

    def body(q_ref, kv_ref, o_ref, rc_ref):
        i = pl.program_id(1)
        diff = (lax.broadcasted_iota(jnp.int32, (tq, tk), 1) - lax.broadcasted_iota(jnp.int32, (tq, tk), 0))
        u_suf = (lax.broadcasted_iota(jnp.int32, (tk, tk), 0)
                 > lax.broadcasted_iota(jnp.int32, (tk, tk), 1)).astype(BF16)
        nfull = (i * tq) // tk
        rc_ref[...] = jnp.zeros_like(rc_ref)
        qw = [_head_in(q_ref[:, dw * (h // 2):dw * (h // 2 + 1)], h, False) for h in hs]

        def block(kb, carry, masked):
            runs, accs = carry
            ks = pl.multiple_of(kb * tk, tk)
            kvb = [kv_ref[pl.ds(ks, tk), dw * h:dw * (h + 1)] for h in hs]
            z = [lax.dot_general(qw[h], kvb[h], (((1,), (1,)), ((), ())), preferred_element_type=F32)
                 for h in hs]
            e = [jnp.exp(-jnp.abs(z[h])) for h in hs]
            sp = [jnp.maximum(z[h], 0.0) + jnp.log(1.0 + e[h]) for h in hs]
            if masked:
                causal = diff < (i * tq - ks)
                lom = [jnp.where(causal, -sp[h], 0.0) for h in hs]
            else:
                lom = [-sp[h] for h in hs]
            rem = [runs[h] + _split_dot(lom[h], u_suf) for h in hs]
            w = [jnp.exp(z[h] - sp[h] + rem[h]) for h in hs]
            if masked:
                w = [jnp.where(causal, w[h], 0.0) for h in hs]
            accs = tuple(accs[h] + jnp.dot(w[h].astype(BF16), kvb[h], preferred_element_type=F32) for h in hs)
            for h in hs:
                rc_ref[h, kb] = _col_to_row(runs[h], tq)
            runs = tuple(runs[h] + jnp.sum(lom[h], axis=1, keepdims=True) for h in hs)
            return runs, accs

        init = (tuple(jnp.zeros((tq, 1), F32) for _ in hs), tuple(jnp.zeros((tq, dw), F32) for _ in hs))
        carry = block(nfull, init, True)
        _, accs = lax.fori_loop(0, nfull, lambda n, c: block(nfull - 1 - n, c, False), carry)
        for p in range(hb // 2):
            o_ref[:, dw * p:dw * (p + 1)] = _pair_out(accs[2 * p], accs[2 * p + 1], True).astype(o_ref.dtype)

    qspec = pl.BlockSpec((tq, hb * HEAD_DIM), lambda h, i: (i, h))
    kspec = pl.BlockSpec((L, hb * dw), lambda h, i: (0, h))
    return _pcall(
        body, name=name, grid=(H // hb, nq), in_specs=[qspec, kspec],
        out_specs=[qspec, pl.BlockSpec((hb, None, nkb, 1, tq), lambda h, i: (h, i, 0, 0, 0))],
        out_shape=[jax.ShapeDtypeStruct((L, H * HEAD_DIM), BF16), jax.ShapeDtypeStruct((H, nq, nkb, 1, tq), F32)],
        compiler_params=_params(("parallel", "arbitrary")),
    )(q, kv)


def _attn_bwd(name, q, kv, do, rc, tq, tk, scale, hb):
    L = q.shape[0]
    H = q.shape[1] // HEAD_DIM
    hb = min(hb, H)
    dw_ = 2 * HEAD_DIM
    nq, nkb = L // tq, L // tk
    hs = range(hb)
    assert tk % tq == 0 and hb % 2 == 0 and H % hb == 0

    def body(q_ref, kv_ref, do_ref, rc_ref, dq_ref, dkv_ref):
        i = pl.program_id(1)

        @pl.when(i == 0)
        def _():
            dkv_ref[...] = jnp.zeros_like(dkv_ref)

        diff = (lax.broadcasted_iota(jnp.int32, (tq, tk), 1) - lax.broadcasted_iota(jnp.int32, (tq, tk), 0))
        r_io = lax.broadcasted_iota(jnp.int32, (tk, tk), 0)
        c_io = lax.broadcasted_iota(jnp.int32, (tk, tk), 1)
        u_suf = (r_io > c_io).astype(BF16)
        u_pre = (r_io < c_io).astype(BF16)
        nfull = (i * tq) // tk
        qb = [_head_in(q_ref[:, dw_ * (h // 2):dw_ * (h // 2 + 1)], h, False) for h in hs]
        dob = [_head_in(do_ref[:, dw_ * (h // 2):dw_ * (h // 2 + 1)], h, True) for h in hs]

        def block(kb, carry, masked):
            pres, dqs = carry
            ks = pl.multiple_of(kb * tk, tk)
            kvb = [kv_ref[pl.ds(ks, tk), dw_ * h:dw_ * (h + 1)] for h in hs]
            z = [lax.dot_general(qb[h], kvb[h], (((1,), (1,)), ((), ())), preferred_element_type=F32) for h in hs]
            dw = [lax.dot_general(dob[h], kvb[h], (((1,), (1,)), ((), ())), preferred_element_type=F32)
                  for h in hs]
            e = [jnp.exp(-jnp.abs(z[h])) for h in hs]
            sp = [jnp.maximum(z[h], 0.0) + jnp.log(1.0 + e[h]) for h in hs]
            if masked:
                causal = diff < (i * tq - ks)
                lom = [jnp.where(causal, -sp[h], 0.0) for h in hs]
            else:
                lom = [-sp[h] for h in hs]
            rem = [_row_to_col(rc_ref[h, kb], tq) + _split_dot(lom[h], u_suf) for h in hs]
            logb = [z[h] - sp[h] for h in hs]
            w = [jnp.exp(logb[h] + rem[h]) for h in hs]
            if masked:
                w = [jnp.where(causal, w[h], 0.0) for h in hs]
            da = [dw[h] * w[h] for h in hs]
            p = [pres[h] + jnp.dot(da[h].astype(BF16), u_pre, preferred_element_type=F32) for h in hs]
            dz = [da[h] - jnp.exp(logb[h]) * (da[h] + p[h]) for h in hs]
            if masked:
                dz = [jnp.where(causal, dz[h], 0.0) for h in hs]
            dqs = tuple(dqs[h] + jnp.dot(dz[h].astype(BF16), kvb[h], preferred_element_type=F32) for h in hs)
            for h in hs:
                lhs = jnp.concatenate([dz[h].T, w[h].T], axis=1).astype(BF16)
                rhs = jnp.concatenate([qb[h], dob[h]], axis=0)
                dkv_ref[pl.ds(ks, tk), dw_ * h:dw_ * (h + 1)] += jnp.dot(lhs, rhs, preferred_element_type=F32)
            pres = tuple(pres[h] + jnp.sum(da[h], axis=1, keepdims=True) for h in hs)
            return pres, dqs

        init = (tuple(jnp.zeros((tq, 1), F32) for _ in hs), tuple(jnp.zeros((tq, dw_), F32) for _ in hs))
        carry = lax.fori_loop(0, nfull, lambda kb, c: block(kb, c, False), init)
        _, dqs = block(nfull, carry, True)
        for p in range(hb // 2):
            dq_ref[:, dw_ * p:dw_ * (p + 1)] = (_pair_out(dqs[2 * p], dqs[2 * p + 1], False)
                                                * scale).astype(dq_ref.dtype)

    qspec = pl.BlockSpec((tq, hb * HEAD_DIM), lambda h, i: (i, h))
    kspec = pl.BlockSpec((L, hb * dw_), lambda h, i: (0, h), pipeline_mode=pl.Buffered(1))
    return _pcall(
        body, name=name, grid=(H // hb, nq),
        in_specs=[qspec, kspec, qspec, pl.BlockSpec((hb, None, nkb, 1, tq), lambda h, i: (h, i, 0, 0, 0))],
        out_specs=[qspec, kspec],
        out_shape=[jax.ShapeDtypeStruct((L, H * HEAD_DIM), BF16), jax.ShapeDtypeStruct((L, 2 * H * HEAD_DIM), F32)],
        compiler_params=_params(("parallel", "arbitrary")),
    )(q, kv, do, rc)


_MESH = pl.DeviceIdType.MESH
_HBM = pl.BlockSpec(memory_space=pltpu.HBM)


def _all_gather(name, shards):
    n = len(shards)

    def body(*refs):
        x_refs, out_refs = refs[:n], refs[n:2 * n]
        send_sems, recv_sems, local_sems = refs[2 * n:]
        x, y, c = lax.axis_index("x"), lax.axis_index("y"), lax.axis_index("c")
        me, sibling = (x, y, c), (x, y, 1 - c)
        chips = [(1 - x, y), (x, 1 - y), (1 - x, 1 - y)]

        def slot(a, px, py, pc):
            return out_refs[a].at[4 * px + 2 * py + pc]

        def copy(a, k, block, to, src=None):
            return pltpu.make_async_remote_copy(
                src_ref=slot(a, *block) if src is None else src, dst_ref=slot(a, *block),
                send_sem=send_sems.at[7 * a + k], recv_sem=recv_sems.at[7 * a + k], device_id=to,
                device_id_type=_MESH)

        mine = [pltpu.make_async_copy(x_refs[a], slot(a, *me), local_sems.at[a]) for a in range(n)]
        for cp in mine:
            cp.start()
        first = []
        for a in range(n):
            first.append(copy(a, 0, me, sibling, src=x_refs[a]))
            first += [copy(a, 1 + j, me, (*chip, c), src=x_refs[a]) for j, chip in enumerate(chips)]
        for cp in first:
            cp.start()
        passed = []
        for j, chip in enumerate(chips):
            for a in range(n):
                copy(a, 1 + j, (*chip, c), me).wait_recv()
                passed.append(copy(a, 4 + j, (*chip, c), sibling))
                passed[-1].start()
        for a in range(n):
            copy(a, 0, sibling, me).wait_recv()
            for j, chip in enumerate(chips):
                copy(a, 4 + j, (*chip, 1 - c), me).wait_recv()
        for cp in first + passed:
            cp.wait_send()
        for cp in mine:
            cp.wait()

    return _pcall(
        body, name=name, out_shape=[jax.ShapeDtypeStruct((N_DEV,) + s.shape, s.dtype) for s in shards],
        in_specs=[_HBM] * n, out_specs=[_HBM] * n,
        scratch_shapes=[pltpu.SemaphoreType.DMA((7 * n,)), pltpu.SemaphoreType.DMA((7 * n,)),
                        pltpu.SemaphoreType.DMA((n,))],
    )(*shards)


def _sibling_exchange(name, xs):
    n = len(xs)

    def body(*refs):
        x_refs, out_refs, send_sems, recv_sems = refs[:n], refs[n:2 * n], refs[2 * n], refs[2 * n + 1]
        c = lax.axis_index("c")
        sibling = (lax.axis_index("x"), lax.axis_index("y"), 1 - c)
        cps = [pltpu.make_async_remote_copy(src_ref=x_refs[a].at[:, 1 - c], dst_ref=out_refs[a],
                                            send_sem=send_sems.at[a], recv_sem=recv_sems.at[a], device_id=sibling,
                                            device_id_type=_MESH)
               for a in range(n)]
        for cp in cps:
            cp.start()
        for cp in cps:
            cp.wait()

    return _pcall(
        body, name=name, out_shape=[jax.ShapeDtypeStruct(x.shape[:1] + x.shape[2:], x.dtype) for x in xs],
        in_specs=[_HBM] * n, out_specs=[_HBM] * n,
        scratch_shapes=[pltpu.SemaphoreType.DMA((n,)), pltpu.SemaphoreType.DMA((n,))],
    )(*xs)


def _chip_exchange(name, xs):
    n = len(xs)

    def body(*refs):
        x_refs, out_refs, send_sems, recv_sems = refs[:n], refs[n:2 * n], refs[2 * n], refs[2 * n + 1]
        mx, my, mc = lax.axis_index("x"), lax.axis_index("y"), lax.axis_index("c")
        chips = [(1 - mx, my), (mx, 1 - my), (1 - mx, 1 - my)]
        cps = [pltpu.make_async_remote_copy(src_ref=x_refs[a].at[2 * px + py], dst_ref=out_refs[a].at[j],
                                            send_sem=send_sems.at[3 * a + j], recv_sem=recv_sems.at[3 * a + j],
                                            device_id=(px, py, mc), device_id_type=_MESH)
               for a in range(n) for j, (px, py) in enumerate(chips)]
        for cp in cps:
            cp.start()
        for cp in cps:
            cp.wait()

    return _pcall(
        body, name=name, out_shape=[jax.ShapeDtypeStruct((3,) + x.shape[1:], x.dtype) for x in xs],
        in_specs=[_HBM] * n, out_specs=[_HBM] * n,
        scratch_shapes=[pltpu.SemaphoreType.DMA((3 * n,)), pltpu.SemaphoreType.DMA((3 * n,))],
    )(*xs)


def _pair_sum(name, a, b):
    nb, R, C = a.shape
    tm = _tile(R, 512, 16)

    def fn(i, ab, bb):
        s = ab.astype(F32) + bb.astype(F32)
        return s, s

    spec = lambda arr: (arr, (1, tm, C), lambda i: (i // (R // tm), i % (R // tm), 0))
    out = lambda dt: ((nb, R, C), dt, (1, tm, C), lambda i: (i // (R // tm), i % (R // tm), 0))
    return _rowwise(name, fn, nb * (R // tm), [spec(a), spec(b)], [out(F32), out(a.dtype)])


def _sum8(name, g):
    n, R, C = g.shape
    tm = _tile(R, 256)

    def fn(i, gb):
        s = gb[0]
        for d in range(1, n):
            s = s + gb[d]
        return (s,)

    return _rowwise(name, fn, R // tm, [(g, (n, tm, C), lambda i: (0, i, 0))], [_orows(R, C, F32, tm)])[0]


def _adamw(name, w, m, v, grads, tm=512):
    R, C = w.shape
    tm = _tile(R, tm, 16)
    ng = len(grads)

    def fn(i, wb, mb, vb, *gs):
        g = gs[0].astype(F32)
        for t in gs[1:]:
            g = g + t.astype(F32)
        mn = ADAM_B1 * mb + (1.0 - ADAM_B1) * g
        vn = ADAM_B2 * vb + (1.0 - ADAM_B2) * (g * g)
        m_hat = mn / (1.0 - ADAM_B1 ** ADAM_STEP)
        v_hat = vn / (1.0 - ADAM_B2 ** ADAM_STEP)
        delta = -ADAM_LR * (m_hat / (jnp.sqrt(v_hat) + ADAM_EPS) + ADAM_WD * wb)
        return g, delta, mn, vn

    ins = [_rows(w, tm), _rows(m, tm), _rows(v, tm)] + [_rows(g, tm) for g in grads]
    return _rowwise(name, fn, R // tm, ins, [_orows(R, C, F32, tm) for _ in range(4)])


CLASSES = [
    (BF16, True, [('ssm_w_in', None), ('attn_w_q', None), ('attn_w_o', None), ('ffn_w_down', 0), ('ffn_w_down', 1)]),
    (BF16, False, [('ssm_w_glu', None), ('kv_w', None)]),
    (BF16, False, [('ffn_w_up', 0), ('ffn_w_up', 1)]),
    (F32, False, [('ffn_conv_w', 0), ('ffn_conv_w', 1)]),
]


def _shard2d(a, name, layer):
    if name == 'ffn_conv_w':
        return a[layer, :, 0, :]
    if name == 'kv_w':
        return a
    return a[0 if layer is None else layer]


def _weights_of(members):
    names = []
    for n, _ in members:
        if not names or names[-1] != n:
            names.append(n)
    return names


def _pack_class(d, members, dtype):
    width = _shard2d(d[members[0][0]], *members[0]).shape[1]
    parts = [d[n].reshape(-1, width).astype(dtype) for n in _weights_of(members)]
    return parts[0] if len(parts) == 1 else jnp.concatenate(parts, axis=0)


def _full_weights(gaths, d):
    w = {}
    for (_, row_sharded, members), g in zip(CLASSES, gaths):
        off = 0
        for n, l in members:
            r = _shard2d(d[n], n, l).shape[0]
            blk = g[:, off:off + r]
            if row_sharded:
                w[(n, l)] = blk.reshape(N_DEV * r, g.shape[2])
            else:
                w[(n, l)] = blk.transpose(1, 0, 2).reshape(r, N_DEV * g.shape[2])
            off += r
    return w


def _grad_classes(gw):
    out = []
    for dtype, row_sharded, members in CLASSES:
        parts = []
        for n, l in members:
            g = gw[(n, l)].astype(dtype)
            if row_sharded:
                parts.append(g.reshape(N_DEV, g.shape[0] // N_DEV, g.shape[1]))
            else:
                parts.append(g.reshape(g.shape[0], N_DEV, g.shape[1] // N_DEV).transpose(1, 0, 2))
        out.append(jnp.concatenate(parts, axis=1))
    return out


def _unpack_class(buf, members, shapes):
    out, off = {}, 0
    for n in _weights_of(members):
        r = math.prod(shapes[n]) // buf.shape[1]
        out[n] = buf[off:off + r].reshape(shapes[n])
        off += r
    return out


def _small_layout(shapes, D):
    lay, off = {}, 0
    for n in SMALL:
        r = -(-math.prod(shapes[n]) // D)
        lay[n] = (off, r)
        off += r
    lay['loss'] = (off, 1)
    off += 1
    return lay, -(-off // 8) * 8


def _pack_small(d, lay, total, D):
    parts = []
    for n in SMALL + ['loss']:
        if n not in d:
            parts.append(jnp.zeros((lay[n][1], D), F32))
            continue
        flat = d[n].reshape(-1).astype(F32)
        parts.append(jnp.pad(flat, (0, lay[n][1] * D - flat.shape[0])).reshape(lay[n][1], D))
    used = sum(lay[n][1] for n in SMALL + ['loss'])
    if total > used:
        parts.append(jnp.zeros((total - used, D), F32))
    return jnp.concatenate(parts, axis=0)


def _unpack_small(pack, lay, shapes, D):
    out = {}
    for n in SMALL:
        off, r = lay[n]
        out[n] = pack[off:off + r].reshape(-1)[:math.prod(shapes[n])].reshape(shapes[n])
    return out


def _ffn_fwd(tag, x, g_norm, w_up, conv_w, conv_b, w_down):
    h = _rms_fwd(f"{tag}_norm", x, g_norm)
    gu = _mm(f"{tag}_up", h, w_up, 'nn', BF16)
    a = _gate_fwd(f"{tag}_gate", gu, conv_w, conv_b)
    return _mm(f"{tag}_down", a, w_down, 'nn', F32, resid=x), (h, gu, a)


def _ffn_bwd(tag, dres, dres_bf, x, saved, g_norm, w_up, conv_w, conv_b, w_down):
    h, gu, a = saved
    da = _mm(f"{tag}_dgate", dres_bf, w_down, 'nt', BF16)
    d_w_down = _mm(f"{tag}_dwdown", a, dres_bf, 'tn', BF16)
    dgu, dw0, dw1, dw2, dcb = _gate_bwd(f"{tag}_gate_b", gu, da, conv_w, conv_b)
    d_w_up = _mm(f"{tag}_dwup", h, dgu, 'tn', BF16)
    dh = _mm(f"{tag}_dh", dgu, w_up, 'nt', F32)
    dres, dres_bf, dg = _rms_bwd(f"{tag}_norm_b", x, dres, [(g_norm, dh)])
    return dres, dres_bf, dg, d_w_up, jnp.concatenate([dw0, dw1, dw2], axis=0), dcb[0], d_w_down


def kernel(x, norm_mix, norm_ffn, norm_kv, norm_final, ssm_w_in, ssm_a_re, ssm_a_im, ssm_log_dt, ssm_b_re, ssm_b_im, ssm_c_re, ssm_c_im, ssm_d, ssm_w_glu, kv_w, attn_w_q, attn_w_o, ffn_w_up, ffn_conv_w, ffn_conv_b, ffn_w_down, loss_target, m_norm_mix, m_norm_ffn, m_norm_kv, m_norm_final, m_ssm_w_in, m_ssm_a_re, m_ssm_a_im, m_ssm_log_dt, m_ssm_b_re, m_ssm_b_im, m_ssm_c_re, m_ssm_c_im, m_ssm_d, m_ssm_w_glu, m_kv_w, m_attn_w_q, m_attn_w_o, m_ffn_w_up, m_ffn_conv_w, m_ffn_conv_b, m_ffn_w_down, v_norm_mix, v_norm_ffn, v_norm_kv, v_norm_final, v_ssm_w_in, v_ssm_a_re, v_ssm_a_im, v_ssm_log_dt, v_ssm_b_re, v_ssm_b_im, v_ssm_c_re, v_ssm_c_im, v_ssm_d, v_ssm_w_glu, v_kv_w, v_attn_w_q, v_attn_w_o, v_ffn_w_up, v_ffn_conv_w, v_ffn_conv_b, v_ffn_w_down):
    wts = dict(zip(W_NAMES, (norm_mix, norm_ffn, norm_kv, norm_final, ssm_w_in, ssm_a_re, ssm_a_im, ssm_log_dt,
                             ssm_b_re, ssm_b_im, ssm_c_re, ssm_c_im, ssm_d, ssm_w_glu, kv_w, attn_w_q, attn_w_o,
                             ffn_w_up, ffn_conv_w, ffn_conv_b, ffn_w_down)))
    mom = dict(zip(W_NAMES, (m_norm_mix, m_norm_ffn, m_norm_kv, m_norm_final, m_ssm_w_in, m_ssm_a_re, m_ssm_a_im,
                             m_ssm_log_dt, m_ssm_b_re, m_ssm_b_im, m_ssm_c_re, m_ssm_c_im, m_ssm_d, m_ssm_w_glu,
                             m_kv_w, m_attn_w_q, m_attn_w_o, m_ffn_w_up, m_ffn_conv_w, m_ffn_conv_b, m_ffn_w_down)))
    vel = dict(zip(W_NAMES, (v_norm_mix, v_norm_ffn, v_norm_kv, v_norm_final, v_ssm_w_in, v_ssm_a_re, v_ssm_a_im,
                             v_ssm_log_dt, v_ssm_b_re, v_ssm_b_im, v_ssm_c_re, v_ssm_c_im, v_ssm_d, v_ssm_w_glu,
                             v_kv_w, v_attn_w_q, v_attn_w_o, v_ffn_w_up, v_ffn_conv_w, v_ffn_conv_b, v_ffn_w_down)))
    shapes = {n: wts[n].shape for n in W_NAMES}
    _, L, D = x.shape
    Fh = ffn_conv_b.shape[1]
    G, P = ssm_a_re.shape[1], ssm_a_re.shape[2]
    Hg = ssm_d.shape[2]
    x0 = x[0]
    target = loss_target[0]
    scale = HEAD_DIM ** -0.5

    gaths = _all_gather("gather_weights", [_pack_class(wts, members, dt) for dt, _, members in CLASSES])
    w = _full_weights(gaths, wts)
    nm = [norm_mix[l:l + 1] for l in range(2)]
    nf = [norm_ffn[l:l + 1] for l in range(2)]
    nkv = norm_kv[None]
    nfin = norm_final[None]
    cb = [ffn_conv_b[l:l + 1] for l in range(2)]
    cwt = [w[('ffn_conv_w', l)] for l in range(2)]

    s5p = (ssm_a_re[0], ssm_a_im[0], ssm_log_dt[0], ssm_b_re[0], ssm_b_im[0], ssm_c_re[0], ssm_c_im[0], ssm_d[0])
    (m_mat, n_mat, o_mat, lam1, lam2), s5_vjp = jax.vjp(_s5_build, *s5p)
    m_bf, n_bf, o_bf = m_mat.astype(BF16), n_mat.astype(BF16), o_mat.astype(BF16)

    h0 = _rms_fwd("l0_norm", x0, nm[0])
    u = _mm("l0_win", h0, w[('ssm_w_in', None)], 'nn', BF16)
    ug = _to_groups("s5_group_u", u, G, Hg, BF16)
    vloc = _bmm("s5_local_state", [(ug, n_bf, 'nt')], F32)
    st = _chunk_scan("s5_scan", vloc, lam1, lam2, reverse=False)
    yraw, yg = _bmm("s5_out", [(ug, m_bf, 'nt'), (st, o_bf, 'nt')], (F32, BF16),
                    post=lambda acc: (acc, _gelu(acc)), ungroup=(1, Hg))
    z = _mm("l0_wglu", yg, w[('ssm_w_glu', None)], 'nn', F32)
    x1 = _glu_fwd("l0_glu", x0, z)
    x2, ffn0 = _ffn_fwd("f0", x1, nf[0], w[('ffn_w_up', 0)], cwt[0], cb[0], w[('ffn_w_down', 0)])

    hkv = _rms_fwd("kv_norm", x2, nkv)
    nh = D // HEAD_DIM
    w_kv = w[('kv_w', None)].reshape(D, 2, nh, HEAD_DIM).transpose(0, 2, 1, 3).reshape(D, 2 * D)
    kvp = _mm("kv_proj", hkv, w_kv, 'nn', BF16)
    h2 = _rms_fwd("l1_norm", x2, nm[1])
    qn = _mm("l1_wq", h2, w[('attn_w_q', None)], 'nn', BF16, scale=scale)
    o, rc = _attn_fwd("attn_fwd", qn, kvp, ATT_TQ, ATT_TK, ATT_HB_FWD)
    x3 = _mm("l1_wo", o, w[('attn_w_o', None)], 'nn', F32, resid=x2)
    x4, ffn1 = _ffn_fwd("f1", x3, nf[1], w[('ffn_w_up', 1)], cwt[1], cb[1], w[('ffn_w_down', 1)])

    dres, dres_bf, dg_final, loss_part = _final_loss("loss_head", x4, nfin, target)

    gw = {}
    dres, dres_bf, dg_nf1, gup1, gcw1, gcb1, gdn1 = _ffn_bwd("f1", dres, dres_bf, x3, ffn1, nf[1], w[('ffn_w_up', 1)],
                                                             cwt[1], cb[1], w[('ffn_w_down', 1)])
    do = _mm("l1_do", dres_bf, w[('attn_w_o', None)], 'nt', BF16)
    gw[('attn_w_o', None)] = _mm("l1_dwo", o, dres_bf, 'tn', BF16)
    dq, dkv = _attn_bwd("attn_bwd", qn, kvp, do, rc, ATT_TQ, ATT_TK, scale, ATT_HB_BWD)
    gw[('attn_w_q', None)] = _mm("l1_dwq", h2, dq, 'tn', BF16)
    dh2 = _mm("l1_dh", dq, w[('attn_w_q', None)], 'nt', F32)
    g_kv = _mm("kv_dw", hkv, dkv, 'tn', BF16)
    gw[('kv_w', None)] = g_kv.reshape(D, nh, 2, HEAD_DIM).transpose(0, 2, 1, 3).reshape(D, 2 * D)
    dhkv = _mm("kv_dh", dkv, w_kv, 'nt', F32)
    dres, dres_bf, dg_nm1, dg_nkv = _rms_bwd("l1_norm_b", x2, dres, [(nm[1], dh2), (nkv, dhkv)])
    dres, dres_bf, dg_nf0, gup0, gcw0, gcb0, gdn0 = _ffn_bwd("f0", dres, dres_bf, x1, ffn0, nf[0], w[('ffn_w_up', 0)],
                                                             cwt[0], cb[0], w[('ffn_w_down', 0)])
    dz = _glu_bwd("l0_glu_b", dres, z)
    gw[('ssm_w_glu', None)] = _mm("l0_dwglu", yg, dz, 'tn', BF16)
    dyg = _mm("l0_dyg", dz, w[('ssm_w_glu', None)], 'nt', F32)
    dy = _to_groups("s5_group_dy", dyg, G, Hg, BF16, gelu_arg=yraw)
    ds = _bmm("s5_dstate", [(dy, o_bf, 'nn')], F32)
    dv_loc, dlam1, dlam2 = _chunk_scan("s5_scan_b", ds, lam1, lam2, reverse=True, s_fwd=st)
    du = _bmm("s5_du", [(dy, m_bf, 'nn'), (dv_loc, n_bf, 'nn')], BF16, ungroup=(0, Hg))
    d_m = _bmm("s5_dm", [(dy, ug, 'tn')], F32)
    d_o = _bmm("s5_do", [(dy, st, 'tn')], F32)
    d_n = _bmm("s5_dn", [(dv_loc, ug, 'tn')], F32)
    s5g = s5_vjp((d_m, d_n, d_o, dlam1, dlam2))
    gw[('ssm_w_in', None)] = _mm("l0_dwin", h0, du, 'tn', BF16)
    dh0 = _mm("l0_dh", du, w[('ssm_w_in', None)], 'nt', F32)
    grad_x, _, dg_nm0 = _rms_bwd("l0_norm_b", x0, dres, [(nm[0], dh0)])

    gw.update({('ffn_w_up', 0): gup0, ('ffn_w_up', 1): gup1, ('ffn_w_down', 0): gdn0, ('ffn_w_down', 1): gdn1,
               ('ffn_conv_w', 0): gcw0, ('ffn_conv_w', 1): gcw1})

    small_g = {
        'norm_mix': jnp.concatenate([dg_nm0, dg_nm1], axis=0), 'norm_ffn': jnp.concatenate([dg_nf0, dg_nf1], axis=0),
        'norm_kv': dg_nkv, 'norm_final': dg_final, 'ffn_conv_b': jnp.stack([gcb0, gcb1]),
        'ssm_a_re': s5g[0], 'ssm_a_im': s5g[1], 'ssm_log_dt': s5g[2], 'ssm_b_re': s5g[3], 'ssm_b_im': s5g[4],
        'ssm_c_re': s5g[5], 'ssm_c_im': s5g[6], 'ssm_d': s5g[7], 'loss': loss_part[0, 0:1],
    }
    lay, rs = _small_layout(shapes, D)
    small_sum = _sum8("small_sum", _all_gather("gather_small", [_pack_small(small_g, lay, rs, D)])[0])
    loss = small_sum[lay['loss'][0], 0]
    sg, sdelta, sm, sv = _adamw("adamw_small", _pack_small(wts, lay, rs, D), _pack_small(mom, lay, rs, D),
                                _pack_small(vel, lay, rs, D), [small_sum])
    small_out = [_unpack_small(t, lay, shapes, D) for t in (sg, sdelta, sm, sv)]

    c = lax.axis_index("c")
    chip = 2 * lax.axis_index("x") + lax.axis_index("y")
    gcls = [g.reshape((4, 2) + g.shape[1:]) for g in _grad_classes(gw)]
    mine = [lax.dynamic_index_in_dim(g, c, axis=1, keepdims=False) for g in gcls]
    got = _sibling_exchange("rs_sibling", gcls)
    sums = [_pair_sum(f"rs_pair_sum{k}", a, b) for k, (a, b) in enumerate(zip(mine, got))]
    from_chips = _chip_exchange("rs_chips", [on_wire for _, on_wire in sums])
    big_parts = [{}, {}, {}, {}]
    for k, ((_, _, members), (chip_f32, _), fc) in enumerate(zip(CLASSES, sums, from_chips)):
        own = lax.dynamic_index_in_dim(chip_f32, chip, axis=0, keepdims=False)
        res = _adamw(f"adamw_big{k}", _pack_class(wts, members, F32), _pack_class(mom, members, F32),
                     _pack_class(vel, members, F32), [own, fc[0], fc[1], fc[2]])
        for q in range(4):
            big_parts[q].update(_unpack_class(res[q], members, shapes))
    big_out = big_parts

    outs = [loss, grad_x[None]]
    for k in range(4):
        for n in W_NAMES:
            outs.append(big_out[k][n] if n in BIG else small_out[k][n])
    return tuple(outs)
```

```python
import math

import jax
import jax.numpy as jnp
from jax import lax
from jax.experimental import pallas as pl
from jax.experimental.pallas import tpu as pltpu

F32 = jnp.float32
BF16 = jnp.bfloat16

EPS = 1e-6
HEAD_DIM = 64
CHUNK = 16
N_DEV = 8
ADAM_LR = 0.001
ADAM_B1 = 0.9
ADAM_B2 = 0.999
ADAM_EPS = 1e-08
ADAM_WD = 0.01
ADAM_STEP = 10
VMEM_LIMIT = 48 * 1024 * 1024
MM_BLOCK_BYTES = 28 * 1024 * 1024
MM_TM_MAX = 1024
MM_T_MAX = 512
MM_TN_MAX = 1536
LANES = 128
ATT_TQ = 128
ATT_TK = 256
ATT_HB_FWD = 16
ATT_HB_BWD = 8

W_NAMES = ['norm_mix', 'norm_ffn', 'norm_kv', 'norm_final', 'ssm_w_in', 'ssm_a_re', 'ssm_a_im', 'ssm_log_dt',
           'ssm_b_re', 'ssm_b_im', 'ssm_c_re', 'ssm_c_im', 'ssm_d', 'ssm_w_glu', 'kv_w', 'attn_w_q', 'attn_w_o',
           'ffn_w_up', 'ffn_conv_w', 'ffn_conv_b', 'ffn_w_down']
BIG = ['ssm_w_in', 'ssm_w_glu', 'kv_w', 'attn_w_q', 'attn_w_o', 'ffn_w_up', 'ffn_w_down', 'ffn_conv_w']
SMALL = [n for n in W_NAMES if n not in BIG]


def _pcall(body, **kw):
    return pl.pallas_call(body, **kw)


def _params(sem=None):
    if sem is None:
        return pltpu.CompilerParams(vmem_limit_bytes=VMEM_LIMIT)
    return pltpu.CompilerParams(dimension_semantics=sem, vmem_limit_bytes=VMEM_LIMIT)


def _tile(n, pref, mult=8):
    best = None
    for t in range(mult, min(n, pref) + 1, mult):
        if n % t == 0:
            best = t
    return n if best is None else best


def _mm_tiles(M, N, K, mode, a_bytes, b_bytes, o_bytes):
    def cands(n, cap):
        c = [t for t in range(LANES, min(n, cap) + 1, LANES) if n % t == 0]
        return c or [n]
    best = None
    for tm in cands(M, MM_T_MAX if mode == 'tn' else MM_TM_MAX):
        for tn in cands(N, MM_TN_MAX):
            need = 2 * (tm * K * a_bytes + K * tn * b_bytes + tm * tn * o_bytes)
            score = tm * tn * (tm if mode == 'tn' else 1)
            if need <= MM_BLOCK_BYTES and (best is None or score > best[0]):
                best = (score, tm, tn)
    assert best is not None, (M, N, K)
    return best[1], best[2]


def _mm(name, a, b, mode, out_dtype=F32, scale=None, resid=None):
    if mode == 'nn':
        (M, K), (K2, N) = a.shape, b.shape
    elif mode == 'nt':
        (M, K), (N, K2) = a.shape, b.shape
    else:
        (K, M), (K2, N) = a.shape, b.shape
    assert K == K2, (name, a.shape, b.shape)
    o_bytes = jnp.dtype(out_dtype).itemsize + (resid.dtype.itemsize if resid is not None else 0)
    tm, tn = _mm_tiles(M, N, K, mode, a.dtype.itemsize, b.dtype.itemsize, o_bytes)
    if mode == 'tn':
        a_spec = pl.BlockSpec((K, tm), lambda i, j: (0, i))
    else:
        a_spec = pl.BlockSpec((tm, K), lambda i, j: (i, 0))
    if mode == 'nt':
        b_spec = pl.BlockSpec((tn, K), lambda i, j: (j, 0))
    else:
        b_spec = pl.BlockSpec((K, tn), lambda i, j: (0, j))
    o_spec = pl.BlockSpec((tm, tn), lambda i, j: (i, j))
    dn = {'nn': ((1,), (0,)), 'nt': ((1,), (1,)), 'tn': ((0,), (0,))}[mode]

    def body(*refs):
        a_ref, b_ref, o_ref = refs[0], refs[1], refs[-1]
        acc = lax.dot_general(a_ref[...].astype(BF16), b_ref[...].astype(BF16), (dn, ((), ())),
                              preferred_element_type=F32)
        if scale is not None:
            acc = acc * scale
        if resid is not None:
            acc = acc + refs[2][...]
        o_ref[...] = acc.astype(o_ref.dtype)

    return _pcall(
        body, name=name, grid=(M // tm, N // tn),
        in_specs=[a_spec, b_spec] + ([o_spec] if resid is not None else []),
        out_specs=o_spec,
        out_shape=jax.ShapeDtypeStruct((M, N), out_dtype),
        compiler_params=_params(("parallel", "parallel")),
    )(*([a, b] + ([resid] if resid is not None else [])))


def _bmm(name, terms, out_dtype, gb=8, post=None, ungroup=None):
    G = terms[0][0].shape[0]
    gb = _tile(G, gb, 1)
    dns = {'nn': ((1,), (0,)), 'nt': ((1,), (1,)), 'tn': ((0,), (0,))}

    def oshape(a, b, mode):
        m = a.shape[2] if mode == 'tn' else a.shape[1]
        n = b.shape[1] if mode == 'nt' else b.shape[2]
        return m, n

    m, n = oshape(*terms[0])
    out_dtypes = out_dtype if isinstance(out_dtype, (tuple, list)) else (out_dtype,)
    n_in = 2 * len(terms)
    if ungroup is not None:
        perm, gpb = _lane_perm(ungroup[1])
        assert gpb == gb and n == CHUNK * ungroup[1]

    def body(*refs):
        ins, outs = refs[:n_in], refs[len(refs) - len(out_dtypes):]
        kept = []
        for gi in range(gb):
            acc = None
            for t, (_, _, mode) in enumerate(terms):
                part = lax.dot_general(ins[2 * t][gi].astype(BF16), ins[2 * t + 1][gi].astype(BF16),
                                       (dns[mode], ((), ())), preferred_element_type=F32)
                acc = part if acc is None else acc + part
            vals = (acc,) if post is None else post(acc)
            for k, (o_ref, v) in enumerate(zip(outs, vals)):
                if ungroup is not None and k == ungroup[0]:
                    kept.append(v.astype(BF16))
                else:
                    o_ref[gi] = v.astype(o_ref.dtype)
        if ungroup is not None:
            x = lax.dot_general(jnp.concatenate(kept, axis=1), refs[n_in][...], (((1,), (1,)), ((), ())),
                                preferred_element_type=F32)
            o_ref = outs[ungroup[0]]
            for t in range(CHUNK):
                o_ref[:, t, :] = x[:, t * LANES:(t + 1) * LANES].astype(o_ref.dtype)

    in_specs, args = [], []
    for a, b, _ in terms:
        in_specs += [pl.BlockSpec((gb,) + a.shape[1:], lambda g: (g, 0, 0)),
                     pl.BlockSpec((gb,) + b.shape[1:], lambda g: (g, 0, 0))]
        args += [a, b]
    out_specs = [pl.BlockSpec((gb, m, n), lambda g: (g, 0, 0)) for _ in out_dtypes]
    out_shape = [jax.ShapeDtypeStruct((G, m, n), dt) for dt in out_dtypes]
    if ungroup is not None:
        in_specs.append(pl.BlockSpec(perm.shape, lambda g: (0, 0)))
        args.append(perm)
        out_specs[ungroup[0]] = pl.BlockSpec((m, CHUNK, LANES), lambda g: (0, 0, g))
        out_shape[ungroup[0]] = jax.ShapeDtypeStruct((m, CHUNK, G * ungroup[1]), out_dtypes[ungroup[0]])
    res = _pcall(
        body, name=name, grid=(G // gb,), in_specs=in_specs, out_specs=out_specs, out_shape=out_shape,
        compiler_params=_params(("parallel",)),
    )(*args)
    if ungroup is not None:
        res = list(res)
        res[ungroup[0]] = res[ungroup[0]].reshape(m * CHUNK, G * ungroup[1])
    return res[0] if len(out_dtypes) == 1 else res


def _rowwise(name, fn, n_steps, ins, outs, n_acc=0):
    n_in, n_out = len(ins), len(outs)

    def body(*refs):
        i = pl.program_id(0)
        vals = fn(i, *[r[...] for r in refs[:n_in]])
        o_refs = refs[n_in:]
        for j in range(n_out - n_acc):
            o_refs[j][...] = vals[j].astype(o_refs[j].dtype)
        if n_acc:
            @pl.when(i == 0)
            def _():
                for j in range(n_out - n_acc, n_out):
                    o_refs[j][...] = vals[j].astype(o_refs[j].dtype)

            @pl.when(i > 0)
            def _():
                for j in range(n_out - n_acc, n_out):
                    o_refs[j][...] += vals[j].astype(o_refs[j].dtype)

    res = _pcall(
        body, name=name, grid=(n_steps,),
        in_specs=[pl.BlockSpec(blk, im) for _, blk, im in ins],
        out_specs=[pl.BlockSpec(blk, im) for _, _, blk, im in outs],
        out_shape=[jax.ShapeDtypeStruct(s, d) for s, d, _, _ in outs],
        compiler_params=_params(("arbitrary",)),
    )(*[a for a, _, _ in ins])
    return res


def _rows(a, tm):
    return (a, (tm, a.shape[1]), lambda i: (i, 0))


def _whole(a):
    nd = a.ndim
    return (a, a.shape, lambda i: (0,) * nd)


def _orows(L, n, dtype, tm):
    return ((L, n), dtype, (tm, n), lambda i: (i, 0))


def _oacc(r, n):
    return ((r, n), F32, (r, n), lambda i: (0, 0))


def _rms_fwd(name, x, g, tm=512):
    L, D = x.shape
    tm = _tile(L, tm)

    def fn(i, xb, gb):
        r = lax.rsqrt(jnp.mean(xb * xb, axis=-1, keepdims=True) + EPS)
        return (xb * r * gb,)

    return _rowwise(name, fn, L // tm, [_rows(x, tm), _whole(g)], [_orows(L, D, BF16, tm)])[0]


def _rms_bwd(name, x, dres, branches, tm=256):
    L, D = x.shape
    tm = _tile(L, tm)
    nb = len(branches)

    def fn(i, xb, db, *rest):
        r = lax.rsqrt(jnp.mean(xb * xb, axis=-1, keepdims=True) + EPS)
        xh = xb * r
        dx = db
        dgs = []
        for b in range(nb):
            gb, dyb = rest[2 * b], rest[2 * b + 1].astype(F32)
            dxh = dyb * gb
            dx = dx + r * (dxh - xh * jnp.mean(dxh * xh, axis=-1, keepdims=True))
            dgs.append(jnp.sum(dyb * xh, axis=0, keepdims=True))
        return (dx, dx, *dgs)

    ins = [_rows(x, tm), _rows(dres, tm)]
    for g, dy in branches:
        ins += [_whole(g), _rows(dy, tm)]
    outs = [_orows(L, D, F32, tm), _orows(L, D, BF16, tm)] + [_oacc(1, D) for _ in range(nb)]
    return _rowwise(name, fn, L // tm, ins, outs, n_acc=nb)


def _final_loss(name, x, g, target, tm=256):
    L, D = x.shape
    tm = _tile(L, tm)

    def fn(i, xb, gb, tb):
        r = lax.rsqrt(jnp.mean(xb * xb, axis=-1, keepdims=True) + EPS)
        xh = xb * r
        err = xh * gb - tb
        dy = err * (1.0 / D)
        dxh = dy * gb
        dx = r * (dxh - xh * jnp.mean(dxh * xh, axis=-1, keepdims=True))
        dg = jnp.sum(dy * xh, axis=0, keepdims=True)
        per_row = jnp.mean(err * err, axis=-1, keepdims=True)
        loss = 0.5 * jnp.sum(per_row, axis=0, keepdims=True)
        return dx, dx, dg, jnp.broadcast_to(loss, (1, LANES))

    return _rowwise(name, fn, L // tm, [_rows(x, tm), _whole(g), _rows(target, tm)],
                    [_orows(L, D, F32, tm), _orows(L, D, BF16, tm), _oacc(1, D), _oacc(1, LANES)], n_acc=2)


def _glu_fwd(name, x, z, tm=256):
    L, D = x.shape
    tm = _tile(L, tm)

    def fn(i, xb, zb):
        return (xb + zb[:, :D] * jax.nn.sigmoid(zb[:, D:]),)

    return _rowwise(name, fn, L // tm, [_rows(x, tm), _rows(z, tm)], [_orows(L, D, F32, tm)])[0]


def _glu_bwd(name, dx, z, tm=256):
    L, D = dx.shape
    tm = _tile(L, tm)

    def fn(i, db, zb):
        z1, sg = zb[:, :D], jax.nn.sigmoid(zb[:, D:])
        return (jnp.concatenate([(db * sg).astype(BF16), (db * z1 * sg * (1.0 - sg)).astype(BF16)], axis=1),)

    return _rowwise(name, fn, L // tm, [_rows(dx, tm), _rows(z, tm)], [_orows(L, 2 * D, BF16, tm)])[0]


HALO = 16


def _halo_prev(a, tm):
    return (a, (HALO, a.shape[1]), lambda i: (jnp.maximum(i * (tm // HALO) - 1, 0), 0))


def _halo_next(a, tm):
    last = a.shape[0] // HALO - 1
    return (a, (HALO, a.shape[1]), lambda i: (jnp.minimum((i + 1) * (tm // HALO), last), 0))


def _shift_down(cur, halo, k, first):
    tm = cur.shape[0]
    rolled = pltpu.roll(cur, k, 0)
    tail = pltpu.roll(halo, k, 0)
    tail = jnp.where(first, 0.0, tail)
    row = lax.broadcasted_iota(jnp.int32, (tm, 1), 0)
    head = jnp.concatenate([tail, jnp.zeros((tm - 8, cur.shape[1]), cur.dtype)], axis=0) if tm > 8 else tail
    return jnp.where(row < k, head, rolled)


def _shift_up(cur, halo, k, last):
    tm = cur.shape[0]
    rolled = pltpu.roll(cur, tm - k, 0)
    head = pltpu.roll(halo, 8 - k, 0)
    head = jnp.where(last, 0.0, head)
    row = lax.broadcasted_iota(jnp.int32, (tm, 1), 0)
    tail = jnp.concatenate([jnp.zeros((tm - 8, cur.shape[1]), cur.dtype), head], axis=0) if tm > 8 else head
    return jnp.where(row >= tm - k, tail, rolled)


def _conv_pre(gb, hb, cw, cb, first):
    g1 = _shift_down(gb, hb, 1, first)
    g2 = _shift_down(gb, hb, 2, first)
    return cw[0:1] * g2 + cw[1:2] * g1 + cw[2:3] * gb + cb, g1, g2


def _gate_fwd(name, gu, cw, cb, tm=256):
    L, F2 = gu.shape
    Fh = F2 // 2
    tm = _tile(L, tm)

    def fn(i, gub, halo, cwb, cbb):
        gub, halo = gub.astype(F32), halo.astype(F32)[HALO - 8:]
        gc, _, _ = _conv_pre(gub[:, :Fh], halo[:, :Fh], cwb, cbb, i == 0)
        return (gc * jax.nn.sigmoid(gc) * gub[:, Fh:],)

    return _rowwise(name, fn, L // tm, [_rows(gu, tm), _halo_prev(gu, tm), _whole(cw), _whole(cb)],
                    [_orows(L, Fh, BF16, tm)])[0]


def _gate_bwd(name, gu, da, cw, cb, tm=128):
    L, F2 = gu.shape
    Fh = F2 // 2
    tm = _tile(L, tm)
    n = L // tm

    def dgate(gb, ub, dab, hb, cwb, cbb, first):
        gc, g1, g2 = _conv_pre(gb, hb, cwb, cbb, first)
        sg = jax.nn.sigmoid(gc)
        return dab * ub * sg * (1.0 + gc * (1.0 - sg)), dab * gc * sg, g1, g2

    def fn(i, gub, prev, nxt, dab, dnx, cwb, cbb):
        gub, prev, nxt = gub.astype(F32), prev.astype(F32)[HALO - 8:], nxt.astype(F32)[:8]
        dab, dnx = dab.astype(F32), dnx.astype(F32)[:8]
        gb, ub = gub[:, :Fh], gub[:, Fh:]
        dgc, du, g1, g2 = dgate(gb, ub, dab, prev[:, :Fh], cwb, cbb, i == 0)
        dgc_next, _, _, _ = dgate(nxt[:, :Fh], nxt[:, Fh:], dnx, gb[tm - 8:], cwb, cbb, False)
        d1 = _shift_up(dgc, dgc_next, 1, i == n - 1)
        d2 = _shift_up(dgc, dgc_next, 2, i == n - 1)
        dg = cwb[2:3] * dgc + cwb[1:2] * d1 + cwb[0:1] * d2
        return (jnp.concatenate([dg.astype(BF16), du.astype(BF16)], axis=1),
                jnp.sum(dgc * g2, axis=0, keepdims=True), jnp.sum(dgc * g1, axis=0, keepdims=True),
                jnp.sum(dgc * gb, axis=0, keepdims=True), jnp.sum(dgc, axis=0, keepdims=True))

    return _rowwise(name, fn, n,
                    [_rows(gu, tm), _halo_prev(gu, tm), _halo_next(gu, tm), _rows(da, tm), _halo_next(da, tm),
                     _whole(cw), _whole(cb)],
                    [_orows(L, 2 * Fh, BF16, tm)] + [_oacc(1, Fh) for _ in range(4)], n_acc=4)


def _s5_build(a_re, a_im, log_dt, b_re, b_im, c_re, c_im, d):
    T = CHUNK
    G, P = a_re.shape
    H = d.shape[1]
    hi = lax.Precision.HIGH
    dt = jnp.exp(log_dt)[:, None]
    mag = jnp.exp(a_re * dt)
    ab_re = mag * jnp.cos(a_im * dt)
    ab_im = mag * jnp.sin(a_im * dt)
    den = a_re * a_re + a_im * a_im
    f_re = ((ab_re - 1.0) * a_re + ab_im * a_im) / den
    f_im = (ab_im * a_re - (ab_re - 1.0) * a_im) / den
    bb_re = f_re[..., None] * b_re - f_im[..., None] * b_im
    bb_im = f_re[..., None] * b_im + f_im[..., None] * b_re
    tau = jnp.arange(T + 1, dtype=F32)[None, :, None]
    pmag = jnp.exp(tau * (a_re * dt)[:, None, :])
    pang = tau * (a_im * dt)[:, None, :]
    pw_re = pmag * jnp.cos(pang)
    pw_im = pmag * jnp.sin(pang)
    cp_re = c_re[:, :, None, :] * pw_re[:, None] - c_im[:, :, None, :] * pw_im[:, None]
    cp_im = c_re[:, :, None, :] * pw_im[:, None] + c_im[:, :, None, :] * pw_re[:, None]
    cp_cat = jnp.concatenate([cp_re, -cp_im], axis=-1)
    bb_cat = jnp.concatenate([bb_re, bb_im], axis=1)
    kt = jnp.einsum('ghtq,gqk->ghtk', cp_cat[:, :, :T], bb_cat, precision=hi)
    kt = kt.at[:, :, 0, :].add(d[:, :, None] * jnp.eye(H, dtype=F32)[None])
    kp = jnp.concatenate([kt[:, :, ::-1, :], jnp.zeros((G, H, T - 1, H), F32)], axis=2).reshape(G, H, (2 * T - 1) * H)
    m_mat = jnp.stack([kp[:, :, (T - 1 - t) * H:(2 * T - 1 - t) * H] for t in range(T)], axis=1)
    m_mat = m_mat.reshape(G, T * H, T * H)
    pr = jnp.repeat(pw_re[:, :T][:, ::-1].transpose(0, 2, 1), H, axis=2)
    pi = jnp.repeat(pw_im[:, :T][:, ::-1].transpose(0, 2, 1), H, axis=2)
    br, bi = jnp.tile(bb_re, (1, 1, T)), jnp.tile(bb_im, (1, 1, T))
    n_mat = jnp.concatenate([pr * br - pi * bi, pr * bi + pi * br], axis=1)
    o_mat = cp_cat[:, :, 1:].transpose(0, 2, 1, 3).reshape(G, T * H, 2 * P)
    lam1 = jnp.concatenate([pw_re[:, T], pw_re[:, T]], axis=-1)[:, None, :]
    lam2 = jnp.concatenate([-pw_im[:, T], pw_im[:, T]], axis=-1)[:, None, :]
    return m_mat, n_mat, o_mat, lam1, lam2


def _chunk_scan(name, v, lam1, lam2, reverse, s_fwd=None, gb=32):
    G, nc, W = v.shape
    gb = _tile(G, gb, 1)
    half = W // 2
    ntile = nc // 8

    def body(*refs):
        if reverse:
            v_ref, l1_ref, l2_ref, s_ref, o_ref, d1_ref, d2_ref = refs
        else:
            v_ref, l1_ref, l2_ref, o_ref = refs
        l1 = jnp.broadcast_to(l1_ref[...], (gb, 8, W))
        l2 = jnp.broadcast_to(l2_ref[...], (gb, 8, W))
        if reverse:
            l2 = -l2
        row = lax.broadcasted_iota(jnp.int32, (gb, 8, W), 1)

        def tile_step(n, carry):
            if reverse:
                st, a1, a2 = carry
                base = pl.multiple_of((ntile - 1 - n) * 8, 8)
            else:
                st = carry
                base = pl.multiple_of(n * 8, 8)
            vt = v_ref[:, pl.ds(base, 8), :]
            out = jnp.zeros((gb, 8, W), F32)
            order = range(7, -1, -1) if reverse else range(8)
            for r in order:
                out = jnp.where(row == r, st, out)
                vr = jnp.broadcast_to(vt[:, r:r + 1, :], (gb, 8, W))
                st = l1 * st + l2 * pltpu.roll(st, half, 2) + vr
            o_ref[:, pl.ds(base, 8), :] = out
            if reverse:
                sv = s_ref[:, pl.ds(base, 8), :]
                a1 = a1 + out * sv
                a2 = a2 + out * pltpu.roll(sv, half, 2)
                return st, a1, a2
            return st

        zero = jnp.zeros((gb, 8, W), F32)
        if reverse:
            _, a1, a2 = lax.fori_loop(0, ntile, tile_step, (zero, zero, zero))
            d1_ref[...] = jnp.sum(a1, axis=1, keepdims=True)
            d2_ref[...] = jnp.sum(a2, axis=1, keepdims=True)
        else:
            lax.fori_loop(0, ntile, tile_step, zero)

    big = pl.BlockSpec((gb, nc, W), lambda g: (g, 0, 0))
    vec = pl.BlockSpec((gb, 1, W), lambda g: (g, 0, 0))
    if reverse:
        return _pcall(body, name=name, grid=(G // gb,), in_specs=[big, vec, vec, big],
                      out_specs=[big, vec, vec],
                      out_shape=[jax.ShapeDtypeStruct((G, nc, W), F32), jax.ShapeDtypeStruct((G, 1, W), F32),
                                 jax.ShapeDtypeStruct((G, 1, W), F32)],
                      compiler_params=_params(("parallel",)))(v, lam1, lam2, s_fwd)
    return _pcall(body, name=name, grid=(G // gb,), in_specs=[big, vec, vec], out_specs=big,
                  out_shape=jax.ShapeDtypeStruct((G, nc, W), F32),
                  compiler_params=_params(("parallel",)))(v, lam1, lam2)


_GELU_C = math.sqrt(2.0 / math.pi)


def _gelu(y):
    return 0.5 * y * (1.0 + jnp.tanh(_GELU_C * (y + 0.044715 * y * y * y)))


def _gelu_grad(y):
    t = jnp.tanh(_GELU_C * (y + 0.044715 * y * y * y))
    return 0.5 * (1.0 + t) + 0.5 * y * (1.0 - t * t) * _GELU_C * (1.0 + 3.0 * 0.044715 * y * y)


def _lane_perm(Hg):
    gpb = LANES // Hg
    r = jnp.arange(CHUNK * LANES)
    t, gl, h = r // LANES, (r % LANES) // Hg, r % Hg
    target = gl * (CHUNK * Hg) + t * Hg + h
    return (target[:, None] == r[None, :]).astype(BF16), gpb


def _to_groups(name, a, G, Hg, out_dtype, gelu_arg=None):
    L, D = a.shape
    nc = L // CHUNK
    perm, gpb = _lane_perm(Hg)
    gw = CHUNK * Hg

    def body(*refs):
        x_ref, p_ref, o_ref = refs[0], refs[1], refs[-1]
        x = jnp.concatenate([x_ref[:, t, :] for t in range(CHUNK)], axis=1)
        if x.dtype == BF16:
            z = jnp.dot(x, p_ref[...], preferred_element_type=F32)
        else:
            z = _split_dot(x, p_ref[...])
        for gl in range(gpb):
            piece = z[:, gl * gw:(gl + 1) * gw]
            if gelu_arg is not None:
                piece = piece * _gelu_grad(refs[2][gl])
            o_ref[gl] = piece.astype(o_ref.dtype)

    grouped = pl.BlockSpec((gpb, nc, gw), lambda g: (g, 0, 0))
    return _pcall(
        body, name=name, grid=(G // gpb,),
        in_specs=[pl.BlockSpec((nc, CHUNK, LANES), lambda g: (0, 0, g)),
                  pl.BlockSpec(perm.shape, lambda g: (0, 0))] + ([grouped] if gelu_arg is not None else []),
        out_specs=grouped,
        out_shape=jax.ShapeDtypeStruct((G, nc, gw), out_dtype),
        compiler_params=_params(("parallel",)),
    )(*([a.reshape(nc, CHUNK, D), perm] + ([gelu_arg] if gelu_arg is not None else [])))


def _split_dot(x, u):
    hi = x.astype(BF16)
    lo = (x - hi.astype(F32)).astype(BF16)
    return (jnp.dot(hi, u, preferred_element_type=F32) + jnp.dot(lo, u, preferred_element_type=F32))


def _col_to_row(col, n):
    return jnp.broadcast_to(col, (n, LANES)).T[0:1, :]


def _row_to_col(row, n):
    return jnp.broadcast_to(row, (LANES, n)).T[:, 0:1]


def _head_in(pair, h, upper):
    x = pair.astype(F32)
    lane = lax.broadcasted_iota(jnp.int32, x.shape, 1)
    if (h % 2 == 1) != upper:
        x = pltpu.roll(x, HEAD_DIM, 1)
    keep = (lane >= HEAD_DIM) if upper else (lane < HEAD_DIM)
    return jnp.where(keep, x, 0.0).astype(BF16)


def _pair_out(even, odd, upper):
    lane = lax.broadcasted_iota(jnp.int32, even.shape, 1)
    if upper:
        return jnp.where(lane < HEAD_DIM, pltpu.roll(even, HEAD_DIM, 1), odd)
    return jnp.where(lane < HEAD_DIM, even, pltpu.roll(odd, HEAD_DIM, 1))


def _attn_fwd(name, q, kv, tq, tk, hb):
    L = q.shape[0]
    H = q.shape[1] // HEAD_DIM
    hb = min(hb, H)
    dw = 2 * HEAD_DIM
    nq, nkb = L // tq, L // tk
    hs = range(hb)
    assert tk % tq == 0 and hb % 2 == 0 and H % hb == 0

    def body(q_ref, kv_ref, o_ref, rc_ref):
        i = pl.program_id(1)
        diff = (lax.broadcasted_iota(jnp.int32, (tq, tk), 1) - lax.broadcasted_iota(jnp.int32, (tq, tk), 0))
        u_suf = (lax.broadcasted_iota(jnp.int32, (tk, tk), 0)
                 > lax.broadcasted_iota(jnp.int32, (tk, tk), 1)).astype(BF16)
        nfull = (i * tq) // tk
        rc_ref[...] = jnp.zeros_like(rc_ref)
        qw = [_head_in(q_ref[:, dw * (h // 2):dw * (h // 2 + 1)], h, False) for h in hs]

        def block(kb, carry, masked):
            runs, accs = carry
            ks = pl.multiple_of(kb * tk, tk)
            kvb = [kv_ref[pl.ds(ks, tk), dw * h:dw * (h + 1)] for h in hs]
            z = [lax.dot_general(qw[h], kvb[h], (((1,), (1,)), ((), ())), preferred_element_type=F32)
                 for h in hs]
            e = [jnp.exp(-jnp.abs(z[h])) for h in hs]
            sp = [jnp.maximum(z[h], 0.0) + jnp.log(1.0 + e[h]) for h in hs]
            if masked:
                causal = diff < (i * tq - ks)
                lom = [jnp.where(causal, -sp[h], 0.0) for h in hs]
            else:
                lom = [-sp[h] for h in hs]
            rem = [runs[h] + _split_dot(lom[h], u_suf) for h in hs]
            w = [jnp.exp(z[h] - sp[h] + rem[h]) for h in hs]
            if masked:
                w = [jnp.where(causal, w[h], 0.0) for h in hs]
            accs = tuple(accs[h] + jnp.dot(w[h].astype(BF16), kvb[h], preferred_element_type=F32) for h in hs)
            for h in hs:
                rc_ref[h, kb] = _col_to_row(runs[h], tq)
            runs = tuple(runs[h] + jnp.sum(lom[h], axis=1, keepdims=True) for h in hs)
            return runs, accs

        init = (tuple(jnp.zeros((tq, 1), F32) for _ in hs), tuple(jnp.zeros((tq, dw), F32) for _ in hs))
        carry = block(nfull, init, True)
        _, accs = lax.fori_loop(0, nfull, lambda n, c: block(nfull - 1 - n, c, False), carry)
        for p in range(hb // 2):
            o_ref[:, dw * p:dw * (p + 1)] = _pair_out(accs[2 * p], accs[2 * p + 1], True).astype(o_ref.dtype)

    qspec = pl.BlockSpec((tq, hb * HEAD_DIM), lambda h, i: (i, h))
    kspec = pl.BlockSpec((L, hb * dw), lambda h, i: (0, h), pipeline_mode=pl.Buffered(1))
    return _pcall(
        body, name=name, grid=(H // hb, nq), in_specs=[qspec, kspec],
        out_specs=[qspec, pl.BlockSpec((hb, None, nkb, 1, tq), lambda h, i: (h, i, 0, 0, 0))],
        out_shape=[jax.ShapeDtypeStruct((L, H * HEAD_DIM), BF16), jax.ShapeDtypeStruct((H, nq, nkb, 1, tq), F32)],
        compiler_params=_params(("parallel", "arbitrary")),
    )(q, kv)


def _attn_bwd(name, q, kv, do, rc, tq, tk, scale, hb):
    L = q.shape[0]
    H = q.shape[1] // HEAD_DIM
    hb = min(hb, H)
    dw_ = 2 * HEAD_DIM
    nq, nkb = L // tq, L // tk
    hs = range(hb)
    assert tk % tq == 0 and hb % 2 == 0 and H % hb == 0

    def body(q_ref, kv_ref, do_ref, rc_ref, dq_ref, dkv_ref):
        i = pl.program_id(1)

        @pl.when(i == 0)
        def _():
            dkv_ref[...] = jnp.zeros_like(dkv_ref)

        diff = (lax.broadcasted_iota(jnp.int32, (tq, tk), 1) - lax.broadcasted_iota(jnp.int32, (tq, tk), 0))
        r_io = lax.broadcasted_iota(jnp.int32, (tk, tk), 0)
        c_io = lax.broadcasted_iota(jnp.int32, (tk, tk), 1)
        u_suf = (r_io > c_io).astype(BF16)
        u_pre = (r_io < c_io).astype(BF16)
        nfull = (i * tq) // tk
        qb = [_head_in(q_ref[:, dw_ * (h // 2):dw_ * (h // 2 + 1)], h, False) for h in hs]
        dob = [_head_in(do_ref[:, dw_ * (h // 2):dw_ * (h // 2 + 1)], h, True) for h in hs]

        def block(kb, carry, masked):
            pres, dqs = carry
            ks = pl.multiple_of(kb * tk, tk)
            kvb = [kv_ref[pl.ds(ks, tk), dw_ * h:dw_ * (h + 1)] for h in hs]
            z = [lax.dot_general(qb[h], kvb[h], (((1,), (1,)), ((), ())), preferred_element_type=F32) for h in hs]
            dw = [lax.dot_general(dob[h], kvb[h], (((1,), (1,)), ((), ())), preferred_element_type=F32)
                  for h in hs]
            e = [jnp.exp(-jnp.abs(z[h])) for h in hs]
            sp = [jnp.maximum(z[h], 0.0) + jnp.log(1.0 + e[h]) for h in hs]
            if masked:
                causal = diff < (i * tq - ks)
                lom = [jnp.where(causal, -sp[h], 0.0) for h in hs]
            else:
                lom = [-sp[h] for h in hs]
            rem = [_row_to_col(rc_ref[h, kb], tq) + _split_dot(lom[h], u_suf) for h in hs]
            logb = [z[h] - sp[h] for h in hs]
            w = [jnp.exp(logb[h] + rem[h]) for h in hs]
            if masked:
                w = [jnp.where(causal, w[h], 0.0) for h in hs]
            da = [dw[h] * w[h] for h in hs]
            p = [pres[h] + jnp.dot(da[h].astype(BF16), u_pre, preferred_element_type=F32) for h in hs]
            dz = [da[h] - jnp.exp(logb[h]) * (da[h] + p[h]) for h in hs]
            if masked:
                dz = [jnp.where(causal, dz[h], 0.0) for h in hs]
            dqs = tuple(dqs[h] + jnp.dot(dz[h].astype(BF16), kvb[h], preferred_element_type=F32) for h in hs)
            for h in hs:
                lhs = jnp.concatenate([dz[h].T, w[h].T], axis=1).astype(BF16)
                rhs = jnp.concatenate([qb[h], dob[h]], axis=0)
                dkv_ref[pl.ds(ks, tk), dw_ * h:dw_ * (h + 1)] += jnp.dot(lhs, rhs, preferred_element_type=F32)
            pres = tuple(pres[h] + jnp.sum(da[h], axis=1, keepdims=True) for h in hs)
            return pres, dqs

        init = (tuple(jnp.zeros((tq, 1), F32) for _ in hs), tuple(jnp.zeros((tq, dw_), F32) for _ in hs))
        carry = lax.fori_loop(0, nfull, lambda kb, c: block(kb, c, False), init)
        _, dqs = block(nfull, carry, True)
        for p in range(hb // 2):
            dq_ref[:, dw_ * p:dw_ * (p + 1)] = (_pair_out(dqs[2 * p], dqs[2 * p + 1], False)
                                                * scale).astype(dq_ref.dtype)

    qspec = pl.BlockSpec((tq, hb * HEAD_DIM), lambda h, i: (i, h))
    kspec = pl.BlockSpec((L, hb * dw_), lambda h, i: (0, h), pipeline_mode=pl.Buffered(1))
    return _pcall(
        body, name=name, grid=(H // hb, nq),
        in_specs=[qspec, kspec, qspec, pl.BlockSpec((hb, None, nkb, 1, tq), lambda h, i: (h, i, 0, 0, 0))],
        out_specs=[qspec, kspec],
        out_shape=[jax.ShapeDtypeStruct((L, H * HEAD_DIM), BF16), jax.ShapeDtypeStruct((L, 2 * H * HEAD_DIM), F32)],
        compiler_params=_params(("parallel", "arbitrary")),
    )(q, kv, do, rc)


_MESH = pl.DeviceIdType.MESH
_HBM = pl.BlockSpec(memory_space=pltpu.HBM)


def _all_gather(name, shards):
    n = len(shards)

    def body(*refs):
        x_refs, out_refs = refs[:n], refs[n:2 * n]
        send_sems, recv_sems, local_sems = refs[2 * n:]
        x, y, c = lax.axis_index("x"), lax.axis_index("y"), lax.axis_index("c")
        me, sibling = (x, y, c), (x, y, 1 - c)
        chips = [(1 - x, y), (x, 1 - y), (1 - x, 1 - y)]

        def slot(a, px, py, pc):
            return out_refs[a].at[4 * px + 2 * py + pc]

        def copy(a, k, block, to, src=None):
            return pltpu.make_async_remote_copy(
                src_ref=slot(a, *block) if src is None else src, dst_ref=slot(a, *block),
                send_sem=send_sems.at[7 * a + k], recv_sem=recv_sems.at[7 * a + k], device_id=to,
                device_id_type=_MESH)

        mine = [pltpu.make_async_copy(x_refs[a], slot(a, *me), local_sems.at[a]) for a in range(n)]
        for cp in mine:
            cp.start()
        first = []
        for a in range(n):
            first.append(copy(a, 0, me, sibling, src=x_refs[a]))
            first += [copy(a, 1 + j, me, (*chip, c), src=x_refs[a]) for j, chip in enumerate(chips)]
        for cp in first:
            cp.start()
        passed = []
        for j, chip in enumerate(chips):
            for a in range(n):
                copy(a, 1 + j, (*chip, c), me).wait_recv()
                passed.append(copy(a, 4 + j, (*chip, c), sibling))
                passed[-1].start()
        for a in range(n):
            copy(a, 0, sibling, me).wait_recv()
            for j, chip in enumerate(chips):
                copy(a, 4 + j, (*chip, 1 - c), me).wait_recv()
        for cp in first + passed:
            cp.wait_send()
        for cp in mine:
            cp.wait()

    return _pcall(
        body, name=name, out_shape=[jax.ShapeDtypeStruct((N_DEV,) + s.shape, s.dtype) for s in shards],
        in_specs=[_HBM] * n, out_specs=[_HBM] * n,
        scratch_shapes=[pltpu.SemaphoreType.DMA((7 * n,)), pltpu.SemaphoreType.DMA((7 * n,)),
                        pltpu.SemaphoreType.DMA((n,))],
    )(*shards)


def _sibling_exchange(name, xs):
    n = len(xs)

    def body(*refs):
        x_refs, out_refs, send_sems, recv_sems = refs[:n], refs[n:2 * n], refs[2 * n], refs[2 * n + 1]
        c = lax.axis_index("c")
        sibling = (lax.axis_index("x"), lax.axis_index("y"), 1 - c)
        cps = [pltpu.make_async_remote_copy(src_ref=x_refs[a].at[:, 1 - c], dst_ref=out_refs[a],
                                            send_sem=send_sems.at[a], recv_sem=recv_sems.at[a], device_id=sibling,
                                            device_id_type=_MESH)
               for a in range(n)]
        for cp in cps:
            cp.start()
        for cp in cps:
            cp.wait()

    return _pcall(
        body, name=name, out_shape=[jax.ShapeDtypeStruct(x.shape[:1] + x.shape[2:], x.dtype) for x in xs],
        in_specs=[_HBM] * n, out_specs=[_HBM] * n,
        scratch_shapes=[pltpu.SemaphoreType.DMA((n,)), pltpu.SemaphoreType.DMA((n,))],
    )(*xs)


def _chip_exchange(name, xs):
    n = len(xs)

    def body(*refs):
        x_refs, out_refs, send_sems, recv_sems = refs[:n], refs[n:2 * n], refs[2 * n], refs[2 * n + 1]
        mx, my, mc = lax.axis_index("x"), lax.axis_index("y"), lax.axis_index("c")
        chips = [(1 - mx, my), (mx, 1 - my), (1 - mx, 1 - my)]
        cps = [pltpu.make_async_remote_copy(src_ref=x_refs[a].at[2 * px + py], dst_ref=out_refs[a].at[j],
                                            send_sem=send_sems.at[3 * a + j], recv_sem=recv_sems.at[3 * a + j],
                                            device_id=(px, py, mc), device_id_type=_MESH)
               for a in range(n) for j, (px, py) in enumerate(chips)]
        for cp in cps:
            cp.start()
        for cp in cps:
            cp.wait()

    return _pcall(
        body, name=name, out_shape=[jax.ShapeDtypeStruct((3,) + x.shape[1:], x.dtype) for x in xs],
        in_specs=[_HBM] * n, out_specs=[_HBM] * n,
        scratch_shapes=[pltpu.SemaphoreType.DMA((3 * n,)), pltpu.SemaphoreType.DMA((3 * n,))],
    )(*xs)


def _pair_sum(name, a, b):
    nb, R, C = a.shape
    tm = _tile(R, 512, 16)

    def fn(i, ab, bb):
        s = ab.astype(F32) + bb.astype(F32)
        return s, s

    spec = lambda arr: (arr, (1, tm, C), lambda i: (i // (R // tm), i % (R // tm), 0))
    out = lambda dt: ((nb, R, C), dt, (1, tm, C), lambda i: (i // (R // tm), i % (R // tm), 0))
    return _rowwise(name, fn, nb * (R // tm), [spec(a), spec(b)], [out(F32), out(a.dtype)])


def _sum8(name, g):
    n, R, C = g.shape
    tm = _tile(R, 256)

    def fn(i, gb):
        s = gb[0]
        for d in range(1, n):
            s = s + gb[d]
        return (s,)

    return _rowwise(name, fn, R // tm, [(g, (n, tm, C), lambda i: (0, i, 0))], [_orows(R, C, F32, tm)])[0]


def _adamw(name, w, m, v, grads, tm=512):
    R, C = w.shape
    tm = _tile(R, tm, 16)
    ng = len(grads)

    def fn(i, wb, mb, vb, *gs):
        g = gs[0].astype(F32)
        for t in gs[1:]:
            g = g + t.astype(F32)
        mn = ADAM_B1 * mb + (1.0 - ADAM_B1) * g
        vn = ADAM_B2 * vb + (1.0 - ADAM_B2) * (g * g)
        m_hat = mn / (1.0 - ADAM_B1 ** ADAM_STEP)
        v_hat = vn / (1.0 - ADAM_B2 ** ADAM_STEP)
        delta = -ADAM_LR * (m_hat / (jnp.sqrt(v_hat) + ADAM_EPS) + ADAM_WD * wb)
        return g, delta, mn, vn

    ins = [_rows(w, tm), _rows(m, tm), _rows(v, tm)] + [_rows(g, tm) for g in grads]
    return _rowwise(name, fn, R // tm, ins, [_orows(R, C, F32, tm) for _ in range(4)])


CLASSES = [
    (BF16, True, [('ssm_w_in', None), ('attn_w_q', None), ('attn_w_o', None), ('ffn_w_down', 0), ('ffn_w_down', 1)]),
    (BF16, False, [('ssm_w_glu', None), ('kv_w', None)]),
    (BF16, False, [('ffn_w_up', 0), ('ffn_w_up', 1)]),
    (F32, False, [('ffn_conv_w', 0), ('ffn_conv_w', 1)]),
]


def _shard2d(a, name, layer):
    if name == 'ffn_conv_w':
        return a[layer, :, 0, :]
    if name == 'kv_w':
        return a
    return a[0 if layer is None else layer]


def _weights_of(members):
    names = []
    for n, _ in members:
        if not names or names[-1] != n:
            names.append(n)
    return names


def _pack_class(d, members, dtype):
    width = _shard2d(d[members[0][0]], *members[0]).shape[1]
    parts = [d[n].reshape(-1, width).astype(dtype) for n in _weights_of(members)]
    return parts[0] if len(parts) == 1 else jnp.concatenate(parts, axis=0)


def _full_weights(gaths, d):
    w = {}
    for (_, row_sharded, members), g in zip(CLASSES, gaths):
        off = 0
        for n, l in members:
            r = _shard2d(d[n], n, l).shape[0]
            blk = g[:, off:off + r]
            if row_sharded:
                w[(n, l)] = blk.reshape(N_DEV * r, g.shape[2])
            else:
                w[(n, l)] = blk.transpose(1, 0, 2).reshape(r, N_DEV * g.shape[2])
            off += r
    return w


def _grad_classes(gw):
    out = []
    for dtype, row_sharded, members in CLASSES:
        parts = []
        for n, l in members:
            g = gw[(n, l)].astype(dtype)
            if row_sharded:
                parts.append(g.reshape(N_DEV, g.shape[0] // N_DEV, g.shape[1]))
            else:
                parts.append(g.reshape(g.shape[0], N_DEV, g.shape[1] // N_DEV).transpose(1, 0, 2))
        out.append(jnp.concatenate(parts, axis=1))
    return out


def _unpack_class(buf, members, shapes):
    out, off = {}, 0
    for n in _weights_of(members):
        r = math.prod(shapes[n]) // buf.shape[1]
        out[n] = buf[off:off + r].reshape(shapes[n])
        off += r
    return out


def _small_layout(shapes, D):
    lay, off = {}, 0
    for n in SMALL:
        r = -(-math.prod(shapes[n]) // D)
        lay[n] = (off, r)
        off += r
    lay['loss'] = (off, 1)
    off += 1
    return lay, -(-off // 8) * 8


def _pack_small(d, lay, total, D):
    parts = []
    for n in SMALL + ['loss']:
        if n not in d:
            parts.append(jnp.zeros((lay[n][1], D), F32))
            continue
        flat = d[n].reshape(-1).astype(F32)
        parts.append(jnp.pad(flat, (0, lay[n][1] * D - flat.shape[0])).reshape(lay[n][1], D))
    used = sum(lay[n][1] for n in SMALL + ['loss'])
    if total > used:
        parts.append(jnp.zeros((total - used, D), F32))
    return jnp.concatenate(parts, axis=0)


def _unpack_small(pack, lay, shapes, D):
    out = {}
    for n in SMALL:
        off, r = lay[n]
        out[n] = pack[off:off + r].reshape(-1)[:math.prod(shapes[n])].reshape(shapes[n])
    return out


def _ffn_fwd(tag, x, g_norm, w_up, conv_w, conv_b, w_down):
    h = _rms_fwd(f"{tag}_norm", x, g_norm)
    gu = _mm(f"{tag}_up", h, w_up, 'nn', BF16)
    a = _gate_fwd(f"{tag}_gate", gu, conv_w, conv_b)
    return _mm(f"{tag}_down", a, w_down, 'nn', F32, resid=x), (h, gu, a)


def _ffn_bwd(tag, dres, dres_bf, x, saved, g_norm, w_up, conv_w, conv_b, w_down):
    h, gu, a = saved
    da = _mm(f"{tag}_dgate", dres_bf, w_down, 'nt', BF16)
    d_w_down = _mm(f"{tag}_dwdown", a, dres_bf, 'tn', BF16)
    dgu, dw0, dw1, dw2, dcb = _gate_bwd(f"{tag}_gate_b", gu, da, conv_w, conv_b)
    d_w_up = _mm(f"{tag}_dwup", h, dgu, 'tn', BF16)
    dh = _mm(f"{tag}_dh", dgu, w_up, 'nt', F32)
    dres, dres_bf, dg = _rms_bwd(f"{tag}_norm_b", x, dres, [(g_norm, dh)])
    return dres, dres_bf, dg, d_w_up, jnp.concatenate([dw0, dw1, dw2], axis=0), dcb[0], d_w_down


def kernel(x, norm_mix, norm_ffn, norm_kv, norm_final, ssm_w_in, ssm_a_re, ssm_a_im, ssm_log_dt, ssm_b_re, ssm_b_im, ssm_c_re, ssm_c_im, ssm_d, ssm_w_glu, kv_w, attn_w_q, attn_w_o, ffn_w_up, ffn_conv_w, ffn_conv_b, ffn_w_down, loss_target, m_norm_mix, m_norm_ffn, m_norm_kv, m_norm_final, m_ssm_w_in, m_ssm_a_re, m_ssm_a_im, m_ssm_log_dt, m_ssm_b_re, m_ssm_b_im, m_ssm_c_re, m_ssm_c_im, m_ssm_d, m_ssm_w_glu, m_kv_w, m_attn_w_q, m_attn_w_o, m_ffn_w_up, m_ffn_conv_w, m_ffn_conv_b, m_ffn_w_down, v_norm_mix, v_norm_ffn, v_norm_kv, v_norm_final, v_ssm_w_in, v_ssm_a_re, v_ssm_a_im, v_ssm_log_dt, v_ssm_b_re, v_ssm_b_im, v_ssm_c_re, v_ssm_c_im, v_ssm_d, v_ssm_w_glu, v_kv_w, v_attn_w_q, v_attn_w_o, v_ffn_w_up, v_ffn_conv_w, v_ffn_conv_b, v_ffn_w_down):
    wts = dict(zip(W_NAMES, (norm_mix, norm_ffn, norm_kv, norm_final, ssm_w_in, ssm_a_re, ssm_a_im, ssm_log_dt,
                             ssm_b_re, ssm_b_im, ssm_c_re, ssm_c_im, ssm_d, ssm_w_glu, kv_w, attn_w_q, attn_w_o,
                             ffn_w_up, ffn_conv_w, ffn_conv_b, ffn_w_down)))
    mom = dict(zip(W_NAMES, (m_norm_mix, m_norm_ffn, m_norm_kv, m_norm_final, m_ssm_w_in, m_ssm_a_re, m_ssm_a_im,
                             m_ssm_log_dt, m_ssm_b_re, m_ssm_b_im, m_ssm_c_re, m_ssm_c_im, m_ssm_d, m_ssm_w_glu,
                             m_kv_w, m_attn_w_q, m_attn_w_o, m_ffn_w_up, m_ffn_conv_w, m_ffn_conv_b, m_ffn_w_down)))
    vel = dict(zip(W_NAMES, (v_norm_mix, v_norm_ffn, v_norm_kv, v_norm_final, v_ssm_w_in, v_ssm_a_re, v_ssm_a_im,
                             v_ssm_log_dt, v_ssm_b_re, v_ssm_b_im, v_ssm_c_re, v_ssm_c_im, v_ssm_d, v_ssm_w_glu,
                             v_kv_w, v_attn_w_q, v_attn_w_o, v_ffn_w_up, v_ffn_conv_w, v_ffn_conv_b, v_ffn_w_down)))
    shapes = {n: wts[n].shape for n in W_NAMES}
    _, L, D = x.shape
    Fh = ffn_conv_b.shape[1]
    G, P = ssm_a_re.shape[1], ssm_a_re.shape[2]
    Hg = ssm_d.shape[2]
    x0 = x[0]
    target = loss_target[0]
    scale = HEAD_DIM ** -0.5

    gaths = _all_gather("gather_weights", [_pack_class(wts, members, dt) for dt, _, members in CLASSES])
    w = _full_weights(gaths, wts)
    nm = [norm_mix[l:l + 1] for l in range(2)]
    nf = [norm_ffn[l:l + 1] for l in range(2)]
    nkv = norm_kv[None]
    nfin = norm_final[None]
    cb = [ffn_conv_b[l:l + 1] for l in range(2)]
    cwt = [w[('ffn_conv_w', l)] for l in range(2)]

    s5p = (ssm_a_re[0], ssm_a_im[0], ssm_log_dt[0], ssm_b_re[0], ssm_b_im[0], ssm_c_re[0], ssm_c_im[0], ssm_d[0])
    (m_mat, n_mat, o_mat, lam1, lam2), s5_vjp = jax.vjp(_s5_build, *s5p)
    m_bf, n_bf, o_bf = m_mat.astype(BF16), n_mat.astype(BF16), o_mat.astype(BF16)

    h0 = _rms_fwd("l0_norm", x0, nm[0])
    u = _mm("l0_win", h0, w[('ssm_w_in', None)], 'nn', BF16)
    ug = _to_groups("s5_group_u", u, G, Hg, BF16)
    vloc = _bmm("s5_local_state", [(ug, n_bf, 'nt')], F32)
    st = _chunk_scan("s5_scan", vloc, lam1, lam2, reverse=False)
    yraw, yg = _bmm("s5_out", [(ug, m_bf, 'nt'), (st, o_bf, 'nt')], (F32, BF16),
                    post=lambda acc: (acc, _gelu(acc)), ungroup=(1, Hg))
    z = _mm("l0_wglu", yg, w[('ssm_w_glu', None)], 'nn', F32)
    x1 = _glu_fwd("l0_glu", x0, z)
    x2, ffn0 = _ffn_fwd("f0", x1, nf[0], w[('ffn_w_up', 0)], cwt[0], cb[0], w[('ffn_w_down', 0)])

    hkv = _rms_fwd("kv_norm", x2, nkv)
    nh = D // HEAD_DIM
    w_kv = w[('kv_w', None)].reshape(D, 2, nh, HEAD_DIM).transpose(0, 2, 1, 3).reshape(D, 2 * D)
    kvp = _mm("kv_proj", hkv, w_kv, 'nn', BF16)
    h2 = _rms_fwd("l1_norm", x2, nm[1])
    qn = _mm("l1_wq", h2, w[('attn_w_q', None)], 'nn', BF16, scale=scale)
    o, rc = _attn_fwd("attn_fwd", qn, kvp, ATT_TQ, ATT_TK, ATT_HB_FWD)
    x3 = _mm("l1_wo", o, w[('attn_w_o', None)], 'nn', F32, resid=x2)
    x4, ffn1 = _ffn_fwd("f1", x3, nf[1], w[('ffn_w_up', 1)], cwt[1], cb[1], w[('ffn_w_down', 1)])

    dres, dres_bf, dg_final, loss_part = _final_loss("loss_head", x4, nfin, target)

    gw = {}
    dres, dres_bf, dg_nf1, gup1, gcw1, gcb1, gdn1 = _ffn_bwd("f1", dres, dres_bf, x3, ffn1, nf[1], w[('ffn_w_up', 1)],
                                                             cwt[1], cb[1], w[('ffn_w_down', 1)])
    do = _mm("l1_do", dres_bf, w[('attn_w_o', None)], 'nt', BF16)
    gw[('attn_w_o', None)] = _mm("l1_dwo", o, dres_bf, 'tn', BF16)
    dq, dkv = _attn_bwd("attn_bwd", qn, kvp, do, rc, ATT_TQ, ATT_TK, scale, ATT_HB_BWD)
    gw[('attn_w_q', None)] = _mm("l1_dwq", h2, dq, 'tn', BF16)
    dh2 = _mm("l1_dh", dq, w[('attn_w_q', None)], 'nt', F32)
    g_kv = _mm("kv_dw", hkv, dkv, 'tn', BF16)
    gw[('kv_w', None)] = g_kv.reshape(D, nh, 2, HEAD_DIM).transpose(0, 2, 1, 3).reshape(D, 2 * D)
    dhkv = _mm("kv_dh", dkv, w_kv, 'nt', F32)
    dres, dres_bf, dg_nm1, dg_nkv = _rms_bwd("l1_norm_b", x2, dres, [(nm[1], dh2), (nkv, dhkv)])
    dres, dres_bf, dg_nf0, gup0, gcw0, gcb0, gdn0 = _ffn_bwd("f0", dres, dres_bf, x1, ffn0, nf[0], w[('ffn_w_up', 0)],
                                                             cwt[0], cb[0], w[('ffn_w_down', 0)])
    dz = _glu_bwd("l0_glu_b", dres, z)
    gw[('ssm_w_glu', None)] = _mm("l0_dwglu", yg, dz, 'tn', BF16)
    dyg = _mm("l0_dyg", dz, w[('ssm_w_glu', None)], 'nt', F32)
    dy = _to_groups("s5_group_dy", dyg, G, Hg, BF16, gelu_arg=yraw)
    ds = _bmm("s5_dstate", [(dy, o_bf, 'nn')], F32)
    dv_loc, dlam1, dlam2 = _chunk_scan("s5_scan_b", ds, lam1, lam2, reverse=True, s_fwd=st)
    du = _bmm("s5_du", [(dy, m_bf, 'nn'), (dv_loc, n_bf, 'nn')], BF16, ungroup=(0, Hg))
    d_m = _bmm("s5_dm", [(dy, ug, 'tn')], F32)
    d_o = _bmm("s5_do", [(dy, st, 'tn')], F32)
    d_n = _bmm("s5_dn", [(dv_loc, ug, 'tn')], F32)
    s5g = s5_vjp((d_m, d_n, d_o, dlam1, dlam2))
    gw[('ssm_w_in', None)] = _mm("l0_dwin", h0, du, 'tn', BF16)
    dh0 = _mm("l0_dh", du, w[('ssm_w_in', None)], 'nt', F32)
    grad_x, _, dg_nm0 = _rms_bwd("l0_norm_b", x0, dres, [(nm[0], dh0)])

    gw.update({('ffn_w_up', 0): gup0, ('ffn_w_up', 1): gup1, ('ffn_w_down', 0): gdn0, ('ffn_w_down', 1): gdn1,
               ('ffn_conv_w', 0): gcw0, ('ffn_conv_w', 1): gcw1})

    small_g = {
        'norm_mix': jnp.concatenate([dg_nm0, dg_nm1], axis=0), 'norm_ffn': jnp.concatenate([dg_nf0, dg_nf1], axis=0),
        'norm_kv': dg_nkv, 'norm_final': dg_final, 'ffn_conv_b': jnp.stack([gcb0, gcb1]),
        'ssm_a_re': s5g[0], 'ssm_a_im': s5g[1], 'ssm_log_dt': s5g[2], 'ssm_b_re': s5g[3], 'ssm_b_im': s5g[4],
        'ssm_c_re': s5g[5], 'ssm_c_im': s5g[6], 'ssm_d': s5g[7], 'loss': loss_part[0, 0:1],
    }
    lay, rs = _small_layout(shapes, D)
    small_sum = _sum8("small_sum", _all_gather("gather_small", [_pack_small(small_g, lay, rs, D)])[0])
    loss = small_sum[lay['loss'][0], 0]
    sg, sdelta, sm, sv = _adamw("adamw_small", _pack_small(wts, lay, rs, D), _pack_small(mom, lay, rs, D),
                                _pack_small(vel, lay, rs, D), [small_sum])
    small_out = [_unpack_small(t, lay, shapes, D) for t in (sg, sdelta, sm, sv)]

    c = lax.axis_index("c")
    chip = 2 * lax.axis_index("x") + lax.axis_index("y")
    gcls = [g.reshape((4, 2) + g.shape[1:]) for g in _grad_classes(gw)]
    mine = [lax.dynamic_index_in_dim(g, c, axis=1, keepdims=False) for g in gcls]
    got = _sibling_exchange("rs_sibling", gcls)
    sums = [_pair_sum(f"rs_pair_sum{k}", a, b) for k, (a, b) in enumerate(zip(mine, got))]
    from_chips = _chip_exchange("rs_chips", [on_wire for _, on_wire in sums])
    big_parts = [{}, {}, {}, {}]
    for k, ((_, _, members), (chip_f32, _), fc) in enumerate(zip(CLASSES, sums, from_chips)):
        own = lax.dynamic_index_in_dim(chip_f32, chip, axis=0, keepdims=False)
        res = _adamw(f"adamw_big{k}", _pack_class(wts, members, F32), _pack_class(mom, members, F32),
                     _pack_class(vel, members, F32), [own, fc[0], fc[1], fc[2]])
        for q in range(4):
            big_parts[q].update(_unpack_class(res[q], members, shapes))
    big_out = big_parts

    outs = [loss, grad_x[None]]
    for k in range(4):
        for n in W_NAMES:
            outs.append(big_out[k][n] if n in BIG else small_out[k][n])
    return tuple(outs)
```

```python
import math

import jax
import jax.numpy as jnp
from jax import lax
from jax.experimental import pallas as pl
from jax.experimental.pallas import tpu as pltpu

F32 = jnp.float32
BF16 = jnp.bfloat16

EPS = 1e-6
HEAD_DIM = 64
CHUNK = 16
N_DEV = 8
ADAM_LR = 0.001
ADAM_B1 = 0.9
ADAM_B2 = 0.999
ADAM_EPS = 1e-08
ADAM_WD = 0.01
ADAM_STEP = 10
VMEM_LIMIT = 48 * 1024 * 1024
MM_BLOCK_BYTES = 28 * 1024 * 1024
MM_TM_MAX = 1024
MM_T_MAX = 512
MM_TN_MAX = 1536
LANES = 128
ATT_TQ = 128
ATT_TK = 256
ATT_HB_FWD = 16
ATT_HB_BWD = 8

W_NAMES = ['norm_mix', 'norm_ffn', 'norm_kv', 'norm_final', 'ssm_w_in', 'ssm_a_re', 'ssm_a_im', 'ssm_log_dt',
           'ssm_b_re', 'ssm_b_im', 'ssm_c_re', 'ssm_c_im', 'ssm_d', 'ssm_w_glu', 'kv_w', 'attn_w_q', 'attn_w_o',
           'ffn_w_up', 'ffn_conv_w', 'ffn_conv_b', 'ffn_w_down']
BIG = ['ssm_w_in', 'ssm_w_glu', 'kv_w', 'attn_w_q', 'attn_w_o', 'ffn_w_up', 'ffn_w_down', 'ffn_conv_w']
SMALL = [n for n in W_NAMES if n not in BIG]


def _pcall(body, **kw):
    return pl.pallas_call(body, **kw)


def _params(sem=None):
    if sem is None:
        return pltpu.CompilerParams(vmem_limit_bytes=VMEM_LIMIT)
    return pltpu.CompilerParams(dimension_semantics=sem, vmem_limit_bytes=VMEM_LIMIT)


def _tile(n, pref, mult=8):
    best = None
    for t in range(mult, min(n, pref) + 1, mult):
        if n % t == 0:
            best = t
    return n if best is None else best


def _mm_tiles(M, N, K, mode, a_bytes, b_bytes, o_bytes):
    def cands(n, cap):
        c = [t for t in range(LANES, min(n, cap) + 1, LANES) if n % t == 0]
        return c or [n]
    best = None
    for tm in cands(M, MM_T_MAX if mode == 'tn' else MM_TM_MAX):
        for tn in cands(N, MM_TN_MAX):
            need = 2 * (tm * K * a_bytes + K * tn * b_bytes + tm * tn * o_bytes)
            score = tm * tn * (tm if mode == 'tn' else 1)
            if need <= MM_BLOCK_BYTES and (best is None or score > best[0]):
                best = (score, tm, tn)
    assert best is not None, (M, N, K)
    return best[1], best[2]


def _mm(name, a, b, mode, out_dtype=F32, scale=None, resid=None):
    if mode == 'nn':
        (M, K), (K2, N) = a.shape, b.shape
    elif mode == 'nt':
        (M, K), (N, K2) = a.shape, b.shape
    else:
        (K, M), (K2, N) = a.shape, b.shape
    assert K == K2, (name, a.shape, b.shape)
    o_bytes = jnp.dtype(out_dtype).itemsize + (resid.dtype.itemsize if resid is not None else 0)
    tm, tn = _mm_tiles(M, N, K, mode, a.dtype.itemsize, b.dtype.itemsize, o_bytes)
    if mode == 'tn':
        a_spec = pl.BlockSpec((K, tm), lambda i, j: (0, i))
    else:
        a_spec = pl.BlockSpec((tm, K), lambda i, j: (i, 0))
    if mode == 'nt':
        b_spec = pl.BlockSpec((tn, K), lambda i, j: (j, 0))
    else:
        b_spec = pl.BlockSpec((K, tn), lambda i, j: (0, j))
    o_spec = pl.BlockSpec((tm, tn), lambda i, j: (i, j))
    dn = {'nn': ((1,), (0,)), 'nt': ((1,), (1,)), 'tn': ((0,), (0,))}[mode]

    def body(*refs):
        a_ref, b_ref, o_ref = refs[0], refs[1], refs[-1]
        acc = lax.dot_general(a_ref[...].astype(BF16), b_ref[...].astype(BF16), (dn, ((), ())),
                              preferred_element_type=F32)
        if scale is not None:
            acc = acc * scale
        if resid is not None:
            acc = acc + refs[2][...]
        o_ref[...] = acc.astype(o_ref.dtype)

    return _pcall(
        body, name=name, grid=(M // tm, N // tn),
        in_specs=[a_spec, b_spec] + ([o_spec] if resid is not None else []),
        out_specs=o_spec,
        out_shape=jax.ShapeDtypeStruct((M, N), out_dtype),
        compiler_params=_params(("parallel", "parallel")),
    )(*([a, b] + ([resid] if resid is not None else [])))


def _bmm(name, terms, out_dtype, gb=8, post=None, ungroup=None):
    G = terms[0][0].shape[0]
    gb = _tile(G, gb, 1)
    dns = {'nn': ((1,), (0,)), 'nt': ((1,), (1,)), 'tn': ((0,), (0,))}

    def oshape(a, b, mode):
        m = a.shape[2] if mode == 'tn' else a.shape[1]
        n = b.shape[1] if mode == 'nt' else b.shape[2]
        return m, n

    m, n = oshape(*terms[0])
    out_dtypes = out_dtype if isinstance(out_dtype, (tuple, list)) else (out_dtype,)
    n_in = 2 * len(terms)
    if ungroup is not None:
        perm, gpb = _lane_perm(ungroup[1])
        assert gpb == gb and n == CHUNK * ungroup[1]

    def body(*refs):
        ins, outs = refs[:n_in], refs[len(refs) - len(out_dtypes):]
        kept = []
        for gi in range(gb):
            acc = None
            for t, (_, _, mode) in enumerate(terms):
                part = lax.dot_general(ins[2 * t][gi].astype(BF16), ins[2 * t + 1][gi].astype(BF16),
                                       (dns[mode], ((), ())), preferred_element_type=F32)
                acc = part if acc is None else acc + part
            vals = (acc,) if post is None else post(acc)
            for k, (o_ref, v) in enumerate(zip(outs, vals)):
                if ungroup is not None and k == ungroup[0]:
                    kept.append(v.astype(BF16))
                else:
                    o_ref[gi] = v.astype(o_ref.dtype)
        if ungroup is not None:
            x = lax.dot_general(jnp.concatenate(kept, axis=1), refs[n_in][...], (((1,), (1,)), ((), ())),
                                preferred_element_type=F32)
            o_ref = outs[ungroup[0]]
            for t in range(CHUNK):
                o_ref[:, t, :] = x[:, t * LANES:(t + 1) * LANES].astype(o_ref.dtype)

    in_specs, args = [], []
    for a, b, _ in terms:
        in_specs += [pl.BlockSpec((gb,) + a.shape[1:], lambda g: (g, 0, 0)),
                     pl.BlockSpec((gb,) + b.shape[1:], lambda g: (g, 0, 0))]
        args += [a, b]
    out_specs = [pl.BlockSpec((gb, m, n), lambda g: (g, 0, 0)) for _ in out_dtypes]
    out_shape = [jax.ShapeDtypeStruct((G, m, n), dt) for dt in out_dtypes]
    if ungroup is not None:
        in_specs.append(pl.BlockSpec(perm.shape, lambda g: (0, 0)))
        args.append(perm)
        out_specs[ungroup[0]] = pl.BlockSpec((m, CHUNK, LANES), lambda g: (0, 0, g))
        out_shape[ungroup[0]] = jax.ShapeDtypeStruct((m, CHUNK, G * ungroup[1]), out_dtypes[ungroup[0]])
    res = _pcall(
        body, name=name, grid=(G // gb,), in_specs=in_specs, out_specs=out_specs, out_shape=out_shape,
        compiler_params=_params(("parallel",)),
    )(*args)
    if ungroup is not None:
        res = list(res)
        res[ungroup[0]] = res[ungroup[0]].reshape(m * CHUNK, G * ungroup[1])
    return res[0] if len(out_dtypes) == 1 else res


def _rowwise(name, fn, n_steps, ins, outs, n_acc=0):
    n_in, n_out = len(ins), len(outs)

    def body(*refs):
        i = pl.program_id(0)
        vals = fn(i, *[r[...] for r in refs[:n_in]])
        o_refs = refs[n_in:]
        for j in range(n_out - n_acc):
            o_refs[j][...] = vals[j].astype(o_refs[j].dtype)
        if n_acc:
            @pl.when(i == 0)
            def _():
                for j in range(n_out - n_acc, n_out):
                    o_refs[j][...] = vals[j].astype(o_refs[j].dtype)

            @pl.when(i > 0)
            def _():
                for j in range(n_out - n_acc, n_out):
                    o_refs[j][...] += vals[j].astype(o_refs[j].dtype)

    res = _pcall(
        body, name=name, grid=(n_steps,),
        in_specs=[pl.BlockSpec(blk, im) for _, blk, im in ins],
        out_specs=[pl.BlockSpec(blk, im) for _, _, blk, im in outs],
        out_shape=[jax.ShapeDtypeStruct(s, d) for s, d, _, _ in outs],
        compiler_params=_params(("arbitrary",)),
    )(*[a for a, _, _ in ins])
    return res


def _rows(a, tm):
    return (a, (tm, a.shape[1]), lambda i: (i, 0))


def _whole(a):
    nd = a.ndim
    return (a, a.shape, lambda i: (0,) * nd)


def _orows(L, n, dtype, tm):
    return ((L, n), dtype, (tm, n), lambda i: (i, 0))


def _oacc(r, n):
    return ((r, n), F32, (r, n), lambda i: (0, 0))


def _rms_fwd(name, x, g, tm=512):
    L, D = x.shape
    tm = _tile(L, tm)

    def fn(i, xb, gb):
        r = lax.rsqrt(jnp.mean(xb * xb, axis=-1, keepdims=True) + EPS)
        return (xb * r * gb,)

    return _rowwise(name, fn, L // tm, [_rows(x, tm), _whole(g)], [_orows(L, D, BF16, tm)])[0]


def _rms_bwd(name, x, dres, branches, tm=256):
    L, D = x.shape
    tm = _tile(L, tm)
    nb = len(branches)

    def fn(i, xb, db, *rest):
        r = lax.rsqrt(jnp.mean(xb * xb, axis=-1, keepdims=True) + EPS)
        xh = xb * r
        dx = db
        dgs = []
        for b in range(nb):
            gb, dyb = rest[2 * b], rest[2 * b + 1].astype(F32)
            dxh = dyb * gb
            dx = dx + r * (dxh - xh * jnp.mean(dxh * xh, axis=-1, keepdims=True))
            dgs.append(jnp.sum(dyb * xh, axis=0, keepdims=True))
        return (dx, dx, *dgs)

    ins = [_rows(x, tm), _rows(dres, tm)]
    for g, dy in branches:
        ins += [_whole(g), _rows(dy, tm)]
    outs = [_orows(L, D, F32, tm), _orows(L, D, BF16, tm)] + [_oacc(1, D) for _ in range(nb)]
    return _rowwise(name, fn, L // tm, ins, outs, n_acc=nb)


def _final_loss(name, x, g, target, tm=256):
    L, D = x.shape
    tm = _tile(L, tm)

    def fn(i, xb, gb, tb):
        r = lax.rsqrt(jnp.mean(xb * xb, axis=-1, keepdims=True) + EPS)
        xh = xb * r
        err = xh * gb - tb
        dy = err * (1.0 / D)
        dxh = dy * gb
        dx = r * (dxh - xh * jnp.mean(dxh * xh, axis=-1, keepdims=True))
        dg = jnp.sum(dy * xh, axis=0, keepdims=True)
        per_row = jnp.mean(err * err, axis=-1, keepdims=True)
        loss = 0.5 * jnp.sum(per_row, axis=0, keepdims=True)
        return dx, dx, dg, jnp.broadcast_to(loss, (1, LANES))

    return _rowwise(name, fn, L // tm, [_rows(x, tm), _whole(g), _rows(target, tm)],
                    [_orows(L, D, F32, tm), _orows(L, D, BF16, tm), _oacc(1, D), _oacc(1, LANES)], n_acc=2)


def _glu_fwd(name, x, z, tm=256):
    L, D = x.shape
    tm = _tile(L, tm)

    def fn(i, xb, zb):
        return (xb + zb[:, :D] * jax.nn.sigmoid(zb[:, D:]),)

    return _rowwise(name, fn, L // tm, [_rows(x, tm), _rows(z, tm)], [_orows(L, D, F32, tm)])[0]


def _glu_bwd(name, dx, z, tm=256):
    L, D = dx.shape
    tm = _tile(L, tm)

    def fn(i, db, zb):
        z1, sg = zb[:, :D], jax.nn.sigmoid(zb[:, D:])
        return (jnp.concatenate([(db * sg).astype(BF16), (db * z1 * sg * (1.0 - sg)).astype(BF16)], axis=1),)

    return _rowwise(name, fn, L // tm, [_rows(dx, tm), _rows(z, tm)], [_orows(L, 2 * D, BF16, tm)])[0]


HALO = 16


def _halo_prev(a, tm):
    return (a, (HALO, a.shape[1]), lambda i: (jnp.maximum(i * (tm // HALO) - 1, 0), 0))


def _halo_next(a, tm):
    last = a.shape[0] // HALO - 1
    return (a, (HALO, a.shape[1]), lambda i: (jnp.minimum((i + 1) * (tm // HALO), last), 0))


def _shift_down(cur, halo, k, first):
    tm = cur.shape[0]
    rolled = pltpu.roll(cur, k, 0)
    tail = pltpu.roll(halo, k, 0)
    tail = jnp.where(first, 0.0, tail)
    row = lax.broadcasted_iota(jnp.int32, (tm, 1), 0)
    head = jnp.concatenate([tail, jnp.zeros((tm - 8, cur.shape[1]), cur.dtype)], axis=0) if tm > 8 else tail
    return jnp.where(row < k, head, rolled)


def _shift_up(cur, halo, k, last):
    tm = cur.shape[0]
    rolled = pltpu.roll(cur, tm - k, 0)
    head = pltpu.roll(halo, 8 - k, 0)
    head = jnp.where(last, 0.0, head)
    row = lax.broadcasted_iota(jnp.int32, (tm, 1), 0)
    tail = jnp.concatenate([jnp.zeros((tm - 8, cur.shape[1]), cur.dtype), head], axis=0) if tm > 8 else head
    return jnp.where(row >= tm - k, tail, rolled)


def _conv_pre(gb, hb, cw, cb, first):
    g1 = _shift_down(gb, hb, 1, first)
    g2 = _shift_down(gb, hb, 2, first)
    return cw[0:1] * g2 + cw[1:2] * g1 + cw[2:3] * gb + cb, g1, g2


def _gate_fwd(name, gu, cw, cb, tm=256):
    L, F2 = gu.shape
    Fh = F2 // 2
    tm = _tile(L, tm)

    def fn(i, gub, halo, cwb, cbb):
        gub, halo = gub.astype(F32), halo.astype(F32)[HALO - 8:]
        gc, _, _ = _conv_pre(gub[:, :Fh], halo[:, :Fh], cwb, cbb, i == 0)
        return (gc * jax.nn.sigmoid(gc) * gub[:, Fh:],)

    return _rowwise(name, fn, L // tm, [_rows(gu, tm), _halo_prev(gu, tm), _whole(cw), _whole(cb)],
                    [_orows(L, Fh, BF16, tm)])[0]


def _gate_bwd(name, gu, da, cw, cb, tm=128):
    L, F2 = gu.shape
    Fh = F2 // 2
    tm = _tile(L, tm)
    n = L // tm

    def dgate(gb, ub, dab, hb, cwb, cbb, first):
        gc, g1, g2 = _conv_pre(gb, hb, cwb, cbb, first)
        sg = jax.nn.sigmoid(gc)
        return dab * ub * sg * (1.0 + gc * (1.0 - sg)), dab * gc * sg, g1, g2

    def fn(i, gub, prev, nxt, dab, dnx, cwb, cbb):
        gub, prev, nxt = gub.astype(F32), prev.astype(F32)[HALO - 8:], nxt.astype(F32)[:8]
        dab, dnx = dab.astype(F32), dnx.astype(F32)[:8]
        gb, ub = gub[:, :Fh], gub[:, Fh:]
        dgc, du, g1, g2 = dgate(gb, ub, dab, prev[:, :Fh], cwb, cbb, i == 0)
        dgc_next, _, _, _ = dgate(nxt[:, :Fh], nxt[:, Fh:], dnx, gb[tm - 8:], cwb, cbb, False)
        d1 = _shift_up(dgc, dgc_next, 1, i == n - 1)
        d2 = _shift_up(dgc, dgc_next, 2, i == n - 1)
        dg = cwb[2:3] * dgc + cwb[1:2] * d1 + cwb[0:1] * d2
        return (jnp.concatenate([dg.astype(BF16), du.astype(BF16)], axis=1),
                jnp.sum(dgc * g2, axis=0, keepdims=True), jnp.sum(dgc * g1, axis=0, keepdims=True),
                jnp.sum(dgc * gb, axis=0, keepdims=True), jnp.sum(dgc, axis=0, keepdims=True))

    return _rowwise(name, fn, n,
                    [_rows(gu, tm), _halo_prev(gu, tm), _halo_next(gu, tm), _rows(da, tm), _halo_next(da, tm),
                     _whole(cw), _whole(cb)],
                    [_orows(L, 2 * Fh, BF16, tm)] + [_oacc(1, Fh) for _ in range(4)], n_acc=4)


def _s5_build(a_re, a_im, log_dt, b_re, b_im, c_re, c_im, d):
    T = CHUNK
    G, P = a_re.shape
    H = d.shape[1]
    hi = lax.Precision.HIGH
    dt = jnp.exp(log_dt)[:, None]
    mag = jnp.exp(a_re * dt)
    ab_re = mag * jnp.cos(a_im * dt)
    ab_im = mag * jnp.sin(a_im * dt)
    den = a_re * a_re + a_im * a_im
    f_re = ((ab_re - 1.0) * a_re + ab_im * a_im) / den
    f_im = (ab_im * a_re - (ab_re - 1.0) * a_im) / den
    bb_re = f_re[..., None] * b_re - f_im[..., None] * b_im
    bb_im = f_re[..., None] * b_im + f_im[..., None] * b_re
    tau = jnp.arange(T + 1, dtype=F32)[None, :, None]
    pmag = jnp.exp(tau * (a_re * dt)[:, None, :])
    pang = tau * (a_im * dt)[:, None, :]
    pw_re = pmag * jnp.cos(pang)
    pw_im = pmag * jnp.sin(pang)
    cp_re = c_re[:, :, None, :] * pw_re[:, None] - c_im[:, :, None, :] * pw_im[:, None]
    cp_im = c_re[:, :, None, :] * pw_im[:, None] + c_im[:, :, None, :] * pw_re[:, None]
    cp_cat = jnp.concatenate([cp_re, -cp_im], axis=-1)
    bb_cat = jnp.concatenate([bb_re, bb_im], axis=1)
    kt = jnp.einsum('ghtq,gqk->ghtk', cp_cat[:, :, :T], bb_cat, precision=hi)
    kt = kt.at[:, :, 0, :].add(d[:, :, None] * jnp.eye(H, dtype=F32)[None])
    kp = jnp.concatenate([kt[:, :, ::-1, :], jnp.zeros((G, H, T - 1, H), F32)], axis=2).reshape(G, H, (2 * T - 1) * H)
    m_mat = jnp.stack([kp[:, :, (T - 1 - t) * H:(2 * T - 1 - t) * H] for t in range(T)], axis=1)
    m_mat = m_mat.reshape(G, T * H, T * H)
    pr = jnp.repeat(pw_re[:, :T][:, ::-1].transpose(0, 2, 1), H, axis=2)
    pi = jnp.repeat(pw_im[:, :T][:, ::-1].transpose(0, 2, 1), H, axis=2)
    br, bi = jnp.tile(bb_re, (1, 1, T)), jnp.tile(bb_im, (1, 1, T))
    n_mat = jnp.concatenate([pr * br - pi * bi, pr * bi + pi * br], axis=1)
    o_mat = cp_cat[:, :, 1:].transpose(0, 2, 1, 3).reshape(G, T * H, 2 * P)
    lam1 = jnp.concatenate([pw_re[:, T], pw_re[:, T]], axis=-1)[:, None, :]
    lam2 = jnp.concatenate([-pw_im[:, T], pw_im[:, T]], axis=-1)[:, None, :]
    return m_mat, n_mat, o_mat, lam1, lam2


def _chunk_scan(name, v, lam1, lam2, reverse, s_fwd=None, gb=64):
    G, nc, W = v.shape
    gb = _tile(G, gb, 1)
    half = W // 2
    ntile = nc // 8

    def body(*refs):
        if reverse:
            v_ref, l1_ref, l2_ref, s_ref, o_ref, d1_ref, d2_ref = refs
        else:
            v_ref, l1_ref, l2_ref, o_ref = refs
        l1 = jnp.broadcast_to(l1_ref[...], (gb, 8, W))
        l2 = jnp.broadcast_to(l2_ref[...], (gb, 8, W))
        if reverse:
            l2 = -l2
        row = lax.broadcasted_iota(jnp.int32, (gb, 8, W), 1)

        def tile_step(n, carry):
            if reverse:
                st, a1, a2 = carry
                base = pl.multiple_of((ntile - 1 - n) * 8, 8)
            else:
                st = carry
                base = pl.multiple_of(n * 8, 8)
            vt = v_ref[:, pl.ds(base, 8), :]
            out = jnp.zeros((gb, 8, W), F32)
            order = range(7, -1, -1) if reverse else range(8)
            for r in order:
                out = jnp.where(row == r, st, out)
                vr = jnp.broadcast_to(vt[:, r:r + 1, :], (gb, 8, W))
                st = l1 * st + l2 * pltpu.roll(st, half, 2) + vr
            o_ref[:, pl.ds(base, 8), :] = out
            if reverse:
                sv = s_ref[:, pl.ds(base, 8), :]
                a1 = a1 + out * sv
                a2 = a2 + out * pltpu.roll(sv, half, 2)
                return st, a1, a2
            return st

        zero = jnp.zeros((gb, 8, W), F32)
        if reverse:
            _, a1, a2 = lax.fori_loop(0, ntile, tile_step, (zero, zero, zero))
            d1_ref[...] = jnp.sum(a1, axis=1, keepdims=True)
            d2_ref[...] = jnp.sum(a2, axis=1, keepdims=True)
        else:
            lax.fori_loop(0, ntile, tile_step, zero)

    big = pl.BlockSpec((gb, nc, W), lambda g: (g, 0, 0), pipeline_mode=pl.Buffered(1))
    vec = pl.BlockSpec((gb, 1, W), lambda g: (g, 0, 0))
    if reverse:
        return _pcall(body, name=name, grid=(G // gb,), in_specs=[big, vec, vec, big],
                      out_specs=[big, vec, vec],
                      out_shape=[jax.ShapeDtypeStruct((G, nc, W), F32), jax.ShapeDtypeStruct((G, 1, W), F32),
                                 jax.ShapeDtypeStruct((G, 1, W), F32)],
                      compiler_params=_params(("parallel",)))(v, lam1, lam2, s_fwd)
    return _pcall(body, name=name, grid=(G // gb,), in_specs=[big, vec, vec], out_specs=big,
                  out_shape=jax.ShapeDtypeStruct((G, nc, W), F32),
                  compiler_params=_params(("parallel",)))(v, lam1, lam2)


_GELU_C = math.sqrt(2.0 / math.pi)


def _gelu(y):
    return 0.5 * y * (1.0 + jnp.tanh(_GELU_C * (y + 0.044715 * y * y * y)))


def _gelu_grad(y):
    t = jnp.tanh(_GELU_C * (y + 0.044715 * y * y * y))
    return 0.5 * (1.0 + t) + 0.5 * y * (1.0 - t * t) * _GELU_C * (1.0 + 3.0 * 0.044715 * y * y)


def _lane_perm(Hg):
    gpb = LANES // Hg
    r = jnp.arange(CHUNK * LANES)
    t, gl, h = r // LANES, (r % LANES) // Hg, r % Hg
    target = gl * (CHUNK * Hg) + t * Hg + h
    return (target[:, None] == r[None, :]).astype(BF16), gpb


def _to_groups(name, a, G, Hg, out_dtype, gelu_arg=None):
    L, D = a.shape
    nc = L // CHUNK
    perm, gpb = _lane_perm(Hg)
    gw = CHUNK * Hg

    def body(*refs):
        x_ref, p_ref, o_ref = refs[0], refs[1], refs[-1]
        x = jnp.concatenate([x_ref[:, t, :] for t in range(CHUNK)], axis=1)
        if x.dtype == BF16:
            z = jnp.dot(x, p_ref[...], preferred_element_type=F32)
        else:
            z = _split_dot(x, p_ref[...])
        for gl in range(gpb):
            piece = z[:, gl * gw:(gl + 1) * gw]
            if gelu_arg is not None:
                piece = piece * _gelu_grad(refs[2][gl])
            o_ref[gl] = piece.astype(o_ref.dtype)

    grouped = pl.BlockSpec((gpb, nc, gw), lambda g: (g, 0, 0))
    return _pcall(
        body, name=name, grid=(G // gpb,),
        in_specs=[pl.BlockSpec((nc, CHUNK, LANES), lambda g: (0, 0, g)),
                  pl.BlockSpec(perm.shape, lambda g: (0, 0))] + ([grouped] if gelu_arg is not None else []),
        out_specs=grouped,
        out_shape=jax.ShapeDtypeStruct((G, nc, gw), out_dtype),
        compiler_params=_params(("parallel",)),
    )(*([a.reshape(nc, CHUNK, D), perm] + ([gelu_arg] if gelu_arg is not None else [])))


def _split_dot(x, u):
    hi = x.astype(BF16)
    lo = (x - hi.astype(F32)).astype(BF16)
    return (jnp.dot(hi, u, preferred_element_type=F32) + jnp.dot(lo, u, preferred_element_type=F32))


def _col_to_row(col, n):
    return jnp.broadcast_to(col, (n, LANES)).T[0:1, :]


def _row_to_col(row, n):
    return jnp.broadcast_to(row, (LANES, n)).T[:, 0:1]


def _head_in(pair, h, upper):
    x = pair.astype(F32)
    lane = lax.broadcasted_iota(jnp.int32, x.shape, 1)
    if (h % 2 == 1) != upper:
        x = pltpu.roll(x, HEAD_DIM, 1)
    keep = (lane >= HEAD_DIM) if upper else (lane < HEAD_DIM)
    return jnp.where(keep, x, 0.0).astype(BF16)


def _pair_out(even, odd, upper):
    lane = lax.broadcasted_iota(jnp.int32, even.shape, 1)
    if upper:
        return jnp.where(lane < HEAD_DIM, pltpu.roll(even, HEAD_DIM, 1), odd)
    return jnp.where(lane < HEAD_DIM, even, pltpu.roll(odd, HEAD_DIM, 1))


def _attn_fwd(name, q, kv, tq, tk, hb):
    L = q.shape[0]
    H = q.shape[1] // HEAD_DIM
    hb = min(hb, H)
    dw = 2 * HEAD_DIM
    nq, nkb = L // tq, L // tk
    hs = range(hb)
    assert tk % tq == 0 and hb % 2 == 0 and H % hb == 0

    def body(q_ref, kv_ref, o_ref, rc_ref):
        i = pl.program_id(1)
        diff = (lax.broadcasted_iota(jnp.int32, (tq, tk), 1) - lax.broadcasted_iota(jnp.int32, (tq, tk), 0))
        u_suf = (lax.broadcasted_iota(jnp.int32, (tk, tk), 0)
                 > lax.broadcasted_iota(jnp.int32, (tk, tk), 1)).astype(BF16)
        nfull = (i * tq) // tk
        rc_ref[...] = jnp.zeros_like(rc_ref)
        qw = [_head_in(q_ref[:, dw * (h // 2):dw * (h // 2 + 1)], h, False) for h in hs]

        def block(kb, carry, masked):
            runs, accs = carry
            ks = pl.multiple_of(kb * tk, tk)
            kvb = [kv_ref[pl.ds(ks, tk), dw * h:dw * (h + 1)] for h in hs]
            z = [lax.dot_general(qw[h], kvb[h], (((1,), (1,)), ((), ())), preferred_element_type=F32)
                 for h in hs]
            e = [jnp.exp(-jnp.abs(z[h])) for h in hs]
            sp = [jnp.maximum(z[h], 0.0) + jnp.log(1.0 + e[h]) for h in hs]
            if masked:
                causal = diff < (i * tq - ks)
                lom = [jnp.where(causal, -sp[h], 0.0) for h in hs]
            else:
                lom = [-sp[h] for h in hs]
            rem = [runs[h] + _split_dot(lom[h], u_suf) for h in hs]
            w = [jnp.exp(z[h] - sp[h] + rem[h]) for h in hs]
            if masked:
                w = [jnp.where(causal, w[h], 0.0) for h in hs]
            accs = tuple(accs[h] + jnp.dot(w[h].astype(BF16), kvb[h], preferred_element_type=F32) for h in hs)
            for h in hs:
                rc_ref[h, kb] = _col_to_row(runs[h], tq)
            runs = tuple(runs[h] + jnp.sum(lom[h], axis=1, keepdims=True) for h in hs)
            return runs, accs

        init = (tuple(jnp.zeros((tq, 1), F32) for _ in hs), tuple(jnp.zeros((tq, dw), F32) for _ in hs))
        carry = block(nfull, init, True)
        _, accs = lax.fori_loop(0, nfull, lambda n, c: block(nfull - 1 - n, c, False), carry)
        for p in range(hb // 2):
            o_ref[:, dw * p:dw * (p + 1)] = _pair_out(accs[2 * p], accs[2 * p + 1], True).astype(o_ref.dtype)

    qspec = pl.BlockSpec((tq, hb * HEAD_DIM), lambda h, i: (i, h))
    kspec = pl.BlockSpec((L, hb * dw), lambda h, i: (0, h), pipeline_mode=pl.Buffered(1))
    return _pcall(
        body, name=name, grid=(H // hb, nq), in_specs=[qspec, kspec],
        out_specs=[qspec, pl.BlockSpec((hb, None, nkb, 1, tq), lambda h, i: (h, i, 0, 0, 0))],
        out_shape=[jax.ShapeDtypeStruct((L, H * HEAD_DIM), BF16), jax.ShapeDtypeStruct((H, nq, nkb, 1, tq), F32)],
        compiler_params=_params(("parallel", "arbitrary")),
    )(q, kv)


def _attn_bwd(name, q, kv, do, rc, tq, tk, scale, hb):
    L = q.shape[0]
    H = q.shape[1] // HEAD_DIM
    hb = min(hb, H)
    dw_ = 2 * HEAD_DIM
    nq, nkb = L // tq, L // tk
    hs = range(hb)
    assert tk % tq == 0 and hb % 2 == 0 and H % hb == 0

    def body(q_ref, kv_ref, do_ref, rc_ref, dq_ref, dkv_ref):
        i = pl.program_id(1)

        @pl.when(i == 0)
        def _():
            dkv_ref[...] = jnp.zeros_like(dkv_ref)

        diff = (lax.broadcasted_iota(jnp.int32, (tq, tk), 1) - lax.broadcasted_iota(jnp.int32, (tq, tk), 0))
        r_io = lax.broadcasted_iota(jnp.int32, (tk, tk), 0)
        c_io = lax.broadcasted_iota(jnp.int32, (tk, tk), 1)
        u_suf = (r_io > c_io).astype(BF16)
        u_pre = (r_io < c_io).astype(BF16)
        nfull = (i * tq) // tk
        qb = [_head_in(q_ref[:, dw_ * (h // 2):dw_ * (h // 2 + 1)], h, False) for h in hs]
        dob = [_head_in(do_ref[:, dw_ * (h // 2):dw_ * (h // 2 + 1)], h, True) for h in hs]

        def block(kb, carry, masked):
            pres, dqs = carry
            ks = pl.multiple_of(kb * tk, tk)
            kvb = [kv_ref[pl.ds(ks, tk), dw_ * h:dw_ * (h + 1)] for h in hs]
            z = [lax.dot_general(qb[h], kvb[h], (((1,), (1,)), ((), ())), preferred_element_type=F32) for h in hs]
            dw = [lax.dot_general(dob[h], kvb[h], (((1,), (1,)), ((), ())), preferred_element_type=F32)
                  for h in hs]
            e = [jnp.exp(-jnp.abs(z[h])) for h in hs]
            sp = [jnp.maximum(z[h], 0.0) + jnp.log(1.0 + e[h]) for h in hs]
            if masked:
                causal = diff < (i * tq - ks)
                lom = [jnp.where(causal, -sp[h], 0.0) for h in hs]
            else:
                lom = [-sp[h] for h in hs]
            rem = [_row_to_col(rc_ref[h, kb], tq) + _split_dot(lom[h], u_suf) for h in hs]
            logb = [z[h] - sp[h] for h in hs]
            w = [jnp.exp(logb[h] + rem[h]) for h in hs]
            if masked:
                w = [jnp.where(causal, w[h], 0.0) for h in hs]
            da = [dw[h] * w[h] for h in hs]
            p = [pres[h] + jnp.dot(da[h].astype(BF16), u_pre, preferred_element_type=F32) for h in hs]
            dz = [da[h] - jnp.exp(logb[h]) * (da[h] + p[h]) for h in hs]
            if masked:
                dz = [jnp.where(causal, dz[h], 0.0) for h in hs]
            dqs = tuple(dqs[h] + jnp.dot(dz[h].astype(BF16), kvb[h], preferred_element_type=F32) for h in hs)
            for h in hs:
                lhs = jnp.concatenate([dz[h].T, w[h].T], axis=1).astype(BF16)
                rhs = jnp.concatenate([qb[h], dob[h]], axis=0)
                dkv_ref[pl.ds(ks, tk), dw_ * h:dw_ * (h + 1)] += jnp.dot(lhs, rhs, preferred_element_type=F32)
            pres = tuple(pres[h] + jnp.sum(da[h], axis=1, keepdims=True) for h in hs)
            return pres, dqs

        init = (tuple(jnp.zeros((tq, 1), F32) for _ in hs), tuple(jnp.zeros((tq, dw_), F32) for _ in hs))
        carry = lax.fori_loop(0, nfull, lambda kb, c: block(kb, c, False), init)
        _, dqs = block(nfull, carry, True)
        for p in range(hb // 2):
            dq_ref[:, dw_ * p:dw_ * (p + 1)] = (_pair_out(dqs[2 * p], dqs[2 * p + 1], False)
                                                * scale).astype(dq_ref.dtype)

    qspec = pl.BlockSpec((tq, hb * HEAD_DIM), lambda h, i: (i, h))
    kspec = pl.BlockSpec((L, hb * dw_), lambda h, i: (0, h), pipeline_mode=pl.Buffered(1))
    return _pcall(
        body, name=name, grid=(H // hb, nq),
        in_specs=[qspec, kspec, qspec, pl.BlockSpec((hb, None, nkb, 1, tq), lambda h, i: (h, i, 0, 0, 0))],
        out_specs=[qspec, kspec],
        out_shape=[jax.ShapeDtypeStruct((L, H * HEAD_DIM), BF16), jax.ShapeDtypeStruct((L, 2 * H * HEAD_DIM), F32)],
        compiler_params=_params(("parallel", "arbitrary")),
    )(q, kv, do, rc)


_MESH = pl.DeviceIdType.MESH
_HBM = pl.BlockSpec(memory_space=pltpu.HBM)


def _all_gather(name, shards):
    n = len(shards)

    def body(*refs):
        x_refs, out_refs = refs[:n], refs[n:2 * n]
        send_sems, recv_sems, local_sems = refs[2 * n:]
        x, y, c = lax.axis_index("x"), lax.axis_index("y"), lax.axis_index("c")
        me, sibling = (x, y, c), (x, y, 1 - c)
        chips = [(1 - x, y), (x, 1 - y), (1 - x, 1 - y)]

        def slot(a, px, py, pc):
            return out_refs[a].at[4 * px + 2 * py + pc]

        def copy(a, k, block, to, src=None):
            return pltpu.make_async_remote_copy(
                src_ref=slot(a, *block) if src is None else src, dst_ref=slot(a, *block),
                send_sem=send_sems.at[7 * a + k], recv_sem=recv_sems.at[7 * a + k], device_id=to,
                device_id_type=_MESH)

        mine = [pltpu.make_async_copy(x_refs[a], slot(a, *me), local_sems.at[a]) for a in range(n)]
        for cp in mine:
            cp.start()
        first = []
        for a in range(n):
            first.append(copy(a, 0, me, sibling, src=x_refs[a]))
            first += [copy(a, 1 + j, me, (*chip, c), src=x_refs[a]) for j, chip in enumerate(chips)]
        for cp in first:
            cp.start()
        passed = []
        for j, chip in enumerate(chips):
            for a in range(n):
                copy(a, 1 + j, (*chip, c), me).wait_recv()
                passed.append(copy(a, 4 + j, (*chip, c), sibling))
                passed[-1].start()
        for a in range(n):
            copy(a, 0, sibling, me).wait_recv()
            for j, chip in enumerate(chips):
                copy(a, 4 + j, (*chip, 1 - c), me).wait_recv()
        for cp in first + passed:
            cp.wait_send()
        for cp in mine:
            cp.wait()

    return _pcall(
        body, name=name, out_shape=[jax.ShapeDtypeStruct((N_DEV,) + s.shape, s.dtype) for s in shards],
        in_specs=[_HBM] * n, out_specs=[_HBM] * n,
        scratch_shapes=[pltpu.SemaphoreType.DMA((7 * n,)), pltpu.SemaphoreType.DMA((7 * n,)),
                        pltpu.SemaphoreType.DMA((n,))],
    )(*shards)


def _sibling_exchange(name, xs):
    n = len(xs)

    def body(*refs):
        x_refs, out_refs, send_sems, recv_sems = refs[:n], refs[n:2 * n], refs[2 * n], refs[2 * n + 1]
        c = lax.axis_index("c")
        sibling = (lax.axis_index("x"), lax.axis_index("y"), 1 - c)
        cps = [pltpu.make_async_remote_copy(src_ref=x_refs[a].at[:, 1 - c], dst_ref=out_refs[a],
                                            send_sem=send_sems.at[a], recv_sem=recv_sems.at[a], device_id=sibling,
                                            device_id_type=_MESH)
               for a in range(n)]
        for cp in cps:
            cp.start()
        for cp in cps:
            cp.wait()

    return _pcall(
        body, name=name, out_shape=[jax.ShapeDtypeStruct(x.shape[:1] + x.shape[2:], x.dtype) for x in xs],
        in_specs=[_HBM] * n, out_specs=[_HBM] * n,
        scratch_shapes=[pltpu.SemaphoreType.DMA((n,)), pltpu.SemaphoreType.DMA((n,))],
    )(*xs)


def _chip_exchange(name, xs):
    n = len(xs)

    def body(*refs):
        x_refs, out_refs, send_sems, recv_sems = refs[:n], refs[n:2 * n], refs[2 * n], refs[2 * n + 1]
        mx, my, mc = lax.axis_index("x"), lax.axis_index("y"), lax.axis_index("c")
        chips = [(1 - mx, my), (mx, 1 - my), (1 - mx, 1 - my)]
        cps = [pltpu.make_async_remote_copy(src_ref=x_refs[a].at[2 * px + py], dst_ref=out_refs[a].at[j],
                                            send_sem=send_sems.at[3 * a + j], recv_sem=recv_sems.at[3 * a + j],
                                            device_id=(px, py, mc), device_id_type=_MESH)
               for a in range(n) for j, (px, py) in enumerate(chips)]
        for cp in cps:
            cp.start()
        for cp in cps:
            cp.wait()

    return _pcall(
        body, name=name, out_shape=[jax.ShapeDtypeStruct((3,) + x.shape[1:], x.dtype) for x in xs],
        in_specs=[_HBM] * n, out_specs=[_HBM] * n,
        scratch_shapes=[pltpu.SemaphoreType.DMA((3 * n,)), pltpu.SemaphoreType.DMA((3 * n,))],
    )(*xs)


def _pair_sum(name, a, b):
    nb, R, C = a.shape
    tm = _tile(R, 512, 16)

    def fn(i, ab, bb):
        s = ab.astype(F32) + bb.astype(F32)
        return s, s

    spec = lambda arr: (arr, (1, tm, C), lambda i: (i // (R // tm), i % (R // tm), 0))
    out = lambda dt: ((nb, R, C), dt, (1, tm, C), lambda i: (i // (R // tm), i % (R // tm), 0))
    return _rowwise(name, fn, nb * (R // tm), [spec(a), spec(b)], [out(F32), out(a.dtype)])


def _sum8(name, g):
    n, R, C = g.shape
    tm = _tile(R, 256)

    def fn(i, gb):
        s = gb[0]
        for d in range(1, n):
            s = s + gb[d]
        return (s,)

    return _rowwise(name, fn, R // tm, [(g, (n, tm, C), lambda i: (0, i, 0))], [_orows(R, C, F32, tm)])[0]


def _adamw(name, w, m, v, grads, tm=512):
    R, C = w.shape
    tm = _tile(R, tm, 16)
    ng = len(grads)

    def fn(i, wb, mb, vb, *gs):
        g = gs[0].astype(F32)
        for t in gs[1:]:
            g = g + t.astype(F32)
        mn = ADAM_B1 * mb + (1.0 - ADAM_B1) * g
        vn = ADAM_B2 * vb + (1.0 - ADAM_B2) * (g * g)
        m_hat = mn / (1.0 - ADAM_B1 ** ADAM_STEP)
        v_hat = vn / (1.0 - ADAM_B2 ** ADAM_STEP)
        delta = -ADAM_LR * (m_hat / (jnp.sqrt(v_hat) + ADAM_EPS) + ADAM_WD * wb)
        return g, delta, mn, vn

    ins = [_rows(w, tm), _rows(m, tm), _rows(v, tm)] + [_rows(g, tm) for g in grads]
    return _rowwise(name, fn, R // tm, ins, [_orows(R, C, F32, tm) for _ in range(4)])


CLASSES = [
    (BF16, True, [('ssm_w_in', None), ('attn_w_q', None), ('attn_w_o', None), ('ffn_w_down', 0), ('ffn_w_down', 1)]),
    (BF16, False, [('ssm_w_glu', None), ('kv_w', None)]),
    (BF16, False, [('ffn_w_up', 0), ('ffn_w_up', 1)]),
    (F32, False, [('ffn_conv_w', 0), ('ffn_conv_w', 1)]),
]


def _shard2d(a, name, layer):
    if name == 'ffn_conv_w':
        return a[layer, :, 0, :]
    if name == 'kv_w':
        return a
    return a[0 if layer is None else layer]


def _weights_of(members):
    names = []
    for n, _ in members:
        if not names or names[-1] != n:
            names.append(n)
    return names


def _pack_class(d, members, dtype):
    width = _shard2d(d[members[0][0]], *members[0]).shape[1]
    parts = [d[n].reshape(-1, width).astype(dtype) for n in _weights_of(members)]
    return parts[0] if len(parts) == 1 else jnp.concatenate(parts, axis=0)


def _full_weights(gaths, d):
    w = {}
    for (_, row_sharded, members), g in zip(CLASSES, gaths):
        off = 0
        for n, l in members:
            r = _shard2d(d[n], n, l).shape[0]
            blk = g[:, off:off + r]
            if row_sharded:
                w[(n, l)] = blk.reshape(N_DEV * r, g.shape[2])
            else:
                w[(n, l)] = blk.transpose(1, 0, 2).reshape(r, N_DEV * g.shape[2])
            off += r
    return w


def _grad_classes(gw):
    out = []
    for dtype, row_sharded, members in CLASSES:
        parts = []
        for n, l in members:
            g = gw[(n, l)].astype(dtype)
            if row_sharded:
                parts.append(g.reshape(N_DEV, g.shape[0] // N_DEV, g.shape[1]))
            else:
                parts.append(g.reshape(g.shape[0], N_DEV, g.shape[1] // N_DEV).transpose(1, 0, 2))
        out.append(jnp.concatenate(parts, axis=1))
    return out


def _unpack_class(buf, members, shapes):
    out, off = {}, 0
    for n in _weights_of(members):
        r = math.prod(shapes[n]) // buf.shape[1]
        out[n] = buf[off:off + r].reshape(shapes[n])
        off += r
    return out


def _small_layout(shapes, D):
    lay, off = {}, 0
    for n in SMALL:
        r = -(-math.prod(shapes[n]) // D)
        lay[n] = (off, r)
        off += r
    lay['loss'] = (off, 1)
    off += 1
    return lay, -(-off // 8) * 8


def _pack_small(d, lay, total, D):
    parts = []
    for n in SMALL + ['loss']:
        if n not in d:
            parts.append(jnp.zeros((lay[n][1], D), F32))
            continue
        flat = d[n].reshape(-1).astype(F32)
        parts.append(jnp.pad(flat, (0, lay[n][1] * D - flat.shape[0])).reshape(lay[n][1], D))
    used = sum(lay[n][1] for n in SMALL + ['loss'])
    if total > used:
        parts.append(jnp.zeros((total - used, D), F32))
    return jnp.concatenate(parts, axis=0)


def _unpack_small(pack, lay, shapes, D):
    out = {}
    for n in SMALL:
        off, r = lay[n]
        out[n] = pack[off:off + r].reshape(-1)[:math.prod(shapes[n])].reshape(shapes[n])
    return out


def _ffn_fwd(tag, x, g_norm, w_up, conv_w, conv_b, w_down):
    h = _rms_fwd(f"{tag}_norm", x, g_norm)
    gu = _mm(f"{tag}_up", h, w_up, 'nn', BF16)
    a = _gate_fwd(f"{tag}_gate", gu, conv_w, conv_b)
    return _mm(f"{tag}_down", a, w_down, 'nn', F32, resid=x), (h, gu, a)


def _ffn_bwd(tag, dres, dres_bf, x, saved, g_norm, w_up, conv_w, conv_b, w_down):
    h, gu, a = saved
    da = _mm(f"{tag}_dgate", dres_bf, w_down, 'nt', BF16)
    d_w_down = _mm(f"{tag}_dwdown", a, dres_bf, 'tn', BF16)
    dgu, dw0, dw1, dw2, dcb = _gate_bwd(f"{tag}_gate_b", gu, da, conv_w, conv_b)
    d_w_up = _mm(f"{tag}_dwup", h, dgu, 'tn', BF16)
    dh = _mm(f"{tag}_dh", dgu, w_up, 'nt', F32)
    dres, dres_bf, dg = _rms_bwd(f"{tag}_norm_b", x, dres, [(g_norm, dh)])
    return dres, dres_bf, dg, d_w_up, jnp.concatenate([dw0, dw1, dw2], axis=0), dcb[0], d_w_down


def kernel(x, norm_mix, norm_ffn, norm_kv, norm_final, ssm_w_in, ssm_a_re, ssm_a_im, ssm_log_dt, ssm_b_re, ssm_b_im, ssm_c_re, ssm_c_im, ssm_d, ssm_w_glu, kv_w, attn_w_q, attn_w_o, ffn_w_up, ffn_conv_w, ffn_conv_b, ffn_w_down, loss_target, m_norm_mix, m_norm_ffn, m_norm_kv, m_norm_final, m_ssm_w_in, m_ssm_a_re, m_ssm_a_im, m_ssm_log_dt, m_ssm_b_re, m_ssm_b_im, m_ssm_c_re, m_ssm_c_im, m_ssm_d, m_ssm_w_glu, m_kv_w, m_attn_w_q, m_attn_w_o, m_ffn_w_up, m_ffn_conv_w, m_ffn_conv_b, m_ffn_w_down, v_norm_mix, v_norm_ffn, v_norm_kv, v_norm_final, v_ssm_w_in, v_ssm_a_re, v_ssm_a_im, v_ssm_log_dt, v_ssm_b_re, v_ssm_b_im, v_ssm_c_re, v_ssm_c_im, v_ssm_d, v_ssm_w_glu, v_kv_w, v_attn_w_q, v_attn_w_o, v_ffn_w_up, v_ffn_conv_w, v_ffn_conv_b, v_ffn_w_down):
    wts = dict(zip(W_NAMES, (norm_mix, norm_ffn, norm_kv, norm_final, ssm_w_in, ssm_a_re, ssm_a_im, ssm_log_dt,
                             ssm_b_re, ssm_b_im, ssm_c_re, ssm_c_im, ssm_d, ssm_w_glu, kv_w, attn_w_q, attn_w_o,
                             ffn_w_up, ffn_conv_w, ffn_conv_b, ffn_w_down)))
    mom = dict(zip(W_NAMES, (m_norm_mix, m_norm_ffn, m_norm_kv, m_norm_final, m_ssm_w_in, m_ssm_a_re, m_ssm_a_im,
                             m_ssm_log_dt, m_ssm_b_re, m_ssm_b_im, m_ssm_c_re, m_ssm_c_im, m_ssm_d, m_ssm_w_glu,
                             m_kv_w, m_attn_w_q, m_attn_w_o, m_ffn_w_up, m_ffn_conv_w, m_ffn_conv_b, m_ffn_w_down)))
    vel = dict(zip(W_NAMES, (v_norm_mix, v_norm_ffn, v_norm_kv, v_norm_final, v_ssm_w_in, v_ssm_a_re, v_ssm_a_im,
                             v_ssm_log_dt, v_ssm_b_re, v_ssm_b_im, v_ssm_c_re, v_ssm_c_im, v_ssm_d, v_ssm_w_glu,
                             v_kv_w, v_attn_w_q, v_attn_w_o, v_ffn_w_up, v_ffn_conv_w, v_ffn_conv_b, v_ffn_w_down)))
    shapes = {n: wts[n].shape for n in W_NAMES}
    _, L, D = x.shape
    Fh = ffn_conv_b.shape[1]
    G, P = ssm_a_re.shape[1], ssm_a_re.shape[2]
    Hg = ssm_d.shape[2]
    x0 = x[0]
    target = loss_target[0]
    scale = HEAD_DIM ** -0.5

    gaths = _all_gather("gather_weights", [_pack_class(wts, members, dt) for dt, _, members in CLASSES])
    w = _full_weights(gaths, wts)
    nm = [norm_mix[l:l + 1] for l in range(2)]
    nf = [norm_ffn[l:l + 1] for l in range(2)]
    nkv = norm_kv[None]
    nfin = norm_final[None]
    cb = [ffn_conv_b[l:l + 1] for l in range(2)]
    cwt = [w[('ffn_conv_w', l)] for l in range(2)]

    s5p = (ssm_a_re[0], ssm_a_im[0], ssm_log_dt[0], ssm_b_re[0], ssm_b_im[0], ssm_c_re[0], ssm_c_im[0], ssm_d[0])
    (m_mat, n_mat, o_mat, lam1, lam2), s5_vjp = jax.vjp(_s5_build, *s5p)
    m_bf, n_bf, o_bf = m_mat.astype(BF16), n_mat.astype(BF16), o_mat.astype(BF16)

    h0 = _rms_fwd("l0_norm", x0, nm[0])
    u = _mm("l0_win", h0, w[('ssm_w_in', None)], 'nn', BF16)
    ug = _to_groups("s5_group_u", u, G, Hg, BF16)
    vloc = _bmm("s5_local_state", [(ug, n_bf, 'nt')], F32)
    st = _chunk_scan("s5_scan", vloc, lam1, lam2, reverse=False)
    yraw, yg = _bmm("s5_out", [(ug, m_bf, 'nt'), (st, o_bf, 'nt')], (F32, BF16),
                    post=lambda acc: (acc, _gelu(acc)), ungroup=(1, Hg))
    z = _mm("l0_wglu", yg, w[('ssm_w_glu', None)], 'nn', F32)
    x1 = _glu_fwd("l0_glu", x0, z)
    x2, ffn0 = _ffn_fwd("f0", x1, nf[0], w[('ffn_w_up', 0)], cwt[0], cb[0], w[('ffn_w_down', 0)])

    hkv = _rms_fwd("kv_norm", x2, nkv)
    nh = D // HEAD_DIM
    w_kv = w[('kv_w', None)].reshape(D, 2, nh, HEAD_DIM).transpose(0, 2, 1, 3).reshape(D, 2 * D)
    kvp = _mm("kv_proj", hkv, w_kv, 'nn', BF16)
    h2 = _rms_fwd("l1_norm", x2, nm[1])
    qn = _mm("l1_wq", h2, w[('attn_w_q', None)], 'nn', BF16, scale=scale)
    o, rc = _attn_fwd("attn_fwd", qn, kvp, ATT_TQ, ATT_TK, ATT_HB_FWD)
    x3 = _mm("l1_wo", o, w[('attn_w_o', None)], 'nn', F32, resid=x2)
    x4, ffn1 = _ffn_fwd("f1", x3, nf[1], w[('ffn_w_up', 1)], cwt[1], cb[1], w[('ffn_w_down', 1)])

    dres, dres_bf, dg_final, loss_part = _final_loss("loss_head", x4, nfin, target)

    gw = {}
    dres, dres_bf, dg_nf1, gup1, gcw1, gcb1, gdn1 = _ffn_bwd("f1", dres, dres_bf, x3, ffn1, nf[1], w[('ffn_w_up', 1)],
                                                             cwt[1], cb[1], w[('ffn_w_down', 1)])
    do = _mm("l1_do", dres_bf, w[('attn_w_o', None)], 'nt', BF16)
    gw[('attn_w_o', None)] = _mm("l1_dwo", o, dres_bf, 'tn', BF16)
    dq, dkv = _attn_bwd("attn_bwd", qn, kvp, do, rc, ATT_TQ, ATT_TK, scale, ATT_HB_BWD)
    gw[('attn_w_q', None)] = _mm("l1_dwq", h2, dq, 'tn', BF16)
    dh2 = _mm("l1_dh", dq, w[('attn_w_q', None)], 'nt', F32)
    g_kv = _mm("kv_dw", hkv, dkv, 'tn', BF16)
    gw[('kv_w', None)] = g_kv.reshape(D, nh, 2, HEAD_DIM).transpose(0, 2, 1, 3).reshape(D, 2 * D)
    dhkv = _mm("kv_dh", dkv, w_kv, 'nt', F32)
    dres, dres_bf, dg_nm1, dg_nkv = _rms_bwd("l1_norm_b", x2, dres, [(nm[1], dh2), (nkv, dhkv)])
    dres, dres_bf, dg_nf0, gup0, gcw0, gcb0, gdn0 = _ffn_bwd("f0", dres, dres_bf, x1, ffn0, nf[0], w[('ffn_w_up', 0)],
                                                             cwt[0], cb[0], w[('ffn_w_down', 0)])
    dz = _glu_bwd("l0_glu_b", dres, z)
    gw[('ssm_w_glu', None)] = _mm("l0_dwglu", yg, dz, 'tn', BF16)
    dyg = _mm("l0_dyg", dz, w[('ssm_w_glu', None)], 'nt', F32)
    dy = _to_groups("s5_group_dy", dyg, G, Hg, BF16, gelu_arg=yraw)
    ds = _bmm("s5_dstate", [(dy, o_bf, 'nn')], F32)
    dv_loc, dlam1, dlam2 = _chunk_scan("s5_scan_b", ds, lam1, lam2, reverse=True, s_fwd=st)
    du = _bmm("s5_du", [(dy, m_bf, 'nn'), (dv_loc, n_bf, 'nn')], BF16, ungroup=(0, Hg))
    d_m = _bmm("s5_dm", [(dy, ug, 'tn')], F32)
    d_o = _bmm("s5_do", [(dy, st, 'tn')], F32)
    d_n = _bmm("s5_dn", [(dv_loc, ug, 'tn')], F32)
    s5g = s5_vjp((d_m, d_n, d_o, dlam1, dlam2))
    gw[('ssm_w_in', None)] = _mm("l0_dwin", h0, du, 'tn', BF16)
    dh0 = _mm("l0_dh", du, w[('ssm_w_in', None)], 'nt', F32)
    grad_x, _, dg_nm0 = _rms_bwd("l0_norm_b", x0, dres, [(nm[0], dh0)])

    gw.update({('ffn_w_up', 0): gup0, ('ffn_w_up', 1): gup1, ('ffn_w_down', 0): gdn0, ('ffn_w_down', 1): gdn1,
               ('ffn_conv_w', 0): gcw0, ('ffn_conv_w', 1): gcw1})

    small_g = {
        'norm_mix': jnp.concatenate([dg_nm0, dg_nm1], axis=0), 'norm_ffn': jnp.concatenate([dg_nf0, dg_nf1], axis=0),
        'norm_kv': dg_nkv, 'norm_final': dg_final, 'ffn_conv_b': jnp.stack([gcb0, gcb1]),
        'ssm_a_re': s5g[0], 'ssm_a_im': s5g[1], 'ssm_log_dt': s5g[2], 'ssm_b_re': s5g[3], 'ssm_b_im': s5g[4],
        'ssm_c_re': s5g[5], 'ssm_c_im': s5g[6], 'ssm_d': s5g[7], 'loss': loss_part[0, 0:1],
    }
    lay, rs = _small_layout(shapes, D)
    small_sum = _sum8("small_sum", _all_gather("gather_small", [_pack_small(small_g, lay, rs, D)])[0])
    loss = small_sum[lay['loss'][0], 0]
    sg, sdelta, sm, sv = _adamw("adamw_small", _pack_small(wts, lay, rs, D), _pack_small(mom, lay, rs, D),
                                _pack_small(vel, lay, rs, D), [small_sum])
    small_out = [_unpack_small(t, lay, shapes, D) for t in (sg, sdelta, sm, sv)]

    c = lax.axis_index("c")
    chip = 2 * lax.axis_index("x") + lax.axis_index("y")
    gcls = [g.reshape((4, 2) + g.shape[1:]) for g in _grad_classes(gw)]
    mine = [lax.dynamic_index_in_dim(g, c, axis=1, keepdims=False) for g in gcls]
    got = _sibling_exchange("rs_sibling", gcls)
    sums = [_pair_sum(f"rs_pair_sum{k}", a, b) for k, (a, b) in enumerate(zip(mine, got))]
    from_chips = _chip_exchange("rs_chips", [on_wire for _, on_wire in sums])
    big_parts = [{}, {}, {}, {}]
    for k, ((_, _, members), (chip_f32, _), fc) in enumerate(zip(CLASSES, sums, from_chips)):
        own = lax.dynamic_index_in_dim(chip_f32, chip, axis=0, keepdims=False)
        res = _adamw(f"adamw_big{k}", _pack_class(wts, members, F32), _pack_class(mom, members, F32),
                     _pack_class(vel, members, F32), [own, fc[0], fc[1], fc[2]])
        for q in range(4):
            big_parts[q].update(_unpack_class(res[q], members, shapes))
    big_out = big_parts

    outs = [loss, grad_x[None]]
    for k in range(4):
        for n in W_NAMES:
            outs.append(big_out[k][n] if n in BIG else small_out[k][n])
    return tuple(outs)
```

```python
import math

import jax
import jax.numpy as jnp
from jax import lax
from jax.experimental import pallas as pl
from jax.experimental.pallas import tpu as pltpu

F32 = jnp.float32
BF16 = jnp.bfloat16

EPS = 1e-6
HEAD_DIM = 64
CHUNK = 16
N_DEV = 8
ADAM_LR = 0.001
ADAM_B1 = 0.9
ADAM_B2 = 0.999
ADAM_EPS = 1e-08
ADAM_WD = 0.01
ADAM_STEP = 10
VMEM_LIMIT = 48 * 1024 * 1024
MM_BLOCK_BYTES = 28 * 1024 * 1024
MM_TM_MAX = 1024
MM_T_MAX = 512
MM_TN_MAX = 1536
LANES = 128
ATT_TQ = 128
ATT_TK = 256
ATT_HB_FWD = 16
ATT_HB_BWD = 8

W_NAMES = ['norm_mix', 'norm_ffn', 'norm_kv', 'norm_final', 'ssm_w_in', 'ssm_a_re', 'ssm_a_im', 'ssm_log_dt',
           'ssm_b_re', 'ssm_b_im', 'ssm_c_re', 'ssm_c_im', 'ssm_d', 'ssm_w_glu', 'kv_w', 'attn_w_q', 'attn_w_o',
           'ffn_w_up', 'ffn_conv_w', 'ffn_conv_b', 'ffn_w_down']
BIG = ['ssm_w_in', 'ssm_w_glu', 'kv_w', 'attn_w_q', 'attn_w_o', 'ffn_w_up', 'ffn_w_down', 'ffn_conv_w']
SMALL = [n for n in W_NAMES if n not in BIG]


def _pcall(body, **kw):
    return pl.pallas_call(body, **kw)


def _params(sem=None):
    if sem is None:
        return pltpu.CompilerParams(vmem_limit_bytes=VMEM_LIMIT)
    return pltpu.CompilerParams(dimension_semantics=sem, vmem_limit_bytes=VMEM_LIMIT)


def _tile(n, pref, mult=8):
    best = None
    for t in range(mult, min(n, pref) + 1, mult):
        if n % t == 0:
            best = t
    return n if best is None else best


def _mm_tiles(M, N, K, mode, a_bytes, b_bytes, o_bytes):
    def cands(n, cap):
        c = [t for t in range(LANES, min(n, cap) + 1, LANES) if n % t == 0]
        return c or [n]
    best = None
    for tm in cands(M, MM_T_MAX if mode == 'tn' else MM_TM_MAX):
        for tn in cands(N, MM_TN_MAX):
            need = 2 * (tm * K * a_bytes + K * tn * b_bytes + tm * tn * o_bytes)
            score = tm * tn * (tm if mode == 'tn' else 1)
            if need <= MM_BLOCK_BYTES and (best is None or score > best[0]):
                best = (score, tm, tn)
    assert best is not None, (M, N, K)
    return best[1], best[2]


def _mm(name, a, b, mode, out_dtype=F32, scale=None, resid=None):
    if mode == 'nn':
        (M, K), (K2, N) = a.shape, b.shape
    elif mode == 'nt':
        (M, K), (N, K2) = a.shape, b.shape
    else:
        (K, M), (K2, N) = a.shape, b.shape
    assert K == K2, (name, a.shape, b.shape)
    o_bytes = jnp.dtype(out_dtype).itemsize + (resid.dtype.itemsize if resid is not None else 0)
    tm, tn = _mm_tiles(M, N, K, mode, a.dtype.itemsize, b.dtype.itemsize, o_bytes)
    if mode == 'tn':
        a_spec = pl.BlockSpec((K, tm), lambda i, j: (0, i))
    else:
        a_spec = pl.BlockSpec((tm, K), lambda i, j: (i, 0))
    if mode == 'nt':
        b_spec = pl.BlockSpec((tn, K), lambda i, j: (j, 0))
    else:
        b_spec = pl.BlockSpec((K, tn), lambda i, j: (0, j))
    o_spec = pl.BlockSpec((tm, tn), lambda i, j: (i, j))
    dn = {'nn': ((1,), (0,)), 'nt': ((1,), (1,)), 'tn': ((0,), (0,))}[mode]

    def body(*refs):
        a_ref, b_ref, o_ref = refs[0], refs[1], refs[-1]
        acc = lax.dot_general(a_ref[...].astype(BF16), b_ref[...].astype(BF16), (dn, ((), ())),
                              preferred_element_type=F32)
        if scale is not None:
            acc = acc * scale
        if resid is not None:
            acc = acc + refs[2][...]
        o_ref[...] = acc.astype(o_ref.dtype)

    return _pcall(
        body, name=name, grid=(M // tm, N // tn),
        in_specs=[a_spec, b_spec] + ([o_spec] if resid is not None else []),
        out_specs=o_spec,
        out_shape=jax.ShapeDtypeStruct((M, N), out_dtype),
        compiler_params=_params(("parallel", "parallel")),
    )(*([a, b] + ([resid] if resid is not None else [])))


def _bmm(name, terms, out_dtype, gb=8, post=None, ungroup=None):
    G = terms[0][0].shape[0]
    gb = _tile(G, gb, 1)
    dns = {'nn': ((1,), (0,)), 'nt': ((1,), (1,)), 'tn': ((0,), (0,))}

    def oshape(a, b, mode):
        m = a.shape[2] if mode == 'tn' else a.shape[1]
        n = b.shape[1] if mode == 'nt' else b.shape[2]
        return m, n

    m, n = oshape(*terms[0])
    out_dtypes = out_dtype if isinstance(out_dtype, (tuple, list)) else (out_dtype,)
    n_in = 2 * len(terms)
    if ungroup is not None:
        perm, gpb = _lane_perm(ungroup[1])
        assert gpb == gb and n == CHUNK * ungroup[1]

    def body(*refs):
        ins, outs = refs[:n_in], refs[len(refs) - len(out_dtypes):]
        kept = []
        for gi in range(gb):
            acc = None
            for t, (_, _, mode) in enumerate(terms):
                part = lax.dot_general(ins[2 * t][gi].astype(BF16), ins[2 * t + 1][gi].astype(BF16),
                                       (dns[mode], ((), ())), preferred_element_type=F32)
                acc = part if acc is None else acc + part
            vals = (acc,) if post is None else post(acc)
            for k, (o_ref, v) in enumerate(zip(outs, vals)):
                if ungroup is not None and k == ungroup[0]:
                    kept.append(v.astype(BF16))
                else:
                    o_ref[gi] = v.astype(o_ref.dtype)
        if ungroup is not None:
            x = lax.dot_general(jnp.concatenate(kept, axis=1), refs[n_in][...], (((1,), (1,)), ((), ())),
                                preferred_element_type=F32)
            o_ref = outs[ungroup[0]]
            for t in range(CHUNK):
                o_ref[:, t, :] = x[:, t * LANES:(t + 1) * LANES].astype(o_ref.dtype)

    in_specs, args = [], []
    for a, b, _ in terms:
        in_specs += [pl.BlockSpec((gb,) + a.shape[1:], lambda g: (g, 0, 0)),
                     pl.BlockSpec((gb,) + b.shape[1:], lambda g: (g, 0, 0))]
        args += [a, b]
    out_specs = [pl.BlockSpec((gb, m, n), lambda g: (g, 0, 0)) for _ in out_dtypes]
    out_shape = [jax.ShapeDtypeStruct((G, m, n), dt) for dt in out_dtypes]
    if ungroup is not None:
        in_specs.append(pl.BlockSpec(perm.shape, lambda g: (0, 0)))
        args.append(perm)
        out_specs[ungroup[0]] = pl.BlockSpec((m, CHUNK, LANES), lambda g: (0, 0, g))
        out_shape[ungroup[0]] = jax.ShapeDtypeStruct((m, CHUNK, G * ungroup[1]), out_dtypes[ungroup[0]])
    res = _pcall(
        body, name=name, grid=(G // gb,), in_specs=in_specs, out_specs=out_specs, out_shape=out_shape,
        compiler_params=_params(("parallel",)),
    )(*args)
    if ungroup is not None:
        res = list(res)
        res[ungroup[0]] = res[ungroup[0]].reshape(m * CHUNK, G * ungroup[1])
    return res[0] if len(out_dtypes) == 1 else res


def _rowwise(name, fn, n_steps, ins, outs, n_acc=0):
    n_in, n_out = len(ins), len(outs)

    def body(*refs):
        i = pl.program_id(0)
        vals = fn(i, *[r[...] for r in refs[:n_in]])
        o_refs = refs[n_in:]
        for j in range(n_out - n_acc):
            o_refs[j][...] = vals[j].astype(o_refs[j].dtype)
        if n_acc:
            @pl.when(i == 0)
            def _():
                for j in range(n_out - n_acc, n_out):
                    o_refs[j][...] = vals[j].astype(o_refs[j].dtype)

            @pl.when(i > 0)
            def _():
                for j in range(n_out - n_acc, n_out):
                    o_refs[j][...] += vals[j].astype(o_refs[j].dtype)

    res = _pcall(
        body, name=name, grid=(n_steps,),
        in_specs=[pl.BlockSpec(blk, im) for _, blk, im in ins],
        out_specs=[pl.BlockSpec(blk, im) for _, _, blk, im in outs],
        out_shape=[jax.ShapeDtypeStruct(s, d) for s, d, _, _ in outs],
        compiler_params=_params(("arbitrary",)),
    )(*[a for a, _, _ in ins])
    return res


def _rows(a, tm):
    return (a, (tm, a.shape[1]), lambda i: (i, 0))


def _whole(a):
    nd = a.ndim
    return (a, a.shape, lambda i: (0,) * nd)


def _orows(L, n, dtype, tm):
    return ((L, n), dtype, (tm, n), lambda i: (i, 0))


def _oacc(r, n):
    return ((r, n), F32, (r, n), lambda i: (0, 0))


def _rms_fwd(name, x, g, tm=512):
    L, D = x.shape
    tm = _tile(L, tm)

    def fn(i, xb, gb):
        r = lax.rsqrt(jnp.mean(xb * xb, axis=-1, keepdims=True) + EPS)
        return (xb * r * gb,)

    return _rowwise(name, fn, L // tm, [_rows(x, tm), _whole(g)], [_orows(L, D, BF16, tm)])[0]


def _rms_bwd(name, x, dres, branches, tm=256):
    L, D = x.shape
    tm = _tile(L, tm)
    nb = len(branches)

    def fn(i, xb, db, *rest):
        r = lax.rsqrt(jnp.mean(xb * xb, axis=-1, keepdims=True) + EPS)
        xh = xb * r
        dx = db
        dgs = []
        for b in range(nb):
            gb, dyb = rest[2 * b], rest[2 * b + 1].astype(F32)
            dxh = dyb * gb
            dx = dx + r * (dxh - xh * jnp.mean(dxh * xh, axis=-1, keepdims=True))
            dgs.append(jnp.sum(dyb * xh, axis=0, keepdims=True))
        return (dx, dx, *dgs)

    ins = [_rows(x, tm), _rows(dres, tm)]
    for g, dy in branches:
        ins += [_whole(g), _rows(dy, tm)]
    outs = [_orows(L, D, F32, tm), _orows(L, D, BF16, tm)] + [_oacc(1, D) for _ in range(nb)]
    return _rowwise(name, fn, L // tm, ins, outs, n_acc=nb)


def _final_loss(name, x, g, target, tm=256):
    L, D = x.shape
    tm = _tile(L, tm)

    def fn(i, xb, gb, tb):
        r = lax.rsqrt(jnp.mean(xb * xb, axis=-1, keepdims=True) + EPS)
        xh = xb * r
        err = xh * gb - tb
        dy = err * (1.0 / D)
        dxh = dy * gb
        dx = r * (dxh - xh * jnp.mean(dxh * xh, axis=-1, keepdims=True))
        dg = jnp.sum(dy * xh, axis=0, keepdims=True)
        per_row = jnp.mean(err * err, axis=-1, keepdims=True)
        loss = 0.5 * jnp.sum(per_row, axis=0, keepdims=True)
        return dx, dx, dg, jnp.broadcast_to(loss, (1, LANES))

    return _rowwise(name, fn, L // tm, [_rows(x, tm), _whole(g), _rows(target, tm)],
                    [_orows(L, D, F32, tm), _orows(L, D, BF16, tm), _oacc(1, D), _oacc(1, LANES)], n_acc=2)


def _glu_fwd(name, x, z, tm=256):
    L, D = x.shape
    tm = _tile(L, tm)

    def fn(i, xb, zb):
        return (xb + zb[:, :D] * jax.nn.sigmoid(zb[:, D:]),)

    return _rowwise(name, fn, L // tm, [_rows(x, tm), _rows(z, tm)], [_orows(L, D, F32, tm)])[0]


def _glu_bwd(name, dx, z, tm=256):
    L, D = dx.shape
    tm = _tile(L, tm)

    def fn(i, db, zb):
        z1, sg = zb[:, :D], jax.nn.sigmoid(zb[:, D:])
        return (jnp.concatenate([(db * sg).astype(BF16), (db * z1 * sg * (1.0 - sg)).astype(BF16)], axis=1),)

    return _rowwise(name, fn, L // tm, [_rows(dx, tm), _rows(z, tm)], [_orows(L, 2 * D, BF16, tm)])[0]


HALO = 16


def _halo_prev(a, tm):
    return (a, (HALO, a.shape[1]), lambda i: (jnp.maximum(i * (tm // HALO) - 1, 0), 0))


def _halo_next(a, tm):
    last = a.shape[0] // HALO - 1
    return (a, (HALO, a.shape[1]), lambda i: (jnp.minimum((i + 1) * (tm // HALO), last), 0))


def _shift_down(cur, halo, k, first):
    tm = cur.shape[0]
    rolled = pltpu.roll(cur, k, 0)
    tail = pltpu.roll(halo, k, 0)
    tail = jnp.where(first, 0.0, tail)
    row = lax.broadcasted_iota(jnp.int32, (tm, 1), 0)
    head = jnp.concatenate([tail, jnp.zeros((tm - 8, cur.shape[1]), cur.dtype)], axis=0) if tm > 8 else tail
    return jnp.where(row < k, head, rolled)


def _shift_up(cur, halo, k, last):
    tm = cur.shape[0]
    rolled = pltpu.roll(cur, tm - k, 0)
    head = pltpu.roll(halo, 8 - k, 0)
    head = jnp.where(last, 0.0, head)
    row = lax.broadcasted_iota(jnp.int32, (tm, 1), 0)
    tail = jnp.concatenate([jnp.zeros((tm - 8, cur.shape[1]), cur.dtype), head], axis=0) if tm > 8 else head
    return jnp.where(row >= tm - k, tail, rolled)


def _conv_pre(gb, hb, cw, cb, first):
    g1 = _shift_down(gb, hb, 1, first)
    g2 = _shift_down(gb, hb, 2, first)
    return cw[0:1] * g2 + cw[1:2] * g1 + cw[2:3] * gb + cb, g1, g2


def _gate_fwd(name, gu, cw, cb, tm=256):
    L, F2 = gu.shape
    Fh = F2 // 2
    tm = _tile(L, tm)

    def fn(i, gub, halo, cwb, cbb):
        gub, halo = gub.astype(F32), halo.astype(F32)[HALO - 8:]
        gc, _, _ = _conv_pre(gub[:, :Fh], halo[:, :Fh], cwb, cbb, i == 0)
        return (gc * jax.nn.sigmoid(gc) * gub[:, Fh:],)

    return _rowwise(name, fn, L // tm, [_rows(gu, tm), _halo_prev(gu, tm), _whole(cw), _whole(cb)],
                    [_orows(L, Fh, BF16, tm)])[0]


def _gate_bwd(name, gu, da, cw, cb, tm=128):
    L, F2 = gu.shape
    Fh = F2 // 2
    tm = _tile(L, tm)
    n = L // tm

    def dgate(gb, ub, dab, hb, cwb, cbb, first):
        gc, g1, g2 = _conv_pre(gb, hb, cwb, cbb, first)
        sg = jax.nn.sigmoid(gc)
        return dab * ub * sg * (1.0 + gc * (1.0 - sg)), dab * gc * sg, g1, g2

    def fn(i, gub, prev, nxt, dab, dnx, cwb, cbb):
        gub, prev, nxt = gub.astype(F32), prev.astype(F32)[HALO - 8:], nxt.astype(F32)[:8]
        dab, dnx = dab.astype(F32), dnx.astype(F32)[:8]
        gb, ub = gub[:, :Fh], gub[:, Fh:]
        dgc, du, g1, g2 = dgate(gb, ub, dab, prev[:, :Fh], cwb, cbb, i == 0)
        dgc_next, _, _, _ = dgate(nxt[:, :Fh], nxt[:, Fh:], dnx, gb[tm - 8:], cwb, cbb, False)
        d1 = _shift_up(dgc, dgc_next, 1, i == n - 1)
        d2 = _shift_up(dgc, dgc_next, 2, i == n - 1)
        dg = cwb[2:3] * dgc + cwb[1:2] * d1 + cwb[0:1] * d2
        return (jnp.concatenate([dg.astype(BF16), du.astype(BF16)], axis=1),
                jnp.sum(dgc * g2, axis=0, keepdims=True), jnp.sum(dgc * g1, axis=0, keepdims=True),
                jnp.sum(dgc * gb, axis=0, keepdims=True), jnp.sum(dgc, axis=0, keepdims=True))

    return _rowwise(name, fn, n,
                    [_rows(gu, tm), _halo_prev(gu, tm), _halo_next(gu, tm), _rows(da, tm), _halo_next(da, tm),
                     _whole(cw), _whole(cb)],
                    [_orows(L, 2 * Fh, BF16, tm)] + [_oacc(1, Fh) for _ in range(4)], n_acc=4)


def _s5_build(a_re, a_im, log_dt, b_re, b_im, c_re, c_im, d):
    T = CHUNK
    G, P = a_re.shape
    H = d.shape[1]
    hi = lax.Precision.HIGH
    dt = jnp.exp(log_dt)[:, None]
    mag = jnp.exp(a_re * dt)
    ab_re = mag * jnp.cos(a_im * dt)
    ab_im = mag * jnp.sin(a_im * dt)
    den = a_re * a_re + a_im * a_im
    f_re = ((ab_re - 1.0) * a_re + ab_im * a_im) / den
    f_im = (ab_im * a_re - (ab_re - 1.0) * a_im) / den
    bb_re = f_re[..., None] * b_re - f_im[..., None] * b_im
    bb_im = f_re[..., None] * b_im + f_im[..., None] * b_re
    tau = jnp.arange(T + 1, dtype=F32)[None, :, None]
    pmag = jnp.exp(tau * (a_re * dt)[:, None, :])
    pang = tau * (a_im * dt)[:, None, :]
    pw_re = pmag * jnp.cos(pang)
    pw_im = pmag * jnp.sin(pang)
    cp_re = c_re[:, :, None, :] * pw_re[:, None] - c_im[:, :, None, :] * pw_im[:, None]
    cp_im = c_re[:, :, None, :] * pw_im[:, None] + c_im[:, :, None, :] * pw_re[:, None]
    cp_cat = jnp.concatenate([cp_re, -cp_im], axis=-1)
    bb_cat = jnp.concatenate([bb_re, bb_im], axis=1)
    kt = jnp.einsum('ghtq,gqk->ghtk', cp_cat[:, :, :T], bb_cat, precision=hi)
    kt = kt.at[:, :, 0, :].add(d[:, :, None] * jnp.eye(H, dtype=F32)[None])
    kp = jnp.concatenate([kt[:, :, ::-1, :], jnp.zeros((G, H, T - 1, H), F32)], axis=2).reshape(G, H, (2 * T - 1) * H)
    m_mat = jnp.stack([kp[:, :, (T - 1 - t) * H:(2 * T - 1 - t) * H] for t in range(T)], axis=1)
    m_mat = m_mat.reshape(G, T * H, T * H)
    pr = jnp.repeat(pw_re[:, :T][:, ::-1].transpose(0, 2, 1), H, axis=2)
    pi = jnp.repeat(pw_im[:, :T][:, ::-1].transpose(0, 2, 1), H, axis=2)
    br, bi = jnp.tile(bb_re, (1, 1, T)), jnp.tile(bb_im, (1, 1, T))
    n_mat = jnp.concatenate([pr * br - pi * bi, pr * bi + pi * br], axis=1)
    o_mat = cp_cat[:, :, 1:].transpose(0, 2, 1, 3).reshape(G, T * H, 2 * P)
    lam1 = jnp.concatenate([pw_re[:, T], pw_re[:, T]], axis=-1)[:, None, :]
    lam2 = jnp.concatenate([-pw_im[:, T], pw_im[:, T]], axis=-1)[:, None, :]
    return m_mat, n_mat, o_mat, lam1, lam2


def _chunk_scan(name, v, lam1, lam2, reverse, s_fwd=None, gb=64):
    G, nc, W = v.shape
    gb = _tile(G, gb, 1)
    half = W // 2
    ntile = nc // 8

    def body(*refs):
        if reverse:
            v_ref, l1_ref, l2_ref, s_ref, o_ref, d1_ref, d2_ref = refs
        else:
            v_ref, l1_ref, l2_ref, o_ref = refs
        l1 = jnp.broadcast_to(l1_ref[...], (gb, 8, W))
        l2 = jnp.broadcast_to(l2_ref[...], (gb, 8, W))
        if reverse:
            l2 = -l2
        row = lax.broadcasted_iota(jnp.int32, (gb, 8, W), 1)

        def tile_step(n, carry):
            if reverse:
                st, a1, a2 = carry
                base = pl.multiple_of((ntile - 1 - n) * 8, 8)
            else:
                st = carry
                base = pl.multiple_of(n * 8, 8)
            vt = v_ref[:, pl.ds(base, 8), :]
            out = jnp.zeros((gb, 8, W), F32)
            order = range(7, -1, -1) if reverse else range(8)
            for r in order:
                out = jnp.where(row == r, st, out)
                vr = jnp.broadcast_to(vt[:, r:r + 1, :], (gb, 8, W))
                st = l1 * st + l2 * pltpu.roll(st, half, 2) + vr
            o_ref[:, pl.ds(base, 8), :] = out
            if reverse:
                sv = s_ref[:, pl.ds(base, 8), :]
                a1 = a1 + out * sv
                a2 = a2 + out * pltpu.roll(sv, half, 2)
                return st, a1, a2
            return st

        zero = jnp.zeros((gb, 8, W), F32)
        if reverse:
            _, a1, a2 = lax.fori_loop(0, ntile, tile_step, (zero, zero, zero))
            d1_ref[...] = jnp.sum(a1, axis=1, keepdims=True)
            d2_ref[...] = jnp.sum(a2, axis=1, keepdims=True)
        else:
            lax.fori_loop(0, ntile, tile_step, zero)

    big = pl.BlockSpec((gb, nc, W), lambda g: (g, 0, 0), pipeline_mode=pl.Buffered(1))
    vec = pl.BlockSpec((gb, 1, W), lambda g: (g, 0, 0))
    if reverse:
        return _pcall(body, name=name, grid=(G // gb,), in_specs=[big, vec, vec, big],
                      out_specs=[big, vec, vec],
                      out_shape=[jax.ShapeDtypeStruct((G, nc, W), F32), jax.ShapeDtypeStruct((G, 1, W), F32),
                                 jax.ShapeDtypeStruct((G, 1, W), F32)],
                      compiler_params=_params(("parallel",)))(v, lam1, lam2, s_fwd)
    return _pcall(body, name=name, grid=(G // gb,), in_specs=[big, vec, vec], out_specs=big,
                  out_shape=jax.ShapeDtypeStruct((G, nc, W), F32),
                  compiler_params=_params(("parallel",)))(v, lam1, lam2)


_GELU_C = math.sqrt(2.0 / math.pi)


def _gelu(y):
    return 0.5 * y * (1.0 + jnp.tanh(_GELU_C * (y + 0.044715 * y * y * y)))


def _gelu_grad(y):
    t = jnp.tanh(_GELU_C * (y + 0.044715 * y * y * y))
    return 0.5 * (1.0 + t) + 0.5 * y * (1.0 - t * t) * _GELU_C * (1.0 + 3.0 * 0.044715 * y * y)


def _lane_perm(Hg):
    gpb = LANES // Hg
    r = jnp.arange(CHUNK * LANES)
    t, gl, h = r // LANES, (r % LANES) // Hg, r % Hg
    target = gl * (CHUNK * Hg) + t * Hg + h
    return (target[:, None] == r[None, :]).astype(BF16), gpb


def _to_groups(name, a, G, Hg, out_dtype, gelu_arg=None):
    L, D = a.shape
    nc = L // CHUNK
    perm, gpb = _lane_perm(Hg)
    gw = CHUNK * Hg

    def body(*refs):
        x_ref, p_ref, o_ref = refs[0], refs[1], refs[-1]
        x = jnp.concatenate([x_ref[:, t, :] for t in range(CHUNK)], axis=1)
        if x.dtype == BF16:
            z = jnp.dot(x, p_ref[...], preferred_element_type=F32)
        else:
            z = _split_dot(x, p_ref[...])
        for gl in range(gpb):
            piece = z[:, gl * gw:(gl + 1) * gw]
            if gelu_arg is not None:
                piece = piece * _gelu_grad(refs[2][gl])
            o_ref[gl] = piece.astype(o_ref.dtype)

    grouped = pl.BlockSpec((gpb, nc, gw), lambda g: (g, 0, 0))
    return _pcall(
        body, name=name, grid=(G // gpb,),
        in_specs=[pl.BlockSpec((nc, CHUNK, LANES), lambda g: (0, 0, g)),
                  pl.BlockSpec(perm.shape, lambda g: (0, 0))] + ([grouped] if gelu_arg is not None else []),
        out_specs=grouped,
        out_shape=jax.ShapeDtypeStruct((G, nc, gw), out_dtype),
        compiler_params=_params(("parallel",)),
    )(*([a.reshape(nc, CHUNK, D), perm] + ([gelu_arg] if gelu_arg is not None else [])))


def _split_dot(x, u):
    hi = x.astype(BF16)
    lo = (x - hi.astype(F32)).astype(BF16)
    return (jnp.dot(hi, u, preferred_element_type=F32) + jnp.dot(lo, u, preferred_element_type=F32))


def _col_to_row(col, n):
    return jnp.broadcast_to(col, (n, LANES)).T[0:1, :]


def _row_to_col(row, n):
    return jnp.broadcast_to(row, (LANES, n)).T[:, 0:1]


def _head_in(pair, h, upper):
    x = pair.astype(F32)
    lane = lax.broadcasted_iota(jnp.int32, x.shape, 1)
    if (h % 2 == 1) != upper:
        x = pltpu.roll(x, HEAD_DIM, 1)
    keep = (lane >= HEAD_DIM) if upper else (lane < HEAD_DIM)
    return jnp.where(keep, x, 0.0).astype(BF16)


def _pair_out(even, odd, upper):
    lane = lax.broadcasted_iota(jnp.int32, even.shape, 1)
    if upper:
        return jnp.where(lane < HEAD_DIM, pltpu.roll(even, HEAD_DIM, 1), odd)
    return jnp.where(lane < HEAD_DIM, even, pltpu.roll(odd, HEAD_DIM, 1))


def _attn_fwd(name, q, kv, tq, tk, hb):
    L = q.shape[0]
    H = q.shape[1] // HEAD_DIM
    hb = min(hb, H)
    dw = 2 * HEAD_DIM
    nq, nkb = L // tq, L // tk
    hs = range(hb)
    assert tk % tq == 0 and hb % 2 == 0 and H % hb == 0

    def body(q_ref, kv_ref, o_ref, rc_ref):
        i = pl.program_id(1)
        diff = (lax.broadcasted_iota(jnp.int32, (tq, tk), 1) - lax.broadcasted_iota(jnp.int32, (tq, tk), 0))
        u_suf = (lax.broadcasted_iota(jnp.int32, (tk, tk), 0)
                 > lax.broadcasted_iota(jnp.int32, (tk, tk), 1)).astype(BF16)
        nfull = (i * tq) // tk
        rc_ref[...] = jnp.zeros_like(rc_ref)
        qw = [_head_in(q_ref[:, dw * (h // 2):dw * (h // 2 + 1)], h, False) for h in hs]

        def block(kb, carry, masked):
            runs, accs = carry
            ks = pl.multiple_of(kb * tk, tk)
            kvb = [kv_ref[pl.ds(ks, tk), dw * h:dw * (h + 1)] for h in hs]
            z = [lax.dot_general(qw[h], kvb[h], (((1,), (1,)), ((), ())), preferred_element_type=F32)
                 for h in hs]
            e = [jnp.exp(-jnp.abs(z[h])) for h in hs]
            sp = [jnp.maximum(z[h], 0.0) + jnp.log(1.0 + e[h]) for h in hs]
            if masked:
                causal = diff < (i * tq - ks)
                lom = [jnp.where(causal, -sp[h], 0.0) for h in hs]
            else:
                lom = [-sp[h] for h in hs]
            rem = [runs[h] + _split_dot(lom[h], u_suf) for h in hs]
            w = [jnp.exp(z[h] - sp[h] + rem[h]) for h in hs]
            if masked:
                w = [jnp.where(causal, w[h], 0.0) for h in hs]
            accs = tuple(accs[h] + jnp.dot(w[h].astype(BF16), kvb[h], preferred_element_type=F32) for h in hs)
            for h in hs:
                rc_ref[h, kb] = _col_to_row(runs[h], tq)
            runs = tuple(runs[h] + jnp.sum(lom[h], axis=1, keepdims=True) for h in hs)
            return runs, accs

        init = (tuple(jnp.zeros((tq, 1), F32) for _ in hs), tuple(jnp.zeros((tq, dw), F32) for _ in hs))
        carry = block(nfull, init, True)
        _, accs = lax.fori_loop(0, nfull, lambda n, c: block(nfull - 1 - n, c, False), carry)
        for p in range(hb // 2):
            o_ref[:, dw * p:dw * (p + 1)] = _pair_out(accs[2 * p], accs[2 * p + 1], True).astype(o_ref.dtype)

    qspec = pl.BlockSpec((tq, hb * HEAD_DIM), lambda h, i: (i, h))
    kspec = pl.BlockSpec((L, hb * dw), lambda h, i: (0, h), pipeline_mode=pl.Buffered(1))
    return _pcall(
        body, name=name, grid=(H // hb, nq), in_specs=[qspec, kspec],
        out_specs=[qspec, pl.BlockSpec((hb, None, nkb, 1, tq), lambda h, i: (h, i, 0, 0, 0))],
        out_shape=[jax.ShapeDtypeStruct((L, H * HEAD_DIM), BF16), jax.ShapeDtypeStruct((H, nq, nkb, 1, tq), F32)],
        compiler_params=_params(("parallel", "arbitrary")),
    )(q, kv)


def _attn_bwd(name, q, kv, do, rc, tq, tk, scale, hb):
    L = q.shape[0]
    H = q.shape[1] // HEAD_DIM
    hb = min(hb, H)
    dw_ = 2 * HEAD_DIM
    nq, nkb = L // tq, L // tk
    hs = range(hb)
    assert tk % tq == 0 and hb % 2 == 0 and H % hb == 0

    def body(q_ref, kv_ref, do_ref, rc_ref, dq_ref, dkv_ref):
        i = pl.program_id(1)

        @pl.when(i == 0)
        def _():
            dkv_ref[...] = jnp.zeros_like(dkv_ref)

        diff = (lax.broadcasted_iota(jnp.int32, (tq, tk), 1) - lax.broadcasted_iota(jnp.int32, (tq, tk), 0))
        r_io = lax.broadcasted_iota(jnp.int32, (tk, tk), 0)
        c_io = lax.broadcasted_iota(jnp.int32, (tk, tk), 1)
        u_suf = (r_io > c_io).astype(BF16)
        u_pre = (r_io < c_io).astype(BF16)
        nfull = (i * tq) // tk
        qb = [_head_in(q_ref[:, dw_ * (h // 2):dw_ * (h // 2 + 1)], h, False) for h in hs]
        dob = [_head_in(do_ref[:, dw_ * (h // 2):dw_ * (h // 2 + 1)], h, True) for h in hs]

        def block(kb, carry, masked):
            pres, dqs = carry
            ks = pl.multiple_of(kb * tk, tk)
            kvb = [kv_ref[pl.ds(ks, tk), dw_ * h:dw_ * (h + 1)] for h in hs]
            z = [lax.dot_general(qb[h], kvb[h], (((1,), (1,)), ((), ())), preferred_element_type=F32) for h in hs]
            dw = [lax.dot_general(dob[h], kvb[h], (((1,), (1,)), ((), ())), preferred_element_type=F32)
                  for h in hs]
            e = [jnp.exp(-jnp.abs(z[h])) for h in hs]
            sp = [jnp.maximum(z[h], 0.0) + jnp.log(1.0 + e[h]) for h in hs]
            if masked:
                causal = diff < (i * tq - ks)
                lom = [jnp.where(causal, -sp[h], 0.0) for h in hs]
            else:
                lom = [-sp[h] for h in hs]
            rem = [_row_to_col(rc_ref[h, kb], tq) + _split_dot(lom[h], u_suf) for h in hs]
            logb = [z[h] - sp[h] for h in hs]
            w = [jnp.exp(logb[h] + rem[h]) for h in hs]
            if masked:
                w = [jnp.where(causal, w[h], 0.0) for h in hs]
            da = [dw[h] * w[h] for h in hs]
            p = [pres[h] + jnp.dot(da[h].astype(BF16), u_pre, preferred_element_type=F32) for h in hs]
            dz = [da[h] - jnp.exp(logb[h]) * (da[h] + p[h]) for h in hs]
            if masked:
                dz = [jnp.where(causal, dz[h], 0.0) for h in hs]
            dqs = tuple(dqs[h] + jnp.dot(dz[h].astype(BF16), kvb[h], preferred_element_type=F32) for h in hs)
            for h in hs:
                lhs = jnp.concatenate([dz[h].T, w[h].T], axis=1).astype(BF16)
                rhs = jnp.concatenate([qb[h], dob[h]], axis=0)
                dkv_ref[pl.ds(ks, tk), dw_ * h:dw_ * (h + 1)] += jnp.dot(lhs, rhs, preferred_element_type=F32)
            pres = tuple(pres[h] + jnp.sum(da[h], axis=1, keepdims=True) for h in hs)
            return pres, dqs

        init = (tuple(jnp.zeros((tq, 1), F32) for _ in hs), tuple(jnp.zeros((tq, dw_), F32) for _ in hs))
        carry = lax.fori_loop(0, nfull, lambda kb, c: block(kb, c, False), init)
        _, dqs = block(nfull, carry, True)
        for p in range(hb // 2):
            dq_ref[:, dw_ * p:dw_ * (p + 1)] = (_pair_out(dqs[2 * p], dqs[2 * p + 1], False)
                                                * scale).astype(dq_ref.dtype)

    qspec = pl.BlockSpec((tq, hb * HEAD_DIM), lambda h, i: (i, h))
    kspec = pl.BlockSpec((L, hb * dw_), lambda h, i: (0, h), pipeline_mode=pl.Buffered(1))
    return _pcall(
        body, name=name, grid=(H // hb, nq),
        in_specs=[qspec, kspec, qspec, pl.BlockSpec((hb, None, nkb, 1, tq), lambda h, i: (h, i, 0, 0, 0))],
        out_specs=[qspec, kspec],
        out_shape=[jax.ShapeDtypeStruct((L, H * HEAD_DIM), BF16), jax.ShapeDtypeStruct((L, 2 * H * HEAD_DIM), F32)],
        compiler_params=_params(("parallel", "arbitrary")),
    )(q, kv, do, rc)


_MESH = pl.DeviceIdType.MESH
_HBM = pl.BlockSpec(memory_space=pltpu.HBM)


def _all_gather(name, shards):
    n = len(shards)

    def body(*refs):
        x_refs, out_refs = refs[:n], refs[n:2 * n]
        send_sems, recv_sems, local_sems = refs[2 * n:]
        x, y, c = lax.axis_index("x"), lax.axis_index("y"), lax.axis_index("c")
        me, sibling = (x, y, c), (x, y, 1 - c)
        chips = [(1 - x, y), (x, 1 - y), (1 - x, 1 - y)]

        def slot(a, px, py, pc):
            return out_refs[a].at[4 * px + 2 * py + pc]

        def copy(a, k, block, to, src=None):
            return pltpu.make_async_remote_copy(
                src_ref=slot(a, *block) if src is None else src, dst_ref=slot(a, *block),
                send_sem=send_sems.at[7 * a + k], recv_sem=recv_sems.at[7 * a + k], device_id=to,
                device_id_type=_MESH)

        mine = [pltpu.make_async_copy(x_refs[a], slot(a, *me), local_sems.at[a]) for a in range(n)]
        for cp in mine:
            cp.start()
        first = []
        for j in (2, 0, 1):
            first += [copy(a, 1 + j, me, (*chips[j], c), src=x_refs[a]) for a in range(n)]
        first += [copy(a, 0, me, sibling, src=x_refs[a]) for a in range(n)]
        for cp in first:
            cp.start()
        passed = []
        for j, chip in enumerate(chips):
            for a in range(n):
                copy(a, 1 + j, (*chip, c), me).wait_recv()
                passed.append(copy(a, 4 + j, (*chip, c), sibling))
                passed[-1].start()
        for a in range(n):
            copy(a, 0, sibling, me).wait_recv()
            for j, chip in enumerate(chips):
                copy(a, 4 + j, (*chip, 1 - c), me).wait_recv()
        for cp in first + passed:
            cp.wait_send()
        for cp in mine:
            cp.wait()

    return _pcall(
        body, name=name, out_shape=[jax.ShapeDtypeStruct((N_DEV,) + s.shape, s.dtype) for s in shards],
        in_specs=[_HBM] * n, out_specs=[_HBM] * n,
        scratch_shapes=[pltpu.SemaphoreType.DMA((7 * n,)), pltpu.SemaphoreType.DMA((7 * n,)),
                        pltpu.SemaphoreType.DMA((n,))],
    )(*shards)


def _sibling_exchange(name, xs):
    n = len(xs)

    def body(*refs):
        x_refs, out_refs, send_sems, recv_sems = refs[:n], refs[n:2 * n], refs[2 * n], refs[2 * n + 1]
        c = lax.axis_index("c")
        sibling = (lax.axis_index("x"), lax.axis_index("y"), 1 - c)
        cps = [pltpu.make_async_remote_copy(src_ref=x_refs[a].at[:, 1 - c], dst_ref=out_refs[a],
                                            send_sem=send_sems.at[a], recv_sem=recv_sems.at[a], device_id=sibling,
                                            device_id_type=_MESH)
               for a in range(n)]
        for cp in cps:
            cp.start()
        for cp in cps:
            cp.wait()

    return _pcall(
        body, name=name, out_shape=[jax.ShapeDtypeStruct(x.shape[:1] + x.shape[2:], x.dtype) for x in xs],
        in_specs=[_HBM] * n, out_specs=[_HBM] * n,
        scratch_shapes=[pltpu.SemaphoreType.DMA((n,)), pltpu.SemaphoreType.DMA((n,))],
    )(*xs)


def _chip_exchange(name, xs):
    n = len(xs)

    def body(*refs):
        x_refs, out_refs, send_sems, recv_sems = refs[:n], refs[n:2 * n], refs[2 * n], refs[2 * n + 1]
        mx, my, mc = lax.axis_index("x"), lax.axis_index("y"), lax.axis_index("c")
        chips = [(1 - mx, my), (mx, 1 - my), (1 - mx, 1 - my)]
        cps = [pltpu.make_async_remote_copy(src_ref=x_refs[a].at[2 * chips[j][0] + chips[j][1]],
                                            dst_ref=out_refs[a].at[j],
                                            send_sem=send_sems.at[3 * a + j], recv_sem=recv_sems.at[3 * a + j],
                                            device_id=(*chips[j], mc), device_id_type=_MESH)
               for j in (2, 0, 1) for a in range(n)]
        for cp in cps:
            cp.start()
        for cp in cps:
            cp.wait()

    return _pcall(
        body, name=name, out_shape=[jax.ShapeDtypeStruct((3,) + x.shape[1:], x.dtype) for x in xs],
        in_specs=[_HBM] * n, out_specs=[_HBM] * n,
        scratch_shapes=[pltpu.SemaphoreType.DMA((3 * n,)), pltpu.SemaphoreType.DMA((3 * n,))],
    )(*xs)


def _pair_sum(name, a, b):
    nb, R, C = a.shape
    tm = _tile(R, 512, 16)

    def fn(i, ab, bb):
        s = ab.astype(F32) + bb.astype(F32)
        return s, s

    spec = lambda arr: (arr, (1, tm, C), lambda i: (i // (R // tm), i % (R // tm), 0))
    out = lambda dt: ((nb, R, C), dt, (1, tm, C), lambda i: (i // (R // tm), i % (R // tm), 0))
    return _rowwise(name, fn, nb * (R // tm), [spec(a), spec(b)], [out(F32), out(a.dtype)])


def _sum8(name, g):
    n, R, C = g.shape
    tm = _tile(R, 256)

    def fn(i, gb):
        s = gb[0]
        for d in range(1, n):
            s = s + gb[d]
        return (s,)

    return _rowwise(name, fn, R // tm, [(g, (n, tm, C), lambda i: (0, i, 0))], [_orows(R, C, F32, tm)])[0]


def _adamw(name, w, m, v, grads, tm=512):
    R, C = w.shape
    tm = _tile(R, tm, 16)
    ng = len(grads)

    def fn(i, wb, mb, vb, *gs):
        g = gs[0].astype(F32)
        for t in gs[1:]:
            g = g + t.astype(F32)
        mn = ADAM_B1 * mb + (1.0 - ADAM_B1) * g
        vn = ADAM_B2 * vb + (1.0 - ADAM_B2) * (g * g)
        m_hat = mn / (1.0 - ADAM_B1 ** ADAM_STEP)
        v_hat = vn / (1.0 - ADAM_B2 ** ADAM_STEP)
        delta = -ADAM_LR * (m_hat / (jnp.sqrt(v_hat) + ADAM_EPS) + ADAM_WD * wb)
        return g, delta, mn, vn

    ins = [_rows(w, tm), _rows(m, tm), _rows(v, tm)] + [_rows(g, tm) for g in grads]
    return _rowwise(name, fn, R // tm, ins, [_orows(R, C, F32, tm) for _ in range(4)])


CLASSES = [
    (BF16, True, [('ssm_w_in', None), ('attn_w_q', None), ('attn_w_o', None), ('ffn_w_down', 0), ('ffn_w_down', 1)]),
    (BF16, False, [('ssm_w_glu', None), ('kv_w', None)]),
    (BF16, False, [('ffn_w_up', 0), ('ffn_w_up', 1)]),
    (F32, False, [('ffn_conv_w', 0), ('ffn_conv_w', 1)]),
]


def _shard2d(a, name, layer):
    if name == 'ffn_conv_w':
        return a[layer, :, 0, :]
    if name == 'kv_w':
        return a
    return a[0 if layer is None else layer]


def _weights_of(members):
    names = []
    for n, _ in members:
        if not names or names[-1] != n:
            names.append(n)
    return names


def _pack_class(d, members, dtype):
    width = _shard2d(d[members[0][0]], *members[0]).shape[1]
    parts = [d[n].reshape(-1, width).astype(dtype) for n in _weights_of(members)]
    return parts[0] if len(parts) == 1 else jnp.concatenate(parts, axis=0)


def _full_weights(gaths, d):
    w = {}
    for (_, row_sharded, members), g in zip(CLASSES, gaths):
        off = 0
        for n, l in members:
            r = _shard2d(d[n], n, l).shape[0]
            blk = g[:, off:off + r]
            if row_sharded:
                w[(n, l)] = blk.reshape(N_DEV * r, g.shape[2])
            else:
                w[(n, l)] = blk.transpose(1, 0, 2).reshape(r, N_DEV * g.shape[2])
            off += r
    return w


def _grad_classes(gw):
    out = []
    for dtype, row_sharded, members in CLASSES:
        parts = []
        for n, l in members:
            g = gw[(n, l)].astype(dtype)
            if row_sharded:
                parts.append(g.reshape(N_DEV, g.shape[0] // N_DEV, g.shape[1]))
            else:
                parts.append(g.reshape(g.shape[0], N_DEV, g.shape[1] // N_DEV).transpose(1, 0, 2))
        out.append(jnp.concatenate(parts, axis=1))
    return out


def _unpack_class(buf, members, shapes):
    out, off = {}, 0
    for n in _weights_of(members):
        r = math.prod(shapes[n]) // buf.shape[1]
        out[n] = buf[off:off + r].reshape(shapes[n])
        off += r
    return out


def _small_layout(shapes, D):
    lay, off = {}, 0
    for n in SMALL:
        r = -(-math.prod(shapes[n]) // D)
        lay[n] = (off, r)
        off += r
    lay['loss'] = (off, 1)
    off += 1
    return lay, -(-off // 8) * 8


def _pack_small(d, lay, total, D):
    parts = []
    for n in SMALL + ['loss']:
        if n not in d:
            parts.append(jnp.zeros((lay[n][1], D), F32))
            continue
        flat = d[n].reshape(-1).astype(F32)
        parts.append(jnp.pad(flat, (0, lay[n][1] * D - flat.shape[0])).reshape(lay[n][1], D))
    used = sum(lay[n][1] for n in SMALL + ['loss'])
    if total > used:
        parts.append(jnp.zeros((total - used, D), F32))
    return jnp.concatenate(parts, axis=0)


def _unpack_small(pack, lay, shapes, D):
    out = {}
    for n in SMALL:
        off, r = lay[n]
        out[n] = pack[off:off + r].reshape(-1)[:math.prod(shapes[n])].reshape(shapes[n])
    return out


def _ffn_fwd(tag, x, g_norm, w_up, conv_w, conv_b, w_down):
    h = _rms_fwd(f"{tag}_norm", x, g_norm)
    gu = _mm(f"{tag}_up", h, w_up, 'nn', BF16)
    a = _gate_fwd(f"{tag}_gate", gu, conv_w, conv_b)
    return _mm(f"{tag}_down", a, w_down, 'nn', F32, resid=x), (h, gu, a)


def _ffn_bwd(tag, dres, dres_bf, x, saved, g_norm, w_up, conv_w, conv_b, w_down):
    h, gu, a = saved
    da = _mm(f"{tag}_dgate", dres_bf, w_down, 'nt', BF16)
    d_w_down = _mm(f"{tag}_dwdown", a, dres_bf, 'tn', BF16)
    dgu, dw0, dw1, dw2, dcb = _gate_bwd(f"{tag}_gate_b", gu, da, conv_w, conv_b)
    d_w_up = _mm(f"{tag}_dwup", h, dgu, 'tn', BF16)
    dh = _mm(f"{tag}_dh", dgu, w_up, 'nt', F32)
    dres, dres_bf, dg = _rms_bwd(f"{tag}_norm_b", x, dres, [(g_norm, dh)])
    return dres, dres_bf, dg, d_w_up, jnp.concatenate([dw0, dw1, dw2], axis=0), dcb[0], d_w_down


def kernel(x, norm_mix, norm_ffn, norm_kv, norm_final, ssm_w_in, ssm_a_re, ssm_a_im, ssm_log_dt, ssm_b_re, ssm_b_im, ssm_c_re, ssm_c_im, ssm_d, ssm_w_glu, kv_w, attn_w_q, attn_w_o, ffn_w_up, ffn_conv_w, ffn_conv_b, ffn_w_down, loss_target, m_norm_mix, m_norm_ffn, m_norm_kv, m_norm_final, m_ssm_w_in, m_ssm_a_re, m_ssm_a_im, m_ssm_log_dt, m_ssm_b_re, m_ssm_b_im, m_ssm_c_re, m_ssm_c_im, m_ssm_d, m_ssm_w_glu, m_kv_w, m_attn_w_q, m_attn_w_o, m_ffn_w_up, m_ffn_conv_w, m_ffn_conv_b, m_ffn_w_down, v_norm_mix, v_norm_ffn, v_norm_kv, v_norm_final, v_ssm_w_in, v_ssm_a_re, v_ssm_a_im, v_ssm_log_dt, v_ssm_b_re, v_ssm_b_im, v_ssm_c_re, v_ssm_c_im, v_ssm_d, v_ssm_w_glu, v_kv_w, v_attn_w_q, v_attn_w_o, v_ffn_w_up, v_ffn_conv_w, v_ffn_conv_b, v_ffn_w_down):
    wts = dict(zip(W_NAMES, (norm_mix, norm_ffn, norm_kv, norm_final, ssm_w_in, ssm_a_re, ssm_a_im, ssm_log_dt,
                             ssm_b_re, ssm_b_im, ssm_c_re, ssm_c_im, ssm_d, ssm_w_glu, kv_w, attn_w_q, attn_w_o,
                             ffn_w_up, ffn_conv_w, ffn_conv_b, ffn_w_down)))
    mom = dict(zip(W_NAMES, (m_norm_mix, m_norm_ffn, m_norm_kv, m_norm_final, m_ssm_w_in, m_ssm_a_re, m_ssm_a_im,
                             m_ssm_log_dt, m_ssm_b_re, m_ssm_b_im, m_ssm_c_re, m_ssm_c_im, m_ssm_d, m_ssm_w_glu,
                             m_kv_w, m_attn_w_q, m_attn_w_o, m_ffn_w_up, m_ffn_conv_w, m_ffn_conv_b, m_ffn_w_down)))
    vel = dict(zip(W_NAMES, (v_norm_mix, v_norm_ffn, v_norm_kv, v_norm_final, v_ssm_w_in, v_ssm_a_re, v_ssm_a_im,
                             v_ssm_log_dt, v_ssm_b_re, v_ssm_b_im, v_ssm_c_re, v_ssm_c_im, v_ssm_d, v_ssm_w_glu,
                             v_kv_w, v_attn_w_q, v_attn_w_o, v_ffn_w_up, v_ffn_conv_w, v_ffn_conv_b, v_ffn_w_down)))
    shapes = {n: wts[n].shape for n in W_NAMES}
    _, L, D = x.shape
    Fh = ffn_conv_b.shape[1]
    G, P = ssm_a_re.shape[1], ssm_a_re.shape[2]
    Hg = ssm_d.shape[2]
    x0 = x[0]
    target = loss_target[0]
    scale = HEAD_DIM ** -0.5

    gaths = _all_gather("gather_weights", [_pack_class(wts, members, dt) for dt, _, members in CLASSES])
    w = _full_weights(gaths, wts)
    nm = [norm_mix[l:l + 1] for l in range(2)]
    nf = [norm_ffn[l:l + 1] for l in range(2)]
    nkv = norm_kv[None]
    nfin = norm_final[None]
    cb = [ffn_conv_b[l:l + 1] for l in range(2)]
    cwt = [w[('ffn_conv_w', l)] for l in range(2)]

    s5p = (ssm_a_re[0], ssm_a_im[0], ssm_log_dt[0], ssm_b_re[0], ssm_b_im[0], ssm_c_re[0], ssm_c_im[0], ssm_d[0])
    (m_mat, n_mat, o_mat, lam1, lam2), s5_vjp = jax.vjp(_s5_build, *s5p)
    m_bf, n_bf, o_bf = m_mat.astype(BF16), n_mat.astype(BF16), o_mat.astype(BF16)

    h0 = _rms_fwd("l0_norm", x0, nm[0])
    u = _mm("l0_win", h0, w[('ssm_w_in', None)], 'nn', BF16)
    ug = _to_groups("s5_group_u", u, G, Hg, BF16)
    vloc = _bmm("s5_local_state", [(ug, n_bf, 'nt')], F32)
    st = _chunk_scan("s5_scan", vloc, lam1, lam2, reverse=False)
    yraw, yg = _bmm("s5_out", [(ug, m_bf, 'nt'), (st, o_bf, 'nt')], (F32, BF16),
                    post=lambda acc: (acc, _gelu(acc)), ungroup=(1, Hg))
    z = _mm("l0_wglu", yg, w[('ssm_w_glu', None)], 'nn', F32)
    x1 = _glu_fwd("l0_glu", x0, z)
    x2, ffn0 = _ffn_fwd("f0", x1, nf[0], w[('ffn_w_up', 0)], cwt[0], cb[0], w[('ffn_w_down', 0)])

    hkv = _rms_fwd("kv_norm", x2, nkv)
    nh = D // HEAD_DIM
    w_kv = w[('kv_w', None)].reshape(D, 2, nh, HEAD_DIM).transpose(0, 2, 1, 3).reshape(D, 2 * D)
    kvp = _mm("kv_proj", hkv, w_kv, 'nn', BF16)
    h2 = _rms_fwd("l1_norm", x2, nm[1])
    qn = _mm("l1_wq", h2, w[('attn_w_q', None)], 'nn', BF16, scale=scale)
    o, rc = _attn_fwd("attn_fwd", qn, kvp, ATT_TQ, ATT_TK, ATT_HB_FWD)
    x3 = _mm("l1_wo", o, w[('attn_w_o', None)], 'nn', F32, resid=x2)
    x4, ffn1 = _ffn_fwd("f1", x3, nf[1], w[('ffn_w_up', 1)], cwt[1], cb[1], w[('ffn_w_down', 1)])

    dres, dres_bf, dg_final, loss_part = _final_loss("loss_head", x4, nfin, target)

    gw = {}
    dres, dres_bf, dg_nf1, gup1, gcw1, gcb1, gdn1 = _ffn_bwd("f1", dres, dres_bf, x3, ffn1, nf[1], w[('ffn_w_up', 1)],
                                                             cwt[1], cb[1], w[('ffn_w_down', 1)])
    do = _mm("l1_do", dres_bf, w[('attn_w_o', None)], 'nt', BF16)
    gw[('attn_w_o', None)] = _mm("l1_dwo", o, dres_bf, 'tn', BF16)
    dq, dkv = _attn_bwd("attn_bwd", qn, kvp, do, rc, ATT_TQ, ATT_TK, scale, ATT_HB_BWD)
    gw[('attn_w_q', None)] = _mm("l1_dwq", h2, dq, 'tn', BF16)
    dh2 = _mm("l1_dh", dq, w[('attn_w_q', None)], 'nt', F32)
    g_kv = _mm("kv_dw", hkv, dkv, 'tn', BF16)
    gw[('kv_w', None)] = g_kv.reshape(D, nh, 2, HEAD_DIM).transpose(0, 2, 1, 3).reshape(D, 2 * D)
    dhkv = _mm("kv_dh", dkv, w_kv, 'nt', F32)
    dres, dres_bf, dg_nm1, dg_nkv = _rms_bwd("l1_norm_b", x2, dres, [(nm[1], dh2), (nkv, dhkv)])
    dres, dres_bf, dg_nf0, gup0, gcw0, gcb0, gdn0 = _ffn_bwd("f0", dres, dres_bf, x1, ffn0, nf[0], w[('ffn_w_up', 0)],
                                                             cwt[0], cb[0], w[('ffn_w_down', 0)])
    dz = _glu_bwd("l0_glu_b", dres, z)
    gw[('ssm_w_glu', None)] = _mm("l0_dwglu", yg, dz, 'tn', BF16)
    dyg = _mm("l0_dyg", dz, w[('ssm_w_glu', None)], 'nt', F32)
    dy = _to_groups("s5_group_dy", dyg, G, Hg, BF16, gelu_arg=yraw)
    ds = _bmm("s5_dstate", [(dy, o_bf, 'nn')], F32)
    dv_loc, dlam1, dlam2 = _chunk_scan("s5_scan_b", ds, lam1, lam2, reverse=True, s_fwd=st)
    du = _bmm("s5_du", [(dy, m_bf, 'nn'), (dv_loc, n_bf, 'nn')], BF16, ungroup=(0, Hg))
    d_m = _bmm("s5_dm", [(dy, ug, 'tn')], F32)
    d_o = _bmm("s5_do", [(dy, st, 'tn')], F32)
    d_n = _bmm("s5_dn", [(dv_loc, ug, 'tn')], F32)
    s5g = s5_vjp((d_m, d_n, d_o, dlam1, dlam2))
    gw[('ssm_w_in', None)] = _mm("l0_dwin", h0, du, 'tn', BF16)
    dh0 = _mm("l0_dh", du, w[('ssm_w_in', None)], 'nt', F32)
    grad_x, _, dg_nm0 = _rms_bwd("l0_norm_b", x0, dres, [(nm[0], dh0)])

    gw.update({('ffn_w_up', 0): gup0, ('ffn_w_up', 1): gup1, ('ffn_w_down', 0): gdn0, ('ffn_w_down', 1): gdn1,
               ('ffn_conv_w', 0): gcw0, ('ffn_conv_w', 1): gcw1})

    small_g = {
        'norm_mix': jnp.concatenate([dg_nm0, dg_nm1], axis=0), 'norm_ffn': jnp.concatenate([dg_nf0, dg_nf1], axis=0),
        'norm_kv': dg_nkv, 'norm_final': dg_final, 'ffn_conv_b': jnp.stack([gcb0, gcb1]),
        'ssm_a_re': s5g[0], 'ssm_a_im': s5g[1], 'ssm_log_dt': s5g[2], 'ssm_b_re': s5g[3], 'ssm_b_im': s5g[4],
        'ssm_c_re': s5g[5], 'ssm_c_im': s5g[6], 'ssm_d': s5g[7], 'loss': loss_part[0, 0:1],
    }
    lay, rs = _small_layout(shapes, D)
    small_sum = _sum8("small_sum", _all_gather("gather_small", [_pack_small(small_g, lay, rs, D)])[0])
    loss = small_sum[lay['loss'][0], 0]
    sg, sdelta, sm, sv = _adamw("adamw_small", _pack_small(wts, lay, rs, D), _pack_small(mom, lay, rs, D),
                                _pack_small(vel, lay, rs, D), [small_sum])
    small_out = [_unpack_small(t, lay, shapes, D) for t in (sg, sdelta, sm, sv)]

    c = lax.axis_index("c")
    chip = 2 * lax.axis_index("x") + lax.axis_index("y")
    gcls = [g.reshape((4, 2) + g.shape[1:]) for g in _grad_classes(gw)]
    mine = [lax.dynamic_index_in_dim(g, c, axis=1, keepdims=False) for g in gcls]
    got = _sibling_exchange("rs_sibling", gcls)
    sums = [_pair_sum(f"rs_pair_sum{k}", a, b) for k, (a, b) in enumerate(zip(mine, got))]
    from_chips = _chip_exchange("rs_chips", [on_wire for _, on_wire in sums])
    big_parts = [{}, {}, {}, {}]
    for k, ((_, _, members), (chip_f32, _), fc) in enumerate(zip(CLASSES, sums, from_chips)):
        own = lax.dynamic_index_in_dim(chip_f32, chip, axis=0, keepdims=False)
        res = _adamw(f"adamw_big{k}", _pack_class(wts, members, F32), _pack_class(mom, members, F32),
                     _pack_class(vel, members, F32), [own, fc[0], fc[1], fc[2]])
        for q in range(4):
            big_parts[q].update(_unpack_class(res[q], members, shapes))
    big_out = big_parts

    outs = [loss, grad_x[None]]
    for k in range(4):
        for n in W_NAMES:
            outs.append(big_out[k][n] if n in BIG else small_out[k][n])
    return tuple(outs)
```

```python
import math

import jax
import jax.numpy as jnp
from jax import lax
from jax.experimental import pallas as pl
from jax.experimental.pallas import tpu as pltpu

F32 = jnp.float32
BF16 = jnp.bfloat16

EPS = 1e-6
HEAD_DIM = 64
CHUNK = 16
N_DEV = 8
ADAM_LR = 0.001
ADAM_B1 = 0.9
ADAM_B2 = 0.999
ADAM_EPS = 1e-08
ADAM_WD = 0.01
ADAM_STEP = 10
VMEM_LIMIT = 48 * 1024 * 1024
MM_BLOCK_BYTES = 28 * 1024 * 1024
MM_TM_MAX = 1024
MM_T_MAX = 512
MM_TN_MAX = 1536
LANES = 128
ATT_TQ = 128
ATT_TK = 256
ATT_HB_FWD = 16
ATT_HB_BWD = 8

W_NAMES = ['norm_mix', 'norm_ffn', 'norm_kv', 'norm_final', 'ssm_w_in', 'ssm_a_re', 'ssm_a_im', 'ssm_log_dt',
           'ssm_b_re', 'ssm_b_im', 'ssm_c_re', 'ssm_c_im', 'ssm_d', 'ssm_w_glu', 'kv_w', 'attn_w_q', 'attn_w_o',
           'ffn_w_up', 'ffn_conv_w', 'ffn_conv_b', 'ffn_w_down']
BIG = ['ssm_w_in', 'ssm_w_glu', 'kv_w', 'attn_w_q', 'attn_w_o', 'ffn_w_up', 'ffn_w_down', 'ffn_conv_w']
SMALL = [n for n in W_NAMES if n not in BIG]


def _pcall(body, **kw):
    return pl.pallas_call(body, **kw)


def _params(sem=None):
    if sem is None:
        return pltpu.CompilerParams(vmem_limit_bytes=VMEM_LIMIT)
    return pltpu.CompilerParams(dimension_semantics=sem, vmem_limit_bytes=VMEM_LIMIT)


def _tile(n, pref, mult=8):
    best = None
    for t in range(mult, min(n, pref) + 1, mult):
        if n % t == 0:
            best = t
    return n if best is None else best


def _mm_tiles(M, N, K, mode, a_bytes, b_bytes, o_bytes):
    def cands(n, cap):
        c = [t for t in range(LANES, min(n, cap) + 1, LANES) if n % t == 0]
        return c or [n]
    best = None
    for tm in cands(M, MM_T_MAX if mode == 'tn' else MM_TM_MAX):
        for tn in cands(N, MM_TN_MAX):
            need = 2 * (tm * K * a_bytes + K * tn * b_bytes + tm * tn * o_bytes)
            score = tm * tn * (tm if mode == 'tn' else 1)
            if need <= MM_BLOCK_BYTES and (best is None or score > best[0]):
                best = (score, tm, tn)
    assert best is not None, (M, N, K)
    return best[1], best[2]


def _mm(name, a, b, mode, out_dtype=F32, scale=None, resid=None):
    if mode == 'nn':
        (M, K), (K2, N) = a.shape, b.shape
    elif mode == 'nt':
        (M, K), (N, K2) = a.shape, b.shape
    else:
        (K, M), (K2, N) = a.shape, b.shape
    assert K == K2, (name, a.shape, b.shape)
    o_bytes = jnp.dtype(out_dtype).itemsize + (resid.dtype.itemsize if resid is not None else 0)
    tm, tn = _mm_tiles(M, N, K, mode, a.dtype.itemsize, b.dtype.itemsize, o_bytes)
    if mode == 'tn':
        a_spec = pl.BlockSpec((K, tm), lambda i, j: (0, i))
    else:
        a_spec = pl.BlockSpec((tm, K), lambda i, j: (i, 0))
    if mode == 'nt':
        b_spec = pl.BlockSpec((tn, K), lambda i, j: (j, 0))
    else:
        b_spec = pl.BlockSpec((K, tn), lambda i, j: (0, j))
    o_spec = pl.BlockSpec((tm, tn), lambda i, j: (i, j))
    dn = {'nn': ((1,), (0,)), 'nt': ((1,), (1,)), 'tn': ((0,), (0,))}[mode]

    def body(*refs):
        a_ref, b_ref, o_ref = refs[0], refs[1], refs[-1]
        acc = lax.dot_general(a_ref[...].astype(BF16), b_ref[...].astype(BF16), (dn, ((), ())),
                              preferred_element_type=F32)
        if scale is not None:
            acc = acc * scale
        if resid is not None:
            acc = acc + refs[2][...]
        o_ref[...] = acc.astype(o_ref.dtype)

    return _pcall(
        body, name=name, grid=(M // tm, N // tn),
        in_specs=[a_spec, b_spec] + ([o_spec] if resid is not None else []),
        out_specs=o_spec,
        out_shape=jax.ShapeDtypeStruct((M, N), out_dtype),
        compiler_params=_params(("parallel", "parallel")),
    )(*([a, b] + ([resid] if resid is not None else [])))


def _bmm(name, terms, out_dtype, gb=8, post=None, ungroup=None):
    G = terms[0][0].shape[0]
    gb = _tile(G, gb, 1)
    dns = {'nn': ((1,), (0,)), 'nt': ((1,), (1,)), 'tn': ((0,), (0,))}

    def oshape(a, b, mode):
        m = a.shape[2] if mode == 'tn' else a.shape[1]
        n = b.shape[1] if mode == 'nt' else b.shape[2]
        return m, n

    m, n = oshape(*terms[0])
    out_dtypes = out_dtype if isinstance(out_dtype, (tuple, list)) else (out_dtype,)
    n_in = 2 * len(terms)
    if ungroup is not None:
        perm, gpb = _lane_perm(ungroup[1])
        assert gpb == gb and n == CHUNK * ungroup[1]

    def body(*refs):
        ins, outs = refs[:n_in], refs[len(refs) - len(out_dtypes):]
        kept = []
        for gi in range(gb):
            acc = None
            for t, (_, _, mode) in enumerate(terms):
                part = lax.dot_general(ins[2 * t][gi].astype(BF16), ins[2 * t + 1][gi].astype(BF16),
                                       (dns[mode], ((), ())), preferred_element_type=F32)
                acc = part if acc is None else acc + part
            vals = (acc,) if post is None else post(acc)
            for k, (o_ref, v) in enumerate(zip(outs, vals)):
                if ungroup is not None and k == ungroup[0]:
                    kept.append(v.astype(BF16))
                else:
                    o_ref[gi] = v.astype(o_ref.dtype)
        if ungroup is not None:
            x = lax.dot_general(jnp.concatenate(kept, axis=1), refs[n_in][...], (((1,), (1,)), ((), ())),
                                preferred_element_type=F32)
            o_ref = outs[ungroup[0]]
            for t in range(CHUNK):
                o_ref[:, t, :] = x[:, t * LANES:(t + 1) * LANES].astype(o_ref.dtype)

    in_specs, args = [], []
    for a, b, _ in terms:
        in_specs += [pl.BlockSpec((gb,) + a.shape[1:], lambda g: (g, 0, 0)),
                     pl.BlockSpec((gb,) + b.shape[1:], lambda g: (g, 0, 0))]
        args += [a, b]
    out_specs = [pl.BlockSpec((gb, m, n), lambda g: (g, 0, 0)) for _ in out_dtypes]
    out_shape = [jax.ShapeDtypeStruct((G, m, n), dt) for dt in out_dtypes]
    if ungroup is not None:
        in_specs.append(pl.BlockSpec(perm.shape, lambda g: (0, 0)))
        args.append(perm)
        out_specs[ungroup[0]] = pl.BlockSpec((m, CHUNK, LANES), lambda g: (0, 0, g))
        out_shape[ungroup[0]] = jax.ShapeDtypeStruct((m, CHUNK, G * ungroup[1]), out_dtypes[ungroup[0]])
    res = _pcall(
        body, name=name, grid=(G // gb,), in_specs=in_specs, out_specs=out_specs, out_shape=out_shape,
        compiler_params=_params(("parallel",)),
    )(*args)
    if ungroup is not None:
        res = list(res)
        res[ungroup[0]] = res[ungroup[0]].reshape(m * CHUNK, G * ungroup[1])
    return res[0] if len(out_dtypes) == 1 else res


def _rowwise(name, fn, n_steps, ins, outs, n_acc=0):
    n_in, n_out = len(ins), len(outs)

    def body(*refs):
        i = pl.program_id(0)
        vals = fn(i, *[r[...] for r in refs[:n_in]])
        o_refs = refs[n_in:]
        for j in range(n_out - n_acc):
            o_refs[j][...] = vals[j].astype(o_refs[j].dtype)
        if n_acc:
            @pl.when(i == 0)
            def _():
                for j in range(n_out - n_acc, n_out):
                    o_refs[j][...] = vals[j].astype(o_refs[j].dtype)

            @pl.when(i > 0)
            def _():
                for j in range(n_out - n_acc, n_out):
                    o_refs[j][...] += vals[j].astype(o_refs[j].dtype)

    res = _pcall(
        body, name=name, grid=(n_steps,),
        in_specs=[pl.BlockSpec(blk, im) for _, blk, im in ins],
        out_specs=[pl.BlockSpec(blk, im) for _, _, blk, im in outs],
        out_shape=[jax.ShapeDtypeStruct(s, d) for s, d, _, _ in outs],
        compiler_params=_params(("arbitrary",)),
    )(*[a for a, _, _ in ins])
    return res


def _rows(a, tm):
    return (a, (tm, a.shape[1]), lambda i: (i, 0))


def _whole(a):
    nd = a.ndim
    return (a, a.shape, lambda i: (0,) * nd)


def _orows(L, n, dtype, tm):
    return ((L, n), dtype, (tm, n), lambda i: (i, 0))


def _oacc(r, n):
    return ((r, n), F32, (r, n), lambda i: (0, 0))


def _rms_fwd(name, x, g, tm=512):
    L, D = x.shape
    tm = _tile(L, tm)

    def fn(i, xb, gb):
        r = lax.rsqrt(jnp.mean(xb * xb, axis=-1, keepdims=True) + EPS)
        return (xb * r * gb,)

    return _rowwise(name, fn, L // tm, [_rows(x, tm), _whole(g)], [_orows(L, D, BF16, tm)])[0]


def _rms_bwd(name, x, dres, branches, tm=512):
    L, D = x.shape
    tm = _tile(L, tm)
    nb = len(branches)

    def fn(i, xb, db, *rest):
        r = lax.rsqrt(jnp.mean(xb * xb, axis=-1, keepdims=True) + EPS)
        xh = xb * r
        dx = db
        dgs = []
        for b in range(nb):
            gb, dyb = rest[2 * b], rest[2 * b + 1].astype(F32)
            dxh = dyb * gb
            dx = dx + r * (dxh - xh * jnp.mean(dxh * xh, axis=-1, keepdims=True))
            dgs.append(jnp.sum(dyb * xh, axis=0, keepdims=True))
        return (dx, dx, *dgs)

    ins = [_rows(x, tm), _rows(dres, tm)]
    for g, dy in branches:
        ins += [_whole(g), _rows(dy, tm)]
    outs = [_orows(L, D, F32, tm), _orows(L, D, BF16, tm)] + [_oacc(1, D) for _ in range(nb)]
    return _rowwise(name, fn, L // tm, ins, outs, n_acc=nb)


def _final_loss(name, x, g, target, tm=512):
    L, D = x.shape
    tm = _tile(L, tm)

    def fn(i, xb, gb, tb):
        r = lax.rsqrt(jnp.mean(xb * xb, axis=-1, keepdims=True) + EPS)
        xh = xb * r
        err = xh * gb - tb
        dy = err * (1.0 / D)
        dxh = dy * gb
        dx = r * (dxh - xh * jnp.mean(dxh * xh, axis=-1, keepdims=True))
        dg = jnp.sum(dy * xh, axis=0, keepdims=True)
        per_row = jnp.mean(err * err, axis=-1, keepdims=True)
        loss = 0.5 * jnp.sum(per_row, axis=0, keepdims=True)
        return dx, dx, dg, jnp.broadcast_to(loss, (1, LANES))

    return _rowwise(name, fn, L // tm, [_rows(x, tm), _whole(g), _rows(target, tm)],
                    [_orows(L, D, F32, tm), _orows(L, D, BF16, tm), _oacc(1, D), _oacc(1, LANES)], n_acc=2)


def _glu_fwd(name, x, z, tm=512):
    L, D = x.shape
    tm = _tile(L, tm)

    def fn(i, xb, zb):
        return (xb + zb[:, :D] * jax.nn.sigmoid(zb[:, D:]),)

    return _rowwise(name, fn, L // tm, [_rows(x, tm), _rows(z, tm)], [_orows(L, D, F32, tm)])[0]


def _glu_bwd(name, dx, z, tm=512):
    L, D = dx.shape
    tm = _tile(L, tm)

    def fn(i, db, zb):
        z1, sg = zb[:, :D], jax.nn.sigmoid(zb[:, D:])
        return (jnp.concatenate([(db * sg).astype(BF16), (db * z1 * sg * (1.0 - sg)).astype(BF16)], axis=1),)

    return _rowwise(name, fn, L // tm, [_rows(dx, tm), _rows(z, tm)], [_orows(L, 2 * D, BF16, tm)])[0]


HALO = 16


def _halo_prev(a, tm):
    return (a, (HALO, a.shape[1]), lambda i: (jnp.maximum(i * (tm // HALO) - 1, 0), 0))


def _halo_next(a, tm):
    last = a.shape[0] // HALO - 1
    return (a, (HALO, a.shape[1]), lambda i: (jnp.minimum((i + 1) * (tm // HALO), last), 0))


def _shift_down(cur, halo, k, first):
    tm = cur.shape[0]
    rolled = pltpu.roll(cur, k, 0)
    tail = pltpu.roll(halo, k, 0)
    tail = jnp.where(first, 0.0, tail)
    row = lax.broadcasted_iota(jnp.int32, (tm, 1), 0)
    head = jnp.concatenate([tail, jnp.zeros((tm - 8, cur.shape[1]), cur.dtype)], axis=0) if tm > 8 else tail
    return jnp.where(row < k, head, rolled)


def _shift_up(cur, halo, k, last):
    tm = cur.shape[0]
    rolled = pltpu.roll(cur, tm - k, 0)
    head = pltpu.roll(halo, 8 - k, 0)
    head = jnp.where(last, 0.0, head)
    row = lax.broadcasted_iota(jnp.int32, (tm, 1), 0)
    tail = jnp.concatenate([jnp.zeros((tm - 8, cur.shape[1]), cur.dtype), head], axis=0) if tm > 8 else head
    return jnp.where(row >= tm - k, tail, rolled)


def _conv_pre(gb, hb, cw, cb, first):
    g1 = _shift_down(gb, hb, 1, first)
    g2 = _shift_down(gb, hb, 2, first)
    return cw[0:1] * g2 + cw[1:2] * g1 + cw[2:3] * gb + cb, g1, g2


def _gate_fwd(name, gu, cw, cb, tm=256):
    L, F2 = gu.shape
    Fh = F2 // 2
    tm = _tile(L, tm)

    def fn(i, gub, halo, cwb, cbb):
        gub, halo = gub.astype(F32), halo.astype(F32)[HALO - 8:]
        gc, _, _ = _conv_pre(gub[:, :Fh], halo[:, :Fh], cwb, cbb, i == 0)
        return (gc * jax.nn.sigmoid(gc) * gub[:, Fh:],)

    return _rowwise(name, fn, L // tm, [_rows(gu, tm), _halo_prev(gu, tm), _whole(cw), _whole(cb)],
                    [_orows(L, Fh, BF16, tm)])[0]


def _gate_bwd(name, gu, da, cw, cb, tm=256):
    L, F2 = gu.shape
    Fh = F2 // 2
    tm = _tile(L, tm)
    n = L // tm

    def dgate(gb, ub, dab, hb, cwb, cbb, first):
        gc, g1, g2 = _conv_pre(gb, hb, cwb, cbb, first)
        sg = jax.nn.sigmoid(gc)
        return dab * ub * sg * (1.0 + gc * (1.0 - sg)), dab * gc * sg, g1, g2

    def fn(i, gub, prev, nxt, dab, dnx, cwb, cbb):
        gub, prev, nxt = gub.astype(F32), prev.astype(F32)[HALO - 8:], nxt.astype(F32)[:8]
        dab, dnx = dab.astype(F32), dnx.astype(F32)[:8]
        gb, ub = gub[:, :Fh], gub[:, Fh:]
        dgc, du, g1, g2 = dgate(gb, ub, dab, prev[:, :Fh], cwb, cbb, i == 0)
        dgc_next, _, _, _ = dgate(nxt[:, :Fh], nxt[:, Fh:], dnx, gb[tm - 8:], cwb, cbb, False)
        d1 = _shift_up(dgc, dgc_next, 1, i == n - 1)
        d2 = _shift_up(dgc, dgc_next, 2, i == n - 1)
        dg = cwb[2:3] * dgc + cwb[1:2] * d1 + cwb[0:1] * d2
        return (jnp.concatenate([dg.astype(BF16), du.astype(BF16)], axis=1),
                jnp.sum(dgc * g2, axis=0, keepdims=True), jnp.sum(dgc * g1, axis=0, keepdims=True),
                jnp.sum(dgc * gb, axis=0, keepdims=True), jnp.sum(dgc, axis=0, keepdims=True))

    return _rowwise(name, fn, n,
                    [_rows(gu, tm), _halo_prev(gu, tm), _halo_next(gu, tm), _rows(da, tm), _halo_next(da, tm),
                     _whole(cw), _whole(cb)],
                    [_orows(L, 2 * Fh, BF16, tm)] + [_oacc(1, Fh) for _ in range(4)], n_acc=4)


def _s5_build(a_re, a_im, log_dt, b_re, b_im, c_re, c_im, d):
    T = CHUNK
    G, P = a_re.shape
    H = d.shape[1]
    hi = lax.Precision.HIGH
    dt = jnp.exp(log_dt)[:, None]
    mag = jnp.exp(a_re * dt)
    ab_re = mag * jnp.cos(a_im * dt)
    ab_im = mag * jnp.sin(a_im * dt)
    den = a_re * a_re + a_im * a_im
    f_re = ((ab_re - 1.0) * a_re + ab_im * a_im) / den
    f_im = (ab_im * a_re - (ab_re - 1.0) * a_im) / den
    bb_re = f_re[..., None] * b_re - f_im[..., None] * b_im
    bb_im = f_re[..., None] * b_im + f_im[..., None] * b_re
    tau = jnp.arange(T + 1, dtype=F32)[None, :, None]
    pmag = jnp.exp(tau * (a_re * dt)[:, None, :])
    pang = tau * (a_im * dt)[:, None, :]
    pw_re = pmag * jnp.cos(pang)
    pw_im = pmag * jnp.sin(pang)
    cp_re = c_re[:, :, None, :] * pw_re[:, None] - c_im[:, :, None, :] * pw_im[:, None]
    cp_im = c_re[:, :, None, :] * pw_im[:, None] + c_im[:, :, None, :] * pw_re[:, None]
    cp_cat = jnp.concatenate([cp_re, -cp_im], axis=-1)
    bb_cat = jnp.concatenate([bb_re, bb_im], axis=1)
    kt = jnp.einsum('ghtq,gqk->ghtk', cp_cat[:, :, :T], bb_cat, precision=hi)
    kt = kt.at[:, :, 0, :].add(d[:, :, None] * jnp.eye(H, dtype=F32)[None])
    kp = jnp.concatenate([kt[:, :, ::-1, :], jnp.zeros((G, H, T - 1, H), F32)], axis=2).reshape(G, H, (2 * T - 1) * H)
    m_mat = jnp.stack([kp[:, :, (T - 1 - t) * H:(2 * T - 1 - t) * H] for t in range(T)], axis=1)
    m_mat = m_mat.reshape(G, T * H, T * H)
    pr = jnp.repeat(pw_re[:, :T][:, ::-1].transpose(0, 2, 1), H, axis=2)
    pi = jnp.repeat(pw_im[:, :T][:, ::-1].transpose(0, 2, 1), H, axis=2)
    br, bi = jnp.tile(bb_re, (1, 1, T)), jnp.tile(bb_im, (1, 1, T))
    n_mat = jnp.concatenate([pr * br - pi * bi, pr * bi + pi * br], axis=1)
    o_mat = cp_cat[:, :, 1:].transpose(0, 2, 1, 3).reshape(G, T * H, 2 * P)
    lam1 = jnp.concatenate([pw_re[:, T], pw_re[:, T]], axis=-1)[:, None, :]
    lam2 = jnp.concatenate([-pw_im[:, T], pw_im[:, T]], axis=-1)[:, None, :]
    return m_mat, n_mat, o_mat, lam1, lam2


def _chunk_scan(name, v, lam1, lam2, reverse, s_fwd=None, gb=64):
    G, nc, W = v.shape
    gb = _tile(G, gb, 1)
    half = W // 2
    ntile = nc // 8

    def body(*refs):
        if reverse:
            v_ref, l1_ref, l2_ref, s_ref, o_ref, d1_ref, d2_ref = refs
        else:
            v_ref, l1_ref, l2_ref, o_ref = refs
        l1 = jnp.broadcast_to(l1_ref[...], (gb, 8, W))
        l2 = jnp.broadcast_to(l2_ref[...], (gb, 8, W))
        if reverse:
            l2 = -l2
        row = lax.broadcasted_iota(jnp.int32, (gb, 8, W), 1)

        def tile_step(n, carry):
            if reverse:
                st, a1, a2 = carry
                base = pl.multiple_of((ntile - 1 - n) * 8, 8)
            else:
                st = carry
                base = pl.multiple_of(n * 8, 8)
            vt = v_ref[:, pl.ds(base, 8), :]
            out = jnp.zeros((gb, 8, W), F32)
            order = range(7, -1, -1) if reverse else range(8)
            for r in order:
                out = jnp.where(row == r, st, out)
                vr = jnp.broadcast_to(vt[:, r:r + 1, :], (gb, 8, W))
                st = l1 * st + l2 * pltpu.roll(st, half, 2) + vr
            o_ref[:, pl.ds(base, 8), :] = out
            if reverse:
                sv = s_ref[:, pl.ds(base, 8), :]
                a1 = a1 + out * sv
                a2 = a2 + out * pltpu.roll(sv, half, 2)
                return st, a1, a2
            return st

        zero = jnp.zeros((gb, 8, W), F32)
        if reverse:
            _, a1, a2 = lax.fori_loop(0, ntile, tile_step, (zero, zero, zero))
            d1_ref[...] = jnp.sum(a1, axis=1, keepdims=True)
            d2_ref[...] = jnp.sum(a2, axis=1, keepdims=True)
        else:
            lax.fori_loop(0, ntile, tile_step, zero)

    big = pl.BlockSpec((gb, nc, W), lambda g: (g, 0, 0), pipeline_mode=pl.Buffered(1))
    vec = pl.BlockSpec((gb, 1, W), lambda g: (g, 0, 0))
    if reverse:
        return _pcall(body, name=name, grid=(G // gb,), in_specs=[big, vec, vec, big],
                      out_specs=[big, vec, vec],
                      out_shape=[jax.ShapeDtypeStruct((G, nc, W), F32), jax.ShapeDtypeStruct((G, 1, W), F32),
                                 jax.ShapeDtypeStruct((G, 1, W), F32)],
                      compiler_params=_params(("parallel",)))(v, lam1, lam2, s_fwd)
    return _pcall(body, name=name, grid=(G // gb,), in_specs=[big, vec, vec], out_specs=big,
                  out_shape=jax.ShapeDtypeStruct((G, nc, W), F32),
                  compiler_params=_params(("parallel",)))(v, lam1, lam2)


_GELU_C = math.sqrt(2.0 / math.pi)


def _gelu(y):
    return 0.5 * y * (1.0 + jnp.tanh(_GELU_C * (y + 0.044715 * y * y * y)))


def _gelu_grad(y):
    t = jnp.tanh(_GELU_C * (y + 0.044715 * y * y * y))
    return 0.5 * (1.0 + t) + 0.5 * y * (1.0 - t * t) * _GELU_C * (1.0 + 3.0 * 0.044715 * y * y)


def _lane_perm(Hg):
    gpb = LANES // Hg
    r = jnp.arange(CHUNK * LANES)
    t, gl, h = r // LANES, (r % LANES) // Hg, r % Hg
    target = gl * (CHUNK * Hg) + t * Hg + h
    return (target[:, None] == r[None, :]).astype(BF16), gpb


def _to_groups(name, a, G, Hg, out_dtype, gelu_arg=None):
    L, D = a.shape
    nc = L // CHUNK
    perm, gpb = _lane_perm(Hg)
    gw = CHUNK * Hg

    def body(*refs):
        x_ref, p_ref, o_ref = refs[0], refs[1], refs[-1]
        x = jnp.concatenate([x_ref[:, t, :] for t in range(CHUNK)], axis=1)
        if x.dtype == BF16:
            z = jnp.dot(x, p_ref[...], preferred_element_type=F32)
        else:
            z = _split_dot(x, p_ref[...])
        for gl in range(gpb):
            piece = z[:, gl * gw:(gl + 1) * gw]
            if gelu_arg is not None:
                piece = piece * _gelu_grad(refs[2][gl])
            o_ref[gl] = piece.astype(o_ref.dtype)

    grouped = pl.BlockSpec((gpb, nc, gw), lambda g: (g, 0, 0))
    return _pcall(
        body, name=name, grid=(G // gpb,),
        in_specs=[pl.BlockSpec((nc, CHUNK, LANES), lambda g: (0, 0, g)),
                  pl.BlockSpec(perm.shape, lambda g: (0, 0))] + ([grouped] if gelu_arg is not None else []),
        out_specs=grouped,
        out_shape=jax.ShapeDtypeStruct((G, nc, gw), out_dtype),
        compiler_params=_params(("parallel",)),
    )(*([a.reshape(nc, CHUNK, D), perm] + ([gelu_arg] if gelu_arg is not None else [])))


def _split_dot(x, u):
    hi = x.astype(BF16)
    lo = (x - hi.astype(F32)).astype(BF16)
    return (jnp.dot(hi, u, preferred_element_type=F32) + jnp.dot(lo, u, preferred_element_type=F32))


def _col_to_row(col, n):
    return jnp.broadcast_to(col, (n, LANES)).T[0:1, :]


def _row_to_col(row, n):
    return jnp.broadcast_to(row, (LANES, n)).T[:, 0:1]


def _head_in(pair, h, upper):
    x = pair.astype(F32)
    lane = lax.broadcasted_iota(jnp.int32, x.shape, 1)
    if (h % 2 == 1) != upper:
        x = pltpu.roll(x, HEAD_DIM, 1)
    keep = (lane >= HEAD_DIM) if upper else (lane < HEAD_DIM)
    return jnp.where(keep, x, 0.0).astype(BF16)


def _pair_out(even, odd, upper):
    lane = lax.broadcasted_iota(jnp.int32, even.shape, 1)
    if upper:
        return jnp.where(lane < HEAD_DIM, pltpu.roll(even, HEAD_DIM, 1), odd)
    return jnp.where(lane < HEAD_DIM, even, pltpu.roll(odd, HEAD_DIM, 1))


def _attn_fwd(name, q, kv, tq, tk, hb):
    L = q.shape[0]
    H = q.shape[1] // HEAD_DIM
    hb = min(hb, H)
    dw = 2 * HEAD_DIM
    nq, nkb = L // tq, L // tk
    hs = range(hb)
    assert tk % tq == 0 and hb % 2 == 0 and H % hb == 0

    def body(q_ref, kv_ref, o_ref, rc_ref):
        i = pl.program_id(1)
        diff = (lax.broadcasted_iota(jnp.int32, (tq, tk), 1) - lax.broadcasted_iota(jnp.int32, (tq, tk), 0))
        u_suf = (lax.broadcasted_iota(jnp.int32, (tk, tk), 0)
                 > lax.broadcasted_iota(jnp.int32, (tk, tk), 1)).astype(BF16)
        nfull = (i * tq) // tk
        rc_ref[...] = jnp.zeros_like(rc_ref)
        qw = [_head_in(q_ref[:, dw * (h // 2):dw * (h // 2 + 1)], h, False) for h in hs]

        def block(kb, carry, masked):
            runs, accs = carry
            ks = pl.multiple_of(kb * tk, tk)
            kvb = [kv_ref[pl.ds(ks, tk), dw * h:dw * (h + 1)] for h in hs]
            z = [lax.dot_general(qw[h], kvb[h], (((1,), (1,)), ((), ())), preferred_element_type=F32)
                 for h in hs]
            e = [jnp.exp(-jnp.abs(z[h])) for h in hs]
            sp = [jnp.maximum(z[h], 0.0) + jnp.log(1.0 + e[h]) for h in hs]
            if masked:
                causal = diff < (i * tq - ks)
                lom = [jnp.where(causal, -sp[h], 0.0) for h in hs]
            else:
                lom = [-sp[h] for h in hs]
            rem = [runs[h] + _split_dot(lom[h], u_suf) for h in hs]
            w = [jnp.exp(z[h] - sp[h] + rem[h]) for h in hs]
            if masked:
                w = [jnp.where(causal, w[h], 0.0) for h in hs]
            accs = tuple(accs[h] + jnp.dot(w[h].astype(BF16), kvb[h], preferred_element_type=F32) for h in hs)
            for h in hs:
                rc_ref[h, kb] = _col_to_row(runs[h], tq)
            runs = tuple(runs[h] + jnp.sum(lom[h], axis=1, keepdims=True) for h in hs)
            return runs, accs

        init = (tuple(jnp.zeros((tq, 1), F32) for _ in hs), tuple(jnp.zeros((tq, dw), F32) for _ in hs))
        carry = block(nfull, init, True)
        _, accs = lax.fori_loop(0, nfull, lambda n, c: block(nfull - 1 - n, c, False), carry)
        for p in range(hb // 2):
            o_ref[:, dw * p:dw * (p + 1)] = _pair_out(accs[2 * p], accs[2 * p + 1], True).astype(o_ref.dtype)

    qspec = pl.BlockSpec((tq, hb * HEAD_DIM), lambda h, i: (i, h))
    kspec = pl.BlockSpec((L, hb * dw), lambda h, i: (0, h), pipeline_mode=pl.Buffered(1))
    return _pcall(
        body, name=name, grid=(H // hb, nq), in_specs=[qspec, kspec],
        out_specs=[qspec, pl.BlockSpec((hb, None, nkb, 1, tq), lambda h, i: (h, i, 0, 0, 0))],
        out_shape=[jax.ShapeDtypeStruct((L, H * HEAD_DIM), BF16), jax.ShapeDtypeStruct((H, nq, nkb, 1, tq), F32)],
        compiler_params=_params(("parallel", "arbitrary")),
    )(q, kv)


def _attn_bwd(name, q, kv, do, rc, tq, tk, scale, hb):
    L = q.shape[0]
    H = q.shape[1] // HEAD_DIM
    hb = min(hb, H)
    dw_ = 2 * HEAD_DIM
    nq, nkb = L // tq, L // tk
    hs = range(hb)
    assert tk % tq == 0 and hb % 2 == 0 and H % hb == 0

    def body(q_ref, kv_ref, do_ref, rc_ref, dq_ref, dkv_ref):
        i = pl.program_id(1)

        @pl.when(i == 0)
        def _():
            dkv_ref[...] = jnp.zeros_like(dkv_ref)

        diff = (lax.broadcasted_iota(jnp.int32, (tq, tk), 1) - lax.broadcasted_iota(jnp.int32, (tq, tk), 0))
        r_io = lax.broadcasted_iota(jnp.int32, (tk, tk), 0)
        c_io = lax.broadcasted_iota(jnp.int32, (tk, tk), 1)
        u_suf = (r_io > c_io).astype(BF16)
        u_pre = (r_io < c_io).astype(BF16)
        nfull = (i * tq) // tk
        qb = [_head_in(q_ref[:, dw_ * (h // 2):dw_ * (h // 2 + 1)], h, False) for h in hs]
        dob = [_head_in(do_ref[:, dw_ * (h // 2):dw_ * (h // 2 + 1)], h, True) for h in hs]

        def block(kb, carry, masked):
            pres, dqs = carry
            ks = pl.multiple_of(kb * tk, tk)
            kvb = [kv_ref[pl.ds(ks, tk), dw_ * h:dw_ * (h + 1)] for h in hs]
            z = [lax.dot_general(qb[h], kvb[h], (((1,), (1,)), ((), ())), preferred_element_type=F32) for h in hs]
            dw = [lax.dot_general(dob[h], kvb[h], (((1,), (1,)), ((), ())), preferred_element_type=F32)
                  for h in hs]
            e = [jnp.exp(-jnp.abs(z[h])) for h in hs]
            sp = [jnp.maximum(z[h], 0.0) + jnp.log(1.0 + e[h]) for h in hs]
            if masked:
                causal = diff < (i * tq - ks)
                lom = [jnp.where(causal, -sp[h], 0.0) for h in hs]
            else:
                lom = [-sp[h] for h in hs]
            rem = [_row_to_col(rc_ref[h, kb], tq) + _split_dot(lom[h], u_suf) for h in hs]
            logb = [z[h] - sp[h] for h in hs]
            w = [jnp.exp(logb[h] + rem[h]) for h in hs]
            if masked:
                w = [jnp.where(causal, w[h], 0.0) for h in hs]
            da = [dw[h] * w[h] for h in hs]
            p = [pres[h] + jnp.dot(da[h].astype(BF16), u_pre, preferred_element_type=F32) for h in hs]
            dz = [da[h] - jnp.exp(logb[h]) * (da[h] + p[h]) for h in hs]
            if masked:
                dz = [jnp.where(causal, dz[h], 0.0) for h in hs]
            dqs = tuple(dqs[h] + jnp.dot(dz[h].astype(BF16), kvb[h], preferred_element_type=F32) for h in hs)
            for h in hs:
                lhs = jnp.concatenate([dz[h].T, w[h].T], axis=1).astype(BF16)
                rhs = jnp.concatenate([qb[h], dob[h]], axis=0)
                dkv_ref[pl.ds(ks, tk), dw_ * h:dw_ * (h + 1)] += jnp.dot(lhs, rhs, preferred_element_type=F32)
            pres = tuple(pres[h] + jnp.sum(da[h], axis=1, keepdims=True) for h in hs)
            return pres, dqs

        init = (tuple(jnp.zeros((tq, 1), F32) for _ in hs), tuple(jnp.zeros((tq, dw_), F32) for _ in hs))
        carry = lax.fori_loop(0, nfull, lambda kb, c: block(kb, c, False), init)
        _, dqs = block(nfull, carry, True)
        for p in range(hb // 2):
            dq_ref[:, dw_ * p:dw_ * (p + 1)] = (_pair_out(dqs[2 * p], dqs[2 * p + 1], False)
                                                * scale).astype(dq_ref.dtype)

    qspec = pl.BlockSpec((tq, hb * HEAD_DIM), lambda h, i: (i, h))
    kspec = pl.BlockSpec((L, hb * dw_), lambda h, i: (0, h), pipeline_mode=pl.Buffered(1))
    return _pcall(
        body, name=name, grid=(H // hb, nq),
        in_specs=[qspec, kspec, qspec, pl.BlockSpec((hb, None, nkb, 1, tq), lambda h, i: (h, i, 0, 0, 0))],
        out_specs=[qspec, kspec],
        out_shape=[jax.ShapeDtypeStruct((L, H * HEAD_DIM), BF16), jax.ShapeDtypeStruct((L, 2 * H * HEAD_DIM), F32)],
        compiler_params=_params(("parallel", "arbitrary")),
    )(q, kv, do, rc)


_MESH = pl.DeviceIdType.MESH
_HBM = pl.BlockSpec(memory_space=pltpu.HBM)


def _all_gather(name, shards):
    n = len(shards)

    def body(*refs):
        x_refs, out_refs = refs[:n], refs[n:2 * n]
        send_sems, recv_sems, local_sems = refs[2 * n:]
        x, y, c = lax.axis_index("x"), lax.axis_index("y"), lax.axis_index("c")
        me, sibling = (x, y, c), (x, y, 1 - c)
        chips = [(1 - x, y), (x, 1 - y), (1 - x, 1 - y)]

        def slot(a, px, py, pc):
            return out_refs[a].at[4 * px + 2 * py + pc]

        def copy(a, k, block, to, src=None):
            return pltpu.make_async_remote_copy(
                src_ref=slot(a, *block) if src is None else src, dst_ref=slot(a, *block),
                send_sem=send_sems.at[7 * a + k], recv_sem=recv_sems.at[7 * a + k], device_id=to,
                device_id_type=_MESH)

        mine = [pltpu.make_async_copy(x_refs[a], slot(a, *me), local_sems.at[a]) for a in range(n)]
        for cp in mine:
            cp.start()
        first = []
        for a in range(n):
            first.append(copy(a, 0, me, sibling, src=x_refs[a]))
            first += [copy(a, 1 + j, me, (*chip, c), src=x_refs[a]) for j, chip in enumerate(chips)]
        for cp in first:
            cp.start()
        passed = []
        for j, chip in enumerate(chips):
            for a in range(n):
                copy(a, 1 + j, (*chip, c), me).wait_recv()
                passed.append(copy(a, 4 + j, (*chip, c), sibling))
                passed[-1].start()
        for a in range(n):
            copy(a, 0, sibling, me).wait_recv()
            for j, chip in enumerate(chips):
                copy(a, 4 + j, (*chip, 1 - c), me).wait_recv()
        for cp in first + passed:
            cp.wait_send()
        for cp in mine:
            cp.wait()

    return _pcall(
        body, name=name, out_shape=[jax.ShapeDtypeStruct((N_DEV,) + s.shape, s.dtype) for s in shards],
        in_specs=[_HBM] * n, out_specs=[_HBM] * n,
        scratch_shapes=[pltpu.SemaphoreType.DMA((7 * n,)), pltpu.SemaphoreType.DMA((7 * n,)),
                        pltpu.SemaphoreType.DMA((n,))],
    )(*shards)


def _sibling_exchange(name, xs):
    n = len(xs)

    def body(*refs):
        x_refs, out_refs, send_sems, recv_sems = refs[:n], refs[n:2 * n], refs[2 * n], refs[2 * n + 1]
        c = lax.axis_index("c")
        sibling = (lax.axis_index("x"), lax.axis_index("y"), 1 - c)
        cps = [pltpu.make_async_remote_copy(src_ref=x_refs[a].at[:, 1 - c], dst_ref=out_refs[a],
                                            send_sem=send_sems.at[a], recv_sem=recv_sems.at[a], device_id=sibling,
                                            device_id_type=_MESH)
               for a in range(n)]
        for cp in cps:
            cp.start()
        for cp in cps:
            cp.wait()

    return _pcall(
        body, name=name, out_shape=[jax.ShapeDtypeStruct(x.shape[:1] + x.shape[2:], x.dtype) for x in xs],
        in_specs=[_HBM] * n, out_specs=[_HBM] * n,
        scratch_shapes=[pltpu.SemaphoreType.DMA((n,)), pltpu.SemaphoreType.DMA((n,))],
    )(*xs)


def _chip_exchange(name, xs):
    n = len(xs)

    def body(*refs):
        x_refs, out_refs, send_sems, recv_sems = refs[:n], refs[n:2 * n], refs[2 * n], refs[2 * n + 1]
        mx, my, mc = lax.axis_index("x"), lax.axis_index("y"), lax.axis_index("c")
        chips = [(1 - mx, my), (mx, 1 - my), (1 - mx, 1 - my)]
        cps = [pltpu.make_async_remote_copy(src_ref=x_refs[a].at[2 * px + py], dst_ref=out_refs[a].at[j],
                                            send_sem=send_sems.at[3 * a + j], recv_sem=recv_sems.at[3 * a + j],
                                            device_id=(px, py, mc), device_id_type=_MESH)
               for a in range(n) for j, (px, py) in enumerate(chips)]
        for cp in cps:
            cp.start()
        for cp in cps:
            cp.wait()

    return _pcall(
        body, name=name, out_shape=[jax.ShapeDtypeStruct((3,) + x.shape[1:], x.dtype) for x in xs],
        in_specs=[_HBM] * n, out_specs=[_HBM] * n,
        scratch_shapes=[pltpu.SemaphoreType.DMA((3 * n,)), pltpu.SemaphoreType.DMA((3 * n,))],
    )(*xs)


def _pair_sum(name, a, b):
    nb, R, C = a.shape
    tm = _tile(R, 512, 16)

    def fn(i, ab, bb):
        s = ab.astype(F32) + bb.astype(F32)
        return s, s

    spec = lambda arr: (arr, (1, tm, C), lambda i: (i // (R // tm), i % (R // tm), 0))
    out = lambda dt: ((nb, R, C), dt, (1, tm, C), lambda i: (i // (R // tm), i % (R // tm), 0))
    return _rowwise(name, fn, nb * (R // tm), [spec(a), spec(b)], [out(F32), out(a.dtype)])


def _sum8(name, g):
    n, R, C = g.shape
    tm = _tile(R, 256)

    def fn(i, gb):
        s = gb[0]
        for d in range(1, n):
            s = s + gb[d]
        return (s,)

    return _rowwise(name, fn, R // tm, [(g, (n, tm, C), lambda i: (0, i, 0))], [_orows(R, C, F32, tm)])[0]


def _adamw(name, w, m, v, grads, tm=512):
    R, C = w.shape
    tm = _tile(R, tm, 16)
    ng = len(grads)

    def fn(i, wb, mb, vb, *gs):
        g = gs[0].astype(F32)
        for t in gs[1:]:
            g = g + t.astype(F32)
        mn = ADAM_B1 * mb + (1.0 - ADAM_B1) * g
        vn = ADAM_B2 * vb + (1.0 - ADAM_B2) * (g * g)
        m_hat = mn / (1.0 - ADAM_B1 ** ADAM_STEP)
        v_hat = vn / (1.0 - ADAM_B2 ** ADAM_STEP)
        delta = -ADAM_LR * (m_hat / (jnp.sqrt(v_hat) + ADAM_EPS) + ADAM_WD * wb)
        return g, delta, mn, vn

    ins = [_rows(w, tm), _rows(m, tm), _rows(v, tm)] + [_rows(g, tm) for g in grads]
    return _rowwise(name, fn, R // tm, ins, [_orows(R, C, F32, tm) for _ in range(4)])


CLASSES = [
    (BF16, True, [('ssm_w_in', None), ('attn_w_q', None), ('attn_w_o', None), ('ffn_w_down', 0), ('ffn_w_down', 1)]),
    (BF16, False, [('ssm_w_glu', None), ('kv_w', None)]),
    (BF16, False, [('ffn_w_up', 0), ('ffn_w_up', 1)]),
    (F32, False, [('ffn_conv_w', 0), ('ffn_conv_w', 1)]),
]


def _shard2d(a, name, layer):
    if name == 'ffn_conv_w':
        return a[layer, :, 0, :]
    if name == 'kv_w':
        return a
    return a[0 if layer is None else layer]


def _weights_of(members):
    names = []
    for n, _ in members:
        if not names or names[-1] != n:
            names.append(n)
    return names


def _pack_class(d, members, dtype):
    width = _shard2d(d[members[0][0]], *members[0]).shape[1]
    parts = [d[n].reshape(-1, width).astype(dtype) for n in _weights_of(members)]
    return parts[0] if len(parts) == 1 else jnp.concatenate(parts, axis=0)


def _full_weights(gaths, d):
    w = {}
    for (_, row_sharded, members), g in zip(CLASSES, gaths):
        off = 0
        for n, l in members:
            r = _shard2d(d[n], n, l).shape[0]
            blk = g[:, off:off + r]
            if row_sharded:
                w[(n, l)] = blk.reshape(N_DEV * r, g.shape[2])
            else:
                w[(n, l)] = blk.transpose(1, 0, 2).reshape(r, N_DEV * g.shape[2])
            off += r
    return w


def _grad_classes(gw):
    out = []
    for dtype, row_sharded, members in CLASSES:
        parts = []
        for n, l in members:
            g = gw[(n, l)].astype(dtype)
            if row_sharded:
                parts.append(g.reshape(N_DEV, g.shape[0] // N_DEV, g.shape[1]))
            else:
                parts.append(g.reshape(g.shape[0], N_DEV, g.shape[1] // N_DEV).transpose(1, 0, 2))
        out.append(jnp.concatenate(parts, axis=1))
    return out


def _unpack_class(buf, members, shapes):
    out, off = {}, 0
    for n in _weights_of(members):
        r = math.prod(shapes[n]) // buf.shape[1]
        out[n] = buf[off:off + r].reshape(shapes[n])
        off += r
    return out


def _small_layout(shapes, D):
    lay, off = {}, 0
    for n in SMALL:
        r = -(-math.prod(shapes[n]) // D)
        lay[n] = (off, r)
        off += r
    lay['loss'] = (off, 1)
    off += 1
    return lay, -(-off // 8) * 8


def _pack_small(d, lay, total, D):
    parts = []
    for n in SMALL + ['loss']:
        if n not in d:
            parts.append(jnp.zeros((lay[n][1], D), F32))
            continue
        flat = d[n].reshape(-1).astype(F32)
        parts.append(jnp.pad(flat, (0, lay[n][1] * D - flat.shape[0])).reshape(lay[n][1], D))
    used = sum(lay[n][1] for n in SMALL + ['loss'])
    if total > used:
        parts.append(jnp.zeros((total - used, D), F32))
    return jnp.concatenate(parts, axis=0)


def _unpack_small(pack, lay, shapes, D):
    out = {}
    for n in SMALL:
        off, r = lay[n]
        out[n] = pack[off:off + r].reshape(-1)[:math.prod(shapes[n])].reshape(shapes[n])
    return out


def _ffn_fwd(tag, x, g_norm, w_up, conv_w, conv_b, w_down):
    h = _rms_fwd(f"{tag}_norm", x, g_norm)
    gu = _mm(f"{tag}_up", h, w_up, 'nn', BF16)
    a = _gate_fwd(f"{tag}_gate", gu, conv_w, conv_b)
    return _mm(f"{tag}_down", a, w_down, 'nn', F32, resid=x), (h, gu, a)


def _ffn_bwd(tag, dres, dres_bf, x, saved, g_norm, w_up, conv_w, conv_b, w_down):
    h, gu, a = saved
    da = _mm(f"{tag}_dgate", dres_bf, w_down, 'nt', BF16)
    d_w_down = _mm(f"{tag}_dwdown", a, dres_bf, 'tn', BF16)
    dgu, dw0, dw1, dw2, dcb = _gate_bwd(f"{tag}_gate_b", gu, da, conv_w, conv_b)
    d_w_up = _mm(f"{tag}_dwup", h, dgu, 'tn', BF16)
    dh = _mm(f"{tag}_dh", dgu, w_up, 'nt', F32)
    dres, dres_bf, dg = _rms_bwd(f"{tag}_norm_b", x, dres, [(g_norm, dh)])
    return dres, dres_bf, dg, d_w_up, jnp.concatenate([dw0, dw1, dw2], axis=0), dcb[0], d_w_down


def kernel(x, norm_mix, norm_ffn, norm_kv, norm_final, ssm_w_in, ssm_a_re, ssm_a_im, ssm_log_dt, ssm_b_re, ssm_b_im, ssm_c_re, ssm_c_im, ssm_d, ssm_w_glu, kv_w, attn_w_q, attn_w_o, ffn_w_up, ffn_conv_w, ffn_conv_b, ffn_w_down, loss_target, m_norm_mix, m_norm_ffn, m_norm_kv, m_norm_final, m_ssm_w_in, m_ssm_a_re, m_ssm_a_im, m_ssm_log_dt, m_ssm_b_re, m_ssm_b_im, m_ssm_c_re, m_ssm_c_im, m_ssm_d, m_ssm_w_glu, m_kv_w, m_attn_w_q, m_attn_w_o, m_ffn_w_up, m_ffn_conv_w, m_ffn_conv_b, m_ffn_w_down, v_norm_mix, v_norm_ffn, v_norm_kv, v_norm_final, v_ssm_w_in, v_ssm_a_re, v_ssm_a_im, v_ssm_log_dt, v_ssm_b_re, v_ssm_b_im, v_ssm_c_re, v_ssm_c_im, v_ssm_d, v_ssm_w_glu, v_kv_w, v_attn_w_q, v_attn_w_o, v_ffn_w_up, v_ffn_conv_w, v_ffn_conv_b, v_ffn_w_down):
    wts = dict(zip(W_NAMES, (norm_mix, norm_ffn, norm_kv, norm_final, ssm_w_in, ssm_a_re, ssm_a_im, ssm_log_dt,
                             ssm_b_re, ssm_b_im, ssm_c_re, ssm_c_im, ssm_d, ssm_w_glu, kv_w, attn_w_q, attn_w_o,
                             ffn_w_up, ffn_conv_w, ffn_conv_b, ffn_w_down)))
    mom = dict(zip(W_NAMES, (m_norm_mix, m_norm_ffn, m_norm_kv, m_norm_final, m_ssm_w_in, m_ssm_a_re, m_ssm_a_im,
                             m_ssm_log_dt, m_ssm_b_re, m_ssm_b_im, m_ssm_c_re, m_ssm_c_im, m_ssm_d, m_ssm_w_glu,
                             m_kv_w, m_attn_w_q, m_attn_w_o, m_ffn_w_up, m_ffn_conv_w, m_ffn_conv_b, m_ffn_w_down)))
    vel = dict(zip(W_NAMES, (v_norm_mix, v_norm_ffn, v_norm_kv, v_norm_final, v_ssm_w_in, v_ssm_a_re, v_ssm_a_im,
                             v_ssm_log_dt, v_ssm_b_re, v_ssm_b_im, v_ssm_c_re, v_ssm_c_im, v_ssm_d, v_ssm_w_glu,
                             v_kv_w, v_attn_w_q, v_attn_w_o, v_ffn_w_up, v_ffn_conv_w, v_ffn_conv_b, v_ffn_w_down)))
    shapes = {n: wts[n].shape for n in W_NAMES}
    _, L, D = x.shape
    Fh = ffn_conv_b.shape[1]
    G, P = ssm_a_re.shape[1], ssm_a_re.shape[2]
    Hg = ssm_d.shape[2]
    x0 = x[0]
    target = loss_target[0]
    scale = HEAD_DIM ** -0.5

    gaths = _all_gather("gather_weights", [_pack_class(wts, members, dt) for dt, _, members in CLASSES])
    w = _full_weights(gaths, wts)
    nm = [norm_mix[l:l + 1] for l in range(2)]
    nf = [norm_ffn[l:l + 1] for l in range(2)]
    nkv = norm_kv[None]
    nfin = norm_final[None]
    cb = [ffn_conv_b[l:l + 1] for l in range(2)]
    cwt = [w[('ffn_conv_w', l)] for l in range(2)]

    s5p = (ssm_a_re[0], ssm_a_im[0], ssm_log_dt[0], ssm_b_re[0], ssm_b_im[0], ssm_c_re[0], ssm_c_im[0], ssm_d[0])
    (m_mat, n_mat, o_mat, lam1, lam2), s5_vjp = jax.vjp(_s5_build, *s5p)
    m_bf, n_bf, o_bf = m_mat.astype(BF16), n_mat.astype(BF16), o_mat.astype(BF16)

    h0 = _rms_fwd("l0_norm", x0, nm[0])
    u = _mm("l0_win", h0, w[('ssm_w_in', None)], 'nn', BF16)
    ug = _to_groups("s5_group_u", u, G, Hg, BF16)
    vloc = _bmm("s5_local_state", [(ug, n_bf, 'nt')], F32)
    st = _chunk_scan("s5_scan", vloc, lam1, lam2, reverse=False)
    yraw, yg = _bmm("s5_out", [(ug, m_bf, 'nt'), (st, o_bf, 'nt')], (F32, BF16),
                    post=lambda acc: (acc, _gelu(acc)), ungroup=(1, Hg))
    z = _mm("l0_wglu", yg, w[('ssm_w_glu', None)], 'nn', F32)
    x1 = _glu_fwd("l0_glu", x0, z)
    x2, ffn0 = _ffn_fwd("f0", x1, nf[0], w[('ffn_w_up', 0)], cwt[0], cb[0], w[('ffn_w_down', 0)])

    hkv = _rms_fwd("kv_norm", x2, nkv)
    nh = D // HEAD_DIM
    w_kv = w[('kv_w', None)].reshape(D, 2, nh, HEAD_DIM).transpose(0, 2, 1, 3).reshape(D, 2 * D)
    kvp = _mm("kv_proj", hkv, w_kv, 'nn', BF16)
    h2 = _rms_fwd("l1_norm", x2, nm[1])
    qn = _mm("l1_wq", h2, w[('attn_w_q', None)], 'nn', BF16, scale=scale)
    o, rc = _attn_fwd("attn_fwd", qn, kvp, ATT_TQ, ATT_TK, ATT_HB_FWD)
    x3 = _mm("l1_wo", o, w[('attn_w_o', None)], 'nn', F32, resid=x2)
    x4, ffn1 = _ffn_fwd("f1", x3, nf[1], w[('ffn_w_up', 1)], cwt[1], cb[1], w[('ffn_w_down', 1)])

    dres, dres_bf, dg_final, loss_part = _final_loss("loss_head", x4, nfin, target)

    gw = {}
    dres, dres_bf, dg_nf1, gup1, gcw1, gcb1, gdn1 = _ffn_bwd("f1", dres, dres_bf, x3, ffn1, nf[1], w[('ffn_w_up', 1)],
                                                             cwt[1], cb[1], w[('ffn_w_down', 1)])
    do = _mm("l1_do", dres_bf, w[('attn_w_o', None)], 'nt', BF16)
    gw[('attn_w_o', None)] = _mm("l1_dwo", o, dres_bf, 'tn', BF16)
    dq, dkv = _attn_bwd("attn_bwd", qn, kvp, do, rc, ATT_TQ, ATT_TK, scale, ATT_HB_BWD)
    gw[('attn_w_q', None)] = _mm("l1_dwq", h2, dq, 'tn', BF16)
    dh2 = _mm("l1_dh", dq, w[('attn_w_q', None)], 'nt', F32)
    g_kv = _mm("kv_dw", hkv, dkv, 'tn', BF16)
    gw[('kv_w', None)] = g_kv.reshape(D, nh, 2, HEAD_DIM).transpose(0, 2, 1, 3).reshape(D, 2 * D)
    dhkv = _mm("kv_dh", dkv, w_kv, 'nt', F32)
    dres, dres_bf, dg_nm1, dg_nkv = _rms_bwd("l1_norm_b", x2, dres, [(nm[1], dh2), (nkv, dhkv)])
    dres, dres_bf, dg_nf0, gup0, gcw0, gcb0, gdn0 = _ffn_bwd("f0", dres, dres_bf, x1, ffn0, nf[0], w[('ffn_w_up', 0)],
                                                             cwt[0], cb[0], w[('ffn_w_down', 0)])
    dz = _glu_bwd("l0_glu_b", dres, z)
    gw[('ssm_w_glu', None)] = _mm("l0_dwglu", yg, dz, 'tn', BF16)
    dyg = _mm("l0_dyg", dz, w[('ssm_w_glu', None)], 'nt', F32)
    dy = _to_groups("s5_group_dy", dyg, G, Hg, BF16, gelu_arg=yraw)
    ds = _bmm("s5_dstate", [(dy, o_bf, 'nn')], F32)
    dv_loc, dlam1, dlam2 = _chunk_scan("s5_scan_b", ds, lam1, lam2, reverse=True, s_fwd=st)
    du = _bmm("s5_du", [(dy, m_bf, 'nn'), (dv_loc, n_bf, 'nn')], BF16, ungroup=(0, Hg))
    d_m = _bmm("s5_dm", [(dy, ug, 'tn')], F32)
    d_o = _bmm("s5_do", [(dy, st, 'tn')], F32)
    d_n = _bmm("s5_dn", [(dv_loc, ug, 'tn')], F32)
    s5g = s5_vjp((d_m, d_n, d_o, dlam1, dlam2))
    gw[('ssm_w_in', None)] = _mm("l0_dwin", h0, du, 'tn', BF16)
    dh0 = _mm("l0_dh", du, w[('ssm_w_in', None)], 'nt', F32)
    grad_x, _, dg_nm0 = _rms_bwd("l0_norm_b", x0, dres, [(nm[0], dh0)])

    gw.update({('ffn_w_up', 0): gup0, ('ffn_w_up', 1): gup1, ('ffn_w_down', 0): gdn0, ('ffn_w_down', 1): gdn1,
               ('ffn_conv_w', 0): gcw0, ('ffn_conv_w', 1): gcw1})

    small_g = {
        'norm_mix': jnp.concatenate([dg_nm0, dg_nm1], axis=0), 'norm_ffn': jnp.concatenate([dg_nf0, dg_nf1], axis=0),
        'norm_kv': dg_nkv, 'norm_final': dg_final, 'ffn_conv_b': jnp.stack([gcb0, gcb1]),
        'ssm_a_re': s5g[0], 'ssm_a_im': s5g[1], 'ssm_log_dt': s5g[2], 'ssm_b_re': s5g[3], 'ssm_b_im': s5g[4],
        'ssm_c_re': s5g[5], 'ssm_c_im': s5g[6], 'ssm_d': s5g[7], 'loss': loss_part[0, 0:1],
    }
    lay, rs = _small_layout(shapes, D)
    small_sum = _sum8("small_sum", _all_gather("gather_small", [_pack_small(small_g, lay, rs, D)])[0])
    loss = small_sum[lay['loss'][0], 0]
    sg, sdelta, sm, sv = _adamw("adamw_small", _pack_small(wts, lay, rs, D), _pack_small(mom, lay, rs, D),
                                _pack_small(vel, lay, rs, D), [small_sum])
    small_out = [_unpack_small(t, lay, shapes, D) for t in (sg, sdelta, sm, sv)]

    c = lax.axis_index("c")
    chip = 2 * lax.axis_index("x") + lax.axis_index("y")
    gcls = [g.reshape((4, 2) + g.shape[1:]) for g in _grad_classes(gw)]
    mine = [lax.dynamic_index_in_dim(g, c, axis=1, keepdims=False) for g in gcls]
    got = _sibling_exchange("rs_sibling", gcls)
    sums = [_pair_sum(f"rs_pair_sum{k}", a, b) for k, (a, b) in enumerate(zip(mine, got))]
    from_chips = _chip_exchange("rs_chips", [on_wire for _, on_wire in sums])
    big_parts = [{}, {}, {}, {}]
    for k, ((_, _, members), (chip_f32, _), fc) in enumerate(zip(CLASSES, sums, from_chips)):
        own = lax.dynamic_index_in_dim(chip_f32, chip, axis=0, keepdims=False)
        res = _adamw(f"adamw_big{k}", _pack_class(wts, members, F32), _pack_class(mom, members, F32),
                     _pack_class(vel, members, F32), [own, fc[0], fc[1], fc[2]])
        for q in range(4):
            big_parts[q].update(_unpack_class(res[q], members, shapes))
    big_out = big_parts

    outs = [loss, grad_x[None]]
    for k in range(4):
        for n in W_NAMES:
            outs.append(big_out[k][n] if n in BIG else small_out[k][n])
    return tuple(outs)
```

```python
import math

import jax
import jax.numpy as jnp
from jax import lax
from jax.experimental import pallas as pl
from jax.experimental.pallas import tpu as pltpu

F32 = jnp.float32
BF16 = jnp.bfloat16

EPS = 1e-6
HEAD_DIM = 64
CHUNK = 16
N_DEV = 8
ADAM_LR = 0.001
ADAM_B1 = 0.9
ADAM_B2 = 0.999
ADAM_EPS = 1e-08
ADAM_WD = 0.01
ADAM_STEP = 10
VMEM_LIMIT = 48 * 1024 * 1024
MM_BLOCK_BYTES = 28 * 1024 * 1024
MM_TM_MAX = 1024
MM_T_MAX = 512
MM_TN_MAX = 1536
LANES = 128
ATT_TQ = 128
ATT_TK = 256
ATT_HB_FWD = 16
ATT_HB_BWD = 8

W_NAMES = ['norm_mix', 'norm_ffn', 'norm_kv', 'norm_final', 'ssm_w_in', 'ssm_a_re', 'ssm_a_im', 'ssm_log_dt',
           'ssm_b_re', 'ssm_b_im', 'ssm_c_re', 'ssm_c_im', 'ssm_d', 'ssm_w_glu', 'kv_w', 'attn_w_q', 'attn_w_o',
           'ffn_w_up', 'ffn_conv_w', 'ffn_conv_b', 'ffn_w_down']
BIG = ['ssm_w_in', 'ssm_w_glu', 'kv_w', 'attn_w_q', 'attn_w_o', 'ffn_w_up', 'ffn_w_down', 'ffn_conv_w']
SMALL = [n for n in W_NAMES if n not in BIG]


def _pcall(body, **kw):
    return pl.pallas_call(body, **kw)


def _params(sem=None):
    if sem is None:
        return pltpu.CompilerParams(vmem_limit_bytes=VMEM_LIMIT)
    return pltpu.CompilerParams(dimension_semantics=sem, vmem_limit_bytes=VMEM_LIMIT)


def _tile(n, pref, mult=8):
    best = None
    for t in range(mult, min(n, pref) + 1, mult):
        if n % t == 0:
            best = t
    return n if best is None else best


def _mm_tiles(M, N, K, mode, a_bytes, b_bytes, o_bytes):
    def cands(n, cap):
        c = [t for t in range(LANES, min(n, cap) + 1, LANES) if n % t == 0]
        return c or [n]
    best = None
    for tm in cands(M, MM_T_MAX if mode == 'tn' else MM_TM_MAX):
        for tn in cands(N, MM_TN_MAX):
            need = 2 * (tm * K * a_bytes + K * tn * b_bytes + tm * tn * o_bytes)
            score = tm * tn * (tm if mode == 'tn' else 1)
            if need <= MM_BLOCK_BYTES and (best is None or score > best[0]):
                best = (score, tm, tn)
    assert best is not None, (M, N, K)
    return best[1], best[2]


def _mm(name, a, b, mode, out_dtype=F32, scale=None, resid=None):
    if mode == 'nn':
        (M, K), (K2, N) = a.shape, b.shape
    elif mode == 'nt':
        (M, K), (N, K2) = a.shape, b.shape
    else:
        (K, M), (K2, N) = a.shape, b.shape
    assert K == K2, (name, a.shape, b.shape)
    o_bytes = jnp.dtype(out_dtype).itemsize + (resid.dtype.itemsize if resid is not None else 0)
    tm, tn = _mm_tiles(M, N, K, mode, a.dtype.itemsize, b.dtype.itemsize, o_bytes)
    if mode == 'tn':
        a_spec = pl.BlockSpec((K, tm), lambda i, j: (0, i))
    else:
        a_spec = pl.BlockSpec((tm, K), lambda i, j: (i, 0))
    if mode == 'nt':
        b_spec = pl.BlockSpec((tn, K), lambda i, j: (j, 0))
    else:
        b_spec = pl.BlockSpec((K, tn), lambda i, j: (0, j))
    o_spec = pl.BlockSpec((tm, tn), lambda i, j: (i, j))
    dn = {'nn': ((1,), (0,)), 'nt': ((1,), (1,)), 'tn': ((0,), (0,))}[mode]

    def body(*refs):
        a_ref, b_ref, o_ref = refs[0], refs[1], refs[-1]
        acc = lax.dot_general(a_ref[...].astype(BF16), b_ref[...].astype(BF16), (dn, ((), ())),
                              preferred_element_type=F32)
        if scale is not None:
            acc = acc * scale
        if resid is not None:
            acc = acc + refs[2][...]
        o_ref[...] = acc.astype(o_ref.dtype)

    return _pcall(
        body, name=name, grid=(M // tm, N // tn),
        in_specs=[a_spec, b_spec] + ([o_spec] if resid is not None else []),
        out_specs=o_spec,
        out_shape=jax.ShapeDtypeStruct((M, N), out_dtype),
        compiler_params=_params(("parallel", "parallel")),
    )(*([a, b] + ([resid] if resid is not None else [])))


def _bmm(name, terms, out_dtype, gb=8, post=None, ungroup=None):
    G = terms[0][0].shape[0]
    gb = _tile(G, gb, 1)
    dns = {'nn': ((1,), (0,)), 'nt': ((1,), (1,)), 'tn': ((0,), (0,))}

    def oshape(a, b, mode):
        m = a.shape[2] if mode == 'tn' else a.shape[1]
        n = b.shape[1] if mode == 'nt' else b.shape[2]
        return m, n

    m, n = oshape(*terms[0])
    out_dtypes = out_dtype if isinstance(out_dtype, (tuple, list)) else (out_dtype,)
    n_in = 2 * len(terms)
    if ungroup is not None:
        perm, gpb = _lane_perm(ungroup[1])
        assert gpb == gb and n == CHUNK * ungroup[1]

    def body(*refs):
        ins, outs = refs[:n_in], refs[len(refs) - len(out_dtypes):]
        kept = []
        for gi in range(gb):
            acc = None
            for t, (_, _, mode) in enumerate(terms):
                part = lax.dot_general(ins[2 * t][gi].astype(BF16), ins[2 * t + 1][gi].astype(BF16),
                                       (dns[mode], ((), ())), preferred_element_type=F32)
                acc = part if acc is None else acc + part
            vals = (acc,) if post is None else post(acc)
            for k, (o_ref, v) in enumerate(zip(outs, vals)):
                if ungroup is not None and k == ungroup[0]:
                    kept.append(v.astype(BF16))
                else:
                    o_ref[gi] = v.astype(o_ref.dtype)
        if ungroup is not None:
            x = lax.dot_general(jnp.concatenate(kept, axis=1), refs[n_in][...], (((1,), (1,)), ((), ())),
                                preferred_element_type=F32)
            o_ref = outs[ungroup[0]]
            for t in range(CHUNK):
                o_ref[:, t, :] = x[:, t * LANES:(t + 1) * LANES].astype(o_ref.dtype)

    in_specs, args = [], []
    for a, b, _ in terms:
        in_specs += [pl.BlockSpec((gb,) + a.shape[1:], lambda g: (g, 0, 0)),
                     pl.BlockSpec((gb,) + b.shape[1:], lambda g: (g, 0, 0))]
        args += [a, b]
    out_specs = [pl.BlockSpec((gb, m, n), lambda g: (g, 0, 0)) for _ in out_dtypes]
    out_shape = [jax.ShapeDtypeStruct((G, m, n), dt) for dt in out_dtypes]
    if ungroup is not None:
        in_specs.append(pl.BlockSpec(perm.shape, lambda g: (0, 0)))
        args.append(perm)
        out_specs[ungroup[0]] = pl.BlockSpec((m, CHUNK, LANES), lambda g: (0, 0, g))
        out_shape[ungroup[0]] = jax.ShapeDtypeStruct((m, CHUNK, G * ungroup[1]), out_dtypes[ungroup[0]])
    res = _pcall(
        body, name=name, grid=(G // gb,), in_specs=in_specs, out_specs=out_specs, out_shape=out_shape,
        compiler_params=_params(("parallel",)),
    )(*args)
    if ungroup is not None:
        res = list(res)
        res[ungroup[0]] = res[ungroup[0]].reshape(m * CHUNK, G * ungroup[1])
    return res[0] if len(out_dtypes) == 1 else res


def _rowwise(name, fn, n_steps, ins, outs, n_acc=0):
    n_in, n_out = len(ins), len(outs)

    def body(*refs):
        i = pl.program_id(0)
        vals = fn(i, *[r[...] for r in refs[:n_in]])
        o_refs = refs[n_in:]
        for j in range(n_out - n_acc):
            o_refs[j][...] = vals[j].astype(o_refs[j].dtype)
        if n_acc:
            @pl.when(i == 0)
            def _():
                for j in range(n_out - n_acc, n_out):
                    o_refs[j][...] = vals[j].astype(o_refs[j].dtype)

            @pl.when(i > 0)
            def _():
                for j in range(n_out - n_acc, n_out):
                    o_refs[j][...] += vals[j].astype(o_refs[j].dtype)

    res = _pcall(
        body, name=name, grid=(n_steps,),
        in_specs=[pl.BlockSpec(blk, im) for _, blk, im in ins],
        out_specs=[pl.BlockSpec(blk, im) for _, _, blk, im in outs],
        out_shape=[jax.ShapeDtypeStruct(s, d) for s, d, _, _ in outs],
        compiler_params=_params(("arbitrary",)),
    )(*[a for a, _, _ in ins])
    return res


def _rows(a, tm):
    return (a, (tm, a.shape[1]), lambda i: (i, 0))


def _whole(a):
    nd = a.ndim
    return (a, a.shape, lambda i: (0,) * nd)


def _orows(L, n, dtype, tm):
    return ((L, n), dtype, (tm, n), lambda i: (i, 0))


def _oacc(r, n):
    return ((r, n), F32, (r, n), lambda i: (0, 0))


def _rms_fwd(name, x, g, tm=512):
    L, D = x.shape
    tm = _tile(L, tm)

    def fn(i, xb, gb):
        r = lax.rsqrt(jnp.mean(xb * xb, axis=-1, keepdims=True) + EPS)
        return (xb * r * gb,)

    return _rowwise(name, fn, L // tm, [_rows(x, tm), _whole(g)], [_orows(L, D, BF16, tm)])[0]


def _rms_bwd(name, x, dres, branches, tm=512):
    L, D = x.shape
    tm = _tile(L, tm)
    nb = len(branches)

    def fn(i, xb, db, *rest):
        r = lax.rsqrt(jnp.mean(xb * xb, axis=-1, keepdims=True) + EPS)
        xh = xb * r
        dx = db
        dgs = []
        for b in range(nb):
            gb, dyb = rest[2 * b], rest[2 * b + 1].astype(F32)
            dxh = dyb * gb
            dx = dx + r * (dxh - xh * jnp.mean(dxh * xh, axis=-1, keepdims=True))
            dgs.append(jnp.sum(dyb * xh, axis=0, keepdims=True))
        return (dx, dx, *dgs)

    ins = [_rows(x, tm), _rows(dres, tm)]
    for g, dy in branches:
        ins += [_whole(g), _rows(dy, tm)]
    outs = [_orows(L, D, F32, tm), _orows(L, D, BF16, tm)] + [_oacc(1, D) for _ in range(nb)]
    return _rowwise(name, fn, L // tm, ins, outs, n_acc=nb)


def _final_loss(name, x, g, target, tm=512):
    L, D = x.shape
    tm = _tile(L, tm)

    def fn(i, xb, gb, tb):
        r = lax.rsqrt(jnp.mean(xb * xb, axis=-1, keepdims=True) + EPS)
        xh = xb * r
        err = xh * gb - tb
        dy = err * (1.0 / D)
        dxh = dy * gb
        dx = r * (dxh - xh * jnp.mean(dxh * xh, axis=-1, keepdims=True))
        dg = jnp.sum(dy * xh, axis=0, keepdims=True)
        per_row = jnp.mean(err * err, axis=-1, keepdims=True)
        loss = 0.5 * jnp.sum(per_row, axis=0, keepdims=True)
        return dx, dx, dg, jnp.broadcast_to(loss, (1, LANES))

    return _rowwise(name, fn, L // tm, [_rows(x, tm), _whole(g), _rows(target, tm)],
                    [_orows(L, D, F32, tm), _orows(L, D, BF16, tm), _oacc(1, D), _oacc(1, LANES)], n_acc=2)


def _glu_fwd(name, x, z, tm=512):
    L, D = x.shape
    tm = _tile(L, tm)

    def fn(i, xb, zb):
        return (xb + zb[:, :D] * jax.nn.sigmoid(zb[:, D:]),)

    return _rowwise(name, fn, L // tm, [_rows(x, tm), _rows(z, tm)], [_orows(L, D, F32, tm)])[0]


def _glu_bwd(name, dx, z, tm=512):
    L, D = dx.shape
    tm = _tile(L, tm)

    def fn(i, db, zb):
        z1, sg = zb[:, :D], jax.nn.sigmoid(zb[:, D:])
        return (jnp.concatenate([(db * sg).astype(BF16), (db * z1 * sg * (1.0 - sg)).astype(BF16)], axis=1),)

    return _rowwise(name, fn, L // tm, [_rows(dx, tm), _rows(z, tm)], [_orows(L, 2 * D, BF16, tm)])[0]


HALO = 16


def _halo_prev(a, tm):
    return (a, (HALO, a.shape[1]), lambda i: (jnp.maximum(i * (tm // HALO) - 1, 0), 0))


def _halo_next(a, tm):
    last = a.shape[0] // HALO - 1
    return (a, (HALO, a.shape[1]), lambda i: (jnp.minimum((i + 1) * (tm // HALO), last), 0))


def _shift_down(cur, halo, k, first):
    tm = cur.shape[0]
    rolled = pltpu.roll(cur, k, 0)
    tail = pltpu.roll(halo, k, 0)
    tail = jnp.where(first, 0.0, tail)
    row = lax.broadcasted_iota(jnp.int32, (tm, 1), 0)
    head = jnp.concatenate([tail, jnp.zeros((tm - 8, cur.shape[1]), cur.dtype)], axis=0) if tm > 8 else tail
    return jnp.where(row < k, head, rolled)


def _shift_up(cur, halo, k, last):
    tm = cur.shape[0]
    rolled = pltpu.roll(cur, tm - k, 0)
    head = pltpu.roll(halo, 8 - k, 0)
    head = jnp.where(last, 0.0, head)
    row = lax.broadcasted_iota(jnp.int32, (tm, 1), 0)
    tail = jnp.concatenate([jnp.zeros((tm - 8, cur.shape[1]), cur.dtype), head], axis=0) if tm > 8 else head
    return jnp.where(row >= tm - k, tail, rolled)


def _conv_pre(gb, hb, cw, cb, first):
    g1 = _shift_down(gb, hb, 1, first)
    g2 = _shift_down(gb, hb, 2, first)
    return cw[0:1] * g2 + cw[1:2] * g1 + cw[2:3] * gb + cb, g1, g2


def _gate_fwd(name, gu, cw, cb, tm=256):
    L, F2 = gu.shape
    Fh = F2 // 2
    tm = _tile(L, tm)

    def fn(i, gub, halo, cwb, cbb):
        gub, halo = gub.astype(F32), halo.astype(F32)[HALO - 8:]
        gc, _, _ = _conv_pre(gub[:, :Fh], halo[:, :Fh], cwb, cbb, i == 0)
        return (gc * jax.nn.sigmoid(gc) * gub[:, Fh:],)

    return _rowwise(name, fn, L // tm, [_rows(gu, tm), _halo_prev(gu, tm), _whole(cw), _whole(cb)],
                    [_orows(L, Fh, BF16, tm)])[0]


def _gate_bwd(name, gu, da, cw, cb, tm=256):
    L, F2 = gu.shape
    Fh = F2 // 2
    tm = _tile(L, tm)
    n = L // tm

    def dgate(gb, ub, dab, hb, cwb, cbb, first):
        gc, g1, g2 = _conv_pre(gb, hb, cwb, cbb, first)
        sg = jax.nn.sigmoid(gc)
        return dab * ub * sg * (1.0 + gc * (1.0 - sg)), dab * gc * sg, g1, g2

    def fn(i, gub, prev, nxt, dab, dnx, cwb, cbb):
        gub, prev, nxt = gub.astype(F32), prev.astype(F32)[HALO - 8:], nxt.astype(F32)[:8]
        dab, dnx = dab.astype(F32), dnx.astype(F32)[:8]
        gb, ub = gub[:, :Fh], gub[:, Fh:]
        dgc, du, g1, g2 = dgate(gb, ub, dab, prev[:, :Fh], cwb, cbb, i == 0)
        dgc_next, _, _, _ = dgate(nxt[:, :Fh], nxt[:, Fh:], dnx, gb[tm - 8:], cwb, cbb, False)
        d1 = _shift_up(dgc, dgc_next, 1, i == n - 1)
        d2 = _shift_up(dgc, dgc_next, 2, i == n - 1)
        dg = cwb[2:3] * dgc + cwb[1:2] * d1 + cwb[0:1] * d2
        return (jnp.concatenate([dg.astype(BF16), du.astype(BF16)], axis=1),
                jnp.sum(dgc * g2, axis=0, keepdims=True), jnp.sum(dgc * g1, axis=0, keepdims=True),
                jnp.sum(dgc * gb, axis=0, keepdims=True), jnp.sum(dgc, axis=0, keepdims=True))

    return _rowwise(name, fn, n,
                    [_rows(gu, tm), _halo_prev(gu, tm), _halo_next(gu, tm), _rows(da, tm), _halo_next(da, tm),
                     _whole(cw), _whole(cb)],
                    [_orows(L, 2 * Fh, BF16, tm)] + [_oacc(1, Fh) for _ in range(4)], n_acc=4)


def _s5_build(a_re, a_im, log_dt, b_re, b_im, c_re, c_im, d):
    T = CHUNK
    G, P = a_re.shape
    H = d.shape[1]
    hi = lax.Precision.HIGH
    dt = jnp.exp(log_dt)[:, None]
    mag = jnp.exp(a_re * dt)
    ab_re = mag * jnp.cos(a_im * dt)
    ab_im = mag * jnp.sin(a_im * dt)
    den = a_re * a_re + a_im * a_im
    f_re = ((ab_re - 1.0) * a_re + ab_im * a_im) / den
    f_im = (ab_im * a_re - (ab_re - 1.0) * a_im) / den
    bb_re = f_re[..., None] * b_re - f_im[..., None] * b_im
    bb_im = f_re[..., None] * b_im + f_im[..., None] * b_re
    tau = jnp.arange(T + 1, dtype=F32)[None, :, None]
    pmag = jnp.exp(tau * (a_re * dt)[:, None, :])
    pang = tau * (a_im * dt)[:, None, :]
    pw_re = pmag * jnp.cos(pang)
    pw_im = pmag * jnp.sin(pang)
    cp_re = c_re[:, :, None, :] * pw_re[:, None] - c_im[:, :, None, :] * pw_im[:, None]
    cp_im = c_re[:, :, None, :] * pw_im[:, None] + c_im[:, :, None, :] * pw_re[:, None]
    cp_cat = jnp.concatenate([cp_re, -cp_im], axis=-1)
    bb_cat = jnp.concatenate([bb_re, bb_im], axis=1)
    kt = jnp.einsum('ghtq,gqk->ghtk', cp_cat[:, :, :T], bb_cat, precision=hi)
    kt = kt.at[:, :, 0, :].add(d[:, :, None] * jnp.eye(H, dtype=F32)[None])
    kp = jnp.concatenate([kt[:, :, ::-1, :], jnp.zeros((G, H, T - 1, H), F32)], axis=2).reshape(G, H, (2 * T - 1) * H)
    m_mat = jnp.stack([kp[:, :, (T - 1 - t) * H:(2 * T - 1 - t) * H] for t in range(T)], axis=1)
    m_mat = m_mat.reshape(G, T * H, T * H)
    pr = jnp.repeat(pw_re[:, :T][:, ::-1].transpose(0, 2, 1), H, axis=2)
    pi = jnp.repeat(pw_im[:, :T][:, ::-1].transpose(0, 2, 1), H, axis=2)
    br, bi = jnp.tile(bb_re, (1, 1, T)), jnp.tile(bb_im, (1, 1, T))
    n_mat = jnp.concatenate([pr * br - pi * bi, pr * bi + pi * br], axis=1)
    o_mat = cp_cat[:, :, 1:].transpose(0, 2, 1, 3).reshape(G, T * H, 2 * P)
    lam1 = jnp.concatenate([pw_re[:, T], pw_re[:, T]], axis=-1)[:, None, :]
    lam2 = jnp.concatenate([-pw_im[:, T], pw_im[:, T]], axis=-1)[:, None, :]
    return m_mat, n_mat, o_mat, lam1, lam2


def _chunk_scan(name, v, lam1, lam2, reverse, s_fwd=None, gb=64):
    G, nc, W = v.shape
    gb = _tile(G, gb, 1)
    half = W // 2
    ntile = nc // 8

    def body(*refs):
        if reverse:
            v_ref, l1_ref, l2_ref, s_ref, o_ref, d1_ref, d2_ref = refs
        else:
            v_ref, l1_ref, l2_ref, o_ref = refs
        l1 = jnp.broadcast_to(l1_ref[...], (gb, 8, W))
        l2 = jnp.broadcast_to(l2_ref[...], (gb, 8, W))
        if reverse:
            l2 = -l2
        row = lax.broadcasted_iota(jnp.int32, (gb, 8, W), 1)

        def tile_step(n, carry):
            if reverse:
                st, a1, a2 = carry
                base = pl.multiple_of((ntile - 1 - n) * 8, 8)
            else:
                st = carry
                base = pl.multiple_of(n * 8, 8)
            vt = v_ref[:, pl.ds(base, 8), :]
            out = jnp.zeros((gb, 8, W), F32)
            order = range(7, -1, -1) if reverse else range(8)
            for r in order:
                out = jnp.where(row == r, st, out)
                vr = jnp.broadcast_to(vt[:, r:r + 1, :], (gb, 8, W))
                st = l1 * st + l2 * pltpu.roll(st, half, 2) + vr
            o_ref[:, pl.ds(base, 8), :] = out
            if reverse:
                sv = s_ref[:, pl.ds(base, 8), :]
                a1 = a1 + out * sv
                a2 = a2 + out * pltpu.roll(sv, half, 2)
                return st, a1, a2
            return st

        zero = jnp.zeros((gb, 8, W), F32)
        if reverse:
            _, a1, a2 = lax.fori_loop(0, ntile, tile_step, (zero, zero, zero))
            d1_ref[...] = jnp.sum(a1, axis=1, keepdims=True)
            d2_ref[...] = jnp.sum(a2, axis=1, keepdims=True)
        else:
            lax.fori_loop(0, ntile, tile_step, zero)

    big = pl.BlockSpec((gb, nc, W), lambda g: (g, 0, 0), pipeline_mode=pl.Buffered(1))
    vec = pl.BlockSpec((gb, 1, W), lambda g: (g, 0, 0))
    if reverse:
        return _pcall(body, name=name, grid=(G // gb,), in_specs=[big, vec, vec, big],
                      out_specs=[big, vec, vec],
                      out_shape=[jax.ShapeDtypeStruct((G, nc, W), F32), jax.ShapeDtypeStruct((G, 1, W), F32),
                                 jax.ShapeDtypeStruct((G, 1, W), F32)],
                      compiler_params=_params(("parallel",)))(v, lam1, lam2, s_fwd)
    return _pcall(body, name=name, grid=(G // gb,), in_specs=[big, vec, vec], out_specs=big,
                  out_shape=jax.ShapeDtypeStruct((G, nc, W), F32),
                  compiler_params=_params(("parallel",)))(v, lam1, lam2)


_GELU_C = math.sqrt(2.0 / math.pi)


def _gelu(y):
    return 0.5 * y * (1.0 + jnp.tanh(_GELU_C * (y + 0.044715 * y * y * y)))


def _gelu_grad(y):
    t = jnp.tanh(_GELU_C * (y + 0.044715 * y * y * y))
    return 0.5 * (1.0 + t) + 0.5 * y * (1.0 - t * t) * _GELU_C * (1.0 + 3.0 * 0.044715 * y * y)


def _lane_perm(Hg):
    gpb = LANES // Hg
    r = jnp.arange(CHUNK * LANES)
    t, gl, h = r // LANES, (r % LANES) // Hg, r % Hg
    target = gl * (CHUNK * Hg) + t * Hg + h
    return (target[:, None] == r[None, :]).astype(BF16), gpb


def _to_groups(name, a, G, Hg, out_dtype, gelu_arg=None):
    L, D = a.shape
    nc = L // CHUNK
    perm, gpb = _lane_perm(Hg)
    gw = CHUNK * Hg

    def body(*refs):
        x_ref, p_ref, o_ref = refs[0], refs[1], refs[-1]
        x = jnp.concatenate([x_ref[:, t, :] for t in range(CHUNK)], axis=1)
        if x.dtype == BF16:
            z = jnp.dot(x, p_ref[...], preferred_element_type=F32)
        else:
            z = _split_dot(x, p_ref[...])
        for gl in range(gpb):
            piece = z[:, gl * gw:(gl + 1) * gw]
            if gelu_arg is not None:
                piece = piece * _gelu_grad(refs[2][gl])
            o_ref[gl] = piece.astype(o_ref.dtype)

    grouped = pl.BlockSpec((gpb, nc, gw), lambda g: (g, 0, 0))
    return _pcall(
        body, name=name, grid=(G // gpb,),
        in_specs=[pl.BlockSpec((nc, CHUNK, LANES), lambda g: (0, 0, g)),
                  pl.BlockSpec(perm.shape, lambda g: (0, 0))] + ([grouped] if gelu_arg is not None else []),
        out_specs=grouped,
        out_shape=jax.ShapeDtypeStruct((G, nc, gw), out_dtype),
        compiler_params=_params(("parallel",)),
    )(*([a.reshape(nc, CHUNK, D), perm] + ([gelu_arg] if gelu_arg is not None else [])))


def _split_dot(x, u):
    hi = x.astype(BF16)
    lo = (x - hi.astype(F32)).astype(BF16)
    return (jnp.dot(hi, u, preferred_element_type=F32) + jnp.dot(lo, u, preferred_element_type=F32))


def _col_to_row(col, n):
    return jnp.broadcast_to(col, (n, LANES)).T[0:1, :]


def _row_to_col(row, n):
    return jnp.broadcast_to(row, (LANES, n)).T[:, 0:1]


def _head_in(pair, h, upper):
    x = pair.astype(F32)
    lane = lax.broadcasted_iota(jnp.int32, x.shape, 1)
    if (h % 2 == 1) != upper:
        x = pltpu.roll(x, HEAD_DIM, 1)
    keep = (lane >= HEAD_DIM) if upper else (lane < HEAD_DIM)
    return jnp.where(keep, x, 0.0).astype(BF16)


def _pair_out(even, odd, upper):
    lane = lax.broadcasted_iota(jnp.int32, even.shape, 1)
    if upper:
        return jnp.where(lane < HEAD_DIM, pltpu.roll(even, HEAD_DIM, 1), odd)
    return jnp.where(lane < HEAD_DIM, even, pltpu.roll(odd, HEAD_DIM, 1))


def _attn_fwd(name, q, kv, tq, tk, hb):
    L = q.shape[0]
    H = q.shape[1] // HEAD_DIM
    hb = min(hb, H)
    dw = 2 * HEAD_DIM
    nq, nkb = L // tq, L // tk
    hs = range(hb)
    assert tk % tq == 0 and hb % 2 == 0 and H % hb == 0

    def body(q_ref, kv_ref, o_ref, rc_ref):
        i = pl.program_id(1)
        diff = (lax.broadcasted_iota(jnp.int32, (tq, tk), 1) - lax.broadcasted_iota(jnp.int32, (tq, tk), 0))
        u_suf = (lax.broadcasted_iota(jnp.int32, (tk, tk), 0)
                 > lax.broadcasted_iota(jnp.int32, (tk, tk), 1)).astype(BF16)
        nfull = (i * tq) // tk
        rc_ref[...] = jnp.zeros_like(rc_ref)
        qw = [_head_in(q_ref[:, dw * (h // 2):dw * (h // 2 + 1)], h, False) for h in hs]

        def block(kb, carry, masked):
            runs, accs = carry
            ks = pl.multiple_of(kb * tk, tk)
            kvb = [kv_ref[pl.ds(ks, tk), dw * h:dw * (h + 1)] for h in hs]
            z = [lax.dot_general(qw[h], kvb[h], (((1,), (1,)), ((), ())), preferred_element_type=F32)
                 for h in hs]
            e = [jnp.exp(-jnp.abs(z[h])) for h in hs]
            sp = [jnp.maximum(z[h], 0.0) + jnp.log(1.0 + e[h]) for h in hs]
            if masked:
                causal = diff < (i * tq - ks)
                lom = [jnp.where(causal, -sp[h], 0.0) for h in hs]
            else:
                lom = [-sp[h] for h in hs]
            rem = [runs[h] + _split_dot(lom[h], u_suf) for h in hs]
            w = [jnp.exp(z[h] - sp[h] + rem[h]) for h in hs]
            if masked:
                w = [jnp.where(causal, w[h], 0.0) for h in hs]
            accs = tuple(accs[h] + jnp.dot(w[h].astype(BF16), kvb[h], preferred_element_type=F32) for h in hs)
            for h in hs:
                rc_ref[h, kb] = _col_to_row(runs[h], tq)
            runs = tuple(runs[h] + jnp.sum(lom[h], axis=1, keepdims=True) for h in hs)
            return runs, accs

        init = (tuple(jnp.zeros((tq, 1), F32) for _ in hs), tuple(jnp.zeros((tq, dw), F32) for _ in hs))
        carry = block(nfull, init, True)
        _, accs = lax.fori_loop(0, nfull, lambda n, c: block(nfull - 1 - n, c, False), carry)
        for p in range(hb // 2):
            o_ref[:, dw * p:dw * (p + 1)] = _pair_out(accs[2 * p], accs[2 * p + 1], True).astype(o_ref.dtype)

    qspec = pl.BlockSpec((tq, hb * HEAD_DIM), lambda h, i: (i, h))
    kspec = pl.BlockSpec((L, hb * dw), lambda h, i: (0, h), pipeline_mode=pl.Buffered(1))
    return _pcall(
        body, name=name, grid=(H // hb, nq), in_specs=[qspec, kspec],
        out_specs=[qspec, pl.BlockSpec((hb, None, nkb, 1, tq), lambda h, i: (h, i, 0, 0, 0))],
        out_shape=[jax.ShapeDtypeStruct((L, H * HEAD_DIM), BF16), jax.ShapeDtypeStruct((H, nq, nkb, 1, tq), F32)],
        compiler_params=_params(("parallel", "arbitrary")),
    )(q, kv)


def _attn_bwd(name, q, kv, do, rc, tq, tk, scale, hb):
    L = q.shape[0]
    H = q.shape[1] // HEAD_DIM
    hb = min(hb, H)
    dw_ = 2 * HEAD_DIM
    nq, nkb = L // tq, L // tk
    hs = range(hb)
    assert tk % tq == 0 and hb % 2 == 0 and H % hb == 0

    def body(q_ref, kv_ref, do_ref, rc_ref, dq_ref, dkv_ref):
        i = pl.program_id(1)

        @pl.when(i == 0)
        def _():
            dkv_ref[...] = jnp.zeros_like(dkv_ref)

        diff = (lax.broadcasted_iota(jnp.int32, (tq, tk), 1) - lax.broadcasted_iota(jnp.int32, (tq, tk), 0))
        r_io = lax.broadcasted_iota(jnp.int32, (tk, tk), 0)
        c_io = lax.broadcasted_iota(jnp.int32, (tk, tk), 1)
        u_suf = (r_io > c_io).astype(BF16)
        u_pre = (r_io < c_io).astype(BF16)
        nfull = (i * tq) // tk
        qb = [_head_in(q_ref[:, dw_ * (h // 2):dw_ * (h // 2 + 1)], h, False) for h in hs]
        dob = [_head_in(do_ref[:, dw_ * (h // 2):dw_ * (h // 2 + 1)], h, True) for h in hs]

        def block(kb, carry, masked):
            pres, dqs = carry
            ks = pl.multiple_of(kb * tk, tk)
            kvb = [kv_ref[pl.ds(ks, tk), dw_ * h:dw_ * (h + 1)] for h in hs]
            z = [lax.dot_general(qb[h], kvb[h], (((1,), (1,)), ((), ())), preferred_element_type=F32) for h in hs]
            dw = [lax.dot_general(dob[h], kvb[h], (((1,), (1,)), ((), ())), preferred_element_type=F32)
                  for h in hs]
            e = [jnp.exp(-jnp.abs(z[h])) for h in hs]
            sp = [jnp.maximum(z[h], 0.0) + jnp.log(1.0 + e[h]) for h in hs]
            if masked:
                causal = diff < (i * tq - ks)
                lom = [jnp.where(causal, -sp[h], 0.0) for h in hs]
            else:
                lom = [-sp[h] for h in hs]
            rem = [_row_to_col(rc_ref[h, kb], tq) + _split_dot(lom[h], u_suf) for h in hs]
            logb = [z[h] - sp[h] for h in hs]
            w = [jnp.exp(logb[h] + rem[h]) for h in hs]
            if masked:
                w = [jnp.where(causal, w[h], 0.0) for h in hs]
            da = [dw[h] * w[h] for h in hs]
            p = [pres[h] + jnp.dot(da[h].astype(BF16), u_pre, preferred_element_type=F32) for h in hs]
            dz = [da[h] - jnp.exp(logb[h]) * (da[h] + p[h]) for h in hs]
            if masked:
                dz = [jnp.where(causal, dz[h], 0.0) for h in hs]
            dqs = tuple(dqs[h] + jnp.dot(dz[h].astype(BF16), kvb[h], preferred_element_type=F32) for h in hs)
            for h in hs:
                lhs = jnp.concatenate([dz[h].T, w[h].T], axis=1).astype(BF16)
                rhs = jnp.concatenate([qb[h], dob[h]], axis=0)
                dkv_ref[pl.ds(ks, tk), dw_ * h:dw_ * (h + 1)] += jnp.dot(lhs, rhs, preferred_element_type=F32)
            pres = tuple(pres[h] + jnp.sum(da[h], axis=1, keepdims=True) for h in hs)
            return pres, dqs

        init = (tuple(jnp.zeros((tq, 1), F32) for _ in hs), tuple(jnp.zeros((tq, dw_), F32) for _ in hs))
        carry = lax.fori_loop(0, nfull, lambda kb, c: block(kb, c, False), init)
        _, dqs = block(nfull, carry, True)
        for p in range(hb // 2):
            dq_ref[:, dw_ * p:dw_ * (p + 1)] = (_pair_out(dqs[2 * p], dqs[2 * p + 1], False)
                                                * scale).astype(dq_ref.dtype)

    qspec = pl.BlockSpec((tq, hb * HEAD_DIM), lambda h, i: (i, h))
    kspec = pl.BlockSpec((L, hb * dw_), lambda h, i: (0, h), pipeline_mode=pl.Buffered(1))
    return _pcall(
        body, name=name, grid=(H // hb, nq),
        in_specs=[qspec, kspec, qspec, pl.BlockSpec((hb, None, nkb, 1, tq), lambda h, i: (h, i, 0, 0, 0))],
        out_specs=[qspec, kspec],
        out_shape=[jax.ShapeDtypeStruct((L, H * HEAD_DIM), BF16), jax.ShapeDtypeStruct((L, 2 * H * HEAD_DIM), F32)],
        compiler_params=_params(("parallel", "arbitrary")),
    )(q, kv, do, rc)


_MESH = pl.DeviceIdType.MESH
_HBM = pl.BlockSpec(memory_space=pltpu.HBM)


def _all_gather(name, shards):
    n = len(shards)

    def body(*refs):
        x_refs, out_refs = refs[:n], refs[n:2 * n]
        send_sems, recv_sems, local_sems = refs[2 * n:]
        x, y, c = lax.axis_index("x"), lax.axis_index("y"), lax.axis_index("c")
        me, sibling = (x, y, c), (x, y, 1 - c)
        chips = [(1 - x, y), (x, 1 - y), (1 - x, 1 - y)]

        def slot(a, px, py, pc):
            return out_refs[a].at[4 * px + 2 * py + pc]

        def copy(a, k, block, to, src=None):
            return pltpu.make_async_remote_copy(
                src_ref=slot(a, *block) if src is None else src, dst_ref=slot(a, *block),
                send_sem=send_sems.at[7 * a + k], recv_sem=recv_sems.at[7 * a + k], device_id=to,
                device_id_type=_MESH)

        mine = [pltpu.make_async_copy(x_refs[a], slot(a, *me), local_sems.at[a]) for a in range(n)]
        for cp in mine:
            cp.start()
        first = []
        for a in range(n):
            first.append(copy(a, 0, me, sibling, src=x_refs[a]))
            first += [copy(a, 1 + j, me, (*chip, c), src=x_refs[a]) for j, chip in enumerate(chips)]
        for cp in first:
            cp.start()
        passed = []
        for j, chip in enumerate(chips):
            for a in range(n):
                copy(a, 1 + j, (*chip, c), me).wait_recv()
                passed.append(copy(a, 4 + j, (*chip, c), sibling))
                passed[-1].start()
        for a in range(n):
            copy(a, 0, sibling, me).wait_recv()
            for j, chip in enumerate(chips):
                copy(a, 4 + j, (*chip, 1 - c), me).wait_recv()
        for cp in first + passed:
            cp.wait_send()
        for cp in mine:
            cp.wait()

    return _pcall(
        body, name=name, out_shape=[jax.ShapeDtypeStruct((N_DEV,) + s.shape, s.dtype) for s in shards],
        in_specs=[_HBM] * n, out_specs=[_HBM] * n,
        scratch_shapes=[pltpu.SemaphoreType.DMA((7 * n,)), pltpu.SemaphoreType.DMA((7 * n,)),
                        pltpu.SemaphoreType.DMA((n,))],
    )(*shards)


def _sibling_exchange(name, xs):
    n = len(xs)

    def body(*refs):
        x_refs, out_refs, send_sems, recv_sems = refs[:n], refs[n:2 * n], refs[2 * n], refs[2 * n + 1]
        c = lax.axis_index("c")
        sibling = (lax.axis_index("x"), lax.axis_index("y"), 1 - c)
        cps = [pltpu.make_async_remote_copy(src_ref=x_refs[a].at[:, 1 - c], dst_ref=out_refs[a],
                                            send_sem=send_sems.at[a], recv_sem=recv_sems.at[a], device_id=sibling,
                                            device_id_type=_MESH)
               for a in range(n)]
        for cp in cps:
            cp.start()
        for cp in cps:
            cp.wait()

    return _pcall(
        body, name=name, out_shape=[jax.ShapeDtypeStruct(x.shape[:1] + x.shape[2:], x.dtype) for x in xs],
        in_specs=[_HBM] * n, out_specs=[_HBM] * n,
        scratch_shapes=[pltpu.SemaphoreType.DMA((n,)), pltpu.SemaphoreType.DMA((n,))],
    )(*xs)


def _chip_exchange(name, xs):
    n = len(xs)

    def body(*refs):
        x_refs, out_refs, send_sems, recv_sems = refs[:n], refs[n:2 * n], refs[2 * n], refs[2 * n + 1]
        mx, my, mc = lax.axis_index("x"), lax.axis_index("y"), lax.axis_index("c")
        chips = [(1 - mx, my), (mx, 1 - my), (1 - mx, 1 - my)]
        cps = [pltpu.make_async_remote_copy(src_ref=x_refs[a].at[2 * px + py], dst_ref=out_refs[a].at[j],
                                            send_sem=send_sems.at[3 * a + j], recv_sem=recv_sems.at[3 * a + j],
                                            device_id=(px, py, mc), device_id_type=_MESH)
               for a in range(n) for j, (px, py) in enumerate(chips)]
        for cp in cps:
            cp.start()
        for cp in cps:
            cp.wait()

    return _pcall(
        body, name=name, out_shape=[jax.ShapeDtypeStruct((3,) + x.shape[1:], x.dtype) for x in xs],
        in_specs=[_HBM] * n, out_specs=[_HBM] * n,
        scratch_shapes=[pltpu.SemaphoreType.DMA((3 * n,)), pltpu.SemaphoreType.DMA((3 * n,))],
    )(*xs)


def _pair_sum(name, a, b):
    nb, R, C = a.shape
    tm = _tile(R, 512, 16)

    def fn(i, ab, bb):
        s = ab.astype(F32) + bb.astype(F32)
        return s, s

    spec = lambda arr: (arr, (1, tm, C), lambda i: (i // (R // tm), i % (R // tm), 0))
    out = lambda dt: ((nb, R, C), dt, (1, tm, C), lambda i: (i // (R // tm), i % (R // tm), 0))
    return _rowwise(name, fn, nb * (R // tm), [spec(a), spec(b)], [out(F32), out(a.dtype)])


def _adamw(name, w, m, v, grads, tm=512):
    R, C = w.shape
    tm = _tile(R, tm, 16)
    stacked = not isinstance(grads, (list, tuple))

    def fn(i, wb, mb, vb, *gs):
        if stacked:
            gs = [gs[0][d] for d in range(gs[0].shape[0])]
        g = gs[0].astype(F32)
        for t in gs[1:]:
            g = g + t.astype(F32)
        mn = ADAM_B1 * mb + (1.0 - ADAM_B1) * g
        vn = ADAM_B2 * vb + (1.0 - ADAM_B2) * (g * g)
        m_hat = mn / (1.0 - ADAM_B1 ** ADAM_STEP)
        v_hat = vn / (1.0 - ADAM_B2 ** ADAM_STEP)
        delta = -ADAM_LR * (m_hat / (jnp.sqrt(v_hat) + ADAM_EPS) + ADAM_WD * wb)
        return g, delta, mn, vn

    if stacked:
        g_ins = [(grads, (grads.shape[0], tm, C), lambda i: (0, i, 0))]
    else:
        g_ins = [_rows(g, tm) for g in grads]
    ins = [_rows(w, tm), _rows(m, tm), _rows(v, tm)] + g_ins
    return _rowwise(name, fn, R // tm, ins, [_orows(R, C, F32, tm) for _ in range(4)])


CLASSES = [
    (BF16, True, [('ssm_w_in', None), ('attn_w_q', None), ('attn_w_o', None), ('ffn_w_down', 0), ('ffn_w_down', 1)]),
    (BF16, False, [('ssm_w_glu', None), ('kv_w', None)]),
    (BF16, False, [('ffn_w_up', 0), ('ffn_w_up', 1)]),
    (F32, False, [('ffn_conv_w', 0), ('ffn_conv_w', 1)]),
]


def _shard2d(a, name, layer):
    if name == 'ffn_conv_w':
        return a[layer, :, 0, :]
    if name == 'kv_w':
        return a
    return a[0 if layer is None else layer]


def _weights_of(members):
    names = []
    for n, _ in members:
        if not names or names[-1] != n:
            names.append(n)
    return names


def _pack_class(d, members, dtype):
    width = _shard2d(d[members[0][0]], *members[0]).shape[1]
    parts = [d[n].reshape(-1, width).astype(dtype) for n in _weights_of(members)]
    return parts[0] if len(parts) == 1 else jnp.concatenate(parts, axis=0)


def _full_weights(gaths, d):
    w = {}
    for (_, row_sharded, members), g in zip(CLASSES, gaths):
        off = 0
        for n, l in members:
            r = _shard2d(d[n], n, l).shape[0]
            blk = g[:, off:off + r]
            if row_sharded:
                w[(n, l)] = blk.reshape(N_DEV * r, g.shape[2])
            else:
                w[(n, l)] = blk.transpose(1, 0, 2).reshape(r, N_DEV * g.shape[2])
            off += r
    return w


def _grad_classes(gw):
    out = []
    for dtype, row_sharded, members in CLASSES:
        parts = []
        for n, l in members:
            g = gw[(n, l)].astype(dtype)
            if row_sharded:
                parts.append(g.reshape(N_DEV, g.shape[0] // N_DEV, g.shape[1]))
            else:
                parts.append(g.reshape(g.shape[0], N_DEV, g.shape[1] // N_DEV).transpose(1, 0, 2))
        out.append(jnp.concatenate(parts, axis=1))
    return out


def _unpack_class(buf, members, shapes):
    out, off = {}, 0
    for n in _weights_of(members):
        r = math.prod(shapes[n]) // buf.shape[1]
        out[n] = buf[off:off + r].reshape(shapes[n])
        off += r
    return out


def _small_layout(shapes, D):
    lay, off = {}, 0
    for n in SMALL:
        r = -(-math.prod(shapes[n]) // D)
        lay[n] = (off, r)
        off += r
    lay['loss'] = (off, 1)
    off += 1
    return lay, -(-off // 8) * 8


def _pack_small(d, lay, total, D):
    parts = []
    for n in SMALL + ['loss']:
        if n not in d:
            parts.append(jnp.zeros((lay[n][1], D), F32))
            continue
        flat = d[n].reshape(-1).astype(F32)
        parts.append(jnp.pad(flat, (0, lay[n][1] * D - flat.shape[0])).reshape(lay[n][1], D))
    used = sum(lay[n][1] for n in SMALL + ['loss'])
    if total > used:
        parts.append(jnp.zeros((total - used, D), F32))
    return jnp.concatenate(parts, axis=0)


def _unpack_small(pack, lay, shapes, D):
    out = {}
    for n in SMALL:
        off, r = lay[n]
        out[n] = pack[off:off + r].reshape(-1)[:math.prod(shapes[n])].reshape(shapes[n])
    return out


def _ffn_fwd(tag, x, g_norm, w_up, conv_w, conv_b, w_down):
    h = _rms_fwd(f"{tag}_norm", x, g_norm)
    gu = _mm(f"{tag}_up", h, w_up, 'nn', BF16)
    a = _gate_fwd(f"{tag}_gate", gu, conv_w, conv_b)
    return _mm(f"{tag}_down", a, w_down, 'nn', F32, resid=x), (h, gu, a)


def _ffn_bwd(tag, dres, dres_bf, x, saved, g_norm, w_up, conv_w, conv_b, w_down):
    h, gu, a = saved
    da = _mm(f"{tag}_dgate", dres_bf, w_down, 'nt', BF16)
    d_w_down = _mm(f"{tag}_dwdown", a, dres_bf, 'tn', BF16)
    dgu, dw0, dw1, dw2, dcb = _gate_bwd(f"{tag}_gate_b", gu, da, conv_w, conv_b)
    d_w_up = _mm(f"{tag}_dwup", h, dgu, 'tn', BF16)
    dh = _mm(f"{tag}_dh", dgu, w_up, 'nt', F32)
    dres, dres_bf, dg = _rms_bwd(f"{tag}_norm_b", x, dres, [(g_norm, dh)])
    return dres, dres_bf, dg, d_w_up, jnp.concatenate([dw0, dw1, dw2], axis=0), dcb[0], d_w_down


def kernel(x, norm_mix, norm_ffn, norm_kv, norm_final, ssm_w_in, ssm_a_re, ssm_a_im, ssm_log_dt, ssm_b_re, ssm_b_im, ssm_c_re, ssm_c_im, ssm_d, ssm_w_glu, kv_w, attn_w_q, attn_w_o, ffn_w_up, ffn_conv_w, ffn_conv_b, ffn_w_down, loss_target, m_norm_mix, m_norm_ffn, m_norm_kv, m_norm_final, m_ssm_w_in, m_ssm_a_re, m_ssm_a_im, m_ssm_log_dt, m_ssm_b_re, m_ssm_b_im, m_ssm_c_re, m_ssm_c_im, m_ssm_d, m_ssm_w_glu, m_kv_w, m_attn_w_q, m_attn_w_o, m_ffn_w_up, m_ffn_conv_w, m_ffn_conv_b, m_ffn_w_down, v_norm_mix, v_norm_ffn, v_norm_kv, v_norm_final, v_ssm_w_in, v_ssm_a_re, v_ssm_a_im, v_ssm_log_dt, v_ssm_b_re, v_ssm_b_im, v_ssm_c_re, v_ssm_c_im, v_ssm_d, v_ssm_w_glu, v_kv_w, v_attn_w_q, v_attn_w_o, v_ffn_w_up, v_ffn_conv_w, v_ffn_conv_b, v_ffn_w_down):
    wts = dict(zip(W_NAMES, (norm_mix, norm_ffn, norm_kv, norm_final, ssm_w_in, ssm_a_re, ssm_a_im, ssm_log_dt,
                             ssm_b_re, ssm_b_im, ssm_c_re, ssm_c_im, ssm_d, ssm_w_glu, kv_w, attn_w_q, attn_w_o,
                             ffn_w_up, ffn_conv_w, ffn_conv_b, ffn_w_down)))
    mom = dict(zip(W_NAMES, (m_norm_mix, m_norm_ffn, m_norm_kv, m_norm_final, m_ssm_w_in, m_ssm_a_re, m_ssm_a_im,
                             m_ssm_log_dt, m_ssm_b_re, m_ssm_b_im, m_ssm_c_re, m_ssm_c_im, m_ssm_d, m_ssm_w_glu,
                             m_kv_w, m_attn_w_q, m_attn_w_o, m_ffn_w_up, m_ffn_conv_w, m_ffn_conv_b, m_ffn_w_down)))
    vel = dict(zip(W_NAMES, (v_norm_mix, v_norm_ffn, v_norm_kv, v_norm_final, v_ssm_w_in, v_ssm_a_re, v_ssm_a_im,
                             v_ssm_log_dt, v_ssm_b_re, v_ssm_b_im, v_ssm_c_re, v_ssm_c_im, v_ssm_d, v_ssm_w_glu,
                             v_kv_w, v_attn_w_q, v_attn_w_o, v_ffn_w_up, v_ffn_conv_w, v_ffn_conv_b, v_ffn_w_down)))
    shapes = {n: wts[n].shape for n in W_NAMES}
    _, L, D = x.shape
    Fh = ffn_conv_b.shape[1]
    G, P = ssm_a_re.shape[1], ssm_a_re.shape[2]
    Hg = ssm_d.shape[2]
    x0 = x[0]
    target = loss_target[0]
    scale = HEAD_DIM ** -0.5

    gaths = _all_gather("gather_weights", [_pack_class(wts, members, dt) for dt, _, members in CLASSES])
    w = _full_weights(gaths, wts)
    nm = [norm_mix[l:l + 1] for l in range(2)]
    nf = [norm_ffn[l:l + 1] for l in range(2)]
    nkv = norm_kv[None]
    nfin = norm_final[None]
    cb = [ffn_conv_b[l:l + 1] for l in range(2)]
    cwt = [w[('ffn_conv_w', l)] for l in range(2)]

    s5p = (ssm_a_re[0], ssm_a_im[0], ssm_log_dt[0], ssm_b_re[0], ssm_b_im[0], ssm_c_re[0], ssm_c_im[0], ssm_d[0])
    (m_mat, n_mat, o_mat, lam1, lam2), s5_vjp = jax.vjp(_s5_build, *s5p)
    m_bf, n_bf, o_bf = m_mat.astype(BF16), n_mat.astype(BF16), o_mat.astype(BF16)

    h0 = _rms_fwd("l0_norm", x0, nm[0])
    u = _mm("l0_win", h0, w[('ssm_w_in', None)], 'nn', BF16)
    ug = _to_groups("s5_group_u", u, G, Hg, BF16)
    vloc = _bmm("s5_local_state", [(ug, n_bf, 'nt')], F32)
    st = _chunk_scan("s5_scan", vloc, lam1, lam2, reverse=False)
    yraw, yg = _bmm("s5_out", [(ug, m_bf, 'nt'), (st, o_bf, 'nt')], (F32, BF16),
                    post=lambda acc: (acc, _gelu(acc)), ungroup=(1, Hg))
    z = _mm("l0_wglu", yg, w[('ssm_w_glu', None)], 'nn', F32)
    x1 = _glu_fwd("l0_glu", x0, z)
    x2, ffn0 = _ffn_fwd("f0", x1, nf[0], w[('ffn_w_up', 0)], cwt[0], cb[0], w[('ffn_w_down', 0)])

    hkv = _rms_fwd("kv_norm", x2, nkv)
    nh = D // HEAD_DIM
    w_kv = w[('kv_w', None)].reshape(D, 2, nh, HEAD_DIM).transpose(0, 2, 1, 3).reshape(D, 2 * D)
    kvp = _mm("kv_proj", hkv, w_kv, 'nn', BF16)
    h2 = _rms_fwd("l1_norm", x2, nm[1])
    qn = _mm("l1_wq", h2, w[('attn_w_q', None)], 'nn', BF16, scale=scale)
    o, rc = _attn_fwd("attn_fwd", qn, kvp, ATT_TQ, ATT_TK, ATT_HB_FWD)
    x3 = _mm("l1_wo", o, w[('attn_w_o', None)], 'nn', F32, resid=x2)
    x4, ffn1 = _ffn_fwd("f1", x3, nf[1], w[('ffn_w_up', 1)], cwt[1], cb[1], w[('ffn_w_down', 1)])

    dres, dres_bf, dg_final, loss_part = _final_loss("loss_head", x4, nfin, target)

    gw = {}
    dres, dres_bf, dg_nf1, gup1, gcw1, gcb1, gdn1 = _ffn_bwd("f1", dres, dres_bf, x3, ffn1, nf[1], w[('ffn_w_up', 1)],
                                                             cwt[1], cb[1], w[('ffn_w_down', 1)])
    do = _mm("l1_do", dres_bf, w[('attn_w_o', None)], 'nt', BF16)
    gw[('attn_w_o', None)] = _mm("l1_dwo", o, dres_bf, 'tn', BF16)
    dq, dkv = _attn_bwd("attn_bwd", qn, kvp, do, rc, ATT_TQ, ATT_TK, scale, ATT_HB_BWD)
    gw[('attn_w_q', None)] = _mm("l1_dwq", h2, dq, 'tn', BF16)
    dh2 = _mm("l1_dh", dq, w[('attn_w_q', None)], 'nt', F32)
    g_kv = _mm("kv_dw", hkv, dkv, 'tn', BF16)
    gw[('kv_w', None)] = g_kv.reshape(D, nh, 2, HEAD_DIM).transpose(0, 2, 1, 3).reshape(D, 2 * D)
    dhkv = _mm("kv_dh", dkv, w_kv, 'nt', F32)
    dres, dres_bf, dg_nm1, dg_nkv = _rms_bwd("l1_norm_b", x2, dres, [(nm[1], dh2), (nkv, dhkv)])
    dres, dres_bf, dg_nf0, gup0, gcw0, gcb0, gdn0 = _ffn_bwd("f0", dres, dres_bf, x1, ffn0, nf[0], w[('ffn_w_up', 0)],
                                                             cwt[0], cb[0], w[('ffn_w_down', 0)])
    dz = _glu_bwd("l0_glu_b", dres, z)
    gw[('ssm_w_glu', None)] = _mm("l0_dwglu", yg, dz, 'tn', BF16)
    dyg = _mm("l0_dyg", dz, w[('ssm_w_glu', None)], 'nt', F32)
    dy = _to_groups("s5_group_dy", dyg, G, Hg, BF16, gelu_arg=yraw)
    ds = _bmm("s5_dstate", [(dy, o_bf, 'nn')], F32)
    dv_loc, dlam1, dlam2 = _chunk_scan("s5_scan_b", ds, lam1, lam2, reverse=True, s_fwd=st)
    du = _bmm("s5_du", [(dy, m_bf, 'nn'), (dv_loc, n_bf, 'nn')], BF16, ungroup=(0, Hg))
    d_m = _bmm("s5_dm", [(dy, ug, 'tn')], F32)
    d_o = _bmm("s5_do", [(dy, st, 'tn')], F32)
    d_n = _bmm("s5_dn", [(dv_loc, ug, 'tn')], F32)
    s5g = s5_vjp((d_m, d_n, d_o, dlam1, dlam2))
    gw[('ssm_w_in', None)] = _mm("l0_dwin", h0, du, 'tn', BF16)
    dh0 = _mm("l0_dh", du, w[('ssm_w_in', None)], 'nt', F32)
    grad_x, _, dg_nm0 = _rms_bwd("l0_norm_b", x0, dres, [(nm[0], dh0)])

    gw.update({('ffn_w_up', 0): gup0, ('ffn_w_up', 1): gup1, ('ffn_w_down', 0): gdn0, ('ffn_w_down', 1): gdn1,
               ('ffn_conv_w', 0): gcw0, ('ffn_conv_w', 1): gcw1})

    small_g = {
        'norm_mix': jnp.concatenate([dg_nm0, dg_nm1], axis=0), 'norm_ffn': jnp.concatenate([dg_nf0, dg_nf1], axis=0),
        'norm_kv': dg_nkv, 'norm_final': dg_final, 'ffn_conv_b': jnp.stack([gcb0, gcb1]),
        'ssm_a_re': s5g[0], 'ssm_a_im': s5g[1], 'ssm_log_dt': s5g[2], 'ssm_b_re': s5g[3], 'ssm_b_im': s5g[4],
        'ssm_c_re': s5g[5], 'ssm_c_im': s5g[6], 'ssm_d': s5g[7], 'loss': loss_part[0, 0:1],
    }
    lay, rs = _small_layout(shapes, D)
    small_all = _all_gather("gather_small", [_pack_small(small_g, lay, rs, D)])[0]
    sg, sdelta, sm, sv = _adamw("adamw_small", _pack_small(wts, lay, rs, D), _pack_small(mom, lay, rs, D),
                                _pack_small(vel, lay, rs, D), small_all)
    loss = sg[lay['loss'][0], 0]
    small_out = [_unpack_small(t, lay, shapes, D) for t in (sg, sdelta, sm, sv)]

    c = lax.axis_index("c")
    chip = 2 * lax.axis_index("x") + lax.axis_index("y")
    gcls = [g.reshape((4, 2) + g.shape[1:]) for g in _grad_classes(gw)]
    mine = [lax.dynamic_index_in_dim(g, c, axis=1, keepdims=False) for g in gcls]
    got = _sibling_exchange("rs_sibling", gcls)
    sums = [_pair_sum(f"rs_pair_sum{k}", a, b) for k, (a, b) in enumerate(zip(mine, got))]
    from_chips = _chip_exchange("rs_chips", [on_wire for _, on_wire in sums])
    big_parts = [{}, {}, {}, {}]
    for k, ((_, _, members), (chip_f32, _), fc) in enumerate(zip(CLASSES, sums, from_chips)):
        own = lax.dynamic_index_in_dim(chip_f32, chip, axis=0, keepdims=False)
        res = _adamw(f"adamw_big{k}", _pack_class(wts, members, F32), _pack_class(mom, members, F32),
                     _pack_class(vel, members, F32), [own, fc[0], fc[1], fc[2]])
        for q in range(4):
            big_parts[q].update(_unpack_class(res[q], members, shapes))
    big_out = big_parts

    outs = [loss, grad_x[None]]
    for k in range(4):
        for n in W_NAMES:
            outs.append(big_out[k][n] if n in BIG else small_out[k][n])
    return tuple(outs)
```
